```python
import math
import jax
import jax.numpy as jnp
from jax import lax
import numpy as np

D_MODEL = 1024
BATCH = 4
SEQ = 4096
DEPTH = 2

HEAD_DIM = 64
GROUP_HEADS = 4
GROUP_WIDTH = GROUP_HEADS * HEAD_DIM
N_MIXERS = 4
MIX_WIDTH = N_MIXERS * GROUP_WIDTH
DIFF_HALF = HEAD_DIM // 2
MLSTM_CHUNK = 64
MLSTM_CONV = 4
HGRN_KEY = 64
HGRN_CHUNK = 16
NSA_KV_DIM = HEAD_DIM
CMP_LEN = 32
CMP_STRIDE = 16
SEL_BLOCK = 64
SEL_TOPK = 16
WINDOW = 512
Q_BLOCK = 128
N_GROUPS = 4
EXPERTS_PER_GROUP = 8
TOPK_IN_GROUP = 2
D_EXPERT = 256
EPS = 1e-6
NEG_INF = -1e30
LB_FLOOR = 1e-30
F32 = jnp.float32

IN_WIDTHS = (
    GROUP_WIDTH, GROUP_WIDTH, GROUP_WIDTH,
    GROUP_WIDTH, GROUP_WIDTH, GROUP_WIDTH, GROUP_HEADS, GROUP_HEADS, GROUP_WIDTH,
    GROUP_HEADS * HGRN_KEY, GROUP_HEADS * HGRN_KEY, GROUP_WIDTH, GROUP_WIDTH,
    GROUP_WIDTH, 6 * NSA_KV_DIM, 3 * GROUP_HEADS,
)
IN_COLS = sum(IN_WIDTHS)

kernel_name = "hymba_style_diffattn_mlstm_hgrn2_nsa_hmoe"


def rms_norm(x, gain):
    xf = x.astype(F32)
    y = xf * lax.rsqrt(jnp.mean(xf * xf, axis=-1, keepdims=True) + EPS)
    return (y * gain.astype(F32)).astype(x.dtype)


def alibi_slopes():
    n = 2 * GROUP_HEADS
    s = 2.0 ** (-8.0 * np.arange(1, n + 1) / n)
    return jnp.asarray(s[0::2], F32), jnp.asarray(s[1::2], F32)


def causal_depthwise_conv(x, w):
    k_len, ch = w.shape
    xp = jnp.pad(x, ((0, 0), (k_len - 1, 0), (0, 0)))
    return lax.conv_general_dilated(xp, w[:, None, :].astype(x.dtype), (1,), 'VALID',
                                    dimension_numbers=('NWC', 'WIO', 'NWC'), feature_group_count=ch)


def diff_attention(q_in, k_in, v_in, q_gain, k_gain, lam_vecs, sub_gain, slopes, layer_idx):
    B, S, _ = q_in.shape
    H = GROUP_HEADS
    q = rms_norm(q_in.reshape(B, S, H, 2, DIFF_HALF), q_gain) * (DIFF_HALF ** -0.5)
    k = rms_norm(k_in.reshape(B, S, H, 2, DIFF_HALF), k_gain)
    v = v_in.reshape(B, S, H, HEAD_DIM)
    lam_init = 0.8 - 0.6 * math.exp(-0.3 * layer_idx)
    lv = lam_vecs.astype(F32)
    lam = jnp.exp(jnp.dot(lv[0], lv[1])) - jnp.exp(jnp.dot(lv[2], lv[3])) + lam_init
    nb = S // Q_BLOCK
    q_blocks = q.reshape(B, nb, Q_BLOCK, H, 2, DIFF_HALF).swapaxes(0, 1)
    k_pos = jnp.arange(S)

    def attend_block(args):
        q_blk, bi = args
        q_pos = bi * Q_BLOCK + jnp.arange(Q_BLOCK)
        dist = q_pos[:, None] - k_pos[None, :]
        s = jnp.einsum('bqhcd,bkhcd->bhcqk', q_blk, k).astype(F32)
        s = s - slopes[None, :, None, None, None] * dist.astype(F32)
        s = jnp.where(dist >= 0, s, NEG_INF)
        p = jax.nn.softmax(s, axis=-1)
        a = p[:, :, 0] - lam * p[:, :, 1]
        return jnp.einsum('bhqk,bkhd->bqhd', a.astype(v.dtype), v)

    o = lax.map(attend_block, (q_blocks, jnp.arange(nb)))
    o = o.swapaxes(0, 1).reshape(B, S, H, HEAD_DIM)
    o = rms_norm(o, sub_gain) * (1.0 - lam_init)
    return o.reshape(B, S, GROUP_WIDTH)


def mlstm(q_in, k_in, v_in, i_pre, f_pre, o_pre, out_gain):
    B, S, _ = q_in.shape
    H, d, L = GROUP_HEADS, HEAD_DIM, MLSTM_CHUNK
    nc = S // L

    def to_chunks(t, width):
        return t.astype(F32).reshape(B, nc, L, H, width).transpose(1, 0, 3, 2, 4)

    q = to_chunks(q_in, d) * (d ** -0.5)
    k = to_chunks(k_in, d)
    v = to_chunks(v_in, d)
    log_f = to_chunks(jax.nn.log_sigmoid(f_pre.astype(F32))[..., None], 1)[..., 0]
    log_i = to_chunks(i_pre[..., None], 1)[..., 0]
    causal = jnp.tril(jnp.ones((L, L), dtype=bool))

    def chunk_step(carry, inp):
        C, n, m = carry
        q_c, k_c, v_c, lf, li = inp
        g = jnp.cumsum(lf, axis=-1)
        g_last = g[..., -1]
        log_d = jnp.where(causal, g[..., :, None] - g[..., None, :] + li[..., None, :], NEG_INF)
        log_inter = g + m[..., None]
        m_t = jnp.maximum(log_inter, jnp.max(log_d, axis=-1))
        w_intra = jnp.einsum('bhtd,bhsd->bhts', q_c, k_c) * jnp.exp(log_d - m_t[..., None])
        w_inter = jnp.exp(log_inter - m_t)
        num = (w_inter[..., None] * jnp.einsum('bhtd,bhde->bhte', q_c, C)
               + jnp.einsum('bhts,bhse->bhte', w_intra, v_c))
        den = w_inter * jnp.einsum('bhtd,bhd->bht', q_c, n) + jnp.sum(w_intra, axis=-1)
        h = num / jnp.maximum(jnp.abs(den), jnp.exp(-m_t))[..., None]
        log_a = g_last[..., None] - g + li
        m_new = jnp.maximum(g_last + m, jnp.max(log_a, axis=-1))
        a = jnp.exp(log_a - m_new[..., None])
        decay = jnp.exp(g_last + m - m_new)
        C_new = decay[..., None, None] * C + jnp.einsum('bhs,bhsd,bhse->bhde', a, k_c, v_c)
        n_new = decay[..., None] * n + jnp.einsum('bhs,bhsd->bhd', a, k_c)
        return (C_new, n_new, m_new), h

    init = (jnp.zeros((B, H, d, d), F32), jnp.zeros((B, H, d), F32), jnp.zeros((B, H), F32))
    _, h = lax.scan(chunk_step, init, (q, k, v, log_f, log_i))
    h = h.transpose(1, 0, 3, 2, 4).reshape(B, S, H, d)
    h = rms_norm(h, out_gain) * jax.nn.sigmoid(o_pre.astype(F32)).reshape(B, S, H, d)
    return h.reshape(B, S, GROUP_WIDTH)


def hgrn2(q_in, f_in, i_in, g_in, lower_bound, out_gain):
    B, S, _ = q_in.shape
    H, dk, dv, L = GROUP_HEADS, HGRN_KEY, HEAD_DIM, HGRN_CHUNK
    nc = S // L
    lb = lower_bound.astype(F32)
    log_f = jnp.logaddexp(jnp.log(jnp.maximum(lb, LB_FLOOR)),
                          jnp.log1p(-lb) + jax.nn.log_sigmoid(f_in.astype(F32)))
    k = -jnp.expm1(log_f)
    q = jax.nn.silu(q_in.astype(F32))

    def to_chunks(t, width):
        return t.astype(F32).reshape(B, nc, L, H, width).transpose(1, 0, 3, 2, 4)

    causal = jnp.tril(jnp.ones((L, L), dtype=bool))

    def chunk_step(state, inp):
        q_c, k_c, v_c, lf = inp
        b = jnp.cumsum(lf, axis=2)
        b_last = b[:, :, -1]
        inter = jnp.einsum('bhtk,bhkv->bhtv', q_c * jnp.exp(b), state)
        rel = jnp.where(causal[:, :, None], b[:, :, :, None, :] - b[:, :, None, :, :], NEG_INF)
        attn = jnp.einsum('bhtk,bhsk,bhtsk->bhts', q_c, k_c, jnp.exp(rel))
        o = inter + jnp.einsum('bhts,bhsv->bhtv', attn, v_c)
        state = (jnp.exp(b_last)[..., None] * state
                 + jnp.einsum('bhsk,bhsv->bhkv', k_c * jnp.exp(b_last[:, :, None] - b), v_c))
        return state, o

    init = jnp.zeros((B, H, dk, dv), F32)
    _, o = lax.scan(chunk_step, init,
                    (to_chunks(q, dk), to_chunks(k, dk), to_chunks(i_in, dv), to_chunks(log_f, dk)))
    o = o.transpose(1, 0, 3, 2, 4).reshape(B, S, H, dv)
    o = rms_norm(o, out_gain) * jax.nn.sigmoid(g_in.astype(F32)).reshape(B, S, H, dv)
    return o.reshape(B, S, GROUP_WIDTH)


def nsa(q_in, kv_in, gate_in, cmp_pos, cmp_w1, cmp_w2, q_gain, k_gain, out_gain, slopes):
    B, S, _ = q_in.shape
    H, d = GROUP_HEADS, HEAD_DIM
    k_c, v_c, k_s, v_s, k_w, v_w = jnp.split(kv_in, 6, axis=-1)
    q = rms_norm(q_in.reshape(B, S, H, d), q_gain) * (d ** -0.5)
    t_pos = jnp.arange(S)

    n_cmp = (S - CMP_LEN) // CMP_STRIDE + 1
    cmp_start = jnp.arange(n_cmp) * CMP_STRIDE
    cmp_idx = cmp_start[:, None] + jnp.arange(CMP_LEN)[None, :]

    def compress(t, j):
        blocks = t[:, cmp_idx] + cmp_pos[j]
        hdn = jax.nn.gelu(blocks.reshape(B, n_cmp, CMP_LEN * d) @ cmp_w1[j])
        return hdn @ cmp_w2[j]

    k_cmp = rms_norm(compress(k_c, 0), k_gain[0])
    v_cmp = compress(v_c, 1)
    dist_c = t_pos[:, None] - (cmp_start + CMP_LEN - 1)[None, :]
    s_c = (jnp.einsum('bshd,bnd->bhsn', q, k_cmp).astype(F32)
           - slopes[None, :, None, None] * dist_c.astype(F32))
    valid_c = dist_c >= 0
    p_c = jnp.where(valid_c, jax.nn.softmax(jnp.where(valid_c, s_c, NEG_INF), axis=-1), 0.0)
    o_cmp = jnp.einsum('bhsn,bnd->bshd', p_c.astype(v_cmp.dtype), v_cmp)

    n_sel = S // SEL_BLOCK
    n_top = min(SEL_TOPK, n_sel)
    sel_start = jnp.arange(n_sel) * SEL_BLOCK
    overlap = jnp.clip(jnp.minimum(cmp_start[:, None] + CMP_LEN, sel_start[None, :] + SEL_BLOCK)
                       - jnp.maximum(cmp_start[:, None], sel_start[None, :]), 0) / CMP_LEN
    importance = jnp.einsum('bhsn,nj->bsj', p_c, overlap.astype(F32))
    q_blk = t_pos // SEL_BLOCK
    blk = jnp.arange(n_sel)
    importance = jnp.where(blk[None, :] <= q_blk[:, None], importance, -1.0)
    forced = ((blk[None, :] == 0) | (blk[None, :] == q_blk[:, None])
              | (blk[None, :] == q_blk[:, None] - 1))
    importance = jnp.where(forced, 2.0, importance)
    _, sel_idx = lax.top_k(importance, n_top)

    k_sel_blocks = rms_norm(k_s, k_gain[1]).reshape(B, n_sel, SEL_BLOCK, d)
    v_sel_blocks = v_s.reshape(B, n_sel, SEL_BLOCK, d)
    nqb = S // Q_BLOCK
    q_blocks = q.reshape(B, nqb, Q_BLOCK, H, d).swapaxes(0, 1)
    idx_blocks = sel_idx.reshape(B, nqb, Q_BLOCK, n_top).swapaxes(0, 1)
    gather = jax.vmap(lambda blocks, ids: blocks[ids])

    def select_block(args):
        q_b, ids, bi = args
        kg = gather(k_sel_blocks, ids)
        vg = gather(v_sel_blocks, ids)
        q_pos = bi * Q_BLOCK + jnp.arange(Q_BLOCK)
        k_pos = ids[..., None] * SEL_BLOCK + jnp.arange(SEL_BLOCK)
        dist = (q_pos[None, :, None, None] - k_pos)[:, None]
        s = (jnp.einsum('bqhd,bqnld->bhqnl', q_b, kg).astype(F32)
             - slopes[None, :, None, None, None] * dist.astype(F32))
        s = jnp.where(dist >= 0, s, NEG_INF).reshape(B, H, Q_BLOCK, n_top * SEL_BLOCK)
        p = jax.nn.softmax(s, axis=-1).reshape(B, H, Q_BLOCK, n_top, SEL_BLOCK)
        return jnp.einsum('bhqnl,bqnld->bqhd', p.astype(vg.dtype), vg)

    o_sel = lax.map(select_block, (q_blocks, idx_blocks, jnp.arange(nqb)))
    o_sel = o_sel.swapaxes(0, 1).reshape(B, S, H, d)

    n_back = WINDOW // Q_BLOCK
    k_win = rms_norm(k_w, k_gain[2]).reshape(B, nqb, Q_BLOCK, d)
    v_win = v_w.reshape(B, nqb, Q_BLOCK, d)

    def band(t):
        tp = jnp.pad(t, ((0, 0), (n_back, 0), (0, 0), (0, 0)))
        return jnp.concatenate([tp[:, j:j + nqb] for j in range(n_back + 1)], axis=2)

    k_band, v_band = band(k_win), band(v_win)
    q_pos = jnp.arange(nqb)[:, None] * Q_BLOCK + jnp.arange(Q_BLOCK)[None, :]
    k_pos = (jnp.arange(nqb)[:, None] - n_back) * Q_BLOCK + jnp.arange((n_back + 1) * Q_BLOCK)[None, :]
    dist = q_pos[:, :, None] - k_pos[:, None, :]
    in_win = (dist >= 0) & (dist < WINDOW) & (k_pos[:, None, :] >= 0)
    s_w = jnp.einsum('bnqhd,bnkd->bnhqk', q.reshape(B, nqb, Q_BLOCK, H, d), k_band).astype(F32)
    s_w = s_w - slopes[None, None, :, None, None] * dist[None, :, None].astype(F32)
    s_w = jnp.where(in_win[None, :, None], s_w, NEG_INF)
    p_w = jax.nn.softmax(s_w, axis=-1)
    o_win = jnp.einsum('bnhqk,bnkd->bnqhd', p_w.astype(v_band.dtype), v_band).reshape(B, S, H, d)

    gates = jax.nn.sigmoid(gate_in.astype(F32)).reshape(B, S, H, 3)
    o = gates[..., 0:1] * o_cmp + gates[..., 1:2] * o_sel + gates[..., 2:3] * o_win
    o = rms_norm(o, out_gain)
    return o.reshape(B, S, GROUP_WIDTH)


def hier_moe(h, w_group, b_group, w_expert, b_expert, w_gate, w_up, w_down):
    B, S, D = h.shape
    t = h.reshape(B * S, D)
    p_group = jax.nn.softmax((t @ w_group).astype(F32) + b_group, axis=-1)
    g_sel = jnp.argmax(p_group, axis=-1)
    g_prob = jnp.max(p_group, axis=-1)
    logits_e = ((t @ w_expert).astype(F32) + b_expert).reshape(-1, N_GROUPS, EXPERTS_PER_GROUP)
    logits_e = jnp.take_along_axis(logits_e, g_sel[:, None, None], axis=1)[:, 0]
    top_p, top_i = lax.top_k(jax.nn.softmax(logits_e, axis=-1), TOPK_IN_GROUP)
    top_w = g_prob[:, None] * top_p / jnp.sum(top_p, axis=-1, keepdims=True)
    w_e = jnp.sum(jax.nn.one_hot(top_i, EXPERTS_PER_GROUP, dtype=F32) * top_w[..., None], axis=1)
    gate = jax.nn.one_hot(g_sel, N_GROUPS, dtype=F32)[:, :, None] * w_e[:, None, :]
    out = jnp.zeros_like(t)
    for g in range(N_GROUPS):
        a = jnp.einsum('td,edf->tef', t, w_gate[g])
        u = jnp.einsum('td,edf->tef', t, w_up[g])
        act = jax.nn.silu(a) * u * gate[:, g, :, None].astype(t.dtype)
        out = out + jnp.einsum('tef,efd->td', act, w_down[g])
    return out.reshape(B, S, D)


def setup_inputs(seed: int = 0) -> dict:
    key = jax.random.key(seed)
    keys = iter(jax.random.split(key, 40))

    def nrm(shape, scale):
        return jax.random.normal(next(keys), shape, F32) * scale

    L = DEPTH
    n_exp = N_GROUPS * EXPERTS_PER_GROUP
    f_bias = jnp.linspace(3.0, 6.0, GROUP_HEADS, dtype=F32)
    return {
        'x': nrm((BATCH, SEQ, D_MODEL), 1.0),
        'norm_mix': 1.0 + nrm((L, D_MODEL), 0.02),
        'norm_ffn': 1.0 + nrm((L, D_MODEL), 0.02),
        'w_in': nrm((L, D_MODEL, IN_COLS), D_MODEL ** -0.5),
        'w_out': nrm((L, MIX_WIDTH, D_MODEL), MIX_WIDTH ** -0.5),
        'diff_q_gain': 1.0 + nrm((L, DIFF_HALF), 0.02),
        'diff_k_gain': 1.0 + nrm((L, DIFF_HALF), 0.02),
        'diff_lambda': nrm((L, 4, DIFF_HALF), 0.1),
        'diff_sub_gain': 1.0 + nrm((L, GROUP_HEADS, HEAD_DIM), 0.02),
        'mlstm_conv': nrm((L, MLSTM_CONV, 2 * GROUP_WIDTH), MLSTM_CONV ** -0.5),
        'mlstm_gate_bias': jnp.stack([nrm((L, GROUP_HEADS), 0.1),
                                      f_bias + nrm((L, GROUP_HEADS), 0.1)], axis=1),
        'mlstm_out_gain': 1.0 + nrm((L, GROUP_HEADS, HEAD_DIM), 0.02),
        'hgrn_lower_bounds': nrm((L, GROUP_HEADS * HGRN_KEY), 0.5),
        'hgrn_out_gain': 1.0 + nrm((L, GROUP_HEADS, HEAD_DIM), 0.02),
        'nsa_cmp_pos': nrm((L, 2, CMP_LEN, HEAD_DIM), 0.1),
        'nsa_cmp_w1': nrm((L, 2, CMP_LEN * HEAD_DIM, HEAD_DIM), (CMP_LEN * HEAD_DIM) ** -0.5),
        'nsa_cmp_w2': nrm((L, 2, HEAD_DIM, HEAD_DIM), HEAD_DIM ** -0.5),
        'nsa_q_gain': 1.0 + nrm((L, HEAD_DIM), 0.02),
        'nsa_k_gain': 1.0 + nrm((L, 3, HEAD_DIM), 0.02),
        'nsa_out_gain': 1.0 + nrm((L, GROUP_HEADS, HEAD_DIM), 0.02),
        'moe_w_group': nrm((L, D_MODEL, N_GROUPS), D_MODEL ** -0.5),
        'moe_b_group': nrm((L, N_GROUPS), 0.01),
        'moe_w_expert': nrm((L, D_MODEL, n_exp), D_MODEL ** -0.5),
        'moe_b_expert': nrm((L, n_exp), 0.01),
        'moe_w_gate': nrm((L, N_GROUPS, EXPERTS_PER_GROUP, D_MODEL, D_EXPERT), D_MODEL ** -0.5),
        'moe_w_up': nrm((L, N_GROUPS, EXPERTS_PER_GROUP, D_MODEL, D_EXPERT), D_MODEL ** -0.5),
        'moe_w_down': nrm((L, N_GROUPS, EXPERTS_PER_GROUP, D_EXPERT, D_MODEL), D_EXPERT ** -0.5),
    }


def reference(x, norm_mix, norm_ffn, w_in, w_out, diff_q_gain, diff_k_gain, diff_lambda, diff_sub_gain,
              mlstm_conv, mlstm_gate_bias, mlstm_out_gain, hgrn_lower_bounds, hgrn_out_gain,
              nsa_cmp_pos, nsa_cmp_w1, nsa_cmp_w2, nsa_q_gain, nsa_k_gain, nsa_out_gain,
              moe_w_group, moe_b_group, moe_w_expert, moe_b_expert, moe_w_gate, moe_w_up, moe_w_down):
    slopes_diff, slopes_nsa = alibi_slopes()
    lb_soft = jax.nn.softmax(hgrn_lower_bounds.astype(F32), axis=0)
    lower_bounds = jnp.cumsum(lb_soft, axis=0) - lb_soft[0]
    split_at = [int(v) for v in np.cumsum(IN_WIDTHS)[:-1]]
    for l in range(DEPTH):
        h = rms_norm(x, norm_mix[l])
        (a_q, a_k, a_v, b_q, b_k, b_v, b_i, b_f, b_o,
         c_q, c_f, c_i, c_g, d_q, d_kv, d_g) = jnp.split(h @ w_in[l], split_at, axis=-1)
        y_a = diff_attention(a_q, a_k, a_v, diff_q_gain[l], diff_k_gain[l], diff_lambda[l],
                             diff_sub_gain[l], slopes_diff, l)
        qk = jax.nn.silu(causal_depthwise_conv(jnp.concatenate([b_q, b_k], axis=-1), mlstm_conv[l]))
        y_b = mlstm(qk[..., :GROUP_WIDTH], qk[..., GROUP_WIDTH:], b_v,
                    b_i + mlstm_gate_bias[l, 0], b_f + mlstm_gate_bias[l, 1], b_o, mlstm_out_gain[l])
        y_c = hgrn2(c_q, c_f, c_i, c_g, lower_bounds[l], hgrn_out_gain[l])
        y_d = nsa(d_q, d_kv, d_g, nsa_cmp_pos[l], nsa_cmp_w1[l], nsa_cmp_w2[l], nsa_q_gain[l],
                  nsa_k_gain[l], nsa_out_gain[l], slopes_nsa)
        mixed = jnp.concatenate([y_a.astype(x.dtype), y_b.astype(x.dtype),
                                 y_c.astype(x.dtype), y_d.astype(x.dtype)], axis=-1)
        x = x + mixed @ w_out[l]
        x = x + hier_moe(rms_norm(x, norm_ffn[l]), moe_w_group[l], moe_b_group[l], moe_w_expert[l],
                         moe_b_expert[l], moe_w_gate[l], moe_w_up[l], moe_w_down[l])
    return x
```

```python
import functools
import math

import numpy as np
import jax
import jax.numpy as jnp
from jax import lax
from jax.experimental import pallas as pl
from jax.experimental.pallas import tpu as pltpu

F32 = jnp.float32
BF16 = jnp.bfloat16

D_MODEL = 1024
HEAD_DIM = 64
GROUP_HEADS = 4
GROUP_WIDTH = GROUP_HEADS * HEAD_DIM
DIFF_HALF = HEAD_DIM // 2
EPS = 1e-6
NEG_INF = -1e30
LOG2E = math.log2(math.e)

LANES = 128
VMEM_LIMIT = 48 * 1024 * 1024

COL_A = 0
COL_B = 768
COL_C = 1792
COL_D = 2816
COL_G = 3456
P_COLS = 3584


def _cparams(sem):
    return pltpu.CompilerParams(dimension_semantics=sem, vmem_limit_bytes=VMEM_LIMIT)


def _block_ones(width, seg):
    i = np.arange(width)
    return jnp.asarray((i[:, None] // seg == i[None, :] // seg).astype(np.float32), BF16)


def _seg_sum(x, bd):
    hi = x.astype(BF16)
    lo = (x - hi.astype(F32)).astype(BF16)
    return (jnp.dot(hi, bd, preferred_element_type=F32) + jnp.dot(lo, bd, preferred_element_type=F32))


def _inproj_kernel(x_ref, g_ref, w_ref, o_ref):
    x = x_ref[...]
    ms = jnp.mean(x * x, axis=-1, keepdims=True)
    h = (x * lax.rsqrt(ms + EPS) * g_ref[...]).astype(BF16)
    o_ref[...] = jnp.dot(h, w_ref[...], preferred_element_type=F32)


def _inproj(x2d, gain, w_cat, tm=256):
    t = x2d.shape[0]
    return pl.pallas_call(
        _inproj_kernel,
        grid=(t // tm,),
        in_specs=[pl.BlockSpec((tm, D_MODEL), lambda i: (i, 0)),
                  pl.BlockSpec((1, D_MODEL), lambda i: (0, 0)),
                  pl.BlockSpec((D_MODEL, P_COLS), lambda i: (0, 0))],
        out_specs=pl.BlockSpec((tm, P_COLS), lambda i: (i, 0)),
        out_shape=jax.ShapeDtypeStruct((t, P_COLS), F32),
        compiler_params=_cparams(("arbitrary",)),
        name="inproj",
    )(x2d, gain.reshape(1, D_MODEL), w_cat)


def _pack_w_in(w):
    a = w[:, 0:768]
    bq, bk, bv = w[:, 768:1024], w[:, 1024:1280], w[:, 1280:1536]
    bi, bf, bo = w[:, 1536:1540], w[:, 1540:1544], w[:, 1544:1800]
    c = w[:, 1800:2824]
    dq, dkv, dg = w[:, 2824:3080], w[:, 3080:3464], w[:, 3464:3476]
    pad = jnp.zeros((w.shape[0], P_COLS - COL_G - 20), w.dtype)
    return jnp.concatenate([a, bq, bk, bv, bo, c, dq, dkv, bi, bf, dg, pad], axis=1).astype(BF16)


def _diff_prep_kernel(p_ref, qg_ref, kg_ref, bd_ref, q_ref, k_ref, vt_ref):
    p = p_ref[...]
    bd = bd_ref[...]

    def norm(x, g):
        ss = _seg_sum(x * x, bd)
        return x * lax.rsqrt(ss * (1.0 / DIFF_HALF) + EPS) * g

    q = norm(p[:, 0:256], qg_ref[...]) * (DIFF_HALF ** -0.5 * LOG2E)
    k = norm(p[:, 256:512], kg_ref[...])
    q_ref[...] = q.astype(BF16)
    k_ref[...] = k.astype(BF16)
    vt_ref[...] = p[:, 512:768].T.astype(BF16)


def _diff_attn_kernel(lam_ref, q_ref, k_ref, vt_ref, sg_ref, bd_ref, o_ref, *, tq, slopes, out_scale):
    qi = pl.program_id(1)
    q = q_ref[...]
    lane = lax.broadcasted_iota(jnp.int32, q.shape, 1)
    kio = lax.broadcasted_iota(jnp.int32, (tq, tq), 0)
    qio = lax.broadcasted_iota(jnp.int32, (tq, tq), 1)
    rel = (kio - qio).astype(F32)
    causal = kio <= qio
    lam = lam_ref[0, 0]
    heads = []
    for h in range(GROUP_HEADS):
        sl2 = slopes[h] * LOG2E
        bias = rel * sl2
        vt_h = None
        comps = []
        for c in range(2):
            j = 2 * h + c
            qj = jnp.where((lane >= DIFF_HALF * j) & (lane < DIFF_HALF * (j + 1)), q, jnp.zeros_like(q))

            def tile(kt, carry, masked):
                m, l, acc = carry
                kb = k_ref[pl.ds(pl.multiple_of(kt * tq, tq), tq), :]
                s = lax.dot_general(kb, qj, (((1,), (1,)), ((), ())), preferred_element_type=F32)
                off = (kt - qi).astype(F32) * (tq * sl2)
                t = (s + off) + bias
                if masked:
                    t = jnp.where(causal, t, NEG_INF)
                m_new = jnp.maximum(m, jnp.max(t, axis=0, keepdims=True))
                alpha = jnp.exp2(m - m_new)
                p = jnp.exp2(t - m_new)
                l_new = alpha * l + jnp.sum(p, axis=0, keepdims=True)
                vt = vt_ref[HEAD_DIM * h:HEAD_DIM * (h + 1), pl.ds(pl.multiple_of(kt * tq, tq), tq)]
                pv = jnp.dot(vt, p.astype(BF16), preferred_element_type=F32)
                return m_new, l_new, alpha * acc + pv

            init = (jnp.full((1, tq), NEG_INF, F32), jnp.zeros((1, tq), F32), jnp.zeros((HEAD_DIM, tq), F32))
            carry = lax.fori_loop(0, qi, functools.partial(tile, masked=False), init)
            m, l, acc = tile(qi, carry, True)
            comps.append(acc / l)
        heads.append(comps[0] - lam * comps[1])
    o = jnp.concatenate(heads, axis=0).T
    ss = _seg_sum(o * o, bd_ref[...])
    o_ref[...] = o * lax.rsqrt(ss * (1.0 / HEAD_DIM) + EPS) * (sg_ref[...] * out_scale)


def _diff_attention(p3, q_gain, k_gain, lam_vecs, sub_gain, slopes, layer_idx, tq=256):
    b, s, _ = p3.shape
    tm = tq
    bd32 = _block_ones(GROUP_WIDTH, DIFF_HALF)
    bd64 = _block_ones(GROUP_WIDTH, HEAD_DIM)
    qg = jnp.tile(q_gain.astype(F32), 2 * GROUP_HEADS).reshape(1, GROUP_WIDTH)
    kg = jnp.tile(k_gain.astype(F32), 2 * GROUP_HEADS).reshape(1, GROUP_WIDTH)
    qn, kn, vt = pl.pallas_call(
        _diff_prep_kernel,
        grid=(b, s // tm),
        in_specs=[pl.BlockSpec((None, tm, 768), lambda bi, i: (bi, i, COL_A // 768)),
                  pl.BlockSpec((1, GROUP_WIDTH), lambda bi, i: (0, 0)),
                  pl.BlockSpec((1, GROUP_WIDTH), lambda bi, i: (0, 0)),
                  pl.BlockSpec((GROUP_WIDTH, GROUP_WIDTH), lambda bi, i: (0, 0))],
        out_specs=[pl.BlockSpec((None, tm, GROUP_WIDTH), lambda bi, i: (bi, i, 0)),
                   pl.BlockSpec((None, tm, GROUP_WIDTH), lambda bi, i: (bi, i, 0)),
                   pl.BlockSpec((None, GROUP_WIDTH, tm), lambda bi, i: (bi, 0, i))],
        out_shape=[jax.ShapeDtypeStruct((b, s, GROUP_WIDTH), BF16),
                   jax.ShapeDtypeStruct((b, s, GROUP_WIDTH), BF16),
                   jax.ShapeDtypeStruct((b, GROUP_WIDTH, s), BF16)],
        compiler_params=_cparams(("arbitrary", "arbitrary")),
        name="diff_prep",
    )(p3, qg, kg, bd32)

    lam_init = 0.8 - 0.6 * math.exp(-0.3 * layer_idx)
    lv = lam_vecs.astype(F32)
    lam = (jnp.exp(jnp.dot(lv[0], lv[1])) - jnp.exp(jnp.dot(lv[2], lv[3])) + lam_init).reshape(1, 1)
    kern = functools.partial(_diff_attn_kernel, tq=tq, slopes=tuple(float(v) for v in slopes),
                             out_scale=1.0 - lam_init)
    return pl.pallas_call(
        kern,
        grid=(b, s // tq),
        in_specs=[pl.BlockSpec(memory_space=pltpu.SMEM),
                  pl.BlockSpec((None, tq, GROUP_WIDTH), lambda bi, i: (bi, i, 0)),
                  pl.BlockSpec((None, s, GROUP_WIDTH), lambda bi, i: (bi, 0, 0)),
                  pl.BlockSpec((None, GROUP_WIDTH, s), lambda bi, i: (bi, 0, 0)),
                  pl.BlockSpec((1, GROUP_WIDTH), lambda bi, i: (0, 0)),
                  pl.BlockSpec((GROUP_WIDTH, GROUP_WIDTH), lambda bi, i: (0, 0))],
        out_specs=pl.BlockSpec((None, tq, GROUP_WIDTH), lambda bi, i: (bi, i, 0)),
        out_shape=jax.ShapeDtypeStruct((b, s, GROUP_WIDTH), F32),
        compiler_params=_cparams(("arbitrary", "arbitrary")),
        name="diff_attn",
    )(lam, qn, kn, vt, sub_gain.astype(F32).reshape(1, GROUP_WIDTH), bd64)


MLSTM_CONV = 4
CONV_HALO = 8
AUG = 128


def _tril_sum(x, tril):
    hi = x.astype(BF16)
    lo = (x - hi.astype(F32)).astype(BF16)
    return jnp.dot(tril, hi, preferred_element_type=F32) + jnp.dot(tril, lo, preferred_element_type=F32)


def _log_sigmoid(x):
    return jnp.minimum(x, 0.0) - jnp.log1p(jnp.exp(-jnp.abs(x)))


def _sigmoid(x):
    return 1.0 / (1.0 + jnp.exp(-x))


def _mlstm_kernel(q_ref, k_ref, v_ref, o_ref, g_ref, cw_ref, gb_ref, og_ref, bd_ref, y_ref,
                  halo_ref, c_ref, m_ref, *, L):
    ci = pl.program_id(1)

    @pl.when(ci == 0)
    def _():
        halo_ref[0:CONV_HALO, :] = jnp.zeros((CONV_HALO, 2 * GROUP_WIDTH), F32)
        c_ref[...] = jnp.zeros_like(c_ref)
        m_ref[...] = jnp.zeros_like(m_ref)

    halo_ref[CONV_HALO:CONV_HALO + L, 0:GROUP_WIDTH] = q_ref[...]
    halo_ref[CONV_HALO:CONV_HALO + L, GROUP_WIDTH:2 * GROUP_WIDTH] = k_ref[...]
    conv = jnp.zeros((L, 2 * GROUP_WIDTH), F32)
    for j in range(MLSTM_CONV):
        start = CONV_HALO - (MLSTM_CONV - 1) + j
        conv = conv + halo_ref[start:start + L, :] * cw_ref[j:j + 1, :]
    halo_ref[0:CONV_HALO, :] = halo_ref[L:L + CONV_HALO, :]
    qk = conv * _sigmoid(conv)
    q = qk[:, 0:GROUP_WIDTH] * (HEAD_DIM ** -0.5)
    k = qk[:, GROUP_WIDTH:2 * GROUP_WIDTH]
    v = v_ref[...]

    gates = g_ref[...] + gb_ref[...]
    ri = lax.broadcasted_iota(jnp.int32, (L, L), 0)
    cj = lax.broadcasted_iota(jnp.int32, (L, L), 1)
    causal = ri >= cj
    tril = jnp.where(causal, 1.0, 0.0).astype(BF16)
    gcum = _tril_sum(_log_sigmoid(gates), tril)
    gcum_t = gcum.T
    gates_t = gates.T
    lane_aug = lax.broadcasted_iota(jnp.int32, (L, AUG), 1)
    m_all = m_ref[...]

    outs = []
    for h in range(GROUP_HEADS):
        hs = slice(HEAD_DIM * h, HEAD_DIM * (h + 1))
        q_h, k_h = q[:, hs], k[:, hs]
        v_aug = jnp.where(lane_aug == HEAD_DIM, 1.0,
                          jnp.concatenate([v[:, hs], jnp.zeros((L, AUG - HEAD_DIM), F32)], axis=1))
        g_col = gcum[:, 4 + h:5 + h]
        li_col = gates[:, h:h + 1]
        g_row = gcum_t[4 + h:5 + h, :]
        li_row = gates_t[h:h + 1, :]
        g_last = g_row[:, L - 1:L]
        m_prev = m_all[:, h:h + 1]
        log_d = jnp.where(causal, g_col - g_row + li_row, NEG_INF)
        log_inter = g_col + m_prev
        m_t = jnp.maximum(log_inter, jnp.max(log_d, axis=1, keepdims=True))
        s_qk = lax.dot_general(q_h.astype(BF16), k_h.astype(BF16), (((1,), (1,)), ((), ())),
                               preferred_element_type=F32)
        w_intra = s_qk * jnp.exp(log_d - m_t)
        w_inter = jnp.exp(log_inter - m_t)
        c_aug = c_ref[h]
        num = (w_inter * jnp.dot(q_h.astype(BF16), c_aug.astype(BF16), preferred_element_type=F32)
               + jnp.dot(w_intra.astype(BF16), v_aug.astype(BF16), preferred_element_type=F32))
        den = num[:, HEAD_DIM:HEAD_DIM + 1]
        outs.append(num[:, 0:HEAD_DIM] / jnp.maximum(jnp.abs(den), jnp.exp(-m_t)))

        log_a = g_last - g_col + li_col
        m_new = jnp.maximum(g_last + m_prev, jnp.max(log_a, axis=0, keepdims=True))
        a_col = jnp.exp(log_a - m_new)
        decay = jnp.exp(g_last + m_prev - m_new)
        ak_t = (k_h * a_col).T.astype(BF16)
        c_ref[h] = decay * c_aug + jnp.dot(ak_t, v_aug.astype(BF16), preferred_element_type=F32)
        m_ref[:, h:h + 1] = m_new

    hcat = jnp.concatenate(outs, axis=1)
    ss = _seg_sum(hcat * hcat, bd_ref[...])
    y_ref[...] = hcat * lax.rsqrt(ss * (1.0 / HEAD_DIM) + EPS) * og_ref[...] * _sigmoid(o_ref[...])


def _mlstm(p3, conv_w, gate_bias, out_gain, L=128):
    b, s, _ = p3.shape
    gb = jnp.zeros((1, LANES), F32).at[0, 0:GROUP_HEADS].set(gate_bias[0]).at[0, GROUP_HEADS:2 * GROUP_HEADS].set(gate_bias[1])
    col = lambda off: (lambda bi, i: (bi, i, off // GROUP_WIDTH))
    return pl.pallas_call(
        functools.partial(_mlstm_kernel, L=L),
        grid=(b, s // L),
        in_specs=[pl.BlockSpec((None, L, GROUP_WIDTH), col(COL_B)),
                  pl.BlockSpec((None, L, GROUP_WIDTH), col(COL_B + 256)),
                  pl.BlockSpec((None, L, GROUP_WIDTH), col(COL_B + 512)),
                  pl.BlockSpec((None, L, GROUP_WIDTH), col(COL_B + 768)),
                  pl.BlockSpec((None, L, LANES), lambda bi, i: (bi, i, COL_G // LANES)),
                  pl.BlockSpec((MLSTM_CONV, 2 * GROUP_WIDTH), lambda bi, i: (0, 0)),
                  pl.BlockSpec((1, LANES), lambda bi, i: (0, 0)),
                  pl.BlockSpec((1, GROUP_WIDTH), lambda bi, i: (0, 0)),
                  pl.BlockSpec((GROUP_WIDTH, GROUP_WIDTH), lambda bi, i: (0, 0))],
        out_specs=pl.BlockSpec((None, L, GROUP_WIDTH), lambda bi, i: (bi, i, 0)),
        out_shape=jax.ShapeDtypeStruct((b, s, GROUP_WIDTH), F32),
        scratch_shapes=[pltpu.VMEM((CONV_HALO + L, 2 * GROUP_WIDTH), F32),
                        pltpu.VMEM((GROUP_HEADS, HEAD_DIM, AUG), F32),
                        pltpu.VMEM((1, LANES), F32)],
        compiler_params=_cparams(("arbitrary", "arbitrary")),
        name="mlstm",
    )(p3, p3, p3, p3, p3, conv_w.astype(F32), gb, out_gain.astype(F32).reshape(1, GROUP_WIDTH),
      _block_ones(GROUP_WIDTH, HEAD_DIM))


HGRN_CHUNK = 16
LB_FLOOR = 1e-30


def _hgrn_kernel(q_ref, f_ref, i_ref, g_ref, lb_ref, og_ref, bd_ref, y_ref,
                 lf_s, kk_s, vv_s, st_ref, *, TL):
    ci = pl.program_id(1)
    C = HGRN_CHUNK
    W = GROUP_WIDTH

    @pl.when(ci == 0)
    def _():
        lf_s[0:C, :] = jnp.zeros((C, W), F32)
        kk_s[0:C, :] = jnp.zeros((C, W), F32)
        vv_s[0:C, :] = jnp.zeros((C, W), F32)
        st_ref[...] = jnp.zeros_like(st_ref)

    z = f_ref[...]
    a = lb_ref[0:1, :]
    c = lb_ref[1:2, :] + _log_sigmoid(z)
    mx = jnp.maximum(a, c)
    lf = mx + jnp.log1p(jnp.exp(-jnp.abs(a - c)))
    kk = lb_ref[2:3, :] * _sigmoid(-z) + lb_ref[3:4, :]
    qx = q_ref[...]
    qs = qx * _sigmoid(qx)
    vv = i_ref[...]
    lf_s[C:C + TL, :] = lf
    kk_s[C:C + TL, :] = kk
    vv_s[C:C + TL, :] = vv

    bd = bd_ref[...]
    row = lax.broadcasted_iota(jnp.int32, (TL, W), 0)
    rmod = row & (C - 1)

    acc = jnp.zeros((TL, W), F32)
    dsum = jnp.zeros((TL, W), F32)
    for delta in range(C):
        if delta > 0:
            dsum = dsum + lf_s[C - (delta - 1):C - (delta - 1) + TL, :]
        x = qs * kk_s[C - delta:C - delta + TL, :] * jnp.exp(dsum)
        x = jnp.where(rmod >= delta, x, 0.0)
        att = jnp.dot(x.astype(BF16), bd, preferred_element_type=F32)
        acc = acc + att * vv_s[C - delta:C - delta + TL, :]

    ri = lax.broadcasted_iota(jnp.int32, (TL, TL), 0)
    cj = lax.broadcasted_iota(jnp.int32, (TL, TL), 1)
    same = (ri // C) == (cj // C)
    tril = jnp.where(same & (ri >= cj), 1.0, 0.0).astype(BF16)
    ones = jnp.where(same, 1.0, 0.0).astype(BF16)
    bcum = _tril_sum(lf, tril)
    blast = _tril_sum(lf, ones)
    q_a = (qs * jnp.exp(bcum)).astype(BF16)
    k_b = (kk * jnp.exp(blast - bcum)).astype(BF16)
    dec = jnp.exp(blast)
    vv_b = vv.astype(BF16)
    hmask = bd.astype(F32)
    state = st_ref[...]
    inters = []
    for ch in range(TL // C):
        r = slice(ch * C, (ch + 1) * C)
        inters.append(lax.dot_general(q_a[r], state.astype(BF16), (((1,), (1,)), ((), ())),
                                      preferred_element_type=F32))
        upd = lax.dot_general(vv_b[r], k_b[r], (((0,), (0,)), ((), ())), preferred_element_type=F32)
        state = state * dec[ch * C:ch * C + 1, :] + upd * hmask
    st_ref[...] = state
    o = acc + jnp.concatenate(inters, axis=0)
    ss = _seg_sum(o * o, bd)
    y_ref[...] = o * lax.rsqrt(ss * (1.0 / HEAD_DIM) + EPS) * og_ref[...] * _sigmoid(g_ref[...])


def _hgrn2(p3, lower_bound, out_gain, TL=256):
    b, s, _ = p3.shape
    lb = lower_bound.astype(F32)
    lbf = jnp.maximum(lb, LB_FLOOR)
    lbp = jnp.stack([jnp.log(lbf), jnp.log1p(-lb), 1.0 - lb, lb - lbf])
    col = lambda off: (lambda bi, i: (bi, i, off // GROUP_WIDTH))
    C = HGRN_CHUNK
    return pl.pallas_call(
        functools.partial(_hgrn_kernel, TL=TL),
        grid=(b, s // TL),
        in_specs=[pl.BlockSpec((None, TL, GROUP_WIDTH), col(COL_C)),
                  pl.BlockSpec((None, TL, GROUP_WIDTH), col(COL_C + 256)),
                  pl.BlockSpec((None, TL, GROUP_WIDTH), col(COL_C + 512)),
                  pl.BlockSpec((None, TL, GROUP_WIDTH), col(COL_C + 768)),
                  pl.BlockSpec((4, GROUP_WIDTH), lambda bi, i: (0, 0)),
                  pl.BlockSpec((1, GROUP_WIDTH), lambda bi, i: (0, 0)),
                  pl.BlockSpec((GROUP_WIDTH, GROUP_WIDTH), lambda bi, i: (0, 0))],
        out_specs=pl.BlockSpec((None, TL, GROUP_WIDTH), lambda bi, i: (bi, i, 0)),
        out_shape=jax.ShapeDtypeStruct((b, s, GROUP_WIDTH), F32),
        scratch_shapes=[pltpu.VMEM((C + TL, GROUP_WIDTH), F32),
                        pltpu.VMEM((C + TL, GROUP_WIDTH), F32),
                        pltpu.VMEM((C + TL, GROUP_WIDTH), F32),
                        pltpu.VMEM((GROUP_WIDTH, GROUP_WIDTH), F32)],
        compiler_params=_cparams(("arbitrary", "arbitrary")),
        name="hgrn2",
    )(p3, p3, p3, p3, lbp, out_gain.astype(F32).reshape(1, GROUP_WIDTH), _block_ones(GROUP_WIDTH, HEAD_DIM))


CMP_LEN = 32
CMP_STRIDE = 16
SEL_BLOCK = 64
SEL_TOPK = 16
WINDOW = 512
M_INIT = -1e30
MASKED = -2e30
GATE_ROW = 8


def _rms_rows(x, gain):
    return x * lax.rsqrt(jnp.mean(x * x, axis=-1, keepdims=True) + EPS) * gain


def _nsa_prep_kernel(q_ref, kv_ref, g_ref, qg_ref, kg_ref, bd_ref,
                     qt_ref, kc_ref, vc_ref, ks_ref, vst_ref, kw_ref, vwt_ref, gt_ref):
    q = q_ref[...]
    ss = _seg_sum(q * q, bd_ref[...])
    qn = q * lax.rsqrt(ss * (1.0 / HEAD_DIM) + EPS) * qg_ref[...] * (HEAD_DIM ** -0.5 * LOG2E)
    qt_ref[...] = qn.T.astype(BF16)
    kv = kv_ref[...]
    kc_ref[...] = kv[:, 0:64]
    vc_ref[...] = kv[:, 64:128]
    ks_ref[...] = _rms_rows(kv[:, 128:192], kg_ref[1:2, :]).astype(BF16)
    kw_ref[...] = _rms_rows(kv[:, 256:320], kg_ref[2:3, :]).astype(BF16)
    kvt = kv.T
    vst_ref[...] = kvt[192:256, :].astype(BF16)
    vwt_ref[...] = kvt[320:384, :].astype(BF16)
    gt_ref[...] = _sigmoid(g_ref[...]).T


def _gelu_tanh(x):
    return 0.5 * x * (1.0 + jnp.tanh(math.sqrt(2.0 / math.pi) * (x + 0.044715 * x * x * x)))


def _nsa_cmp_kernel(kr_ref, vr_ref, w1_ref, w2_ref, pos_ref, kg_ref, kc_ref, vct_ref, sh_ref, *, n_rows):
    half = CMP_STRIDE * HEAD_DIM
    sh_ref[n_rows:n_rows + 8, :] = jnp.zeros((8, HEAD_DIM), F32)
    outs = []
    for j, x_ref in enumerate((kr_ref, vr_ref)):
        r = x_ref[...].astype(BF16)
        w1 = w1_ref[j]
        first = jnp.dot(r, w1[0:half, :], preferred_element_type=F32)
        sh_ref[0:n_rows, :] = jnp.dot(r, w1[half:2 * half, :], preferred_element_type=F32)
        pos8 = jnp.broadcast_to(pos_ref[j], (8, 2 * half))
        posw = jnp.dot(pos8, w1.astype(F32), preferred_element_type=F32)[0:1, :]
        hdn = _gelu_tanh(first + sh_ref[1:n_rows + 1, :] + posw)
        outs.append(jnp.dot(hdn, w2_ref[j].astype(F32), preferred_element_type=F32))
    kc_ref[...] = _rms_rows(outs[0], kg_ref[0:1, :]).astype(BF16)
    vct_ref[...] = outs[1].T.astype(BF16)


def _nsa_attn_kernel(qt_ref, kc_ref, vct_ref, ks_ref, vst_ref, kw_ref, vwt_ref, gt_ref, ov_ref, og_ref, bd_ref,
                     y_ref, sel_ref, *, tq, slopes, n_cmp):
    qi = pl.program_id(1)
    n_rows = kc_ref.shape[0]
    n_sel = ov_ref.shape[0]
    qpos = qi * tq + lax.broadcasted_iota(jnp.int32, (1, tq), 1)

    nio = lax.broadcasted_iota(jnp.int32, (n_rows, tq), 0)
    dist_c = qpos - (nio * CMP_STRIDE + (CMP_LEN - 1))
    valid_c = (dist_c >= 0) & (nio < n_cmp)
    dist_cf = dist_c.astype(F32)
    kc = kc_ref[...]
    vct = vct_ref[...]
    o_cmp = []
    p_sum = jnp.zeros((n_rows, tq), F32)
    for h in range(GROUP_HEADS):
        qh = qt_ref[HEAD_DIM * h:HEAD_DIM * (h + 1), :]
        s = jnp.dot(kc, qh, preferred_element_type=F32) - (slopes[h] * LOG2E) * dist_cf
        s = jnp.where(valid_c, s, MASKED)
        m = jnp.maximum(jnp.max(s, axis=0, keepdims=True), M_INIT)
        e = jnp.exp2(s - m)
        p = e / jnp.maximum(jnp.sum(e, axis=0, keepdims=True), 1e-30)
        p_sum = p_sum + p
        o_cmp.append(jnp.dot(vct, p.astype(BF16), preferred_element_type=F32))

    p_hi = p_sum.astype(BF16)
    p_lo = (p_sum - p_hi.astype(F32)).astype(BF16)
    ov = ov_ref[...]
    imp = jnp.dot(ov, p_hi, preferred_element_type=F32) + jnp.dot(ov, p_lo, preferred_element_type=F32)
    jio = lax.broadcasted_iota(jnp.int32, (n_sel, tq), 0)
    qblk = jnp.right_shift(qpos, SEL_BLOCK.bit_length() - 1)
    imp = jnp.where(jio <= qblk, imp, -1.0)
    imp = jnp.where((jio == 0) | (jio == qblk) | (jio == qblk - 1), 2.0, imp)
    rank = jnp.zeros((n_sel, tq), F32)
    for i in range(n_sel):
        row = imp[i:i + 1, :]
        beats = (row > imp) | ((row == imp) & (jio > i))
        rank = rank + jnp.where(beats, 1.0, 0.0)
    sel_ref[...] = jnp.where(rank < float(min(SEL_TOPK, n_sel)), 1.0, 0.0)

    kio = lax.broadcasted_iota(jnp.int32, (tq, tq), 0)
    qio = lax.broadcasted_iota(jnp.int32, (tq, tq), 1)
    rel = (kio - qio).astype(F32)
    causal = kio <= qio
    blocks_per_tile = tq // SEL_BLOCK

    def flash_step(kt, carry, qh, sl2, k_ref, vt_ref, mask_fn):
        m, l, acc = carry
        start = pl.multiple_of(kt * tq, tq)
        kb = k_ref[pl.ds(start, tq), :]
        s = jnp.dot(kb, qh, preferred_element_type=F32)
        t = (s + (kt - qi).astype(F32) * (tq * sl2)) + rel * sl2
        t = jnp.where(mask_fn(kt), t, MASKED)
        m_new = jnp.maximum(m, jnp.max(t, axis=0, keepdims=True))
        alpha = jnp.exp2(m - m_new)
        p = jnp.exp2(t - m_new)
        l_new = alpha * l + jnp.sum(p, axis=0, keepdims=True)
        pv = jnp.dot(vt_ref[:, pl.ds(start, tq)], p.astype(BF16), preferred_element_type=F32)
        return m_new, l_new, alpha * acc + pv

    def sel_mask(kt, diag):
        rows = [jnp.broadcast_to(sel_ref[pl.ds(kt * blocks_per_tile + i, 1), :], (SEL_BLOCK, tq))
                for i in range(blocks_per_tile)]
        msk = jnp.concatenate(rows, axis=0) > 0.5
        return (msk & causal) if diag else msk

    gt = gt_ref[...]
    heads = []
    for h in range(GROUP_HEADS):
        qh = qt_ref[HEAD_DIM * h:HEAD_DIM * (h + 1), :]
        sl2 = slopes[h] * LOG2E
        init = (jnp.full((1, tq), M_INIT, F32), jnp.zeros((1, tq), F32), jnp.zeros((HEAD_DIM, tq), F32))
        step = functools.partial(flash_step, qh=qh, sl2=sl2, k_ref=ks_ref, vt_ref=vst_ref)
        carry = lax.fori_loop(0, qi, functools.partial(step, mask_fn=lambda kt: sel_mask(kt, False)), init)
        m, l, acc = step(qi, carry, mask_fn=lambda kt: sel_mask(kt, True))
        o_sel = acc / l

        wstep = functools.partial(flash_step, qh=qh, sl2=sl2, k_ref=kw_ref, vt_ref=vwt_ref)
        carry = init
        for back in (2, 1, 0):
            kt = qi - back
            ok = kt >= 0
            if back == 2:
                mask_fn = lambda kt_, ok=ok: (kio > qio) & ok
            elif back == 1:
                mask_fn = lambda kt_, ok=ok: jnp.broadcast_to(ok, (tq, tq))
            else:
                mask_fn = lambda kt_: causal
            carry = wstep(jnp.maximum(kt, 0), carry, mask_fn=mask_fn)
        m, l, acc = carry
        o_win = acc / l

        g = GATE_ROW + 3 * h
        heads.append(gt[g:g + 1, :] * o_cmp[h] + gt[g + 1:g + 2, :] * o_sel + gt[g + 2:g + 3, :] * o_win)
    o = jnp.concatenate(heads, axis=0).T
    ss = _seg_sum(o * o, bd_ref[...])
    y_ref[...] = o * lax.rsqrt(ss * (1.0 / HEAD_DIM) + EPS) * og_ref[...]


def _nsa(p3, cmp_pos, cmp_w1, cmp_w2, q_gain, k_gain, out_gain, slopes, tq=256):
    b, s, _ = p3.shape
    tm = tq
    n_rows = s // CMP_STRIDE
    n_cmp = (s - CMP_LEN) // CMP_STRIDE + 1
    n_sel = s // SEL_BLOCK
    bd64 = _block_ones(GROUP_WIDTH, HEAD_DIM)
    qg = jnp.tile(q_gain.astype(F32), GROUP_HEADS).reshape(1, GROUP_WIDTH)
    kg = k_gain.astype(F32)
    tok = lambda w: pl.BlockSpec((None, tm, w), lambda bi, i: (bi, i, 0))
    tok_t = lambda w: pl.BlockSpec((None, w, tm), lambda bi, i: (bi, 0, i))
    full = lambda r, c: pl.BlockSpec((r, c), lambda bi, i: (0, 0))
    qt, kc, vc, ks, vst, kw, vwt, gt = pl.pallas_call(
        _nsa_prep_kernel,
        grid=(b, s // tm),
        in_specs=[pl.BlockSpec((None, tm, GROUP_WIDTH), lambda bi, i: (bi, i, COL_D // GROUP_WIDTH)),
                  pl.BlockSpec((None, tm, 384), lambda bi, i: (bi, i, (COL_D + GROUP_WIDTH) // 384)),
                  pl.BlockSpec((None, tm, LANES), lambda bi, i: (bi, i, COL_G // LANES)),
                  full(1, GROUP_WIDTH), full(3, HEAD_DIM), full(GROUP_WIDTH, GROUP_WIDTH)],
        out_specs=[tok_t(GROUP_WIDTH), tok(HEAD_DIM), tok(HEAD_DIM), tok(HEAD_DIM), tok_t(HEAD_DIM),
                   tok(HEAD_DIM), tok_t(HEAD_DIM), tok_t(LANES)],
        out_shape=[jax.ShapeDtypeStruct((b, GROUP_WIDTH, s), BF16),
                   jax.ShapeDtypeStruct((b, s, HEAD_DIM), F32),
                   jax.ShapeDtypeStruct((b, s, HEAD_DIM), F32),
                   jax.ShapeDtypeStruct((b, s, HEAD_DIM), BF16),
                   jax.ShapeDtypeStruct((b, HEAD_DIM, s), BF16),
                   jax.ShapeDtypeStruct((b, s, HEAD_DIM), BF16),
                   jax.ShapeDtypeStruct((b, HEAD_DIM, s), BF16),
                   jax.ShapeDtypeStruct((b, LANES, s), F32)],
        compiler_params=_cparams(("arbitrary", "arbitrary")),
        name="nsa_prep",
    )(p3, p3, p3, qg, kg, bd64)

    row_w = CMP_STRIDE * HEAD_DIM
    kcmp, vcmp_t = pl.pallas_call(
        functools.partial(_nsa_cmp_kernel, n_rows=n_rows),
        grid=(b,),
        in_specs=[pl.BlockSpec((None, n_rows, row_w), lambda bi: (bi, 0, 0)),
                  pl.BlockSpec((None, n_rows, row_w), lambda bi: (bi, 0, 0)),
                  pl.BlockSpec((2, 2 * row_w, HEAD_DIM), lambda bi: (0, 0, 0)),
                  pl.BlockSpec((2, HEAD_DIM, HEAD_DIM), lambda bi: (0, 0, 0)),
                  pl.BlockSpec((2, 1, 2 * row_w), lambda bi: (0, 0, 0)),
                  pl.BlockSpec((3, HEAD_DIM), lambda bi: (0, 0))],
        out_specs=[pl.BlockSpec((None, n_rows, HEAD_DIM), lambda bi: (bi, 0, 0)),
                   pl.BlockSpec((None, HEAD_DIM, n_rows), lambda bi: (bi, 0, 0))],
        out_shape=[jax.ShapeDtypeStruct((b, n_rows, HEAD_DIM), BF16),
                   jax.ShapeDtypeStruct((b, HEAD_DIM, n_rows), BF16)],
        scratch_shapes=[pltpu.VMEM((n_rows + 8, HEAD_DIM), F32)],
        compiler_params=_cparams(("arbitrary",)),
        name="nsa_cmp",
    )(kc.reshape(b, n_rows, row_w), vc.reshape(b, n_rows, row_w), cmp_w1.astype(BF16), cmp_w2.astype(BF16),
      cmp_pos.astype(F32).reshape(2, 1, 2 * row_w), kg)

    cs = np.arange(n_rows)[:, None] * CMP_STRIDE
    ss = np.arange(n_sel)[None, :] * SEL_BLOCK
    overlap = np.clip(np.minimum(cs + CMP_LEN, ss + SEL_BLOCK) - np.maximum(cs, ss), 0, None) / CMP_LEN
    overlap[n_cmp:, :] = 0.0
    ov_t = jnp.asarray(overlap.T.astype(np.float32), BF16)

    seq = lambda r, c: pl.BlockSpec((None, r, c), lambda bi, i: (bi, 0, 0))
    kern = functools.partial(_nsa_attn_kernel, tq=tq, slopes=tuple(float(v) for v in slopes), n_cmp=n_cmp)
    return pl.pallas_call(
        kern,
        grid=(b, s // tq),
        in_specs=[pl.BlockSpec((None, GROUP_WIDTH, tq), lambda bi, i: (bi, 0, i)),
                  seq(n_rows, HEAD_DIM), seq(HEAD_DIM, n_rows),
                  seq(s, HEAD_DIM), seq(HEAD_DIM, s), seq(s, HEAD_DIM), seq(HEAD_DIM, s),
                  pl.BlockSpec((None, LANES, tq), lambda bi, i: (bi, 0, i)),
                  full(n_sel, n_rows), full(1, GROUP_WIDTH), full(GROUP_WIDTH, GROUP_WIDTH)],
        out_specs=pl.BlockSpec((None, tq, GROUP_WIDTH), lambda bi, i: (bi, i, 0)),
        out_shape=jax.ShapeDtypeStruct((b, s, GROUP_WIDTH), F32),
        scratch_shapes=[pltpu.VMEM((n_sel, tq), F32)],
        compiler_params=_cparams(("arbitrary", "arbitrary")),
        name="nsa_attn",
    )(qt, kcmp, vcmp_t, ks, vst, kw, vwt, gt, ov_t, out_gain.astype(F32).reshape(1, GROUP_WIDTH), bd64)


N_GROUPS = 4
EXPERTS_PER_GROUP = 8
N_EXPERTS = N_GROUPS * EXPERTS_PER_GROUP
D_EXPERT = 256
ROUTER_LANE0 = N_GROUPS


def _split3_dot(a, b_hi, b_lo):
    a_hi = a.astype(BF16)
    a_lo = (a - a_hi.astype(F32)).astype(BF16)
    return (jnp.dot(a_hi, b_hi, preferred_element_type=F32) + jnp.dot(a_lo, b_hi, preferred_element_type=F32)
            + jnp.dot(a_hi, b_lo, preferred_element_type=F32))


def _outproj_kernel(x_ref, ya_ref, yb_ref, yc_ref, yd_ref, w_ref, g_ref, wr_hi_ref, wr_lo_ref, br_ref,
                    xo_ref, t_ref, gate_ref):
    acc = x_ref[...]
    for gi, y_ref in enumerate((ya_ref, yb_ref, yc_ref, yd_ref)):
        acc = acc + jnp.dot(y_ref[...].astype(BF16), w_ref[GROUP_WIDTH * gi:GROUP_WIDTH * (gi + 1), :],
                            preferred_element_type=F32)
    xo_ref[...] = acc
    ms = jnp.mean(acc * acc, axis=-1, keepdims=True)
    t = acc * lax.rsqrt(ms + EPS) * g_ref[...]
    t_ref[...] = t.astype(BF16)

    logits = _split3_dot(t, wr_hi_ref[...], wr_lo_ref[...]) + br_ref[...]
    lane = lax.broadcasted_iota(jnp.int32, logits.shape, 1)
    lane_f = lane.astype(F32)
    big = float(LANES)
    is_g = lane < N_GROUPS
    gl = jnp.where(is_g, logits, MASKED)
    gmax = jnp.max(gl, axis=-1, keepdims=True)
    g_prob = 1.0 / jnp.sum(jnp.where(is_g, jnp.exp(gl - gmax), 0.0), axis=-1, keepdims=True)
    g_sel = jnp.min(jnp.where(is_g & (gl == gmax), lane_f, big), axis=-1, keepdims=True)
    lo = ROUTER_LANE0 + EXPERTS_PER_GROUP * g_sel
    in_grp = (lane_f >= lo) & (lane_f < lo + EXPERTS_PER_GROUP)
    el = jnp.where(in_grp, logits, MASKED)
    m1 = jnp.max(el, axis=-1, keepdims=True)
    i1 = jnp.min(jnp.where(in_grp & (el == m1), lane_f, big), axis=-1, keepdims=True)
    rest = in_grp & (lane_f != i1)
    el2 = jnp.where(rest, logits, MASKED)
    m2 = jnp.max(el2, axis=-1, keepdims=True)
    i2 = jnp.min(jnp.where(rest & (el2 == m2), lane_f, big), axis=-1, keepdims=True)
    r = jnp.exp(m2 - m1)
    w1 = g_prob / (1.0 + r)
    w2 = g_prob * r / (1.0 + r)
    gate_ref[...] = jnp.where(lane_f == i1, w1, 0.0) + jnp.where(lane_f == i2, w2, 0.0)


def _outproj(x2d, ys, w_out, ffn_gain, w_group, b_group, w_expert, b_expert, tm=256):
    t = x2d.shape[0]
    wr = jnp.zeros((D_MODEL, LANES), F32).at[:, 0:N_GROUPS].set(w_group).at[:, ROUTER_LANE0:ROUTER_LANE0 + N_EXPERTS].set(w_expert)
    wr_hi = wr.astype(BF16)
    wr_lo = (wr - wr_hi.astype(F32)).astype(BF16)
    br = jnp.zeros((1, LANES), F32).at[0, 0:N_GROUPS].set(b_group).at[0, ROUTER_LANE0:ROUTER_LANE0 + N_EXPERTS].set(b_expert)
    row = lambda w: pl.BlockSpec((tm, w), lambda i: (i, 0))
    full = lambda r, c: pl.BlockSpec((r, c), lambda i: (0, 0))
    return pl.pallas_call(
        _outproj_kernel,
        grid=(t // tm,),
        in_specs=[row(D_MODEL), row(GROUP_WIDTH), row(GROUP_WIDTH), row(GROUP_WIDTH), row(GROUP_WIDTH),
                  full(D_MODEL, D_MODEL), full(1, D_MODEL), full(D_MODEL, LANES), full(D_MODEL, LANES), full(1, LANES)],
        out_specs=[row(D_MODEL), row(D_MODEL), row(LANES)],
        out_shape=[jax.ShapeDtypeStruct((t, D_MODEL), F32),
                   jax.ShapeDtypeStruct((t, D_MODEL), BF16),
                   jax.ShapeDtypeStruct((t, LANES), F32)],
        compiler_params=_cparams(("arbitrary",)),
        name="outproj_router",
    )(x2d, *ys, w_out.astype(BF16), ffn_gain.reshape(1, D_MODEL), wr_hi, wr_lo, br)


def _moe_dense_kernel(x_ref, t_ref, gate_ref, wg_ref, wu_ref, wd_ref, o_ref, acc_ref):
    e = pl.program_id(1)

    @pl.when(e == 0)
    def _():
        acc_ref[...] = jnp.zeros_like(acc_ref)

    t = t_ref[...]
    a = jnp.dot(t, wg_ref[...], preferred_element_type=F32)
    u = jnp.dot(t, wu_ref[...], preferred_element_type=F32)
    gate = gate_ref[...]
    lane = lax.broadcasted_iota(jnp.int32, gate.shape, 1)
    w = jnp.sum(jnp.where(lane == e + ROUTER_LANE0, gate, 0.0), axis=-1, keepdims=True)
    act = a * _sigmoid(a) * u * w
    acc_ref[...] += jnp.dot(act.astype(BF16), wd_ref[...], preferred_element_type=F32)

    @pl.when(e == N_EXPERTS - 1)
    def _():
        o_ref[...] = x_ref[...] + acc_ref[...]


def _moe_dense(x2d, t_bf, gate, w_gate, w_up, w_down, tm=1024):
    t = x2d.shape[0]
    wg = w_gate.reshape(N_EXPERTS, D_MODEL, D_EXPERT).astype(BF16)
    wu = w_up.reshape(N_EXPERTS, D_MODEL, D_EXPERT).astype(BF16)
    wd = w_down.reshape(N_EXPERTS, D_EXPERT, D_MODEL).astype(BF16)
    return pl.pallas_call(
        _moe_dense_kernel,
        grid=(t // tm, N_EXPERTS),
        in_specs=[pl.BlockSpec((tm, D_MODEL), lambda i, e: (i, 0)),
                  pl.BlockSpec((tm, D_MODEL), lambda i, e: (i, 0)),
                  pl.BlockSpec((tm, LANES), lambda i, e: (i, 0)),
                  pl.BlockSpec((None, D_MODEL, D_EXPERT), lambda i, e: (e, 0, 0)),
                  pl.BlockSpec((None, D_MODEL, D_EXPERT), lambda i, e: (e, 0, 0)),
                  pl.BlockSpec((None, D_EXPERT, D_MODEL), lambda i, e: (e, 0, 0))],
        out_specs=pl.BlockSpec((tm, D_MODEL), lambda i, e: (i, 0)),
        out_shape=jax.ShapeDtypeStruct((t, D_MODEL), F32),
        scratch_shapes=[pltpu.VMEM((tm, D_MODEL), F32)],
        compiler_params=_cparams(("arbitrary", "arbitrary")),
        name="moe_dense",
    )(x2d, t_bf, gate, wg, wu, wd)


def _alibi_slopes():
    n = 2 * GROUP_HEADS
    s = 2.0 ** (-8.0 * np.arange(1, n + 1) / n)
    return s[0::2], s[1::2]


def kernel(x, norm_mix, norm_ffn, w_in, w_out, diff_q_gain, diff_k_gain, diff_lambda, diff_sub_gain, mlstm_conv, mlstm_gate_bias, mlstm_out_gain, hgrn_lower_bounds, hgrn_out_gain, nsa_cmp_pos, nsa_cmp_w1, nsa_cmp_w2, nsa_q_gain, nsa_k_gain, nsa_out_gain, moe_w_group, moe_b_group, moe_w_expert, moe_b_expert, moe_w_gate, moe_w_up, moe_w_down):
    b, s, d = x.shape
    slopes_diff, slopes_nsa = _alibi_slopes()
    lb_soft = jax.nn.softmax(hgrn_lower_bounds.astype(F32), axis=0)
    lower_bounds = jnp.cumsum(lb_soft, axis=0) - lb_soft[0]
    x2d = x.reshape(b * s, d)
    for l in range(norm_mix.shape[0]):
        p = _inproj(x2d, norm_mix[l], _pack_w_in(w_in[l])).reshape(b, s, P_COLS)
        y_a = _diff_attention(p, diff_q_gain[l], diff_k_gain[l], diff_lambda[l], diff_sub_gain[l], slopes_diff, l)
        y_b = _mlstm(p, mlstm_conv[l], mlstm_gate_bias[l], mlstm_out_gain[l])
        y_c = _hgrn2(p, lower_bounds[l], hgrn_out_gain[l])
        y_d = _nsa(p, nsa_cmp_pos[l], nsa_cmp_w1[l], nsa_cmp_w2[l], nsa_q_gain[l], nsa_k_gain[l],
                   nsa_out_gain[l], slopes_nsa)
        ys = [y.reshape(b * s, GROUP_WIDTH) for y in (y_a, y_b, y_c, y_d)]
        x2d, t_bf, gate = _outproj(x2d, ys, w_out[l], norm_ffn[l], moe_w_group[l], moe_b_group[l],
                                   moe_w_expert[l], moe_b_expert[l])
        x2d = _moe_dense(x2d, t_bf, gate, moe_w_gate[l], moe_w_up[l], moe_w_down[l])
    return x2d.reshape(b, s, d)
```

```python
import functools
import math

import numpy as np
import jax
import jax.numpy as jnp
from jax import lax
from jax.experimental import pallas as pl
from jax.experimental.pallas import tpu as pltpu

F32 = jnp.float32
BF16 = jnp.bfloat16

D_MODEL = 1024
HEAD_DIM = 64
GROUP_HEADS = 4
GROUP_WIDTH = GROUP_HEADS * HEAD_DIM
DIFF_HALF = HEAD_DIM // 2
EPS = 1e-6
NEG_INF = -1e30
LOG2E = math.log2(math.e)
M_INIT = -1e30
MASKED = -2e30

LANES = 128
VMEM_LIMIT = 48 * 1024 * 1024

COL_A = 0
COL_B = 768
COL_C = 1792
COL_D = 2816
COL_G = 3456
P_COLS = 3584


def _cparams(sem, flags=None):
    return pltpu.CompilerParams(dimension_semantics=sem, vmem_limit_bytes=VMEM_LIMIT, flags=flags)


def _block_ones(width, seg):
    i = np.arange(width)
    return jnp.asarray((i[:, None] // seg == i[None, :] // seg).astype(np.float32), BF16)


def _seg_sum(x, bd):
    hi = x.astype(BF16)
    lo = (x - hi.astype(F32)).astype(BF16)
    return (jnp.dot(hi, bd, preferred_element_type=F32) + jnp.dot(lo, bd, preferred_element_type=F32))


def _inproj_kernel(x_ref, g_ref, w_ref, o_ref):
    x = x_ref[...]
    ms = jnp.mean(x * x, axis=-1, keepdims=True)
    h = (x * lax.rsqrt(ms + EPS) * g_ref[...]).astype(BF16)
    o_ref[...] = jnp.dot(h, w_ref[...], preferred_element_type=F32)


def _inproj(x2d, gain, w_cat, tm=256):
    t = x2d.shape[0]
    return pl.pallas_call(
        _inproj_kernel,
        grid=(t // tm,),
        in_specs=[pl.BlockSpec((tm, D_MODEL), lambda i: (i, 0)),
                  pl.BlockSpec((1, D_MODEL), lambda i: (0, 0)),
                  pl.BlockSpec((D_MODEL, P_COLS), lambda i: (0, 0))],
        out_specs=pl.BlockSpec((tm, P_COLS), lambda i: (i, 0)),
        out_shape=jax.ShapeDtypeStruct((t, P_COLS), F32),
        compiler_params=_cparams(("arbitrary",)),
        name="inproj",
    )(x2d, gain.reshape(1, D_MODEL), w_cat)


IN_COLS = 3476


def _pack_w_in_kernel(w_ref, o_ref):
    w = w_ref[...]
    a_b = w[:, 0:1536]
    gates_b = w[:, 1536:1544]
    rest = w[:, 1544:3464]
    gates_d = w[:, 3464:3476]
    pad = jnp.zeros((w.shape[0], P_COLS - COL_G - 20), F32)
    o_ref[...] = jnp.concatenate([a_b, rest, gates_b, gates_d, pad], axis=1).astype(BF16)


def _pack_w_in(w, tr=128):
    d = w.shape[0]
    return pl.pallas_call(
        _pack_w_in_kernel,
        grid=(d // tr,),
        in_specs=[pl.BlockSpec((tr, IN_COLS), lambda i: (i, 0))],
        out_specs=pl.BlockSpec((tr, P_COLS), lambda i: (i, 0)),
        out_shape=jax.ShapeDtypeStruct((d, P_COLS), BF16),
        compiler_params=_cparams(("arbitrary",)),
        name="pack_w_in",
    )(w)


def _diff_prep_kernel(p_ref, qg_ref, kg_ref, bd_ref, w_ref, wt_ref, q_ref, k_ref, vt_ref):
    p = p_ref[...]
    bd = bd_ref[...]

    def norm(x, g):
        ss = _seg_sum(x * x, bd)
        return x * lax.rsqrt(ss * (1.0 / DIFF_HALF) + EPS) * g

    q = norm(p[:, 0:256], qg_ref[...]) * (DIFF_HALF ** -0.5 * LOG2E)
    k = norm(p[:, 256:512], kg_ref[...])
    q_ref[...] = q.T.astype(BF16)
    k_ref[...] = k.astype(BF16)
    v = p[:, 512:768]
    v_t = v.T
    vw_t = (v * w_ref[...]).T
    w_t = wt_ref[...]
    tm = v_t.shape[1]
    first = lax.broadcasted_iota(jnp.int32, (V_AUG - HEAD_DIM, tm), 0) == 0
    blocks = []
    for h in range(GROUP_HEADS):
        hs = slice(HEAD_DIM * h, HEAD_DIM * (h + 1))
        blocks += [vw_t[hs, :], jnp.where(first, jnp.broadcast_to(w_t[h:h + 1, :], first.shape), 0.0),
                   v_t[hs, :], jnp.where(first, 1.0, 0.0)]
    vt_ref[...] = jnp.concatenate(blocks, axis=0).astype(BF16)


V_AUG = 80
V_HEAD = 2 * V_AUG


def _key_weights(tile, slopes):
    kl = np.arange(tile, dtype=np.float64) - (tile - 1)
    w = np.stack([np.exp2(sl * LOG2E * kl) for sl in slopes])
    wide = np.repeat(w.T, HEAD_DIM, axis=1)
    w8 = np.zeros((8, tile)); w8[:len(slopes)] = w
    return jnp.asarray(wide.astype(np.float32)), jnp.asarray(w8.astype(np.float32))


def _diag_bias(tq, slopes):
    k = np.arange(tq)[:, None]
    q = np.arange(tq)[None, :]
    tabs = [np.where(k <= q, sl * LOG2E * k.astype(np.float64), MASKED) for sl in slopes]
    return jnp.asarray(np.stack(tabs).astype(np.float32))


AHEAD = 2


def _flash_update(s_ref, bias_ref, off, vt, m_ref, acc_ref, penalty=None):
    tq = s_ref.shape[1]
    for c in range(tq // LANES):
        cols = slice(LANES * c, LANES * (c + 1))
        s = s_ref[:, cols]
        if bias_ref is not None:
            s = s + bias_ref[:, cols]
        if penalty is not None:
            s = s + penalty
        m_tile = jnp.max(s, axis=0, keepdims=True)
        p = jnp.exp2(s - m_tile).astype(BF16)
        m_old = m_ref[0:1, cols]
        m_new = jnp.maximum(m_old, m_tile + off)
        m_ref[0:1, cols] = m_new
        pv = jnp.dot(vt, p, preferred_element_type=F32)
        acc_ref[:, cols] = jnp.exp2(m_old - m_new) * acc_ref[:, cols] + jnp.exp2(m_tile + off - m_new) * pv


def _diff_attn_kernel(lam_ref, qt_ref, k_ref, vt_ref, db_ref, sg_ref, bd_ref, o_ref,
                      qm_ref, acc_ref, m_ref, s_ref, *, tq, slopes, out_scale):
    qi = pl.program_id(1)
    n_half = 2 * GROUP_HEADS
    per_tile = LANES // DIFF_HALF
    row = lax.broadcasted_iota(jnp.int32, (LANES, tq), 0)
    for j in range(n_half):
        slab = qt_ref[LANES * (j // per_tile):LANES * (j // per_tile + 1), :].astype(F32)
        r0 = DIFF_HALF * (j % per_tile)
        qm_ref[j] = jnp.where((row >= r0) & (row < r0 + DIFF_HALF), slab, 0.0).astype(BF16)
    acc_ref[...] = jnp.zeros(acc_ref.shape, F32)
    m_ref[...] = jnp.full(m_ref.shape, M_INIT, F32)

    n_slots = s_ref.shape[0]

    def scores(kt, j):
        start = pl.multiple_of(kt * tq, tq)
        kj = k_ref[pl.ds(start, tq), LANES * (j // per_tile):LANES * (j // per_tile + 1)]
        s_ref[j % n_slots] = jnp.dot(kj, qm_ref[j], preferred_element_type=F32)

    def process(kt, j, diag):
        h = j // 2
        start = pl.multiple_of(kt * tq, tq)
        off = 0.0 if diag else ((kt - qi) * tq + (tq - 1)).astype(F32) * (slopes[h] * LOG2E)
        base = V_HEAD * h + (V_AUG if diag else 0)
        vt = vt_ref[base:base + V_AUG, pl.ds(start, tq)]
        _flash_update(s_ref.at[j % n_slots], db_ref.at[h] if diag else None, off, vt, m_ref.at[j], acc_ref.at[j])

    def step(kt, diag):
        for j in range(n_half):
            if j + AHEAD < n_half:
                scores(kt, j + AHEAD)
            elif not diag:
                scores(kt + 1, j + AHEAD - n_half)
            process(kt, j, diag)

    for j in range(AHEAD):
        scores(0, j)

    def body(kt, carry):
        step(kt, False)
        return carry

    lax.fori_loop(0, qi, body, 0)
    step(qi, True)

    lam = lam_ref[0, 0]
    heads = []
    for h in range(GROUP_HEADS):
        a0, a1 = acc_ref[2 * h], acc_ref[2 * h + 1]
        o0 = a0[0:HEAD_DIM, :] / a0[HEAD_DIM:HEAD_DIM + 1, :]
        o1 = a1[0:HEAD_DIM, :] / a1[HEAD_DIM:HEAD_DIM + 1, :]
        heads.append(o0 - lam * o1)
    o = jnp.concatenate(heads, axis=0).T
    ss = _seg_sum(o * o, bd_ref[...])
    o_ref[...] = o * lax.rsqrt(ss * (1.0 / HEAD_DIM) + EPS) * (sg_ref[...] * out_scale)


def _diff_attention(p3, q_gain, k_gain, lam_vecs, sub_gain, slopes, layer_idx, tq=256):
    b, s, _ = p3.shape
    tm = tq
    bd32 = _block_ones(GROUP_WIDTH, DIFF_HALF)
    bd64 = _block_ones(GROUP_WIDTH, HEAD_DIM)
    qg = jnp.tile(q_gain.astype(F32), 2 * GROUP_HEADS).reshape(1, GROUP_WIDTH)
    kg = jnp.tile(k_gain.astype(F32), 2 * GROUP_HEADS).reshape(1, GROUP_WIDTH)
    w_wide, w_t = _key_weights(tm, slopes)
    v_rows = GROUP_HEADS * V_HEAD
    qn, kn, vt = pl.pallas_call(
        _diff_prep_kernel,
        grid=(b, s // tm),
        in_specs=[pl.BlockSpec((None, tm, 768), lambda bi, i: (bi, i, COL_A // 768)),
                  pl.BlockSpec((1, GROUP_WIDTH), lambda bi, i: (0, 0)),
                  pl.BlockSpec((1, GROUP_WIDTH), lambda bi, i: (0, 0)),
                  pl.BlockSpec((GROUP_WIDTH, GROUP_WIDTH), lambda bi, i: (0, 0)),
                  pl.BlockSpec((tm, GROUP_WIDTH), lambda bi, i: (0, 0)),
                  pl.BlockSpec((8, tm), lambda bi, i: (0, 0))],
        out_specs=[pl.BlockSpec((None, GROUP_WIDTH, tm), lambda bi, i: (bi, 0, i)),
                   pl.BlockSpec((None, tm, GROUP_WIDTH), lambda bi, i: (bi, i, 0)),
                   pl.BlockSpec((None, v_rows, tm), lambda bi, i: (bi, 0, i))],
        out_shape=[jax.ShapeDtypeStruct((b, GROUP_WIDTH, s), BF16),
                   jax.ShapeDtypeStruct((b, s, GROUP_WIDTH), BF16),
                   jax.ShapeDtypeStruct((b, v_rows, s), BF16)],
        compiler_params=_cparams(("arbitrary", "arbitrary")),
        name="diff_prep",
    )(p3, qg, kg, bd32, w_wide, w_t)

    lam_init = 0.8 - 0.6 * math.exp(-0.3 * layer_idx)
    lv = lam_vecs.astype(F32)
    lam = (jnp.exp(jnp.dot(lv[0], lv[1])) - jnp.exp(jnp.dot(lv[2], lv[3])) + lam_init).reshape(1, 1)
    kern = functools.partial(_diff_attn_kernel, tq=tq, slopes=tuple(float(v) for v in slopes),
                             out_scale=1.0 - lam_init)
    return pl.pallas_call(
        kern,
        grid=(b, s // tq),
        in_specs=[pl.BlockSpec(memory_space=pltpu.SMEM),
                  pl.BlockSpec((None, GROUP_WIDTH, tq), lambda bi, i: (bi, 0, i)),
                  pl.BlockSpec((None, s, GROUP_WIDTH), lambda bi, i: (bi, 0, 0)),
                  pl.BlockSpec((None, v_rows, s), lambda bi, i: (bi, 0, 0)),
                  pl.BlockSpec((GROUP_HEADS, tq, tq), lambda bi, i: (0, 0, 0)),
                  pl.BlockSpec((1, GROUP_WIDTH), lambda bi, i: (0, 0)),
                  pl.BlockSpec((GROUP_WIDTH, GROUP_WIDTH), lambda bi, i: (0, 0))],
        out_specs=pl.BlockSpec((None, tq, GROUP_WIDTH), lambda bi, i: (bi, i, 0)),
        out_shape=jax.ShapeDtypeStruct((b, s, GROUP_WIDTH), F32),
        scratch_shapes=[pltpu.VMEM((2 * GROUP_HEADS, LANES, tq), BF16),
                        pltpu.VMEM((2 * GROUP_HEADS, V_AUG, tq), F32),
                        pltpu.VMEM((2 * GROUP_HEADS, 8, tq), F32),
                        pltpu.VMEM((2 * AHEAD, tq, tq), F32)],
        compiler_params=_cparams(("arbitrary", "arbitrary")),
        name="diff_attn",
    )(lam, qn, kn, vt, _diag_bias(tq, slopes), sub_gain.astype(F32).reshape(1, GROUP_WIDTH), bd64)


MLSTM_CONV = 4
CONV_HALO = 8
AUG = 128


def _tril_sum(x, tril):
    hi = x.astype(BF16)
    lo = (x - hi.astype(F32)).astype(BF16)
    return jnp.dot(tril, hi, preferred_element_type=F32) + jnp.dot(tril, lo, preferred_element_type=F32)


def _log_sigmoid(x):
    return jnp.minimum(x, 0.0) - jnp.log1p(jnp.exp(-jnp.abs(x)))


def _sigmoid(x):
    return 1.0 / (1.0 + jnp.exp(-x))


def _mlstm_kernel(q_ref, k_ref, v_ref, o_ref, g_ref, cw_ref, gb_ref, og_ref, bd_ref, y_ref,
                  halo_ref, c_ref, m_ref, *, L):
    ci = pl.program_id(1)

    @pl.when(ci == 0)
    def _():
        halo_ref[0:CONV_HALO, :] = jnp.zeros((CONV_HALO, 2 * GROUP_WIDTH), F32)
        c_ref[...] = jnp.zeros_like(c_ref)
        m_ref[...] = jnp.zeros_like(m_ref)

    halo_ref[CONV_HALO:CONV_HALO + L, 0:GROUP_WIDTH] = q_ref[...]
    halo_ref[CONV_HALO:CONV_HALO + L, GROUP_WIDTH:2 * GROUP_WIDTH] = k_ref[...]
    conv = jnp.zeros((L, 2 * GROUP_WIDTH), F32)
    for j in range(MLSTM_CONV):
        start = CONV_HALO - (MLSTM_CONV - 1) + j
        conv = conv + halo_ref[start:start + L, :] * cw_ref[j:j + 1, :]
    halo_ref[0:CONV_HALO, :] = halo_ref[L:L + CONV_HALO, :]
    qk = conv * _sigmoid(conv)
    q = qk[:, 0:GROUP_WIDTH] * (HEAD_DIM ** -0.5)
    k = qk[:, GROUP_WIDTH:2 * GROUP_WIDTH]
    v = v_ref[...]

    gates = g_ref[...] + gb_ref[...]
    ri = lax.broadcasted_iota(jnp.int32, (L, L), 0)
    cj = lax.broadcasted_iota(jnp.int32, (L, L), 1)
    causal = ri >= cj
    tril = jnp.where(causal, 1.0, 0.0).astype(BF16)
    gcum = _tril_sum(_log_sigmoid(gates), tril)
    gcum_t = gcum.T
    gates_t = gates.T
    lane_aug = lax.broadcasted_iota(jnp.int32, (L, AUG), 1)
    m_all = m_ref[...]

    outs = []
    for h in range(GROUP_HEADS):
        hs = slice(HEAD_DIM * h, HEAD_DIM * (h + 1))
        q_h, k_h = q[:, hs], k[:, hs]
        v_aug = jnp.where(lane_aug == HEAD_DIM, 1.0,
                          jnp.concatenate([v[:, hs], jnp.zeros((L, AUG - HEAD_DIM), F32)], axis=1))
        g_col = gcum[:, 4 + h:5 + h]
        li_col = gates[:, h:h + 1]
        g_row = gcum_t[4 + h:5 + h, :]
        li_row = gates_t[h:h + 1, :]
        g_last = g_row[:, L - 1:L]
        m_prev = m_all[:, h:h + 1]
        log_d = jnp.where(causal, g_col - g_row + li_row, NEG_INF)
        log_inter = g_col + m_prev
        m_t = jnp.maximum(log_inter, jnp.max(log_d, axis=1, keepdims=True))
        s_qk = lax.dot_general(q_h.astype(BF16), k_h.astype(BF16), (((1,), (1,)), ((), ())),
                               preferred_element_type=F32)
        w_intra = s_qk * jnp.exp(log_d - m_t)
        w_inter = jnp.exp(log_inter - m_t)
        c_aug = c_ref[h]
        num = (w_inter * jnp.dot(q_h.astype(BF16), c_aug.astype(BF16), preferred_element_type=F32)
               + jnp.dot(w_intra.astype(BF16), v_aug.astype(BF16), preferred_element_type=F32))
        den = num[:, HEAD_DIM:HEAD_DIM + 1]
        outs.append(num[:, 0:HEAD_DIM] / jnp.maximum(jnp.abs(den), jnp.exp(-m_t)))

        log_a = g_last - g_col + li_col
        m_new = jnp.maximum(g_last + m_prev, jnp.max(log_a, axis=0, keepdims=True))
        a_col = jnp.exp(log_a - m_new)
        decay = jnp.exp(g_last + m_prev - m_new)
        ak_t = (k_h * a_col).T.astype(BF16)
        c_ref[h] = decay * c_aug + jnp.dot(ak_t, v_aug.astype(BF16), preferred_element_type=F32)
        m_ref[:, h:h + 1] = m_new

    hcat = jnp.concatenate(outs, axis=1)
    ss = _seg_sum(hcat * hcat, bd_ref[...])
    y_ref[...] = hcat * lax.rsqrt(ss * (1.0 / HEAD_DIM) + EPS) * og_ref[...] * _sigmoid(o_ref[...])


def _mlstm(p3, conv_w, gate_bias, out_gain, L=128):
    b, s, _ = p3.shape
    gb = jnp.zeros((1, LANES), F32).at[0, 0:GROUP_HEADS].set(gate_bias[0]).at[0, GROUP_HEADS:2 * GROUP_HEADS].set(gate_bias[1])
    col = lambda off: (lambda bi, i: (bi, i, off // GROUP_WIDTH))
    return pl.pallas_call(
        functools.partial(_mlstm_kernel, L=L),
        grid=(b, s // L),
        in_specs=[pl.BlockSpec((None, L, GROUP_WIDTH), col(COL_B)),
                  pl.BlockSpec((None, L, GROUP_WIDTH), col(COL_B + 256)),
                  pl.BlockSpec((None, L, GROUP_WIDTH), col(COL_B + 512)),
                  pl.BlockSpec((None, L, GROUP_WIDTH), col(COL_B + 768)),
                  pl.BlockSpec((None, L, LANES), lambda bi, i: (bi, i, COL_G // LANES)),
                  pl.BlockSpec((MLSTM_CONV, 2 * GROUP_WIDTH), lambda bi, i: (0, 0)),
                  pl.BlockSpec((1, LANES), lambda bi, i: (0, 0)),
                  pl.BlockSpec((1, GROUP_WIDTH), lambda bi, i: (0, 0)),
                  pl.BlockSpec((GROUP_WIDTH, GROUP_WIDTH), lambda bi, i: (0, 0))],
        out_specs=pl.BlockSpec((None, L, GROUP_WIDTH), lambda bi, i: (bi, i, 0)),
        out_shape=jax.ShapeDtypeStruct((b, s, GROUP_WIDTH), F32),
        scratch_shapes=[pltpu.VMEM((CONV_HALO + L, 2 * GROUP_WIDTH), F32),
                        pltpu.VMEM((GROUP_HEADS, HEAD_DIM, AUG), F32),
                        pltpu.VMEM((1, LANES), F32)],
        compiler_params=_cparams(("arbitrary", "arbitrary")),
        name="mlstm",
    )(p3, p3, p3, p3, p3, conv_w.astype(F32), gb, out_gain.astype(F32).reshape(1, GROUP_WIDTH),
      _block_ones(GROUP_WIDTH, HEAD_DIM))


HGRN_CHUNK = 16
LB_FLOOR = 1e-30


def _hgrn_kernel(q_ref, f_ref, i_ref, g_ref, lb_ref, og_ref, bd_ref, y_ref,
                 lf_s, kk_s, vv_s, st_ref, *, TL):
    ci = pl.program_id(1)
    C = HGRN_CHUNK
    W = GROUP_WIDTH

    @pl.when(ci == 0)
    def _():
        lf_s[0:C, :] = jnp.zeros((C, W), F32)
        kk_s[0:C, :] = jnp.zeros((C, W), F32)
        vv_s[0:C, :] = jnp.zeros((C, W), F32)
        st_ref[...] = jnp.zeros_like(st_ref)

    z = f_ref[...]
    a = lb_ref[0:1, :]
    c = lb_ref[1:2, :] + _log_sigmoid(z)
    mx = jnp.maximum(a, c)
    lf = mx + jnp.log1p(jnp.exp(-jnp.abs(a - c)))
    kk = lb_ref[2:3, :] * _sigmoid(-z) + lb_ref[3:4, :]
    qx = q_ref[...]
    qs = qx * _sigmoid(qx)
    vv = i_ref[...]
    lf_s[C:C + TL, :] = lf
    kk_s[C:C + TL, :] = kk
    vv_s[C:C + TL, :] = vv

    bd = bd_ref[...]
    row = lax.broadcasted_iota(jnp.int32, (TL, W), 0)
    rmod = row & (C - 1)

    acc = jnp.zeros((TL, W), F32)
    dsum = jnp.zeros((TL, W), F32)
    for delta in range(C):
        if delta > 0:
            dsum = dsum + lf_s[C - (delta - 1):C - (delta - 1) + TL, :]
        x = qs * kk_s[C - delta:C - delta + TL, :] * jnp.exp(dsum)
        x = jnp.where(rmod >= delta, x, 0.0)
        att = jnp.dot(x.astype(BF16), bd, preferred_element_type=F32)
        acc = acc + att * vv_s[C - delta:C - delta + TL, :]

    ri = lax.broadcasted_iota(jnp.int32, (TL, TL), 0)
    cj = lax.broadcasted_iota(jnp.int32, (TL, TL), 1)
    same = (ri // C) == (cj // C)
    tril = jnp.where(same & (ri >= cj), 1.0, 0.0).astype(BF16)
    ones = jnp.where(same, 1.0, 0.0).astype(BF16)
    bcum = _tril_sum(lf, tril)
    blast = _tril_sum(lf, ones)
    q_a = (qs * jnp.exp(bcum)).astype(BF16)
    k_b = (kk * jnp.exp(blast - bcum)).astype(BF16)
    dec = jnp.exp(blast)
    vv_b = vv.astype(BF16)
    hmask = bd.astype(F32)
    state = st_ref[...]
    inters = []
    for ch in range(TL // C):
        r = slice(ch * C, (ch + 1) * C)
        inters.append(lax.dot_general(q_a[r], state.astype(BF16), (((1,), (1,)), ((), ())),
                                      preferred_element_type=F32))
        upd = lax.dot_general(vv_b[r], k_b[r], (((0,), (0,)), ((), ())), preferred_element_type=F32)
        state = state * dec[ch * C:ch * C + 1, :] + upd * hmask
    st_ref[...] = state
    o = acc + jnp.concatenate(inters, axis=0)
    ss = _seg_sum(o * o, bd)
    y_ref[...] = o * lax.rsqrt(ss * (1.0 / HEAD_DIM) + EPS) * og_ref[...] * _sigmoid(g_ref[...])


def _hgrn2(p3, lower_bound, out_gain, TL=256):
    b, s, _ = p3.shape
    lb = lower_bound.astype(F32)
    lbf = jnp.maximum(lb, LB_FLOOR)
    lbp = jnp.stack([jnp.log(lbf), jnp.log1p(-lb), 1.0 - lb, lb - lbf])
    col = lambda off: (lambda bi, i: (bi, i, off // GROUP_WIDTH))
    C = HGRN_CHUNK
    return pl.pallas_call(
        functools.partial(_hgrn_kernel, TL=TL),
        grid=(b, s // TL),
        in_specs=[pl.BlockSpec((None, TL, GROUP_WIDTH), col(COL_C)),
                  pl.BlockSpec((None, TL, GROUP_WIDTH), col(COL_C + 256)),
                  pl.BlockSpec((None, TL, GROUP_WIDTH), col(COL_C + 512)),
                  pl.BlockSpec((None, TL, GROUP_WIDTH), col(COL_C + 768)),
                  pl.BlockSpec((4, GROUP_WIDTH), lambda bi, i: (0, 0)),
                  pl.BlockSpec((1, GROUP_WIDTH), lambda bi, i: (0, 0)),
                  pl.BlockSpec((GROUP_WIDTH, GROUP_WIDTH), lambda bi, i: (0, 0))],
        out_specs=pl.BlockSpec((None, TL, GROUP_WIDTH), lambda bi, i: (bi, i, 0)),
        out_shape=jax.ShapeDtypeStruct((b, s, GROUP_WIDTH), F32),
        scratch_shapes=[pltpu.VMEM((C + TL, GROUP_WIDTH), F32),
                        pltpu.VMEM((C + TL, GROUP_WIDTH), F32),
                        pltpu.VMEM((C + TL, GROUP_WIDTH), F32),
                        pltpu.VMEM((GROUP_WIDTH, GROUP_WIDTH), F32)],
        compiler_params=_cparams(("arbitrary", "arbitrary")),
        name="hgrn2",
    )(p3, p3, p3, p3, lbp, out_gain.astype(F32).reshape(1, GROUP_WIDTH), _block_ones(GROUP_WIDTH, HEAD_DIM))


CMP_LEN = 32
CMP_STRIDE = 16
SEL_BLOCK = 64
SEL_TOPK = 16
WINDOW = 512
GATE_ROW = 8


def _rms_rows(x, gain):
    return x * lax.rsqrt(jnp.mean(x * x, axis=-1, keepdims=True) + EPS) * gain


NSA_V_ROWS = (GROUP_HEADS + 1) * V_AUG


def _aug_shared_values(v, w_wide, w_t):
    tm = v.shape[0]
    vw_t = (jnp.concatenate([v] * GROUP_HEADS, axis=1) * w_wide).T
    first = lax.broadcasted_iota(jnp.int32, (V_AUG - HEAD_DIM, tm), 0) == 0
    blocks = []
    for h in range(GROUP_HEADS):
        blocks += [vw_t[HEAD_DIM * h:HEAD_DIM * (h + 1), :],
                   jnp.where(first, jnp.broadcast_to(w_t[h:h + 1, :], first.shape), 0.0)]
    blocks += [v.T, jnp.where(first, 1.0, 0.0)]
    return jnp.concatenate(blocks, axis=0).astype(BF16)


def _nsa_prep_kernel(q_ref, kv_ref, g_ref, qg_ref, kg_ref, bd_ref, w_ref, wt_ref,
                     qt_ref, kc_ref, vc_ref, ks_ref, vst_ref, kw_ref, vwt_ref, gt_ref):
    ti = pl.program_id(1)
    q = q_ref[...]
    ss = _seg_sum(q * q, bd_ref[...])
    qn = q * lax.rsqrt(ss * (1.0 / HEAD_DIM) + EPS) * qg_ref[...] * (HEAD_DIM ** -0.5 * LOG2E)
    qt_ref[...] = qn.T.astype(BF16)
    kv = kv_ref[...]
    tm = kv.shape[0]
    kc_ref[...] = kv[:, 0:64]
    vc_ref[...] = kv[:, 64:128]
    ks = _rms_rows(kv[:, 128:192], kg_ref[1:2, :])
    lane = lax.broadcasted_iota(jnp.int32, (tm, LANES), 1)
    blk = jnp.right_shift(ti * tm + lax.broadcasted_iota(jnp.int32, (tm, LANES), 0), SEL_BLOCK.bit_length() - 1)
    onehot = jnp.where(lane == blk + HEAD_DIM, 1.0, 0.0)
    ks_ref[...] = jnp.where(lane < HEAD_DIM, jnp.concatenate([ks, ks], axis=1), onehot).astype(BF16)
    kw = _rms_rows(kv[:, 256:320], kg_ref[2:3, :])
    kw_ref[...] = jnp.concatenate([kw, jnp.zeros_like(kw)], axis=1).astype(BF16)
    vst_ref[...] = _aug_shared_values(kv[:, 192:256], w_ref[...], wt_ref[...])
    vwt_ref[...] = _aug_shared_values(kv[:, 320:384], w_ref[...], wt_ref[...])
    gt_ref[...] = _sigmoid(g_ref[...]).T


def _gelu_tanh(x):
    return 0.5 * x * (1.0 + jnp.tanh(math.sqrt(2.0 / math.pi) * (x + 0.044715 * x * x * x)))


def _nsa_cmp_kernel(kr_ref, vr_ref, w1_ref, w2_ref, pos_ref, kg_ref, kc_ref, vct_ref, sh_ref, *, n_rows):
    half = CMP_STRIDE * HEAD_DIM
    sh_ref[n_rows:n_rows + 8, :] = jnp.zeros((8, HEAD_DIM), F32)
    outs = []
    for j, x_ref in enumerate((kr_ref, vr_ref)):
        r = x_ref[...].astype(BF16)
        w1 = w1_ref[j]
        first = jnp.dot(r, w1[0:half, :], preferred_element_type=F32)
        sh_ref[0:n_rows, :] = jnp.dot(r, w1[half:2 * half, :], preferred_element_type=F32)
        pos8 = jnp.broadcast_to(pos_ref[j], (8, 2 * half))
        posw = jnp.dot(pos8, w1.astype(F32), preferred_element_type=F32)[0:1, :]
        hdn = _gelu_tanh(first + sh_ref[1:n_rows + 1, :] + posw)
        outs.append(jnp.dot(hdn, w2_ref[j].astype(F32), preferred_element_type=F32))
    kc_ref[...] = _rms_rows(outs[0], kg_ref[0:1, :]).astype(BF16)
    vct_ref[...] = outs[1].T.astype(BF16)


def _nsa_attn_kernel(qt_ref, kc_ref, vct_ref, ks_ref, vst_ref, kw_ref, vwt_ref, gt_ref, ov_ref, db_ref, lo_ref,
                     og_ref, bd_ref, y_ref, qa_ref, acc_s, m_s, acc_w, m_w, ss_ref, sw_ref,
                     *, tq, slopes, n_cmp):
    qi = pl.program_id(1)
    n_rows = kc_ref.shape[0]
    n_sel = ov_ref.shape[0]
    qpos = qi * tq + lax.broadcasted_iota(jnp.int32, (1, tq), 1)

    nio = lax.broadcasted_iota(jnp.int32, (n_rows, tq), 0)
    dist_c = qpos - (nio * CMP_STRIDE + (CMP_LEN - 1))
    valid_c = (dist_c >= 0) & (nio < n_cmp)
    dist_cf = dist_c.astype(F32)
    kc = kc_ref[...]
    vct = vct_ref[...]
    o_cmp = []
    p_sum = jnp.zeros((n_rows, tq), F32)
    for h in range(GROUP_HEADS):
        qh = qt_ref[HEAD_DIM * h:HEAD_DIM * (h + 1), :]
        s = jnp.dot(kc, qh, preferred_element_type=F32) - (slopes[h] * LOG2E) * dist_cf
        s = jnp.where(valid_c, s, MASKED)
        m = jnp.maximum(jnp.max(s, axis=0, keepdims=True), M_INIT)
        e = jnp.exp2(s - m)
        p = e / jnp.maximum(jnp.sum(e, axis=0, keepdims=True), 1e-30)
        p_sum = p_sum + p
        o_cmp.append(jnp.dot(vct, p.astype(BF16), preferred_element_type=F32))

    p_hi = p_sum.astype(BF16)
    p_lo = (p_sum - p_hi.astype(F32)).astype(BF16)
    ov = ov_ref[...]
    imp = jnp.dot(ov, p_hi, preferred_element_type=F32) + jnp.dot(ov, p_lo, preferred_element_type=F32)
    jio = lax.broadcasted_iota(jnp.int32, (n_sel, tq), 0)
    qblk = jnp.right_shift(qpos, SEL_BLOCK.bit_length() - 1)
    imp = jnp.where(jio <= qblk, imp, -1.0)
    imp = jnp.where((jio == 0) | (jio == qblk) | (jio == qblk - 1), 2.0, imp)
    rank = jnp.zeros((n_sel, tq), F32)
    for i in range(n_sel):
        row = imp[i:i + 1, :]
        beats = (row > imp) | ((row == imp) & (jio > i))
        rank = rank + jnp.where(beats, 1.0, 0.0)
    bsel = jnp.where(rank < float(min(SEL_TOPK, n_sel)), 0.0, MASKED)
    if n_sel < LANES - HEAD_DIM:
        bsel = jnp.concatenate([bsel, jnp.zeros((LANES - HEAD_DIM - n_sel, tq), F32)], axis=0)
    for h in range(GROUP_HEADS):
        qa_ref[h, 0:HEAD_DIM, :] = qt_ref[HEAD_DIM * h:HEAD_DIM * (h + 1), :]
        qa_ref[h, HEAD_DIM:LANES, :] = bsel.astype(BF16)
    for ref, val in ((acc_s, 0.0), (acc_w, 0.0), (m_s, M_INIT), (m_w, M_INIT)):
        ref[...] = jnp.full(ref.shape, val, F32)

    def tile_off(kt, h):
        return ((kt - qi) * tq + (tq - 1)).astype(F32) * (slopes[h] * LOG2E)

    def values(vt_ref, kt, h, diag):
        base = GROUP_HEADS * V_AUG if diag else V_AUG * h
        return vt_ref[base:base + V_AUG, pl.ds(pl.multiple_of(kt * tq, tq), tq)]

    def sel_scores(kt, h):
        kb = ks_ref[pl.ds(pl.multiple_of(kt * tq, tq), tq), :]
        ss_ref[h] = jnp.dot(kb, qa_ref[h], preferred_element_type=F32)

    def sel_step(kt, diag):
        for h in range(GROUP_HEADS):
            if h + AHEAD < GROUP_HEADS:
                sel_scores(kt, h + AHEAD)
            elif not diag:
                sel_scores(kt + 1, h + AHEAD - GROUP_HEADS)
            _flash_update(ss_ref.at[h], db_ref.at[h] if diag else None, 0.0 if diag else tile_off(kt, h),
                          values(vst_ref, kt, h, diag), m_s.at[h], acc_s.at[h])

    for h in range(AHEAD):
        sel_scores(0, h)

    def body(kt, carry):
        sel_step(kt, False)
        return carry

    lax.fori_loop(0, qi, body, 0)
    sel_step(qi, True)

    def win_step(back, bias_of, diag):
        kt = jnp.maximum(qi - back, 0)
        penalty = None if diag else jnp.where(qi >= back, 0.0, MASKED)
        kb = kw_ref[pl.ds(pl.multiple_of(kt * tq, tq), tq), :]

        def win_scores(h):
            sw_ref[h] = jnp.dot(kb, qa_ref[h], preferred_element_type=F32)

        for h in range(AHEAD):
            win_scores(h)
        for h in range(GROUP_HEADS):
            if h + AHEAD < GROUP_HEADS:
                win_scores(h + AHEAD)
            _flash_update(sw_ref.at[h], bias_of(h), 0.0 if diag else tile_off(kt, h),
                          values(vwt_ref, kt, h, diag), m_w.at[h], acc_w.at[h], penalty)

    win_step(2, lambda h: lo_ref, False)
    win_step(1, lambda h: None, False)
    win_step(0, lambda h: db_ref.at[h], True)

    gt = gt_ref[...]
    heads = []
    for h in range(GROUP_HEADS):
        a_s, a_w = acc_s[h], acc_w[h]
        o_sel = a_s[0:HEAD_DIM, :] / a_s[HEAD_DIM:HEAD_DIM + 1, :]
        o_win = a_w[0:HEAD_DIM, :] / a_w[HEAD_DIM:HEAD_DIM + 1, :]
        g = GATE_ROW + 3 * h
        heads.append(gt[g:g + 1, :] * o_cmp[h] + gt[g + 1:g + 2, :] * o_sel + gt[g + 2:g + 3, :] * o_win)
    o = jnp.concatenate(heads, axis=0).T
    ss = _seg_sum(o * o, bd_ref[...])
    y_ref[...] = o * lax.rsqrt(ss * (1.0 / HEAD_DIM) + EPS) * og_ref[...]


def _nsa(p3, cmp_pos, cmp_w1, cmp_w2, q_gain, k_gain, out_gain, slopes, tq=256):
    b, s, _ = p3.shape
    tm = tq
    n_rows = s // CMP_STRIDE
    n_cmp = (s - CMP_LEN) // CMP_STRIDE + 1
    n_sel = s // SEL_BLOCK
    bd64 = _block_ones(GROUP_WIDTH, HEAD_DIM)
    qg = jnp.tile(q_gain.astype(F32), GROUP_HEADS).reshape(1, GROUP_WIDTH)
    kg = k_gain.astype(F32)
    tok = lambda w: pl.BlockSpec((None, tm, w), lambda bi, i: (bi, i, 0))
    tok_t = lambda w: pl.BlockSpec((None, w, tm), lambda bi, i: (bi, 0, i))
    full = lambda r, c: pl.BlockSpec((r, c), lambda bi, i: (0, 0))
    assert n_sel <= LANES - HEAD_DIM, "block one-hot lanes hold at most 64 selection blocks"
    assert WINDOW == 2 * tq and tq % SEL_BLOCK == 0, "window branch walks exactly the key tiles qi-2, qi-1, qi"
    w_wide, w_t = _key_weights(tm, slopes)
    qt, kc, vc, ks, vst, kw, vwt, gt = pl.pallas_call(
        _nsa_prep_kernel,
        grid=(b, s // tm),
        in_specs=[pl.BlockSpec((None, tm, GROUP_WIDTH), lambda bi, i: (bi, i, COL_D // GROUP_WIDTH)),
                  pl.BlockSpec((None, tm, 384), lambda bi, i: (bi, i, (COL_D + GROUP_WIDTH) // 384)),
                  pl.BlockSpec((None, tm, LANES), lambda bi, i: (bi, i, COL_G // LANES)),
                  full(1, GROUP_WIDTH), full(3, HEAD_DIM), full(GROUP_WIDTH, GROUP_WIDTH),
                  full(tm, GROUP_WIDTH), full(8, tm)],
        out_specs=[tok_t(GROUP_WIDTH), tok(HEAD_DIM), tok(HEAD_DIM), tok(LANES), tok_t(NSA_V_ROWS),
                   tok(LANES), tok_t(NSA_V_ROWS), tok_t(LANES)],
        out_shape=[jax.ShapeDtypeStruct((b, GROUP_WIDTH, s), BF16),
                   jax.ShapeDtypeStruct((b, s, HEAD_DIM), F32),
                   jax.ShapeDtypeStruct((b, s, HEAD_DIM), F32),
                   jax.ShapeDtypeStruct((b, s, LANES), BF16),
                   jax.ShapeDtypeStruct((b, NSA_V_ROWS, s), BF16),
                   jax.ShapeDtypeStruct((b, s, LANES), BF16),
                   jax.ShapeDtypeStruct((b, NSA_V_ROWS, s), BF16),
                   jax.ShapeDtypeStruct((b, LANES, s), F32)],
        compiler_params=_cparams(("arbitrary", "arbitrary")),
        name="nsa_prep",
    )(p3, p3, p3, qg, kg, bd64, w_wide, w_t)

    row_w = CMP_STRIDE * HEAD_DIM
    kcmp, vcmp_t = pl.pallas_call(
        functools.partial(_nsa_cmp_kernel, n_rows=n_rows),
        grid=(b,),
        in_specs=[pl.BlockSpec((None, n_rows, row_w), lambda bi: (bi, 0, 0)),
                  pl.BlockSpec((None, n_rows, row_w), lambda bi: (bi, 0, 0)),
                  pl.BlockSpec((2, 2 * row_w, HEAD_DIM), lambda bi: (0, 0, 0)),
                  pl.BlockSpec((2, HEAD_DIM, HEAD_DIM), lambda bi: (0, 0, 0)),
                  pl.BlockSpec((2, 1, 2 * row_w), lambda bi: (0, 0, 0)),
                  pl.BlockSpec((3, HEAD_DIM), lambda bi: (0, 0))],
        out_specs=[pl.BlockSpec((None, n_rows, HEAD_DIM), lambda bi: (bi, 0, 0)),
                   pl.BlockSpec((None, HEAD_DIM, n_rows), lambda bi: (bi, 0, 0))],
        out_shape=[jax.ShapeDtypeStruct((b, n_rows, HEAD_DIM), BF16),
                   jax.ShapeDtypeStruct((b, HEAD_DIM, n_rows), BF16)],
        scratch_shapes=[pltpu.VMEM((n_rows + 8, HEAD_DIM), F32)],
        compiler_params=_cparams(("arbitrary",)),
        name="nsa_cmp",
    )(kc.reshape(b, n_rows, row_w), vc.reshape(b, n_rows, row_w), cmp_w1.astype(BF16), cmp_w2.astype(BF16),
      cmp_pos.astype(F32).reshape(2, 1, 2 * row_w), kg)

    cs = np.arange(n_rows)[:, None] * CMP_STRIDE
    ss = np.arange(n_sel)[None, :] * SEL_BLOCK
    overlap = np.clip(np.minimum(cs + CMP_LEN, ss + SEL_BLOCK) - np.maximum(cs, ss), 0, None) / CMP_LEN
    overlap[n_cmp:, :] = 0.0
    ov_t = jnp.asarray(overlap.T.astype(np.float32), BF16)

    seq = lambda r, c: pl.BlockSpec((None, r, c), lambda bi, i: (bi, 0, 0))
    kern = functools.partial(_nsa_attn_kernel, tq=tq, slopes=tuple(float(v) for v in slopes), n_cmp=n_cmp)
    return pl.pallas_call(
        kern,
        grid=(b, s // tq),
        in_specs=[pl.BlockSpec((None, GROUP_WIDTH, tq), lambda bi, i: (bi, 0, i)),
                  seq(n_rows, HEAD_DIM), seq(HEAD_DIM, n_rows),
                  seq(s, LANES), seq(NSA_V_ROWS, s), seq(s, LANES), seq(NSA_V_ROWS, s),
                  pl.BlockSpec((None, LANES, tq), lambda bi, i: (bi, 0, i)),
                  full(n_sel, n_rows),
                  pl.BlockSpec((GROUP_HEADS, tq, tq), lambda bi, i: (0, 0, 0)), full(tq, tq),
                  full(1, GROUP_WIDTH), full(GROUP_WIDTH, GROUP_WIDTH)],
        out_specs=pl.BlockSpec((None, tq, GROUP_WIDTH), lambda bi, i: (bi, i, 0)),
        out_shape=jax.ShapeDtypeStruct((b, s, GROUP_WIDTH), F32),
        scratch_shapes=[pltpu.VMEM((GROUP_HEADS, LANES, tq), BF16),
                        pltpu.VMEM((GROUP_HEADS, V_AUG, tq), F32), pltpu.VMEM((GROUP_HEADS, 8, tq), F32),
                        pltpu.VMEM((GROUP_HEADS, V_AUG, tq), F32), pltpu.VMEM((GROUP_HEADS, 8, tq), F32),
                        pltpu.VMEM((GROUP_HEADS, tq, tq), F32), pltpu.VMEM((GROUP_HEADS, tq, tq), F32)],
        compiler_params=_cparams(("arbitrary", "arbitrary")),
        name="nsa_attn",
    )(qt, kcmp, vcmp_t, ks, vst, kw, vwt, gt, ov_t, _diag_bias(tq, slopes), _window_low_bias(tq),
      out_gain.astype(F32).reshape(1, GROUP_WIDTH), bd64)


def _window_low_bias(tq):
    k = np.arange(tq)[:, None]
    q = np.arange(tq)[None, :]
    return jnp.asarray(np.where(k > q, 0.0, MASKED).astype(np.float32))


N_GROUPS = 4
EXPERTS_PER_GROUP = 8
N_EXPERTS = N_GROUPS * EXPERTS_PER_GROUP
D_EXPERT = 256
ROUTER_LANE0 = N_GROUPS


def _split3_dot(a, b_hi, b_lo):
    a_hi = a.astype(BF16)
    a_lo = (a - a_hi.astype(F32)).astype(BF16)
    return (jnp.dot(a_hi, b_hi, preferred_element_type=F32) + jnp.dot(a_lo, b_hi, preferred_element_type=F32)
            + jnp.dot(a_hi, b_lo, preferred_element_type=F32))


def _outproj_kernel(x_ref, ya_ref, yb_ref, yc_ref, yd_ref, w_ref, g_ref, wr_hi_ref, wr_lo_ref, br_ref,
                    xo_ref, t_ref, gate_ref):
    acc = x_ref[...]
    for gi, y_ref in enumerate((ya_ref, yb_ref, yc_ref, yd_ref)):
        acc = acc + jnp.dot(y_ref[...].astype(BF16), w_ref[GROUP_WIDTH * gi:GROUP_WIDTH * (gi + 1), :],
                            preferred_element_type=F32)
    xo_ref[...] = acc
    ms = jnp.mean(acc * acc, axis=-1, keepdims=True)
    t = acc * lax.rsqrt(ms + EPS) * g_ref[...]
    t_ref[...] = t.astype(BF16)

    logits = _split3_dot(t, wr_hi_ref[...], wr_lo_ref[...]) + br_ref[...]
    lane = lax.broadcasted_iota(jnp.int32, logits.shape, 1)
    lane_f = lane.astype(F32)
    big = float(LANES)
    is_g = lane < N_GROUPS
    gl = jnp.where(is_g, logits, MASKED)
    gmax = jnp.max(gl, axis=-1, keepdims=True)
    g_prob = 1.0 / jnp.sum(jnp.where(is_g, jnp.exp(gl - gmax), 0.0), axis=-1, keepdims=True)
    g_sel = jnp.min(jnp.where(is_g & (gl == gmax), lane_f, big), axis=-1, keepdims=True)
    lo = ROUTER_LANE0 + EXPERTS_PER_GROUP * g_sel
    in_grp = (lane_f >= lo) & (lane_f < lo + EXPERTS_PER_GROUP)
    el = jnp.where(in_grp, logits, MASKED)
    m1 = jnp.max(el, axis=-1, keepdims=True)
    i1 = jnp.min(jnp.where(in_grp & (el == m1), lane_f, big), axis=-1, keepdims=True)
    rest = in_grp & (lane_f != i1)
    el2 = jnp.where(rest, logits, MASKED)
    m2 = jnp.max(el2, axis=-1, keepdims=True)
    i2 = jnp.min(jnp.where(rest & (el2 == m2), lane_f, big), axis=-1, keepdims=True)
    r = jnp.exp(m2 - m1)
    w1 = g_prob / (1.0 + r)
    w2 = g_prob * r / (1.0 + r)
    gate_ref[...] = jnp.where(lane_f == i1, w1, 0.0) + jnp.where(lane_f == i2, w2, 0.0)


def _outproj(x2d, ys, w_out, ffn_gain, w_group, b_group, w_expert, b_expert, tm=256):
    t = x2d.shape[0]
    wr = jnp.zeros((D_MODEL, LANES), F32).at[:, 0:N_GROUPS].set(w_group).at[:, ROUTER_LANE0:ROUTER_LANE0 + N_EXPERTS].set(w_expert)
    wr_hi = wr.astype(BF16)
    wr_lo = (wr - wr_hi.astype(F32)).astype(BF16)
    br = jnp.zeros((1, LANES), F32).at[0, 0:N_GROUPS].set(b_group).at[0, ROUTER_LANE0:ROUTER_LANE0 + N_EXPERTS].set(b_expert)
    row = lambda w: pl.BlockSpec((tm, w), lambda i: (i, 0))
    full = lambda r, c: pl.BlockSpec((r, c), lambda i: (0, 0))
    return pl.pallas_call(
        _outproj_kernel,
        grid=(t // tm,),
        in_specs=[row(D_MODEL), row(GROUP_WIDTH), row(GROUP_WIDTH), row(GROUP_WIDTH), row(GROUP_WIDTH),
                  full(D_MODEL, D_MODEL), full(1, D_MODEL), full(D_MODEL, LANES), full(D_MODEL, LANES), full(1, LANES)],
        out_specs=[row(D_MODEL), row(D_MODEL), row(LANES)],
        out_shape=[jax.ShapeDtypeStruct((t, D_MODEL), F32),
                   jax.ShapeDtypeStruct((t, D_MODEL), BF16),
                   jax.ShapeDtypeStruct((t, LANES), F32)],
        compiler_params=_cparams(("arbitrary",)),
        name="outproj_router",
    )(x2d, *ys, w_out.astype(BF16), ffn_gain.reshape(1, D_MODEL), wr_hi, wr_lo, br)


def _moe_dense_kernel(x_ref, t_ref, gate_ref, wg_ref, wu_ref, wd_ref, o_ref, acc_ref):
    e = pl.program_id(1)

    @pl.when(e == 0)
    def _():
        acc_ref[...] = jnp.zeros_like(acc_ref)

    t = t_ref[...]
    a = jnp.dot(t, wg_ref[...], preferred_element_type=F32)
    u = jnp.dot(t, wu_ref[...], preferred_element_type=F32)
    gate = gate_ref[...]
    lane = lax.broadcasted_iota(jnp.int32, gate.shape, 1)
    w = jnp.sum(jnp.where(lane == e + ROUTER_LANE0, gate, 0.0), axis=-1, keepdims=True)
    act = a * _sigmoid(a) * u * w
    acc_ref[...] += jnp.dot(act.astype(BF16), wd_ref[...], preferred_element_type=F32)

    @pl.when(e == N_EXPERTS - 1)
    def _():
        o_ref[...] = x_ref[...] + acc_ref[...]


def _moe_dense(x2d, t_bf, gate, w_gate, w_up, w_down, tm=1024):
    t = x2d.shape[0]
    wg = w_gate.reshape(N_EXPERTS, D_MODEL, D_EXPERT).astype(BF16)
    wu = w_up.reshape(N_EXPERTS, D_MODEL, D_EXPERT).astype(BF16)
    wd = w_down.reshape(N_EXPERTS, D_EXPERT, D_MODEL).astype(BF16)
    return pl.pallas_call(
        _moe_dense_kernel,
        grid=(t // tm, N_EXPERTS),
        in_specs=[pl.BlockSpec((tm, D_MODEL), lambda i, e: (i, 0)),
                  pl.BlockSpec((tm, D_MODEL), lambda i, e: (i, 0)),
                  pl.BlockSpec((tm, LANES), lambda i, e: (i, 0)),
                  pl.BlockSpec((None, D_MODEL, D_EXPERT), lambda i, e: (e, 0, 0)),
                  pl.BlockSpec((None, D_MODEL, D_EXPERT), lambda i, e: (e, 0, 0)),
                  pl.BlockSpec((None, D_EXPERT, D_MODEL), lambda i, e: (e, 0, 0))],
        out_specs=pl.BlockSpec((tm, D_MODEL), lambda i, e: (i, 0)),
        out_shape=jax.ShapeDtypeStruct((t, D_MODEL), F32),
        scratch_shapes=[pltpu.VMEM((tm, D_MODEL), F32)],
        compiler_params=_cparams(("arbitrary", "arbitrary")),
        name="moe_dense",
    )(x2d, t_bf, gate, wg, wu, wd)


def _alibi_slopes():
    n = 2 * GROUP_HEADS
    s = 2.0 ** (-8.0 * np.arange(1, n + 1) / n)
    return s[0::2], s[1::2]


def kernel(x, norm_mix, norm_ffn, w_in, w_out, diff_q_gain, diff_k_gain, diff_lambda, diff_sub_gain, mlstm_conv, mlstm_gate_bias, mlstm_out_gain, hgrn_lower_bounds, hgrn_out_gain, nsa_cmp_pos, nsa_cmp_w1, nsa_cmp_w2, nsa_q_gain, nsa_k_gain, nsa_out_gain, moe_w_group, moe_b_group, moe_w_expert, moe_b_expert, moe_w_gate, moe_w_up, moe_w_down):
    b, s, d = x.shape
    slopes_diff, slopes_nsa = _alibi_slopes()
    lb_soft = jax.nn.softmax(hgrn_lower_bounds.astype(F32), axis=0)
    lower_bounds = jnp.cumsum(lb_soft, axis=0) - lb_soft[0]
    x2d = x.reshape(b * s, d)
    for l in range(norm_mix.shape[0]):
        p = _inproj(x2d, norm_mix[l], _pack_w_in(w_in[l])).reshape(b, s, P_COLS)
        y_a = _diff_attention(p, diff_q_gain[l], diff_k_gain[l], diff_lambda[l], diff_sub_gain[l], slopes_diff, l)
        y_b = _mlstm(p, mlstm_conv[l], mlstm_gate_bias[l], mlstm_out_gain[l])
        y_c = _hgrn2(p, lower_bounds[l], hgrn_out_gain[l])
        y_d = _nsa(p, nsa_cmp_pos[l], nsa_cmp_w1[l], nsa_cmp_w2[l], nsa_q_gain[l], nsa_k_gain[l],
                   nsa_out_gain[l], slopes_nsa)
        ys = [y.reshape(b * s, GROUP_WIDTH) for y in (y_a, y_b, y_c, y_d)]
        x2d, t_bf, gate = _outproj(x2d, ys, w_out[l], norm_ffn[l], moe_w_group[l], moe_b_group[l],
                                   moe_w_expert[l], moe_b_expert[l])
        x2d = _moe_dense(x2d, t_bf, gate, moe_w_gate[l], moe_w_up[l], moe_w_down[l])
    return x2d.reshape(b, s, d)
```

```python
import functools
import math

import numpy as np
import jax
import jax.numpy as jnp
from jax import lax
from jax.experimental import pallas as pl
from jax.experimental.pallas import tpu as pltpu

F32 = jnp.float32
BF16 = jnp.bfloat16

D_MODEL = 1024
HEAD_DIM = 64
GROUP_HEADS = 4
GROUP_WIDTH = GROUP_HEADS * HEAD_DIM
DIFF_HALF = HEAD_DIM // 2
EPS = 1e-6
NEG_INF = -1e30
LOG2E = math.log2(math.e)
M_INIT = -1e30
MASKED = -2e30

LANES = 128
VMEM_LIMIT = 48 * 1024 * 1024

COL_A = 0
COL_B = 768
COL_C = 1792
COL_D = 2816
COL_G = 3456
P_COLS = 3584


def _cparams(sem, flags=None):
    return pltpu.CompilerParams(dimension_semantics=sem, vmem_limit_bytes=VMEM_LIMIT, flags=flags)


def _block_ones(width, seg):
    i = np.arange(width)
    return jnp.asarray((i[:, None] // seg == i[None, :] // seg).astype(np.float32), BF16)


def _seg_sum(x, bd):
    hi = x.astype(BF16)
    lo = (x - hi.astype(F32)).astype(BF16)
    return (jnp.dot(hi, bd, preferred_element_type=F32) + jnp.dot(lo, bd, preferred_element_type=F32))


def _inproj_kernel(x_ref, g_ref, w_ref, o_ref):
    x = x_ref[...]
    ms = jnp.mean(x * x, axis=-1, keepdims=True)
    h = (x * lax.rsqrt(ms + EPS) * g_ref[...]).astype(BF16)
    o_ref[...] = jnp.dot(h, w_ref[...], preferred_element_type=F32)


def _inproj(x2d, gain, w_cat, tm=256):
    t = x2d.shape[0]
    return pl.pallas_call(
        _inproj_kernel,
        grid=(t // tm,),
        in_specs=[pl.BlockSpec((tm, D_MODEL), lambda i: (i, 0)),
                  pl.BlockSpec((1, D_MODEL), lambda i: (0, 0)),
                  pl.BlockSpec((D_MODEL, P_COLS), lambda i: (0, 0))],
        out_specs=pl.BlockSpec((tm, P_COLS), lambda i: (i, 0)),
        out_shape=jax.ShapeDtypeStruct((t, P_COLS), F32),
        compiler_params=_cparams(("arbitrary",)),
        name="inproj",
    )(x2d, gain.reshape(1, D_MODEL), w_cat)


IN_COLS = 3476


def _pack_w_in_kernel(w_ref, o_ref):
    w = w_ref[...]
    a_b = w[:, 0:1536]
    gates_b = w[:, 1536:1544]
    rest = w[:, 1544:3464]
    gates_d = w[:, 3464:3476]
    pad = jnp.zeros((w.shape[0], P_COLS - COL_G - 20), F32)
    o_ref[...] = jnp.concatenate([a_b, rest, gates_b, gates_d, pad], axis=1).astype(BF16)


def _pack_w_in(w, tr=128):
    d = w.shape[0]
    return pl.pallas_call(
        _pack_w_in_kernel,
        grid=(d // tr,),
        in_specs=[pl.BlockSpec((tr, IN_COLS), lambda i: (i, 0))],
        out_specs=pl.BlockSpec((tr, P_COLS), lambda i: (i, 0)),
        out_shape=jax.ShapeDtypeStruct((d, P_COLS), BF16),
        compiler_params=_cparams(("arbitrary",)),
        name="pack_w_in",
    )(w)


def _diff_prep_kernel(p_ref, qg_ref, kg_ref, bd_ref, w_ref, wt_ref, q_ref, k_ref, vt_ref):
    p = p_ref[...]
    bd = bd_ref[...]

    def norm(x, g):
        ss = _seg_sum(x * x, bd)
        return x * lax.rsqrt(ss * (1.0 / DIFF_HALF) + EPS) * g

    q = norm(p[:, 0:256], qg_ref[...]) * (DIFF_HALF ** -0.5 * LOG2E)
    k = norm(p[:, 256:512], kg_ref[...])
    q_ref[...] = q.T.astype(BF16)
    k_ref[...] = k.astype(BF16)
    v = p[:, 512:768]
    v_t = v.T
    vw_t = (v * w_ref[...]).T
    w_t = wt_ref[...]
    tm = v_t.shape[1]
    first = lax.broadcasted_iota(jnp.int32, (V_AUG - HEAD_DIM, tm), 0) == 0
    blocks = []
    for h in range(GROUP_HEADS):
        hs = slice(HEAD_DIM * h, HEAD_DIM * (h + 1))
        blocks += [vw_t[hs, :], jnp.where(first, jnp.broadcast_to(w_t[h:h + 1, :], first.shape), 0.0),
                   v_t[hs, :], jnp.where(first, 1.0, 0.0)]
    vt_ref[...] = jnp.concatenate(blocks, axis=0).astype(BF16)


V_AUG = 80
V_HEAD = 2 * V_AUG


def _key_weights(tile, slopes):
    kl = np.arange(tile, dtype=np.float64) - (tile - 1)
    w = np.stack([np.exp2(sl * LOG2E * kl) for sl in slopes])
    wide = np.repeat(w.T, HEAD_DIM, axis=1)
    w8 = np.zeros((8, tile)); w8[:len(slopes)] = w
    return jnp.asarray(wide.astype(np.float32)), jnp.asarray(w8.astype(np.float32))


def _diag_bias(tq, slopes):
    k = np.arange(tq)[:, None]
    q = np.arange(tq)[None, :]
    tabs = [np.where(k <= q, sl * LOG2E * k.astype(np.float64), MASKED) for sl in slopes]
    return jnp.asarray(np.stack(tabs).astype(np.float32))


AHEAD = 2


def _flash_update(s_ref, bias_ref, off, vt, m_ref, acc_ref, penalty=None):
    tq = s_ref.shape[1]
    for c in range(tq // LANES):
        cols = slice(LANES * c, LANES * (c + 1))
        s = s_ref[:, cols]
        if bias_ref is not None:
            s = s + bias_ref[:, cols]
        if penalty is not None:
            s = s + penalty
        m_tile = jnp.max(s, axis=0, keepdims=True)
        p = jnp.exp2(s - m_tile).astype(BF16)
        m_old = m_ref[0:1, cols]
        m_new = jnp.maximum(m_old, m_tile + off)
        m_ref[0:1, cols] = m_new
        pv = jnp.dot(vt, p, preferred_element_type=F32)
        acc_ref[:, cols] = jnp.exp2(m_old - m_new) * acc_ref[:, cols] + jnp.exp2(m_tile + off - m_new) * pv


def _diff_attn_kernel(lam_ref, qt_ref, k_ref, vt_ref, db_ref, sg_ref, bd_ref, o_ref,
                      qm_ref, acc_ref, m_ref, s_ref, *, tq, slopes, out_scale):
    qi = pl.program_id(1)
    n_half = 2 * GROUP_HEADS
    per_tile = LANES // DIFF_HALF
    row = lax.broadcasted_iota(jnp.int32, (LANES, tq), 0)
    for j in range(n_half):
        slab = qt_ref[LANES * (j // per_tile):LANES * (j // per_tile + 1), :].astype(F32)
        r0 = DIFF_HALF * (j % per_tile)
        qm_ref[j] = jnp.where((row >= r0) & (row < r0 + DIFF_HALF), slab, 0.0).astype(BF16)
    acc_ref[...] = jnp.zeros(acc_ref.shape, F32)
    m_ref[...] = jnp.full(m_ref.shape, M_INIT, F32)

    n_slots = s_ref.shape[0]

    def scores(kt, j):
        start = pl.multiple_of(kt * tq, tq)
        kj = k_ref[pl.ds(start, tq), LANES * (j // per_tile):LANES * (j // per_tile + 1)]
        s_ref[j % n_slots] = jnp.dot(kj, qm_ref[j], preferred_element_type=F32)

    def process(kt, j, diag):
        h = j // 2
        start = pl.multiple_of(kt * tq, tq)
        off = 0.0 if diag else ((kt - qi) * tq + (tq - 1)).astype(F32) * (slopes[h] * LOG2E)
        base = V_HEAD * h + (V_AUG if diag else 0)
        vt = vt_ref[base:base + V_AUG, pl.ds(start, tq)]
        _flash_update(s_ref.at[j % n_slots], db_ref.at[h] if diag else None, off, vt, m_ref.at[j], acc_ref.at[j])

    def step(kt, diag):
        for j in range(n_half):
            if j + AHEAD < n_half:
                scores(kt, j + AHEAD)
            elif not diag:
                scores(kt + 1, j + AHEAD - n_half)
            process(kt, j, diag)

    for j in range(AHEAD):
        scores(0, j)

    def body(kt, carry):
        step(kt, False)
        return carry

    lax.fori_loop(0, qi, body, 0)
    step(qi, True)

    lam = lam_ref[0, 0]
    heads = []
    for h in range(GROUP_HEADS):
        a0, a1 = acc_ref[2 * h], acc_ref[2 * h + 1]
        o0 = a0[0:HEAD_DIM, :] / a0[HEAD_DIM:HEAD_DIM + 1, :]
        o1 = a1[0:HEAD_DIM, :] / a1[HEAD_DIM:HEAD_DIM + 1, :]
        heads.append(o0 - lam * o1)
    o = jnp.concatenate(heads, axis=0).T
    ss = _seg_sum(o * o, bd_ref[...])
    o_ref[...] = o * lax.rsqrt(ss * (1.0 / HEAD_DIM) + EPS) * (sg_ref[...] * out_scale)


def _diff_attention(p3, q_gain, k_gain, lam_vecs, sub_gain, slopes, layer_idx, tq=256):
    b, s, _ = p3.shape
    tm = tq
    bd32 = _block_ones(GROUP_WIDTH, DIFF_HALF)
    bd64 = _block_ones(GROUP_WIDTH, HEAD_DIM)
    qg = jnp.tile(q_gain.astype(F32), 2 * GROUP_HEADS).reshape(1, GROUP_WIDTH)
    kg = jnp.tile(k_gain.astype(F32), 2 * GROUP_HEADS).reshape(1, GROUP_WIDTH)
    w_wide, w_t = _key_weights(tm, slopes)
    v_rows = GROUP_HEADS * V_HEAD
    qn, kn, vt = pl.pallas_call(
        _diff_prep_kernel,
        grid=(b, s // tm),
        in_specs=[pl.BlockSpec((None, tm, 768), lambda bi, i: (bi, i, COL_A // 768)),
                  pl.BlockSpec((1, GROUP_WIDTH), lambda bi, i: (0, 0)),
                  pl.BlockSpec((1, GROUP_WIDTH), lambda bi, i: (0, 0)),
                  pl.BlockSpec((GROUP_WIDTH, GROUP_WIDTH), lambda bi, i: (0, 0)),
                  pl.BlockSpec((tm, GROUP_WIDTH), lambda bi, i: (0, 0)),
                  pl.BlockSpec((8, tm), lambda bi, i: (0, 0))],
        out_specs=[pl.BlockSpec((None, GROUP_WIDTH, tm), lambda bi, i: (bi, 0, i)),
                   pl.BlockSpec((None, tm, GROUP_WIDTH), lambda bi, i: (bi, i, 0)),
                   pl.BlockSpec((None, v_rows, tm), lambda bi, i: (bi, 0, i))],
        out_shape=[jax.ShapeDtypeStruct((b, GROUP_WIDTH, s), BF16),
                   jax.ShapeDtypeStruct((b, s, GROUP_WIDTH), BF16),
                   jax.ShapeDtypeStruct((b, v_rows, s), BF16)],
        compiler_params=_cparams(("arbitrary", "arbitrary")),
        name="diff_prep",
    )(p3, qg, kg, bd32, w_wide, w_t)

    lam_init = 0.8 - 0.6 * math.exp(-0.3 * layer_idx)
    lv = lam_vecs.astype(F32)
    lam = (jnp.exp(jnp.dot(lv[0], lv[1])) - jnp.exp(jnp.dot(lv[2], lv[3])) + lam_init).reshape(1, 1)
    kern = functools.partial(_diff_attn_kernel, tq=tq, slopes=tuple(float(v) for v in slopes),
                             out_scale=1.0 - lam_init)
    return pl.pallas_call(
        kern,
        grid=(b, s // tq),
        in_specs=[pl.BlockSpec(memory_space=pltpu.SMEM),
                  pl.BlockSpec((None, GROUP_WIDTH, tq), lambda bi, i: (bi, 0, i)),
                  pl.BlockSpec((None, s, GROUP_WIDTH), lambda bi, i: (bi, 0, 0)),
                  pl.BlockSpec((None, v_rows, s), lambda bi, i: (bi, 0, 0)),
                  pl.BlockSpec((GROUP_HEADS, tq, tq), lambda bi, i: (0, 0, 0)),
                  pl.BlockSpec((1, GROUP_WIDTH), lambda bi, i: (0, 0)),
                  pl.BlockSpec((GROUP_WIDTH, GROUP_WIDTH), lambda bi, i: (0, 0))],
        out_specs=pl.BlockSpec((None, tq, GROUP_WIDTH), lambda bi, i: (bi, i, 0)),
        out_shape=jax.ShapeDtypeStruct((b, s, GROUP_WIDTH), F32),
        scratch_shapes=[pltpu.VMEM((2 * GROUP_HEADS, LANES, tq), BF16),
                        pltpu.VMEM((2 * GROUP_HEADS, V_AUG, tq), F32),
                        pltpu.VMEM((2 * GROUP_HEADS, 8, tq), F32),
                        pltpu.VMEM((2 * AHEAD, tq, tq), F32)],
        compiler_params=_cparams(("arbitrary", "arbitrary")),
        name="diff_attn",
    )(lam, qn, kn, vt, _diag_bias(tq, slopes), sub_gain.astype(F32).reshape(1, GROUP_WIDTH), bd64)


MLSTM_CONV = 4
CONV_HALO = 8
AUG = 128


def _tril_sum(x, tril):
    hi = x.astype(BF16)
    lo = (x - hi.astype(F32)).astype(BF16)
    return jnp.dot(tril, hi, preferred_element_type=F32) + jnp.dot(tril, lo, preferred_element_type=F32)


def _log_sigmoid(x):
    return jnp.minimum(x, 0.0) - jnp.log1p(jnp.exp(-jnp.abs(x)))


def _sigmoid(x):
    return 1.0 / (1.0 + jnp.exp(-x))


def _mlstm_kernel(q_ref, k_ref, v_ref, o_ref, g_ref, cw_ref, gb_ref, og_ref, bd_ref, y_ref,
                  halo_ref, c_ref, m_ref, *, L):
    ci = pl.program_id(1)

    @pl.when(ci == 0)
    def _():
        halo_ref[0:CONV_HALO, :] = jnp.zeros((CONV_HALO, 2 * GROUP_WIDTH), F32)
        c_ref[...] = jnp.zeros_like(c_ref)
        m_ref[...] = jnp.zeros_like(m_ref)

    halo_ref[CONV_HALO:CONV_HALO + L, 0:GROUP_WIDTH] = q_ref[...]
    halo_ref[CONV_HALO:CONV_HALO + L, GROUP_WIDTH:2 * GROUP_WIDTH] = k_ref[...]
    conv = jnp.zeros((L, 2 * GROUP_WIDTH), F32)
    for j in range(MLSTM_CONV):
        start = CONV_HALO - (MLSTM_CONV - 1) + j
        conv = conv + halo_ref[start:start + L, :] * cw_ref[j:j + 1, :]
    halo_ref[0:CONV_HALO, :] = halo_ref[L:L + CONV_HALO, :]
    qk = conv * _sigmoid(conv)
    q = qk[:, 0:GROUP_WIDTH] * (HEAD_DIM ** -0.5)
    k = qk[:, GROUP_WIDTH:2 * GROUP_WIDTH]
    v = v_ref[...]

    gates = g_ref[...] + gb_ref[...]
    ri = lax.broadcasted_iota(jnp.int32, (L, L), 0)
    cj = lax.broadcasted_iota(jnp.int32, (L, L), 1)
    causal = ri >= cj
    tril = jnp.where(causal, 1.0, 0.0).astype(BF16)
    gcum = _tril_sum(_log_sigmoid(gates), tril)
    gcum_t = gcum.T
    gates_t = gates.T
    lane_aug = lax.broadcasted_iota(jnp.int32, (L, AUG), 1)
    m_all = m_ref[...]

    outs = []
    for h in range(GROUP_HEADS):
        hs = slice(HEAD_DIM * h, HEAD_DIM * (h + 1))
        q_h, k_h = q[:, hs], k[:, hs]
        v_aug = jnp.where(lane_aug == HEAD_DIM, 1.0,
                          jnp.concatenate([v[:, hs], jnp.zeros((L, AUG - HEAD_DIM), F32)], axis=1))
        g_col = gcum[:, 4 + h:5 + h]
        li_col = gates[:, h:h + 1]
        g_row = gcum_t[4 + h:5 + h, :]
        li_row = gates_t[h:h + 1, :]
        g_last = g_row[:, L - 1:L]
        m_prev = m_all[:, h:h + 1]
        log_d = jnp.where(causal, g_col - g_row + li_row, NEG_INF)
        log_inter = g_col + m_prev
        m_t = jnp.maximum(log_inter, jnp.max(log_d, axis=1, keepdims=True))
        s_qk = lax.dot_general(q_h.astype(BF16), k_h.astype(BF16), (((1,), (1,)), ((), ())),
                               preferred_element_type=F32)
        w_intra = s_qk * jnp.exp(log_d - m_t)
        w_inter = jnp.exp(log_inter - m_t)
        c_aug = c_ref[h]
        num = (w_inter * jnp.dot(q_h.astype(BF16), c_aug.astype(BF16), preferred_element_type=F32)
               + jnp.dot(w_intra.astype(BF16), v_aug.astype(BF16), preferred_element_type=F32))
        den = num[:, HEAD_DIM:HEAD_DIM + 1]
        outs.append(num[:, 0:HEAD_DIM] / jnp.maximum(jnp.abs(den), jnp.exp(-m_t)))

        log_a = g_last - g_col + li_col
        m_new = jnp.maximum(g_last + m_prev, jnp.max(log_a, axis=0, keepdims=True))
        a_col = jnp.exp(log_a - m_new)
        decay = jnp.exp(g_last + m_prev - m_new)
        ak_t = (k_h * a_col).T.astype(BF16)
        c_ref[h] = decay * c_aug + jnp.dot(ak_t, v_aug.astype(BF16), preferred_element_type=F32)
        m_ref[:, h:h + 1] = m_new

    hcat = jnp.concatenate(outs, axis=1)
    ss = _seg_sum(hcat * hcat, bd_ref[...])
    y_ref[...] = hcat * lax.rsqrt(ss * (1.0 / HEAD_DIM) + EPS) * og_ref[...] * _sigmoid(o_ref[...])


def _mlstm(p3, conv_w, gate_bias, out_gain, L=128):
    b, s, _ = p3.shape
    gb = jnp.zeros((1, LANES), F32).at[0, 0:GROUP_HEADS].set(gate_bias[0]).at[0, GROUP_HEADS:2 * GROUP_HEADS].set(gate_bias[1])
    col = lambda off: (lambda bi, i: (bi, i, off // GROUP_WIDTH))
    return pl.pallas_call(
        functools.partial(_mlstm_kernel, L=L),
        grid=(b, s // L),
        in_specs=[pl.BlockSpec((None, L, GROUP_WIDTH), col(COL_B)),
                  pl.BlockSpec((None, L, GROUP_WIDTH), col(COL_B + 256)),
                  pl.BlockSpec((None, L, GROUP_WIDTH), col(COL_B + 512)),
                  pl.BlockSpec((None, L, GROUP_WIDTH), col(COL_B + 768)),
                  pl.BlockSpec((None, L, LANES), lambda bi, i: (bi, i, COL_G // LANES)),
                  pl.BlockSpec((MLSTM_CONV, 2 * GROUP_WIDTH), lambda bi, i: (0, 0)),
                  pl.BlockSpec((1, LANES), lambda bi, i: (0, 0)),
                  pl.BlockSpec((1, GROUP_WIDTH), lambda bi, i: (0, 0)),
                  pl.BlockSpec((GROUP_WIDTH, GROUP_WIDTH), lambda bi, i: (0, 0))],
        out_specs=pl.BlockSpec((None, L, GROUP_WIDTH), lambda bi, i: (bi, i, 0)),
        out_shape=jax.ShapeDtypeStruct((b, s, GROUP_WIDTH), F32),
        scratch_shapes=[pltpu.VMEM((CONV_HALO + L, 2 * GROUP_WIDTH), F32),
                        pltpu.VMEM((GROUP_HEADS, HEAD_DIM, AUG), F32),
                        pltpu.VMEM((1, LANES), F32)],
        compiler_params=_cparams(("arbitrary", "arbitrary")),
        name="mlstm",
    )(p3, p3, p3, p3, p3, conv_w.astype(F32), gb, out_gain.astype(F32).reshape(1, GROUP_WIDTH),
      _block_ones(GROUP_WIDTH, HEAD_DIM))


HGRN_CHUNK = 16
LB_FLOOR = 1e-30


def _hgrn_kernel(q_ref, f_ref, i_ref, g_ref, lb_ref, og_ref, bd_ref, y_ref,
                 lf_s, kk_s, vv_s, st_ref, *, TL):
    ci = pl.program_id(1)
    C = HGRN_CHUNK
    W = GROUP_WIDTH

    @pl.when(ci == 0)
    def _():
        lf_s[0:C, :] = jnp.zeros((C, W), F32)
        kk_s[0:C, :] = jnp.zeros((C, W), F32)
        vv_s[0:C, :] = jnp.zeros((C, W), F32)
        st_ref[...] = jnp.zeros_like(st_ref)

    z = f_ref[...]
    a = lb_ref[0:1, :]
    c = lb_ref[1:2, :] + _log_sigmoid(z)
    mx = jnp.maximum(a, c)
    lf = mx + jnp.log1p(jnp.exp(-jnp.abs(a - c)))
    kk = lb_ref[2:3, :] * _sigmoid(-z) + lb_ref[3:4, :]
    qx = q_ref[...]
    qs = qx * _sigmoid(qx)
    vv = i_ref[...]
    lf_s[C:C + TL, :] = lf
    kk_s[C:C + TL, :] = kk
    vv_s[C:C + TL, :] = vv

    bd = bd_ref[...]
    row = lax.broadcasted_iota(jnp.int32, (TL, W), 0)
    rmod = row & (C - 1)

    acc = jnp.zeros((TL, W), F32)
    dsum = jnp.zeros((TL, W), F32)
    for delta in range(C):
        if delta > 0:
            dsum = dsum + lf_s[C - (delta - 1):C - (delta - 1) + TL, :]
        x = qs * kk_s[C - delta:C - delta + TL, :] * jnp.exp(dsum)
        x = jnp.where(rmod >= delta, x, 0.0)
        att = jnp.dot(x.astype(BF16), bd, preferred_element_type=F32)
        acc = acc + att * vv_s[C - delta:C - delta + TL, :]

    ri = lax.broadcasted_iota(jnp.int32, (TL, TL), 0)
    cj = lax.broadcasted_iota(jnp.int32, (TL, TL), 1)
    same = (ri // C) == (cj // C)
    tril = jnp.where(same & (ri >= cj), 1.0, 0.0).astype(BF16)
    ones = jnp.where(same, 1.0, 0.0).astype(BF16)
    bcum = _tril_sum(lf, tril)
    blast = _tril_sum(lf, ones)
    q_a = (qs * jnp.exp(bcum)).astype(BF16)
    k_b = (kk * jnp.exp(blast - bcum)).astype(BF16)
    dec = jnp.exp(blast)
    vv_b = vv.astype(BF16)
    hmask = bd.astype(F32)
    state = st_ref[...]
    inters = []
    for ch in range(TL // C):
        r = slice(ch * C, (ch + 1) * C)
        inters.append(lax.dot_general(q_a[r], state.astype(BF16), (((1,), (1,)), ((), ())),
                                      preferred_element_type=F32))
        upd = lax.dot_general(vv_b[r], k_b[r], (((0,), (0,)), ((), ())), preferred_element_type=F32)
        state = state * dec[ch * C:ch * C + 1, :] + upd * hmask
    st_ref[...] = state
    o = acc + jnp.concatenate(inters, axis=0)
    ss = _seg_sum(o * o, bd)
    y_ref[...] = o * lax.rsqrt(ss * (1.0 / HEAD_DIM) + EPS) * og_ref[...] * _sigmoid(g_ref[...])


def _hgrn2(p3, lower_bound, out_gain, TL=256):
    b, s, _ = p3.shape
    lb = lower_bound.astype(F32)
    lbf = jnp.maximum(lb, LB_FLOOR)
    lbp = jnp.stack([jnp.log(lbf), jnp.log1p(-lb), 1.0 - lb, lb - lbf])
    col = lambda off: (lambda bi, i: (bi, i, off // GROUP_WIDTH))
    C = HGRN_CHUNK
    return pl.pallas_call(
        functools.partial(_hgrn_kernel, TL=TL),
        grid=(b, s // TL),
        in_specs=[pl.BlockSpec((None, TL, GROUP_WIDTH), col(COL_C)),
                  pl.BlockSpec((None, TL, GROUP_WIDTH), col(COL_C + 256)),
                  pl.BlockSpec((None, TL, GROUP_WIDTH), col(COL_C + 512)),
                  pl.BlockSpec((None, TL, GROUP_WIDTH), col(COL_C + 768)),
                  pl.BlockSpec((4, GROUP_WIDTH), lambda bi, i: (0, 0)),
                  pl.BlockSpec((1, GROUP_WIDTH), lambda bi, i: (0, 0)),
                  pl.BlockSpec((GROUP_WIDTH, GROUP_WIDTH), lambda bi, i: (0, 0))],
        out_specs=pl.BlockSpec((None, TL, GROUP_WIDTH), lambda bi, i: (bi, i, 0)),
        out_shape=jax.ShapeDtypeStruct((b, s, GROUP_WIDTH), F32),
        scratch_shapes=[pltpu.VMEM((C + TL, GROUP_WIDTH), F32),
                        pltpu.VMEM((C + TL, GROUP_WIDTH), F32),
                        pltpu.VMEM((C + TL, GROUP_WIDTH), F32),
                        pltpu.VMEM((GROUP_WIDTH, GROUP_WIDTH), F32)],
        compiler_params=_cparams(("arbitrary", "arbitrary")),
        name="hgrn2",
    )(p3, p3, p3, p3, lbp, out_gain.astype(F32).reshape(1, GROUP_WIDTH), _block_ones(GROUP_WIDTH, HEAD_DIM))


CMP_LEN = 32
CMP_STRIDE = 16
SEL_BLOCK = 64
SEL_TOPK = 16
WINDOW = 512
GATE_ROW = 8


def _rms_rows(x, gain):
    return x * lax.rsqrt(jnp.mean(x * x, axis=-1, keepdims=True) + EPS) * gain


NSA_V_ROWS = (GROUP_HEADS + 1) * V_AUG


def _aug_shared_values(v, w_wide, w_t):
    tm = v.shape[0]
    vw_t = (jnp.concatenate([v] * GROUP_HEADS, axis=1) * w_wide).T
    first = lax.broadcasted_iota(jnp.int32, (V_AUG - HEAD_DIM, tm), 0) == 0
    blocks = []
    for h in range(GROUP_HEADS):
        blocks += [vw_t[HEAD_DIM * h:HEAD_DIM * (h + 1), :],
                   jnp.where(first, jnp.broadcast_to(w_t[h:h + 1, :], first.shape), 0.0)]
    blocks += [v.T, jnp.where(first, 1.0, 0.0)]
    return jnp.concatenate(blocks, axis=0).astype(BF16)


def _nsa_prep_kernel(q_ref, kv_ref, g_ref, qg_ref, kg_ref, bd_ref, w_ref, wt_ref,
                     qt_ref, kc_ref, vc_ref, ks_ref, vst_ref, kw_ref, vwt_ref, gt_ref):
    ti = pl.program_id(1)
    q = q_ref[...]
    ss = _seg_sum(q * q, bd_ref[...])
    qn = q * lax.rsqrt(ss * (1.0 / HEAD_DIM) + EPS) * qg_ref[...] * (HEAD_DIM ** -0.5 * LOG2E)
    qt_ref[...] = qn.T.astype(BF16)
    kv = kv_ref[...]
    tm = kv.shape[0]
    kc_ref[...] = kv[:, 0:64]
    vc_ref[...] = kv[:, 64:128]
    ks = _rms_rows(kv[:, 128:192], kg_ref[1:2, :])
    lane = lax.broadcasted_iota(jnp.int32, (tm, LANES), 1)
    blk = jnp.right_shift(ti * tm + lax.broadcasted_iota(jnp.int32, (tm, LANES), 0), SEL_BLOCK.bit_length() - 1)
    onehot = jnp.where(lane == blk + HEAD_DIM, 1.0, 0.0)
    ks_ref[...] = jnp.where(lane < HEAD_DIM, jnp.concatenate([ks, ks], axis=1), onehot).astype(BF16)
    kw = _rms_rows(kv[:, 256:320], kg_ref[2:3, :])
    kw_ref[...] = jnp.concatenate([kw, jnp.zeros_like(kw)], axis=1).astype(BF16)
    vst_ref[...] = _aug_shared_values(kv[:, 192:256], w_ref[...], wt_ref[...])
    vwt_ref[...] = _aug_shared_values(kv[:, 320:384], w_ref[...], wt_ref[...])
    gt_ref[...] = _sigmoid(g_ref[...]).T


def _gelu_tanh(x):
    return 0.5 * x * (1.0 + jnp.tanh(math.sqrt(2.0 / math.pi) * (x + 0.044715 * x * x * x)))


def _nsa_cmp_kernel(kr_ref, vr_ref, w1_ref, w2_ref, pos_ref, kg_ref, kc_ref, vct_ref, sh_ref, *, n_rows):
    half = CMP_STRIDE * HEAD_DIM
    sh_ref[n_rows:n_rows + 8, :] = jnp.zeros((8, HEAD_DIM), F32)
    outs = []
    for j, x_ref in enumerate((kr_ref, vr_ref)):
        r = x_ref[...].astype(BF16)
        w1 = w1_ref[j]
        first = jnp.dot(r, w1[0:half, :], preferred_element_type=F32)
        sh_ref[0:n_rows, :] = jnp.dot(r, w1[half:2 * half, :], preferred_element_type=F32)
        pos8 = jnp.broadcast_to(pos_ref[j], (8, 2 * half))
        posw = jnp.dot(pos8, w1.astype(F32), preferred_element_type=F32)[0:1, :]
        hdn = _gelu_tanh(first + sh_ref[1:n_rows + 1, :] + posw)
        outs.append(jnp.dot(hdn, w2_ref[j].astype(F32), preferred_element_type=F32))
    kc_ref[...] = _rms_rows(outs[0], kg_ref[0:1, :]).astype(BF16)
    vct_ref[...] = outs[1].T.astype(BF16)


def _nsa_attn_kernel(qt_ref, kc_ref, vct_ref, ks_ref, vst_ref, kw_ref, vwt_ref, gt_ref, ov_ref, db_ref, lo_ref,
                     og_ref, bd_ref, y_ref, qa_ref, acc_s, m_s, acc_w, m_w, ss_ref, sw_ref,
                     *, tq, slopes, n_cmp):
    qi = pl.program_id(1)
    n_rows = kc_ref.shape[0]
    n_sel = ov_ref.shape[0]
    qpos = qi * tq + lax.broadcasted_iota(jnp.int32, (1, tq), 1)

    nio = lax.broadcasted_iota(jnp.int32, (n_rows, tq), 0)
    dist_c = qpos - (nio * CMP_STRIDE + (CMP_LEN - 1))
    valid_c = (dist_c >= 0) & (nio < n_cmp)
    dist_cf = dist_c.astype(F32)
    kc = kc_ref[...]
    vct = vct_ref[...]
    o_cmp = []
    p_sum = jnp.zeros((n_rows, tq), F32)
    for h in range(GROUP_HEADS):
        qh = qt_ref[HEAD_DIM * h:HEAD_DIM * (h + 1), :]
        s = jnp.dot(kc, qh, preferred_element_type=F32) - (slopes[h] * LOG2E) * dist_cf
        s = jnp.where(valid_c, s, MASKED)
        m = jnp.maximum(jnp.max(s, axis=0, keepdims=True), M_INIT)
        e = jnp.exp2(s - m)
        p = e / jnp.maximum(jnp.sum(e, axis=0, keepdims=True), 1e-30)
        p_sum = p_sum + p
        o_cmp.append(jnp.dot(vct, p.astype(BF16), preferred_element_type=F32))

    p_hi = p_sum.astype(BF16)
    p_lo = (p_sum - p_hi.astype(F32)).astype(BF16)
    ov = ov_ref[...]
    imp = jnp.dot(ov, p_hi, preferred_element_type=F32) + jnp.dot(ov, p_lo, preferred_element_type=F32)
    jio = lax.broadcasted_iota(jnp.int32, (n_sel, tq), 0)
    qblk = jnp.right_shift(qpos, SEL_BLOCK.bit_length() - 1)
    imp = jnp.where(jio <= qblk, imp, -1.0)
    imp = jnp.where((jio == 0) | (jio == qblk) | (jio == qblk - 1), 2.0, imp)
    rank = jnp.zeros((n_sel, tq), F32)
    for i in range(n_sel):
        row = imp[i:i + 1, :]
        beats = (row > imp) | ((row == imp) & (jio > i))
        rank = rank + jnp.where(beats, 1.0, 0.0)
    bsel = jnp.where(rank < float(min(SEL_TOPK, n_sel)), 0.0, MASKED)
    if n_sel < LANES - HEAD_DIM:
        bsel = jnp.concatenate([bsel, jnp.zeros((LANES - HEAD_DIM - n_sel, tq), F32)], axis=0)
    for h in range(GROUP_HEADS):
        qa_ref[h, 0:HEAD_DIM, :] = qt_ref[HEAD_DIM * h:HEAD_DIM * (h + 1), :]
        qa_ref[h, HEAD_DIM:LANES, :] = bsel.astype(BF16)
    for ref, val in ((acc_s, 0.0), (acc_w, 0.0), (m_s, M_INIT), (m_w, M_INIT)):
        ref[...] = jnp.full(ref.shape, val, F32)

    def tile_off(kt, h):
        return ((kt - qi) * tq + (tq - 1)).astype(F32) * (slopes[h] * LOG2E)

    def values(vt_ref, kt, h, diag):
        base = GROUP_HEADS * V_AUG if diag else V_AUG * h
        return vt_ref[base:base + V_AUG, pl.ds(pl.multiple_of(kt * tq, tq), tq)]

    def sel_scores(kt, h):
        kb = ks_ref[pl.ds(pl.multiple_of(kt * tq, tq), tq), :]
        ss_ref[h] = jnp.dot(kb, qa_ref[h], preferred_element_type=F32)

    def sel_step(kt, diag):
        for h in range(GROUP_HEADS):
            if h + AHEAD < GROUP_HEADS:
                sel_scores(kt, h + AHEAD)
            elif not diag:
                sel_scores(kt + 1, h + AHEAD - GROUP_HEADS)
            _flash_update(ss_ref.at[h], db_ref.at[h] if diag else None, 0.0 if diag else tile_off(kt, h),
                          values(vst_ref, kt, h, diag), m_s.at[h], acc_s.at[h])

    for h in range(AHEAD):
        sel_scores(0, h)

    def body(kt, carry):
        sel_step(kt, False)
        return carry

    lax.fori_loop(0, qi, body, 0)
    sel_step(qi, True)

    def win_step(back, bias_of, diag):
        kt = jnp.maximum(qi - back, 0)
        penalty = None if diag else jnp.where(qi >= back, 0.0, MASKED)
        kb = kw_ref[pl.ds(pl.multiple_of(kt * tq, tq), tq), :]

        def win_scores(h):
            sw_ref[h] = jnp.dot(kb, qa_ref[h], preferred_element_type=F32)

        for h in range(AHEAD):
            win_scores(h)
        for h in range(GROUP_HEADS):
            if h + AHEAD < GROUP_HEADS:
                win_scores(h + AHEAD)
            _flash_update(sw_ref.at[h], bias_of(h), 0.0 if diag else tile_off(kt, h),
                          values(vwt_ref, kt, h, diag), m_w.at[h], acc_w.at[h], penalty)

    win_step(2, lambda h: lo_ref, False)
    win_step(1, lambda h: None, False)
    win_step(0, lambda h: db_ref.at[h], True)

    gt = gt_ref[...]
    heads = []
    for h in range(GROUP_HEADS):
        a_s, a_w = acc_s[h], acc_w[h]
        o_sel = a_s[0:HEAD_DIM, :] / a_s[HEAD_DIM:HEAD_DIM + 1, :]
        o_win = a_w[0:HEAD_DIM, :] / a_w[HEAD_DIM:HEAD_DIM + 1, :]
        g = GATE_ROW + 3 * h
        heads.append(gt[g:g + 1, :] * o_cmp[h] + gt[g + 1:g + 2, :] * o_sel + gt[g + 2:g + 3, :] * o_win)
    o = jnp.concatenate(heads, axis=0).T
    ss = _seg_sum(o * o, bd_ref[...])
    y_ref[...] = o * lax.rsqrt(ss * (1.0 / HEAD_DIM) + EPS) * og_ref[...]


def _nsa(p3, cmp_pos, cmp_w1, cmp_w2, q_gain, k_gain, out_gain, slopes, tq=256):
    b, s, _ = p3.shape
    tm = tq
    n_rows = s // CMP_STRIDE
    n_cmp = (s - CMP_LEN) // CMP_STRIDE + 1
    n_sel = s // SEL_BLOCK
    bd64 = _block_ones(GROUP_WIDTH, HEAD_DIM)
    qg = jnp.tile(q_gain.astype(F32), GROUP_HEADS).reshape(1, GROUP_WIDTH)
    kg = k_gain.astype(F32)
    tok = lambda w: pl.BlockSpec((None, tm, w), lambda bi, i: (bi, i, 0))
    tok_t = lambda w: pl.BlockSpec((None, w, tm), lambda bi, i: (bi, 0, i))
    full = lambda r, c: pl.BlockSpec((r, c), lambda bi, i: (0, 0))
    assert n_sel <= LANES - HEAD_DIM, "block one-hot lanes hold at most 64 selection blocks"
    assert WINDOW == 2 * tq and tq % SEL_BLOCK == 0, "window branch walks exactly the key tiles qi-2, qi-1, qi"
    w_wide, w_t = _key_weights(tm, slopes)
    qt, kc, vc, ks, vst, kw, vwt, gt = pl.pallas_call(
        _nsa_prep_kernel,
        grid=(b, s // tm),
        in_specs=[pl.BlockSpec((None, tm, GROUP_WIDTH), lambda bi, i: (bi, i, COL_D // GROUP_WIDTH)),
                  pl.BlockSpec((None, tm, 384), lambda bi, i: (bi, i, (COL_D + GROUP_WIDTH) // 384)),
                  pl.BlockSpec((None, tm, LANES), lambda bi, i: (bi, i, COL_G // LANES)),
                  full(1, GROUP_WIDTH), full(3, HEAD_DIM), full(GROUP_WIDTH, GROUP_WIDTH),
                  full(tm, GROUP_WIDTH), full(8, tm)],
        out_specs=[tok_t(GROUP_WIDTH), tok(HEAD_DIM), tok(HEAD_DIM), tok(LANES), tok_t(NSA_V_ROWS),
                   tok(LANES), tok_t(NSA_V_ROWS), tok_t(LANES)],
        out_shape=[jax.ShapeDtypeStruct((b, GROUP_WIDTH, s), BF16),
                   jax.ShapeDtypeStruct((b, s, HEAD_DIM), F32),
                   jax.ShapeDtypeStruct((b, s, HEAD_DIM), F32),
                   jax.ShapeDtypeStruct((b, s, LANES), BF16),
                   jax.ShapeDtypeStruct((b, NSA_V_ROWS, s), BF16),
                   jax.ShapeDtypeStruct((b, s, LANES), BF16),
                   jax.ShapeDtypeStruct((b, NSA_V_ROWS, s), BF16),
                   jax.ShapeDtypeStruct((b, LANES, s), F32)],
        compiler_params=_cparams(("arbitrary", "arbitrary")),
        name="nsa_prep",
    )(p3, p3, p3, qg, kg, bd64, w_wide, w_t)

    row_w = CMP_STRIDE * HEAD_DIM
    kcmp, vcmp_t = pl.pallas_call(
        functools.partial(_nsa_cmp_kernel, n_rows=n_rows),
        grid=(b,),
        in_specs=[pl.BlockSpec((None, n_rows, row_w), lambda bi: (bi, 0, 0)),
                  pl.BlockSpec((None, n_rows, row_w), lambda bi: (bi, 0, 0)),
                  pl.BlockSpec((2, 2 * row_w, HEAD_DIM), lambda bi: (0, 0, 0)),
                  pl.BlockSpec((2, HEAD_DIM, HEAD_DIM), lambda bi: (0, 0, 0)),
                  pl.BlockSpec((2, 1, 2 * row_w), lambda bi: (0, 0, 0)),
                  pl.BlockSpec((3, HEAD_DIM), lambda bi: (0, 0))],
        out_specs=[pl.BlockSpec((None, n_rows, HEAD_DIM), lambda bi: (bi, 0, 0)),
                   pl.BlockSpec((None, HEAD_DIM, n_rows), lambda bi: (bi, 0, 0))],
        out_shape=[jax.ShapeDtypeStruct((b, n_rows, HEAD_DIM), BF16),
                   jax.ShapeDtypeStruct((b, HEAD_DIM, n_rows), BF16)],
        scratch_shapes=[pltpu.VMEM((n_rows + 8, HEAD_DIM), F32)],
        compiler_params=_cparams(("arbitrary",)),
        name="nsa_cmp",
    )(kc.reshape(b, n_rows, row_w), vc.reshape(b, n_rows, row_w), cmp_w1.astype(BF16), cmp_w2.astype(BF16),
      cmp_pos.astype(F32).reshape(2, 1, 2 * row_w), kg)

    cs = np.arange(n_rows)[:, None] * CMP_STRIDE
    ss = np.arange(n_sel)[None, :] * SEL_BLOCK
    overlap = np.clip(np.minimum(cs + CMP_LEN, ss + SEL_BLOCK) - np.maximum(cs, ss), 0, None) / CMP_LEN
    overlap[n_cmp:, :] = 0.0
    ov_t = jnp.asarray(overlap.T.astype(np.float32), BF16)

    seq = lambda r, c: pl.BlockSpec((None, r, c), lambda bi, i: (bi, 0, 0))
    kern = functools.partial(_nsa_attn_kernel, tq=tq, slopes=tuple(float(v) for v in slopes), n_cmp=n_cmp)
    return pl.pallas_call(
        kern,
        grid=(b, s // tq),
        in_specs=[pl.BlockSpec((None, GROUP_WIDTH, tq), lambda bi, i: (bi, 0, i)),
                  seq(n_rows, HEAD_DIM), seq(HEAD_DIM, n_rows),
                  seq(s, LANES), seq(NSA_V_ROWS, s), seq(s, LANES), seq(NSA_V_ROWS, s),
                  pl.BlockSpec((None, LANES, tq), lambda bi, i: (bi, 0, i)),
                  full(n_sel, n_rows),
                  pl.BlockSpec((GROUP_HEADS, tq, tq), lambda bi, i: (0, 0, 0)), full(tq, tq),
                  full(1, GROUP_WIDTH), full(GROUP_WIDTH, GROUP_WIDTH)],
        out_specs=pl.BlockSpec((None, tq, GROUP_WIDTH), lambda bi, i: (bi, i, 0)),
        out_shape=jax.ShapeDtypeStruct((b, s, GROUP_WIDTH), F32),
        scratch_shapes=[pltpu.VMEM((GROUP_HEADS, LANES, tq), BF16),
                        pltpu.VMEM((GROUP_HEADS, V_AUG, tq), F32), pltpu.VMEM((GROUP_HEADS, 8, tq), F32),
                        pltpu.VMEM((GROUP_HEADS, V_AUG, tq), F32), pltpu.VMEM((GROUP_HEADS, 8, tq), F32),
                        pltpu.VMEM((GROUP_HEADS, tq, tq), F32), pltpu.VMEM((GROUP_HEADS, tq, tq), F32)],
        compiler_params=_cparams(("arbitrary", "arbitrary")),
        name="nsa_attn",
    )(qt, kcmp, vcmp_t, ks, vst, kw, vwt, gt, ov_t, _diag_bias(tq, slopes), _window_low_bias(tq),
      out_gain.astype(F32).reshape(1, GROUP_WIDTH), bd64)


def _window_low_bias(tq):
    k = np.arange(tq)[:, None]
    q = np.arange(tq)[None, :]
    return jnp.asarray(np.where(k > q, 0.0, MASKED).astype(np.float32))


N_GROUPS = 4
EXPERTS_PER_GROUP = 8
N_EXPERTS = N_GROUPS * EXPERTS_PER_GROUP
D_EXPERT = 256
ROUTER_LANE0 = N_GROUPS


def _split3_dot(a, b_hi, b_lo):
    a_hi = a.astype(BF16)
    a_lo = (a - a_hi.astype(F32)).astype(BF16)
    return (jnp.dot(a_hi, b_hi, preferred_element_type=F32) + jnp.dot(a_lo, b_hi, preferred_element_type=F32)
            + jnp.dot(a_hi, b_lo, preferred_element_type=F32))


def _outproj_kernel(x_ref, ya_ref, yb_ref, yc_ref, yd_ref, w_ref, g_ref, wr_hi_ref, wr_lo_ref, br_ref,
                    xo_ref, tg_ref, route_ref, cnt_ref, run_ref):
    @pl.when(pl.program_id(0) == 0)
    def _():
        run_ref[...] = jnp.zeros_like(run_ref)

    acc = x_ref[...]
    for gi, y_ref in enumerate((ya_ref, yb_ref, yc_ref, yd_ref)):
        acc = acc + jnp.dot(y_ref[...].astype(BF16), w_ref[GROUP_WIDTH * gi:GROUP_WIDTH * (gi + 1), :],
                            preferred_element_type=F32)
    xo_ref[...] = acc
    ms = jnp.mean(acc * acc, axis=-1, keepdims=True)
    t = acc * lax.rsqrt(ms + EPS) * g_ref[...]
    tg_ref[:, 0:D_MODEL] = t

    logits = _split3_dot(t, wr_hi_ref[...], wr_lo_ref[...]) + br_ref[...]
    lane = lax.broadcasted_iota(jnp.int32, logits.shape, 1)
    lane_f = lane.astype(F32)
    big = float(LANES)
    is_g = lane < N_GROUPS
    gl = jnp.where(is_g, logits, MASKED)
    gmax = jnp.max(gl, axis=-1, keepdims=True)
    g_prob = 1.0 / jnp.sum(jnp.where(is_g, jnp.exp(gl - gmax), 0.0), axis=-1, keepdims=True)
    g_sel = jnp.min(jnp.where(is_g & (gl == gmax), lane_f, big), axis=-1, keepdims=True)
    lo = ROUTER_LANE0 + EXPERTS_PER_GROUP * g_sel
    in_grp = (lane_f >= lo) & (lane_f < lo + EXPERTS_PER_GROUP)
    el = jnp.where(in_grp, logits, MASKED)
    m1 = jnp.max(el, axis=-1, keepdims=True)
    i1 = jnp.min(jnp.where(in_grp & (el == m1), lane_f, big), axis=-1, keepdims=True)
    rest = in_grp & (lane_f != i1)
    el2 = jnp.where(rest, logits, MASKED)
    m2 = jnp.max(el2, axis=-1, keepdims=True)
    i2 = jnp.min(jnp.where(rest & (el2 == m2), lane_f, big), axis=-1, keepdims=True)
    r = jnp.exp(m2 - m1)
    w1 = g_prob / (1.0 + r)
    w2 = g_prob * r / (1.0 + r)
    tg_ref[:, D_MODEL:D_MODEL + LANES] = jnp.where(lane_f == i1, w1, 0.0) + jnp.where(lane_f == i2, w2, 0.0)

    tm = logits.shape[0]
    onehot = jnp.where(lane_f == g_sel, 1.0, 0.0)
    ri = lax.broadcasted_iota(jnp.int32, (tm, tm), 0)
    ci = lax.broadcasted_iota(jnp.int32, (tm, tm), 1)
    before = jnp.where(ri > ci, 1.0, 0.0).astype(BF16)
    prefix = jnp.dot(before, onehot.astype(BF16), preferred_element_type=F32)
    rank = jnp.sum(onehot * (run_ref[...] + prefix), axis=-1, keepdims=True)
    route_ref[...] = jnp.where(lane == 0, g_sel, 0.0) + jnp.where(lane == 1, rank, 0.0)
    run_ref[...] = run_ref[...] + jnp.sum(onehot, axis=0, keepdims=True)
    cnt_ref[...] = run_ref[...]


T_AUG = D_MODEL + LANES


def _outproj(x2d, ys, w_out, ffn_gain, w_group, b_group, w_expert, b_expert, tm=256):
    t = x2d.shape[0]
    wr = jnp.zeros((D_MODEL, LANES), F32).at[:, 0:N_GROUPS].set(w_group).at[:, ROUTER_LANE0:ROUTER_LANE0 + N_EXPERTS].set(w_expert)
    wr_hi = wr.astype(BF16)
    wr_lo = (wr - wr_hi.astype(F32)).astype(BF16)
    br = jnp.zeros((1, LANES), F32).at[0, 0:N_GROUPS].set(b_group).at[0, ROUTER_LANE0:ROUTER_LANE0 + N_EXPERTS].set(b_expert)
    row = lambda w: pl.BlockSpec((tm, w), lambda i: (i, 0))
    full = lambda r, c: pl.BlockSpec((r, c), lambda i: (0, 0))
    return pl.pallas_call(
        _outproj_kernel,
        grid=(t // tm,),
        in_specs=[row(D_MODEL), row(GROUP_WIDTH), row(GROUP_WIDTH), row(GROUP_WIDTH), row(GROUP_WIDTH),
                  full(D_MODEL, D_MODEL), full(1, D_MODEL), full(D_MODEL, LANES), full(D_MODEL, LANES), full(1, LANES)],
        out_specs=[row(D_MODEL), row(T_AUG), row(LANES), full(1, LANES)],
        out_shape=[jax.ShapeDtypeStruct((t, D_MODEL), F32),
                   jax.ShapeDtypeStruct((t, T_AUG), F32),
                   jax.ShapeDtypeStruct((t, LANES), F32),
                   jax.ShapeDtypeStruct((1, LANES), F32)],
        scratch_shapes=[pltpu.VMEM((1, LANES), F32)],
        compiler_params=_cparams(("arbitrary",)),
        name="outproj_router",
    )(x2d, *ys, w_out.astype(BF16), ffn_gain.reshape(1, D_MODEL), wr_hi, wr_lo, br)


MOE_TILE = 512


def _row_copies(n, make_copy):
    def start(r, carry):
        make_copy(r).start()
        return carry

    def wait(r, carry):
        make_copy(r).wait()
        return carry

    lax.fori_loop(0, n, start, 0)
    lax.fori_loop(0, n, wait, 0)


def _moe_scatter_kernel(pos_ref, tg_ref, xs_in_ref, xs_ref, sem):
    del xs_in_ref
    tm = tg_ref.shape[0]
    base = pl.program_id(0) * tm
    _row_copies(tm, lambda r: pltpu.make_async_copy(tg_ref.at[pl.ds(r, 1)],
                                                    xs_ref.at[pl.ds(pos_ref[base + r], 1)], sem))


def _moe_expert_kernel(grp_ref, valid_ref, xs_ref, wg_ref, wu_ref, wd_ref, y_ref, acc_ref):
    i, e = pl.program_id(0), pl.program_id(1)

    @pl.when(e == 0)
    def _():
        acc_ref[...] = jnp.zeros_like(acc_ref)

    @pl.when(valid_ref[i] == 1)
    def _():
        x = xs_ref[:, 0:D_MODEL].astype(BF16)
        gate = xs_ref[:, D_MODEL:T_AUG]
        lane = lax.broadcasted_iota(jnp.int32, gate.shape, 1)
        col = ROUTER_LANE0 + grp_ref[i] * EXPERTS_PER_GROUP + e
        w = jnp.sum(jnp.where(lane == col, gate, 0.0), axis=-1, keepdims=True)
        a = jnp.dot(x, wg_ref[...], preferred_element_type=F32)
        u = jnp.dot(x, wu_ref[...], preferred_element_type=F32)
        act = a * _sigmoid(a) * u * w
        acc_ref[...] += jnp.dot(act.astype(BF16), wd_ref[...], preferred_element_type=F32)

    @pl.when(e == EXPERTS_PER_GROUP - 1)
    def _():
        y_ref[...] = acc_ref[...]


def _moe_combine_kernel(pos_ref, x_ref, ys_ref, o_ref, buf_ref, sem):
    tm = x_ref.shape[0]
    base = pl.program_id(0) * tm
    _row_copies(tm, lambda r: pltpu.make_async_copy(ys_ref.at[pl.ds(pos_ref[base + r], 1)],
                                                    buf_ref.at[pl.ds(r, 1)], sem))
    o_ref[...] = x_ref[...] + buf_ref[...]


def _moe_routed(x2d, tg, route, counts, w_gate, w_up, w_down, tm=512):
    t = x2d.shape[0]
    te = MOE_TILE
    n_tiles = t // te + N_GROUPS
    n_rows = n_tiles * te
    grp = route[:, 0].astype(jnp.int32)
    rank = route[:, 1].astype(jnp.int32)
    cnt = counts[0, 0:N_GROUPS].astype(jnp.int32)
    padded = ((cnt + te - 1) // te) * te
    ends = jnp.cumsum(padded)
    pos = (ends - padded)[grp] + rank
    starts = jnp.arange(n_tiles, dtype=jnp.int32) * te
    tile_grp = jnp.minimum(jnp.sum((starts[:, None] >= ends[None, :]).astype(jnp.int32), axis=1), N_GROUPS - 1)
    tile_valid = (starts < ends[-1]).astype(jnp.int32)

    xs = pl.pallas_call(
        _moe_scatter_kernel,
        grid_spec=pltpu.PrefetchScalarGridSpec(
            num_scalar_prefetch=1, grid=(t // tm,),
            in_specs=[pl.BlockSpec((tm, T_AUG), lambda i, pos: (i, 0)), pl.BlockSpec(memory_space=pl.ANY)],
            out_specs=pl.BlockSpec(memory_space=pl.ANY),
            scratch_shapes=[pltpu.SemaphoreType.DMA(())]),
        out_shape=jax.ShapeDtypeStruct((n_rows, T_AUG), F32),
        input_output_aliases={2: 0},
        compiler_params=_cparams(("arbitrary",)),
        name="moe_scatter",
    )(pos, tg, jnp.zeros((n_rows, T_AUG), F32))

    wg = w_gate.reshape(N_EXPERTS, D_MODEL, D_EXPERT).astype(BF16)
    wu = w_up.reshape(N_EXPERTS, D_MODEL, D_EXPERT).astype(BF16)
    wd = w_down.reshape(N_EXPERTS, D_EXPERT, D_MODEL).astype(BF16)
    wsel = lambda i, e, g, v: (g[i] * EXPERTS_PER_GROUP + e, 0, 0)
    ys = pl.pallas_call(
        _moe_expert_kernel,
        grid_spec=pltpu.PrefetchScalarGridSpec(
            num_scalar_prefetch=2, grid=(n_tiles, EXPERTS_PER_GROUP),
            in_specs=[pl.BlockSpec((te, T_AUG), lambda i, e, g, v: (i, 0)),
                      pl.BlockSpec((None, D_MODEL, D_EXPERT), wsel),
                      pl.BlockSpec((None, D_MODEL, D_EXPERT), wsel),
                      pl.BlockSpec((None, D_EXPERT, D_MODEL), wsel)],
            out_specs=pl.BlockSpec((te, D_MODEL), lambda i, e, g, v: (i, 0)),
            scratch_shapes=[pltpu.VMEM((te, D_MODEL), F32)]),
        out_shape=jax.ShapeDtypeStruct((n_rows, D_MODEL), F32),
        compiler_params=_cparams(("arbitrary", "arbitrary")),
        name="moe_experts",
    )(tile_grp, tile_valid, xs, wg, wu, wd)

    return pl.pallas_call(
        _moe_combine_kernel,
        grid_spec=pltpu.PrefetchScalarGridSpec(
            num_scalar_prefetch=1, grid=(t // tm,),
            in_specs=[pl.BlockSpec((tm, D_MODEL), lambda i, pos: (i, 0)), pl.BlockSpec(memory_space=pl.ANY)],
            out_specs=pl.BlockSpec((tm, D_MODEL), lambda i, pos: (i, 0)),
            scratch_shapes=[pltpu.VMEM((tm, D_MODEL), F32), pltpu.SemaphoreType.DMA(())]),
        out_shape=jax.ShapeDtypeStruct((t, D_MODEL), F32),
        compiler_params=_cparams(("arbitrary",)),
        name="moe_combine",
    )(pos, x2d, ys)


def _alibi_slopes():
    n = 2 * GROUP_HEADS
    s = 2.0 ** (-8.0 * np.arange(1, n + 1) / n)
    return s[0::2], s[1::2]


def kernel(x, norm_mix, norm_ffn, w_in, w_out, diff_q_gain, diff_k_gain, diff_lambda, diff_sub_gain, mlstm_conv, mlstm_gate_bias, mlstm_out_gain, hgrn_lower_bounds, hgrn_out_gain, nsa_cmp_pos, nsa_cmp_w1, nsa_cmp_w2, nsa_q_gain, nsa_k_gain, nsa_out_gain, moe_w_group, moe_b_group, moe_w_expert, moe_b_expert, moe_w_gate, moe_w_up, moe_w_down):
    b, s, d = x.shape
    slopes_diff, slopes_nsa = _alibi_slopes()
    lb_soft = jax.nn.softmax(hgrn_lower_bounds.astype(F32), axis=0)
    lower_bounds = jnp.cumsum(lb_soft, axis=0) - lb_soft[0]
    x2d = x.reshape(b * s, d)
    for l in range(norm_mix.shape[0]):
        p = _inproj(x2d, norm_mix[l], _pack_w_in(w_in[l])).reshape(b, s, P_COLS)
        y_a = _diff_attention(p, diff_q_gain[l], diff_k_gain[l], diff_lambda[l], diff_sub_gain[l], slopes_diff, l)
        y_b = _mlstm(p, mlstm_conv[l], mlstm_gate_bias[l], mlstm_out_gain[l])
        y_c = _hgrn2(p, lower_bounds[l], hgrn_out_gain[l])
        y_d = _nsa(p, nsa_cmp_pos[l], nsa_cmp_w1[l], nsa_cmp_w2[l], nsa_q_gain[l], nsa_k_gain[l],
                   nsa_out_gain[l], slopes_nsa)
        ys = [y.reshape(b * s, GROUP_WIDTH) for y in (y_a, y_b, y_c, y_d)]
        x2d, tg, route, counts = _outproj(x2d, ys, w_out[l], norm_ffn[l], moe_w_group[l], moe_b_group[l],
                                          moe_w_expert[l], moe_b_expert[l])
        x2d = _moe_routed(x2d, tg, route, counts, moe_w_gate[l], moe_w_up[l], moe_w_down[l])
    return x2d.reshape(b, s, d)
```

```python
import functools
import math

import numpy as np
import jax
import jax.numpy as jnp
from jax import lax
from jax.experimental import pallas as pl
from jax.experimental.pallas import tpu as pltpu

F32 = jnp.float32
BF16 = jnp.bfloat16

D_MODEL = 1024
HEAD_DIM = 64
GROUP_HEADS = 4
GROUP_WIDTH = GROUP_HEADS * HEAD_DIM
DIFF_HALF = HEAD_DIM // 2
EPS = 1e-6
NEG_INF = -1e30
LOG2E = math.log2(math.e)
M_INIT = -1e30
MASKED = -2e30

LANES = 128
VMEM_LIMIT = 48 * 1024 * 1024

COL_A = 0
COL_B = 768
COL_C = 1792
COL_D = 2816
COL_G = 3456
P_COLS = 3584


def _cparams(sem, flags=None):
    return pltpu.CompilerParams(dimension_semantics=sem, vmem_limit_bytes=VMEM_LIMIT, flags=flags)


def _block_ones(width, seg):
    i = np.arange(width)
    return jnp.asarray((i[:, None] // seg == i[None, :] // seg).astype(np.float32), BF16)


def _seg_sum(x, bd):
    hi = x.astype(BF16)
    lo = (x - hi.astype(F32)).astype(BF16)
    return (jnp.dot(hi, bd, preferred_element_type=F32) + jnp.dot(lo, bd, preferred_element_type=F32))


def _inproj_kernel(x_ref, g_ref, w_ref, o_ref):
    x = x_ref[...]
    ms = jnp.mean(x * x, axis=-1, keepdims=True)
    h = (x * lax.rsqrt(ms + EPS) * g_ref[...]).astype(BF16)
    o_ref[...] = jnp.dot(h, w_ref[...], preferred_element_type=F32)


def _inproj(x2d, gain, w_cat, tm=256):
    t = x2d.shape[0]
    return pl.pallas_call(
        _inproj_kernel,
        grid=(t // tm,),
        in_specs=[pl.BlockSpec((tm, D_MODEL), lambda i: (i, 0)),
                  pl.BlockSpec((1, D_MODEL), lambda i: (0, 0)),
                  pl.BlockSpec((D_MODEL, P_COLS), lambda i: (0, 0))],
        out_specs=pl.BlockSpec((tm, P_COLS), lambda i: (i, 0)),
        out_shape=jax.ShapeDtypeStruct((t, P_COLS), F32),
        compiler_params=_cparams(("arbitrary",)),
        name="inproj",
    )(x2d, gain.reshape(1, D_MODEL), w_cat)


IN_COLS = 3476


def _pack_w_in_kernel(w_ref, o_ref):
    w = w_ref[...]
    a_b = w[:, 0:1536]
    gates_b = w[:, 1536:1544]
    rest = w[:, 1544:3464]
    gates_d = w[:, 3464:3476]
    pad = jnp.zeros((w.shape[0], P_COLS - COL_G - 20), F32)
    o_ref[...] = jnp.concatenate([a_b, rest, gates_b, gates_d, pad], axis=1).astype(BF16)


def _pack_w_in(w, tr=128):
    d = w.shape[0]
    return pl.pallas_call(
        _pack_w_in_kernel,
        grid=(d // tr,),
        in_specs=[pl.BlockSpec((tr, IN_COLS), lambda i: (i, 0))],
        out_specs=pl.BlockSpec((tr, P_COLS), lambda i: (i, 0)),
        out_shape=jax.ShapeDtypeStruct((d, P_COLS), BF16),
        compiler_params=_cparams(("arbitrary",)),
        name="pack_w_in",
    )(w)


def _diff_prep_kernel(p_ref, qg_ref, kg_ref, bd_ref, w_ref, wt_ref, q_ref, k_ref, vt_ref):
    p = p_ref[...]
    bd = bd_ref[...]

    def norm(x, g):
        ss = _seg_sum(x * x, bd)
        return x * lax.rsqrt(ss * (1.0 / DIFF_HALF) + EPS) * g

    q = norm(p[:, 0:256], qg_ref[...]) * (DIFF_HALF ** -0.5 * LOG2E)
    k = norm(p[:, 256:512], kg_ref[...])
    q_ref[...] = q.T.astype(BF16)
    k_ref[...] = k.astype(BF16)
    v = p[:, 512:768]
    v_t = v.T
    vw_t = (v * w_ref[...]).T
    w_t = wt_ref[...]
    tm = v_t.shape[1]
    first = lax.broadcasted_iota(jnp.int32, (V_AUG - HEAD_DIM, tm), 0) == 0
    blocks = []
    for h in range(GROUP_HEADS):
        hs = slice(HEAD_DIM * h, HEAD_DIM * (h + 1))
        blocks += [vw_t[hs, :], jnp.where(first, jnp.broadcast_to(w_t[h:h + 1, :], first.shape), 0.0),
                   v_t[hs, :], jnp.where(first, 1.0, 0.0)]
    vt_ref[...] = jnp.concatenate(blocks, axis=0).astype(BF16)


V_AUG = 80
V_HEAD = 2 * V_AUG


def _key_weights(tile, slopes):
    kl = np.arange(tile, dtype=np.float64) - (tile - 1)
    w = np.stack([np.exp2(sl * LOG2E * kl) for sl in slopes])
    wide = np.repeat(w.T, HEAD_DIM, axis=1)
    w8 = np.zeros((8, tile)); w8[:len(slopes)] = w
    return jnp.asarray(wide.astype(np.float32)), jnp.asarray(w8.astype(np.float32))


def _diag_bias(tq, slopes):
    k = np.arange(tq)[:, None]
    q = np.arange(tq)[None, :]
    tabs = [np.where(k <= q, sl * LOG2E * k.astype(np.float64), MASKED) for sl in slopes]
    return jnp.asarray(np.stack(tabs).astype(np.float32))


AHEAD = 2


def _flash_update(s_ref, bias_ref, off, vt, m_ref, acc_ref, penalty=None):
    tq = s_ref.shape[1]
    for c in range(tq // LANES):
        cols = slice(LANES * c, LANES * (c + 1))
        s = s_ref[:, cols]
        if bias_ref is not None:
            s = s + bias_ref[:, cols]
        if penalty is not None:
            s = s + penalty
        m_tile = jnp.max(s, axis=0, keepdims=True)
        p = jnp.exp2(s - m_tile).astype(BF16)
        m_old = m_ref[0:1, cols]
        m_new = jnp.maximum(m_old, m_tile + off)
        m_ref[0:1, cols] = m_new
        pv = jnp.dot(vt, p, preferred_element_type=F32)
        acc_ref[:, cols] = jnp.exp2(m_old - m_new) * acc_ref[:, cols] + jnp.exp2(m_tile + off - m_new) * pv


def _diff_attn_kernel(lam_ref, qt_ref, k_ref, vt_ref, db_ref, sg_ref, bd_ref, o_ref,
                      qm_ref, acc_ref, m_ref, s_ref, *, tq, slopes, out_scale):
    qi = pl.program_id(1)
    n_half = 2 * GROUP_HEADS
    per_tile = LANES // DIFF_HALF
    row = lax.broadcasted_iota(jnp.int32, (LANES, tq), 0)
    for j in range(n_half):
        slab = qt_ref[LANES * (j // per_tile):LANES * (j // per_tile + 1), :].astype(F32)
        r0 = DIFF_HALF * (j % per_tile)
        qm_ref[j] = jnp.where((row >= r0) & (row < r0 + DIFF_HALF), slab, 0.0).astype(BF16)
    acc_ref[...] = jnp.zeros(acc_ref.shape, F32)
    m_ref[...] = jnp.full(m_ref.shape, M_INIT, F32)

    n_slots = s_ref.shape[0]

    def scores(kt, j):
        start = pl.multiple_of(kt * tq, tq)
        kj = k_ref[pl.ds(start, tq), LANES * (j // per_tile):LANES * (j // per_tile + 1)]
        s_ref[j % n_slots] = jnp.dot(kj, qm_ref[j], preferred_element_type=F32)

    def process(kt, j, diag):
        h = j // 2
        start = pl.multiple_of(kt * tq, tq)
        off = 0.0 if diag else ((kt - qi) * tq + (tq - 1)).astype(F32) * (slopes[h] * LOG2E)
        base = V_HEAD * h + (V_AUG if diag else 0)
        vt = vt_ref[base:base + V_AUG, pl.ds(start, tq)]
        _flash_update(s_ref.at[j % n_slots], db_ref.at[h] if diag else None, off, vt, m_ref.at[j], acc_ref.at[j])

    def step(kt, diag):
        for j in range(n_half):
            if j + AHEAD < n_half:
                scores(kt, j + AHEAD)
            elif not diag:
                scores(kt + 1, j + AHEAD - n_half)
            process(kt, j, diag)

    for j in range(AHEAD):
        scores(0, j)

    def body(kt, carry):
        step(kt, False)
        return carry

    lax.fori_loop(0, qi, body, 0)
    step(qi, True)

    lam = lam_ref[0, 0]
    heads = []
    for h in range(GROUP_HEADS):
        a0, a1 = acc_ref[2 * h], acc_ref[2 * h + 1]
        o0 = a0[0:HEAD_DIM, :] / a0[HEAD_DIM:HEAD_DIM + 1, :]
        o1 = a1[0:HEAD_DIM, :] / a1[HEAD_DIM:HEAD_DIM + 1, :]
        heads.append(o0 - lam * o1)
    o = jnp.concatenate(heads, axis=0).T
    ss = _seg_sum(o * o, bd_ref[...])
    o_ref[...] = o * lax.rsqrt(ss * (1.0 / HEAD_DIM) + EPS) * (sg_ref[...] * out_scale)


def _diff_attention(p3, q_gain, k_gain, lam_vecs, sub_gain, slopes, layer_idx, tq=256):
    b, s, _ = p3.shape
    tm = tq
    bd32 = _block_ones(GROUP_WIDTH, DIFF_HALF)
    bd64 = _block_ones(GROUP_WIDTH, HEAD_DIM)
    qg = jnp.tile(q_gain.astype(F32), 2 * GROUP_HEADS).reshape(1, GROUP_WIDTH)
    kg = jnp.tile(k_gain.astype(F32), 2 * GROUP_HEADS).reshape(1, GROUP_WIDTH)
    w_wide, w_t = _key_weights(tm, slopes)
    v_rows = GROUP_HEADS * V_HEAD
    qn, kn, vt = pl.pallas_call(
        _diff_prep_kernel,
        grid=(b, s // tm),
        in_specs=[pl.BlockSpec((None, tm, 768), lambda bi, i: (bi, i, COL_A // 768)),
                  pl.BlockSpec((1, GROUP_WIDTH), lambda bi, i: (0, 0)),
                  pl.BlockSpec((1, GROUP_WIDTH), lambda bi, i: (0, 0)),
                  pl.BlockSpec((GROUP_WIDTH, GROUP_WIDTH), lambda bi, i: (0, 0)),
                  pl.BlockSpec((tm, GROUP_WIDTH), lambda bi, i: (0, 0)),
                  pl.BlockSpec((8, tm), lambda bi, i: (0, 0))],
        out_specs=[pl.BlockSpec((None, GROUP_WIDTH, tm), lambda bi, i: (bi, 0, i)),
                   pl.BlockSpec((None, tm, GROUP_WIDTH), lambda bi, i: (bi, i, 0)),
                   pl.BlockSpec((None, v_rows, tm), lambda bi, i: (bi, 0, i))],
        out_shape=[jax.ShapeDtypeStruct((b, GROUP_WIDTH, s), BF16),
                   jax.ShapeDtypeStruct((b, s, GROUP_WIDTH), BF16),
                   jax.ShapeDtypeStruct((b, v_rows, s), BF16)],
        compiler_params=_cparams(("arbitrary", "arbitrary")),
        name="diff_prep",
    )(p3, qg, kg, bd32, w_wide, w_t)

    lam_init = 0.8 - 0.6 * math.exp(-0.3 * layer_idx)
    lv = lam_vecs.astype(F32)
    lam = (jnp.exp(jnp.dot(lv[0], lv[1])) - jnp.exp(jnp.dot(lv[2], lv[3])) + lam_init).reshape(1, 1)
    kern = functools.partial(_diff_attn_kernel, tq=tq, slopes=tuple(float(v) for v in slopes),
                             out_scale=1.0 - lam_init)
    return pl.pallas_call(
        kern,
        grid=(b, s // tq),
        in_specs=[pl.BlockSpec(memory_space=pltpu.SMEM),
                  pl.BlockSpec((None, GROUP_WIDTH, tq), lambda bi, i: (bi, 0, i)),
                  pl.BlockSpec((None, s, GROUP_WIDTH), lambda bi, i: (bi, 0, 0)),
                  pl.BlockSpec((None, v_rows, s), lambda bi, i: (bi, 0, 0)),
                  pl.BlockSpec((GROUP_HEADS, tq, tq), lambda bi, i: (0, 0, 0)),
                  pl.BlockSpec((1, GROUP_WIDTH), lambda bi, i: (0, 0)),
                  pl.BlockSpec((GROUP_WIDTH, GROUP_WIDTH), lambda bi, i: (0, 0))],
        out_specs=pl.BlockSpec((None, tq, GROUP_WIDTH), lambda bi, i: (bi, i, 0)),
        out_shape=jax.ShapeDtypeStruct((b, s, GROUP_WIDTH), F32),
        scratch_shapes=[pltpu.VMEM((2 * GROUP_HEADS, LANES, tq), BF16),
                        pltpu.VMEM((2 * GROUP_HEADS, V_AUG, tq), F32),
                        pltpu.VMEM((2 * GROUP_HEADS, 8, tq), F32),
                        pltpu.VMEM((2 * AHEAD, tq, tq), F32)],
        compiler_params=_cparams(("arbitrary", "arbitrary")),
        name="diff_attn",
    )(lam, qn, kn, vt, _diag_bias(tq, slopes), sub_gain.astype(F32).reshape(1, GROUP_WIDTH), bd64)


MLSTM_CONV = 4
CONV_HALO = 8
AUG = 128


def _tril_sum(x, tril):
    hi = x.astype(BF16)
    lo = (x - hi.astype(F32)).astype(BF16)
    return jnp.dot(tril, hi, preferred_element_type=F32) + jnp.dot(tril, lo, preferred_element_type=F32)


def _log_sigmoid(x):
    return jnp.minimum(x, 0.0) - jnp.log1p(jnp.exp(-jnp.abs(x)))


def _sigmoid(x):
    return 1.0 / (1.0 + jnp.exp(-x))


def _mlstm_kernel(q_ref, k_ref, v_ref, o_ref, g_ref, cw_ref, gb_ref, og_ref, bd_ref, y_ref,
                  halo_ref, c_ref, m_ref, *, L):
    ci = pl.program_id(1)

    @pl.when(ci == 0)
    def _():
        halo_ref[0:CONV_HALO, :] = jnp.zeros((CONV_HALO, 2 * GROUP_WIDTH), F32)
        c_ref[...] = jnp.zeros_like(c_ref)
        m_ref[...] = jnp.zeros_like(m_ref)

    halo_ref[CONV_HALO:CONV_HALO + L, 0:GROUP_WIDTH] = q_ref[...]
    halo_ref[CONV_HALO:CONV_HALO + L, GROUP_WIDTH:2 * GROUP_WIDTH] = k_ref[...]
    conv = jnp.zeros((L, 2 * GROUP_WIDTH), F32)
    for j in range(MLSTM_CONV):
        start = CONV_HALO - (MLSTM_CONV - 1) + j
        conv = conv + halo_ref[start:start + L, :] * cw_ref[j:j + 1, :]
    halo_ref[0:CONV_HALO, :] = halo_ref[L:L + CONV_HALO, :]
    qk = conv * _sigmoid(conv)
    q = qk[:, 0:GROUP_WIDTH] * (HEAD_DIM ** -0.5)
    k = qk[:, GROUP_WIDTH:2 * GROUP_WIDTH]
    v = v_ref[...]

    gates = g_ref[...] + gb_ref[...]
    ri = lax.broadcasted_iota(jnp.int32, (L, L), 0)
    cj = lax.broadcasted_iota(jnp.int32, (L, L), 1)
    causal = ri >= cj
    tril = jnp.where(causal, 1.0, 0.0).astype(BF16)
    gcum = _tril_sum(_log_sigmoid(gates), tril)
    gcum_t = gcum.T
    gates_t = gates.T
    lane_aug = lax.broadcasted_iota(jnp.int32, (L, AUG), 1)
    m_all = m_ref[...]

    outs = []
    for h in range(GROUP_HEADS):
        hs = slice(HEAD_DIM * h, HEAD_DIM * (h + 1))
        q_h, k_h = q[:, hs], k[:, hs]
        v_aug = jnp.where(lane_aug == HEAD_DIM, 1.0,
                          jnp.concatenate([v[:, hs], jnp.zeros((L, AUG - HEAD_DIM), F32)], axis=1))
        g_col = gcum[:, 4 + h:5 + h]
        li_col = gates[:, h:h + 1]
        g_row = gcum_t[4 + h:5 + h, :]
        li_row = gates_t[h:h + 1, :]
        g_last = g_row[:, L - 1:L]
        m_prev = m_all[:, h:h + 1]
        log_d = jnp.where(causal, g_col - g_row + li_row, NEG_INF)
        log_inter = g_col + m_prev
        m_t = jnp.maximum(log_inter, jnp.max(log_d, axis=1, keepdims=True))
        s_qk = lax.dot_general(q_h.astype(BF16), k_h.astype(BF16), (((1,), (1,)), ((), ())),
                               preferred_element_type=F32)
        w_intra = s_qk * jnp.exp(log_d - m_t)
        w_inter = jnp.exp(log_inter - m_t)
        c_aug = c_ref[h]
        num = (w_inter * jnp.dot(q_h.astype(BF16), c_aug.astype(BF16), preferred_element_type=F32)
               + jnp.dot(w_intra.astype(BF16), v_aug.astype(BF16), preferred_element_type=F32))
        den = num[:, HEAD_DIM:HEAD_DIM + 1]
        outs.append(num[:, 0:HEAD_DIM] / jnp.maximum(jnp.abs(den), jnp.exp(-m_t)))

        log_a = g_last - g_col + li_col
        m_new = jnp.maximum(g_last + m_prev, jnp.max(log_a, axis=0, keepdims=True))
        a_col = jnp.exp(log_a - m_new)
        decay = jnp.exp(g_last + m_prev - m_new)
        ak_t = (k_h * a_col).T.astype(BF16)
        c_ref[h] = decay * c_aug + jnp.dot(ak_t, v_aug.astype(BF16), preferred_element_type=F32)
        m_ref[:, h:h + 1] = m_new

    hcat = jnp.concatenate(outs, axis=1)
    ss = _seg_sum(hcat * hcat, bd_ref[...])
    y_ref[...] = hcat * lax.rsqrt(ss * (1.0 / HEAD_DIM) + EPS) * og_ref[...] * _sigmoid(o_ref[...])


def _mlstm(p3, conv_w, gate_bias, out_gain, L=128):
    b, s, _ = p3.shape
    gb = jnp.zeros((1, LANES), F32).at[0, 0:GROUP_HEADS].set(gate_bias[0]).at[0, GROUP_HEADS:2 * GROUP_HEADS].set(gate_bias[1])
    col = lambda off: (lambda bi, i: (bi, i, off // GROUP_WIDTH))
    return pl.pallas_call(
        functools.partial(_mlstm_kernel, L=L),
        grid=(b, s // L),
        in_specs=[pl.BlockSpec((None, L, GROUP_WIDTH), col(COL_B)),
                  pl.BlockSpec((None, L, GROUP_WIDTH), col(COL_B + 256)),
                  pl.BlockSpec((None, L, GROUP_WIDTH), col(COL_B + 512)),
                  pl.BlockSpec((None, L, GROUP_WIDTH), col(COL_B + 768)),
                  pl.BlockSpec((None, L, LANES), lambda bi, i: (bi, i, COL_G // LANES)),
                  pl.BlockSpec((MLSTM_CONV, 2 * GROUP_WIDTH), lambda bi, i: (0, 0)),
                  pl.BlockSpec((1, LANES), lambda bi, i: (0, 0)),
                  pl.BlockSpec((1, GROUP_WIDTH), lambda bi, i: (0, 0)),
                  pl.BlockSpec((GROUP_WIDTH, GROUP_WIDTH), lambda bi, i: (0, 0))],
        out_specs=pl.BlockSpec((None, L, GROUP_WIDTH), lambda bi, i: (bi, i, 0)),
        out_shape=jax.ShapeDtypeStruct((b, s, GROUP_WIDTH), F32),
        scratch_shapes=[pltpu.VMEM((CONV_HALO + L, 2 * GROUP_WIDTH), F32),
                        pltpu.VMEM((GROUP_HEADS, HEAD_DIM, AUG), F32),
                        pltpu.VMEM((1, LANES), F32)],
        compiler_params=_cparams(("arbitrary", "arbitrary")),
        name="mlstm",
    )(p3, p3, p3, p3, p3, conv_w.astype(F32), gb, out_gain.astype(F32).reshape(1, GROUP_WIDTH),
      _block_ones(GROUP_WIDTH, HEAD_DIM))


HGRN_CHUNK = 16
LB_FLOOR = 1e-30


def _hgrn_kernel(q_ref, f_ref, i_ref, g_ref, lb_ref, og_ref, bd_ref, y_ref,
                 lf_s, kk_s, vv_s, st_ref, *, TL):
    ci = pl.program_id(1)
    C = HGRN_CHUNK
    W = GROUP_WIDTH

    @pl.when(ci == 0)
    def _():
        lf_s[0:C, :] = jnp.zeros((C, W), F32)
        kk_s[0:C, :] = jnp.zeros((C, W), F32)
        vv_s[0:C, :] = jnp.zeros((C, W), F32)
        st_ref[...] = jnp.zeros_like(st_ref)

    z = f_ref[...]
    a = lb_ref[0:1, :]
    c = lb_ref[1:2, :] + _log_sigmoid(z)
    mx = jnp.maximum(a, c)
    lf = mx + jnp.log1p(jnp.exp(-jnp.abs(a - c)))
    kk = lb_ref[2:3, :] * _sigmoid(-z) + lb_ref[3:4, :]
    qx = q_ref[...]
    qs = qx * _sigmoid(qx)
    vv = i_ref[...]
    lf_s[C:C + TL, :] = lf
    kk_s[C:C + TL, :] = kk
    vv_s[C:C + TL, :] = vv

    bd = bd_ref[...]
    row = lax.broadcasted_iota(jnp.int32, (TL, W), 0)
    rmod = row & (C - 1)

    acc = jnp.zeros((TL, W), F32)
    dsum = jnp.zeros((TL, W), F32)
    for delta in range(C):
        if delta > 0:
            dsum = dsum + lf_s[C - (delta - 1):C - (delta - 1) + TL, :]
        x = qs * kk_s[C - delta:C - delta + TL, :] * jnp.exp(dsum)
        x = jnp.where(rmod >= delta, x, 0.0)
        att = jnp.dot(x.astype(BF16), bd, preferred_element_type=F32)
        acc = acc + att * vv_s[C - delta:C - delta + TL, :]

    ri = lax.broadcasted_iota(jnp.int32, (TL, TL), 0)
    cj = lax.broadcasted_iota(jnp.int32, (TL, TL), 1)
    same = (ri // C) == (cj // C)
    tril = jnp.where(same & (ri >= cj), 1.0, 0.0).astype(BF16)
    ones = jnp.where(same, 1.0, 0.0).astype(BF16)
    bcum = _tril_sum(lf, tril)
    blast = _tril_sum(lf, ones)
    q_a = (qs * jnp.exp(bcum)).astype(BF16)
    k_b = (kk * jnp.exp(blast - bcum)).astype(BF16)
    dec = jnp.exp(blast)
    vv_b = vv.astype(BF16)
    hmask = bd.astype(F32)
    state = st_ref[...]
    inters = []
    for ch in range(TL // C):
        r = slice(ch * C, (ch + 1) * C)
        inters.append(lax.dot_general(q_a[r], state.astype(BF16), (((1,), (1,)), ((), ())),
                                      preferred_element_type=F32))
        upd = lax.dot_general(vv_b[r], k_b[r], (((0,), (0,)), ((), ())), preferred_element_type=F32)
        state = state * dec[ch * C:ch * C + 1, :] + upd * hmask
    st_ref[...] = state
    o = acc + jnp.concatenate(inters, axis=0)
    ss = _seg_sum(o * o, bd)
    y_ref[...] = o * lax.rsqrt(ss * (1.0 / HEAD_DIM) + EPS) * og_ref[...] * _sigmoid(g_ref[...])


def _hgrn2(p3, lower_bound, out_gain, TL=256):
    b, s, _ = p3.shape
    lb = lower_bound.astype(F32)
    lbf = jnp.maximum(lb, LB_FLOOR)
    lbp = jnp.stack([jnp.log(lbf), jnp.log1p(-lb), 1.0 - lb, lb - lbf])
    col = lambda off: (lambda bi, i: (bi, i, off // GROUP_WIDTH))
    C = HGRN_CHUNK
    return pl.pallas_call(
        functools.partial(_hgrn_kernel, TL=TL),
        grid=(b, s // TL),
        in_specs=[pl.BlockSpec((None, TL, GROUP_WIDTH), col(COL_C)),
                  pl.BlockSpec((None, TL, GROUP_WIDTH), col(COL_C + 256)),
                  pl.BlockSpec((None, TL, GROUP_WIDTH), col(COL_C + 512)),
                  pl.BlockSpec((None, TL, GROUP_WIDTH), col(COL_C + 768)),
                  pl.BlockSpec((4, GROUP_WIDTH), lambda bi, i: (0, 0)),
                  pl.BlockSpec((1, GROUP_WIDTH), lambda bi, i: (0, 0)),
                  pl.BlockSpec((GROUP_WIDTH, GROUP_WIDTH), lambda bi, i: (0, 0))],
        out_specs=pl.BlockSpec((None, TL, GROUP_WIDTH), lambda bi, i: (bi, i, 0)),
        out_shape=jax.ShapeDtypeStruct((b, s, GROUP_WIDTH), F32),
        scratch_shapes=[pltpu.VMEM((C + TL, GROUP_WIDTH), F32),
                        pltpu.VMEM((C + TL, GROUP_WIDTH), F32),
                        pltpu.VMEM((C + TL, GROUP_WIDTH), F32),
                        pltpu.VMEM((GROUP_WIDTH, GROUP_WIDTH), F32)],
        compiler_params=_cparams(("arbitrary", "arbitrary")),
        name="hgrn2",
    )(p3, p3, p3, p3, lbp, out_gain.astype(F32).reshape(1, GROUP_WIDTH), _block_ones(GROUP_WIDTH, HEAD_DIM))


CMP_LEN = 32
CMP_STRIDE = 16
SEL_BLOCK = 64
SEL_TOPK = 16
WINDOW = 512
GATE_ROW = 8


def _rms_rows(x, gain):
    return x * lax.rsqrt(jnp.mean(x * x, axis=-1, keepdims=True) + EPS) * gain


NSA_V_ROWS = (GROUP_HEADS + 1) * V_AUG


def _aug_shared_values(v, w_wide, w_t):
    tm = v.shape[0]
    vw_t = (jnp.concatenate([v] * GROUP_HEADS, axis=1) * w_wide).T
    first = lax.broadcasted_iota(jnp.int32, (V_AUG - HEAD_DIM, tm), 0) == 0
    blocks = []
    for h in range(GROUP_HEADS):
        blocks += [vw_t[HEAD_DIM * h:HEAD_DIM * (h + 1), :],
                   jnp.where(first, jnp.broadcast_to(w_t[h:h + 1, :], first.shape), 0.0)]
    blocks += [v.T, jnp.where(first, 1.0, 0.0)]
    return jnp.concatenate(blocks, axis=0).astype(BF16)


def _nsa_prep_kernel(q_ref, kv_ref, g_ref, qg_ref, kg_ref, bd_ref, w_ref, wt_ref,
                     qt_ref, kc_ref, vc_ref, ks_ref, vst_ref, kw_ref, vwt_ref, gt_ref):
    ti = pl.program_id(1)
    q = q_ref[...]
    ss = _seg_sum(q * q, bd_ref[...])
    qn = q * lax.rsqrt(ss * (1.0 / HEAD_DIM) + EPS) * qg_ref[...] * (HEAD_DIM ** -0.5 * LOG2E)
    qt_ref[...] = qn.T.astype(BF16)
    kv = kv_ref[...]
    tm = kv.shape[0]
    kc_ref[...] = kv[:, 0:64]
    vc_ref[...] = kv[:, 64:128]
    ks = _rms_rows(kv[:, 128:192], kg_ref[1:2, :])
    lane = lax.broadcasted_iota(jnp.int32, (tm, LANES), 1)
    blk = jnp.right_shift(ti * tm + lax.broadcasted_iota(jnp.int32, (tm, LANES), 0), SEL_BLOCK.bit_length() - 1)
    onehot = jnp.where(lane == blk + HEAD_DIM, 1.0, 0.0)
    ks_ref[...] = jnp.where(lane < HEAD_DIM, jnp.concatenate([ks, ks], axis=1), onehot).astype(BF16)
    kw = _rms_rows(kv[:, 256:320], kg_ref[2:3, :])
    kw_ref[...] = jnp.concatenate([kw, jnp.zeros_like(kw)], axis=1).astype(BF16)
    vst_ref[...] = _aug_shared_values(kv[:, 192:256], w_ref[...], wt_ref[...])
    vwt_ref[...] = _aug_shared_values(kv[:, 320:384], w_ref[...], wt_ref[...])
    gt_ref[...] = _sigmoid(g_ref[...]).T


def _gelu_tanh(x):
    return 0.5 * x * (1.0 + jnp.tanh(math.sqrt(2.0 / math.pi) * (x + 0.044715 * x * x * x)))


def _nsa_cmp_kernel(kr_ref, vr_ref, w1_ref, w2_ref, pos_ref, kg_ref, kc_ref, vct_ref, sh_ref, *, n_rows):
    half = CMP_STRIDE * HEAD_DIM
    sh_ref[n_rows:n_rows + 8, :] = jnp.zeros((8, HEAD_DIM), F32)
    outs = []
    for j, x_ref in enumerate((kr_ref, vr_ref)):
        r = x_ref[...].astype(BF16)
        w1 = w1_ref[j]
        first = jnp.dot(r, w1[0:half, :], preferred_element_type=F32)
        sh_ref[0:n_rows, :] = jnp.dot(r, w1[half:2 * half, :], preferred_element_type=F32)
        pos8 = jnp.broadcast_to(pos_ref[j], (8, 2 * half))
        posw = jnp.dot(pos8, w1.astype(F32), preferred_element_type=F32)[0:1, :]
        hdn = _gelu_tanh(first + sh_ref[1:n_rows + 1, :] + posw)
        outs.append(jnp.dot(hdn, w2_ref[j].astype(F32), preferred_element_type=F32))
    kc_ref[...] = _rms_rows(outs[0], kg_ref[0:1, :]).astype(BF16)
    vct_ref[...] = outs[1].T.astype(BF16)


def _nsa_attn_kernel(qt_ref, kc_ref, vct_ref, ks_ref, vst_ref, kw_ref, vwt_ref, gt_ref, ov_ref, db_ref, lo_ref,
                     og_ref, bd_ref, y_ref, qa_ref, acc_s, m_s, acc_w, m_w, ss_ref, sw_ref,
                     *, tq, slopes, n_cmp):
    qi = pl.program_id(1)
    n_rows = kc_ref.shape[0]
    n_sel = ov_ref.shape[0]
    qpos = qi * tq + lax.broadcasted_iota(jnp.int32, (1, tq), 1)

    nio = lax.broadcasted_iota(jnp.int32, (n_rows, tq), 0)
    dist_c = qpos - (nio * CMP_STRIDE + (CMP_LEN - 1))
    valid_c = (dist_c >= 0) & (nio < n_cmp)
    dist_cf = dist_c.astype(F32)
    kc = kc_ref[...]
    vct = vct_ref[...]
    o_cmp = []
    p_sum = jnp.zeros((n_rows, tq), F32)
    for h in range(GROUP_HEADS):
        qh = qt_ref[HEAD_DIM * h:HEAD_DIM * (h + 1), :]
        s = jnp.dot(kc, qh, preferred_element_type=F32) - (slopes[h] * LOG2E) * dist_cf
        s = jnp.where(valid_c, s, MASKED)
        m = jnp.maximum(jnp.max(s, axis=0, keepdims=True), M_INIT)
        e = jnp.exp2(s - m)
        p = e / jnp.maximum(jnp.sum(e, axis=0, keepdims=True), 1e-30)
        p_sum = p_sum + p
        o_cmp.append(jnp.dot(vct, p.astype(BF16), preferred_element_type=F32))

    p_hi = p_sum.astype(BF16)
    p_lo = (p_sum - p_hi.astype(F32)).astype(BF16)
    ov = ov_ref[...]
    imp = jnp.dot(ov, p_hi, preferred_element_type=F32) + jnp.dot(ov, p_lo, preferred_element_type=F32)
    jio = lax.broadcasted_iota(jnp.int32, (n_sel, tq), 0)
    qblk = jnp.right_shift(qpos, SEL_BLOCK.bit_length() - 1)
    imp = jnp.where(jio <= qblk, imp, -1.0)
    imp = jnp.where((jio == 0) | (jio == qblk) | (jio == qblk - 1), 2.0, imp)
    rank = jnp.zeros((n_sel, tq), F32)
    for i in range(n_sel):
        row = imp[i:i + 1, :]
        beats = (row > imp) | ((row == imp) & (jio > i))
        rank = rank + jnp.where(beats, 1.0, 0.0)
    bsel = jnp.where(rank < float(min(SEL_TOPK, n_sel)), 0.0, MASKED)
    if n_sel < LANES - HEAD_DIM:
        bsel = jnp.concatenate([bsel, jnp.zeros((LANES - HEAD_DIM - n_sel, tq), F32)], axis=0)
    for h in range(GROUP_HEADS):
        qa_ref[h, 0:HEAD_DIM, :] = qt_ref[HEAD_DIM * h:HEAD_DIM * (h + 1), :]
        qa_ref[h, HEAD_DIM:LANES, :] = bsel.astype(BF16)
    for ref, val in ((acc_s, 0.0), (acc_w, 0.0), (m_s, M_INIT), (m_w, M_INIT)):
        ref[...] = jnp.full(ref.shape, val, F32)

    def tile_off(kt, h):
        return ((kt - qi) * tq + (tq - 1)).astype(F32) * (slopes[h] * LOG2E)

    def values(vt_ref, kt, h, diag):
        base = GROUP_HEADS * V_AUG if diag else V_AUG * h
        return vt_ref[base:base + V_AUG, pl.ds(pl.multiple_of(kt * tq, tq), tq)]

    def sel_scores(kt, h):
        kb = ks_ref[pl.ds(pl.multiple_of(kt * tq, tq), tq), :]
        ss_ref[h] = jnp.dot(kb, qa_ref[h], preferred_element_type=F32)

    def sel_step(kt, diag):
        for h in range(GROUP_HEADS):
            if h + AHEAD < GROUP_HEADS:
                sel_scores(kt, h + AHEAD)
            elif not diag:
                sel_scores(kt + 1, h + AHEAD - GROUP_HEADS)
            _flash_update(ss_ref.at[h], db_ref.at[h] if diag else None, 0.0 if diag else tile_off(kt, h),
                          values(vst_ref, kt, h, diag), m_s.at[h], acc_s.at[h])

    for h in range(AHEAD):
        sel_scores(0, h)

    def body(kt, carry):
        sel_step(kt, False)
        return carry

    lax.fori_loop(0, qi, body, 0)
    sel_step(qi, True)

    def win_step(back, bias_of, diag):
        kt = jnp.maximum(qi - back, 0)
        penalty = None if diag else jnp.where(qi >= back, 0.0, MASKED)
        kb = kw_ref[pl.ds(pl.multiple_of(kt * tq, tq), tq), :]

        def win_scores(h):
            sw_ref[h] = jnp.dot(kb, qa_ref[h], preferred_element_type=F32)

        for h in range(AHEAD):
            win_scores(h)
        for h in range(GROUP_HEADS):
            if h + AHEAD < GROUP_HEADS:
                win_scores(h + AHEAD)
            _flash_update(sw_ref.at[h], bias_of(h), 0.0 if diag else tile_off(kt, h),
                          values(vwt_ref, kt, h, diag), m_w.at[h], acc_w.at[h], penalty)

    win_step(2, lambda h: lo_ref, False)
    win_step(1, lambda h: None, False)
    win_step(0, lambda h: db_ref.at[h], True)

    gt = gt_ref[...]
    heads = []
    for h in range(GROUP_HEADS):
        a_s, a_w = acc_s[h], acc_w[h]
        o_sel = a_s[0:HEAD_DIM, :] / a_s[HEAD_DIM:HEAD_DIM + 1, :]
        o_win = a_w[0:HEAD_DIM, :] / a_w[HEAD_DIM:HEAD_DIM + 1, :]
        g = GATE_ROW + 3 * h
        heads.append(gt[g:g + 1, :] * o_cmp[h] + gt[g + 1:g + 2, :] * o_sel + gt[g + 2:g + 3, :] * o_win)
    o = jnp.concatenate(heads, axis=0).T
    ss = _seg_sum(o * o, bd_ref[...])
    y_ref[...] = o * lax.rsqrt(ss * (1.0 / HEAD_DIM) + EPS) * og_ref[...]


def _nsa(p3, cmp_pos, cmp_w1, cmp_w2, q_gain, k_gain, out_gain, slopes, tq=256):
    b, s, _ = p3.shape
    tm = tq
    n_rows = s // CMP_STRIDE
    n_cmp = (s - CMP_LEN) // CMP_STRIDE + 1
    n_sel = s // SEL_BLOCK
    bd64 = _block_ones(GROUP_WIDTH, HEAD_DIM)
    qg = jnp.tile(q_gain.astype(F32), GROUP_HEADS).reshape(1, GROUP_WIDTH)
    kg = k_gain.astype(F32)
    tok = lambda w: pl.BlockSpec((None, tm, w), lambda bi, i: (bi, i, 0))
    tok_t = lambda w: pl.BlockSpec((None, w, tm), lambda bi, i: (bi, 0, i))
    full = lambda r, c: pl.BlockSpec((r, c), lambda bi, i: (0, 0))
    assert n_sel <= LANES - HEAD_DIM, "block one-hot lanes hold at most 64 selection blocks"
    assert WINDOW == 2 * tq and tq % SEL_BLOCK == 0, "window branch walks exactly the key tiles qi-2, qi-1, qi"
    w_wide, w_t = _key_weights(tm, slopes)
    qt, kc, vc, ks, vst, kw, vwt, gt = pl.pallas_call(
        _nsa_prep_kernel,
        grid=(b, s // tm),
        in_specs=[pl.BlockSpec((None, tm, GROUP_WIDTH), lambda bi, i: (bi, i, COL_D // GROUP_WIDTH)),
                  pl.BlockSpec((None, tm, 384), lambda bi, i: (bi, i, (COL_D + GROUP_WIDTH) // 384)),
                  pl.BlockSpec((None, tm, LANES), lambda bi, i: (bi, i, COL_G // LANES)),
                  full(1, GROUP_WIDTH), full(3, HEAD_DIM), full(GROUP_WIDTH, GROUP_WIDTH),
                  full(tm, GROUP_WIDTH), full(8, tm)],
        out_specs=[tok_t(GROUP_WIDTH), tok(HEAD_DIM), tok(HEAD_DIM), tok(LANES), tok_t(NSA_V_ROWS),
                   tok(LANES), tok_t(NSA_V_ROWS), tok_t(LANES)],
        out_shape=[jax.ShapeDtypeStruct((b, GROUP_WIDTH, s), BF16),
                   jax.ShapeDtypeStruct((b, s, HEAD_DIM), F32),
                   jax.ShapeDtypeStruct((b, s, HEAD_DIM), F32),
                   jax.ShapeDtypeStruct((b, s, LANES), BF16),
                   jax.ShapeDtypeStruct((b, NSA_V_ROWS, s), BF16),
                   jax.ShapeDtypeStruct((b, s, LANES), BF16),
                   jax.ShapeDtypeStruct((b, NSA_V_ROWS, s), BF16),
                   jax.ShapeDtypeStruct((b, LANES, s), F32)],
        compiler_params=_cparams(("arbitrary", "arbitrary")),
        name="nsa_prep",
    )(p3, p3, p3, qg, kg, bd64, w_wide, w_t)

    row_w = CMP_STRIDE * HEAD_DIM
    kcmp, vcmp_t = pl.pallas_call(
        functools.partial(_nsa_cmp_kernel, n_rows=n_rows),
        grid=(b,),
        in_specs=[pl.BlockSpec((None, n_rows, row_w), lambda bi: (bi, 0, 0)),
                  pl.BlockSpec((None, n_rows, row_w), lambda bi: (bi, 0, 0)),
                  pl.BlockSpec((2, 2 * row_w, HEAD_DIM), lambda bi: (0, 0, 0)),
                  pl.BlockSpec((2, HEAD_DIM, HEAD_DIM), lambda bi: (0, 0, 0)),
                  pl.BlockSpec((2, 1, 2 * row_w), lambda bi: (0, 0, 0)),
                  pl.BlockSpec((3, HEAD_DIM), lambda bi: (0, 0))],
        out_specs=[pl.BlockSpec((None, n_rows, HEAD_DIM), lambda bi: (bi, 0, 0)),
                   pl.BlockSpec((None, HEAD_DIM, n_rows), lambda bi: (bi, 0, 0))],
        out_shape=[jax.ShapeDtypeStruct((b, n_rows, HEAD_DIM), BF16),
                   jax.ShapeDtypeStruct((b, HEAD_DIM, n_rows), BF16)],
        scratch_shapes=[pltpu.VMEM((n_rows + 8, HEAD_DIM), F32)],
        compiler_params=_cparams(("arbitrary",)),
        name="nsa_cmp",
    )(kc.reshape(b, n_rows, row_w), vc.reshape(b, n_rows, row_w), cmp_w1.astype(BF16), cmp_w2.astype(BF16),
      cmp_pos.astype(F32).reshape(2, 1, 2 * row_w), kg)

    cs = np.arange(n_rows)[:, None] * CMP_STRIDE
    ss = np.arange(n_sel)[None, :] * SEL_BLOCK
    overlap = np.clip(np.minimum(cs + CMP_LEN, ss + SEL_BLOCK) - np.maximum(cs, ss), 0, None) / CMP_LEN
    overlap[n_cmp:, :] = 0.0
    ov_t = jnp.asarray(overlap.T.astype(np.float32), BF16)

    seq = lambda r, c: pl.BlockSpec((None, r, c), lambda bi, i: (bi, 0, 0))
    kern = functools.partial(_nsa_attn_kernel, tq=tq, slopes=tuple(float(v) for v in slopes), n_cmp=n_cmp)
    return pl.pallas_call(
        kern,
        grid=(b, s // tq),
        in_specs=[pl.BlockSpec((None, GROUP_WIDTH, tq), lambda bi, i: (bi, 0, i)),
                  seq(n_rows, HEAD_DIM), seq(HEAD_DIM, n_rows),
                  seq(s, LANES), seq(NSA_V_ROWS, s), seq(s, LANES), seq(NSA_V_ROWS, s),
                  pl.BlockSpec((None, LANES, tq), lambda bi, i: (bi, 0, i)),
                  full(n_sel, n_rows),
                  pl.BlockSpec((GROUP_HEADS, tq, tq), lambda bi, i: (0, 0, 0)), full(tq, tq),
                  full(1, GROUP_WIDTH), full(GROUP_WIDTH, GROUP_WIDTH)],
        out_specs=pl.BlockSpec((None, tq, GROUP_WIDTH), lambda bi, i: (bi, i, 0)),
        out_shape=jax.ShapeDtypeStruct((b, s, GROUP_WIDTH), F32),
        scratch_shapes=[pltpu.VMEM((GROUP_HEADS, LANES, tq), BF16),
                        pltpu.VMEM((GROUP_HEADS, V_AUG, tq), F32), pltpu.VMEM((GROUP_HEADS, 8, tq), F32),
                        pltpu.VMEM((GROUP_HEADS, V_AUG, tq), F32), pltpu.VMEM((GROUP_HEADS, 8, tq), F32),
                        pltpu.VMEM((GROUP_HEADS, tq, tq), F32), pltpu.VMEM((GROUP_HEADS, tq, tq), F32)],
        compiler_params=_cparams(("arbitrary", "arbitrary")),
        name="nsa_attn",
    )(qt, kcmp, vcmp_t, ks, vst, kw, vwt, gt, ov_t, _diag_bias(tq, slopes), _window_low_bias(tq),
      out_gain.astype(F32).reshape(1, GROUP_WIDTH), bd64)


def _window_low_bias(tq):
    k = np.arange(tq)[:, None]
    q = np.arange(tq)[None, :]
    return jnp.asarray(np.where(k > q, 0.0, MASKED).astype(np.float32))


N_GROUPS = 4
EXPERTS_PER_GROUP = 8
N_EXPERTS = N_GROUPS * EXPERTS_PER_GROUP
D_EXPERT = 256
ROUTER_LANE0 = N_GROUPS


def _split3_dot(a, b_hi, b_lo):
    a_hi = a.astype(BF16)
    a_lo = (a - a_hi.astype(F32)).astype(BF16)
    return (jnp.dot(a_hi, b_hi, preferred_element_type=F32) + jnp.dot(a_lo, b_hi, preferred_element_type=F32)
            + jnp.dot(a_hi, b_lo, preferred_element_type=F32))


def _route(t, wr_hi, wr_lo, br):
    logits = _split3_dot(t, wr_hi, wr_lo) + br
    lane = lax.broadcasted_iota(jnp.int32, logits.shape, 1)
    lane_f = lane.astype(F32)
    big = float(LANES)
    is_g = lane < N_GROUPS
    gl = jnp.where(is_g, logits, MASKED)
    gmax = jnp.max(gl, axis=-1, keepdims=True)
    g_prob = 1.0 / jnp.sum(jnp.where(is_g, jnp.exp(gl - gmax), 0.0), axis=-1, keepdims=True)
    g_sel = jnp.min(jnp.where(is_g & (gl == gmax), lane_f, big), axis=-1, keepdims=True)
    lo = ROUTER_LANE0 + EXPERTS_PER_GROUP * g_sel
    in_grp = (lane_f >= lo) & (lane_f < lo + EXPERTS_PER_GROUP)
    el = jnp.where(in_grp, logits, MASKED)
    m1 = jnp.max(el, axis=-1, keepdims=True)
    i1 = jnp.min(jnp.where(in_grp & (el == m1), lane_f, big), axis=-1, keepdims=True)
    rest = in_grp & (lane_f != i1)
    el2 = jnp.where(rest, logits, MASKED)
    m2 = jnp.max(el2, axis=-1, keepdims=True)
    i2 = jnp.min(jnp.where(rest & (el2 == m2), lane_f, big), axis=-1, keepdims=True)
    r = jnp.exp(m2 - m1)
    w1 = g_prob / (1.0 + r)
    w2 = g_prob * r / (1.0 + r)
    return jnp.where(lane_f == i1, w1, 0.0) + jnp.where(lane_f == i2, w2, 0.0), g_sel


ROW_TILE = (D_MODEL // LANES, LANES)


def _to_row_tiles(ref, x):
    for j in range(ROW_TILE[0]):
        ref[:, j, :] = x[:, LANES * j:LANES * (j + 1)]


def _from_row_tiles(ref):
    return jnp.concatenate([ref[:, j, :] for j in range(ROW_TILE[0])], axis=1)


def _outproj_kernel(x_ref, ya_ref, yb_ref, yc_ref, yd_ref, w_ref, g_ref, wr_hi_ref, wr_lo_ref, br_ref,
                    xo_ref, t3_ref, route_ref, cnt_ref, run_ref):
    @pl.when(pl.program_id(0) == 0)
    def _():
        run_ref[...] = jnp.zeros_like(run_ref)

    acc = x_ref[...]
    for gi, y_ref in enumerate((ya_ref, yb_ref, yc_ref, yd_ref)):
        acc = acc + jnp.dot(y_ref[...].astype(BF16), w_ref[GROUP_WIDTH * gi:GROUP_WIDTH * (gi + 1), :],
                            preferred_element_type=F32)
    xo_ref[...] = acc
    ms = jnp.mean(acc * acc, axis=-1, keepdims=True)
    t = acc * lax.rsqrt(ms + EPS) * g_ref[...]
    _to_row_tiles(t3_ref, t)

    _, g_sel = _route(t, wr_hi_ref[...], wr_lo_ref[...], br_ref[...])

    tm = t.shape[0]
    lane = lax.broadcasted_iota(jnp.int32, (tm, LANES), 1)
    onehot = jnp.where(lane.astype(F32) == g_sel, 1.0, 0.0)
    ri = lax.broadcasted_iota(jnp.int32, (tm, tm), 0)
    ci = lax.broadcasted_iota(jnp.int32, (tm, tm), 1)
    before = jnp.where(ri > ci, 1.0, 0.0).astype(BF16)
    prefix = jnp.dot(before, onehot.astype(BF16), preferred_element_type=F32)
    rank = jnp.sum(onehot * (run_ref[...] + prefix), axis=-1, keepdims=True)
    route_ref[...] = jnp.where(lane == 0, g_sel, 0.0) + jnp.where(lane == 1, rank, 0.0)
    run_ref[...] = run_ref[...] + jnp.sum(onehot, axis=0, keepdims=True)
    cnt_ref[...] = run_ref[...]


def _outproj(x2d, ys, w_out, ffn_gain, w_group, b_group, w_expert, b_expert, tm=256):
    t = x2d.shape[0]
    wr = jnp.zeros((D_MODEL, LANES), F32).at[:, 0:N_GROUPS].set(w_group).at[:, ROUTER_LANE0:ROUTER_LANE0 + N_EXPERTS].set(w_expert)
    wr_hi = wr.astype(BF16)
    wr_lo = (wr - wr_hi.astype(F32)).astype(BF16)
    br = jnp.zeros((1, LANES), F32).at[0, 0:N_GROUPS].set(b_group).at[0, ROUTER_LANE0:ROUTER_LANE0 + N_EXPERTS].set(b_expert)
    row = lambda w: pl.BlockSpec((tm, w), lambda i: (i, 0))
    full = lambda r, c: pl.BlockSpec((r, c), lambda i: (0, 0))
    return list(pl.pallas_call(
        _outproj_kernel,
        grid=(t // tm,),
        in_specs=[row(D_MODEL), row(GROUP_WIDTH), row(GROUP_WIDTH), row(GROUP_WIDTH), row(GROUP_WIDTH),
                  full(D_MODEL, D_MODEL), full(1, D_MODEL), full(D_MODEL, LANES), full(D_MODEL, LANES), full(1, LANES)],
        out_specs=[row(D_MODEL), pl.BlockSpec((tm,) + ROW_TILE, lambda i: (i, 0, 0)), row(LANES), full(1, LANES)],
        out_shape=[jax.ShapeDtypeStruct((t, D_MODEL), F32),
                   jax.ShapeDtypeStruct((t,) + ROW_TILE, F32),
                   jax.ShapeDtypeStruct((t, LANES), F32),
                   jax.ShapeDtypeStruct((1, LANES), F32)],
        scratch_shapes=[pltpu.VMEM((1, LANES), F32)],
        compiler_params=_cparams(("arbitrary",)),
        name="outproj_router",
    )(x2d, *ys, w_out.astype(BF16), ffn_gain.reshape(1, D_MODEL), wr_hi, wr_lo, br)) + [(wr_hi, wr_lo, br)]


MOE_TILE = 512


def _row_copies(n, make_copy):
    def start(r, carry):
        make_copy(r).start()
        return carry

    def wait(r, carry):
        make_copy(r).wait()
        return carry

    lax.fori_loop(0, n, start, 0)
    lax.fori_loop(0, n, wait, 0)


def _moe_scatter_kernel(pos_ref, t3_ref, xs_in_ref, xs_ref, sem):
    del xs_in_ref
    tm = t3_ref.shape[0]
    base = pl.program_id(0) * tm
    _row_copies(tm, lambda r: pltpu.make_async_copy(t3_ref.at[pl.ds(r, 1)],
                                                    xs_ref.at[pl.ds(pos_ref[base + r], 1)], sem))


def _moe_expert_kernel(grp_ref, valid_ref, xs_ref, wr_hi_ref, wr_lo_ref, br_ref, wg_ref, wu_ref, wd_ref, y_ref,
                       x_s, gate_s, acc_ref):
    i, e = pl.program_id(0), pl.program_id(1)
    valid = valid_ref[i] == 1

    @pl.when(e == 0)
    def _():
        acc_ref[...] = jnp.zeros_like(acc_ref)

    @pl.when((e == 0) & valid)
    def _():
        x = _from_row_tiles(xs_ref)
        gate, _ = _route(x, wr_hi_ref[...], wr_lo_ref[...], br_ref[...])
        x_s[...] = x.astype(BF16)
        gate_s[...] = gate

    @pl.when(valid)
    def _():
        x = x_s[...]
        gate = gate_s[...]
        lane = lax.broadcasted_iota(jnp.int32, gate.shape, 1)
        col = ROUTER_LANE0 + grp_ref[i] * EXPERTS_PER_GROUP + e
        w = jnp.sum(jnp.where(lane == col, gate, 0.0), axis=-1, keepdims=True)
        a = jnp.dot(x, wg_ref[...], preferred_element_type=F32)
        u = jnp.dot(x, wu_ref[...], preferred_element_type=F32)
        act = a * _sigmoid(a) * u * w
        acc_ref[...] += jnp.dot(act.astype(BF16), wd_ref[...], preferred_element_type=F32)

    @pl.when(e == EXPERTS_PER_GROUP - 1)
    def _():
        _to_row_tiles(y_ref, acc_ref[...])


def _moe_combine_kernel(pos_ref, x_ref, ys_ref, o_ref, buf_ref, sem):
    tm = x_ref.shape[0]
    base = pl.program_id(0) * tm
    _row_copies(tm, lambda r: pltpu.make_async_copy(ys_ref.at[pl.ds(pos_ref[base + r], 1)],
                                                    buf_ref.at[pl.ds(r, 1)], sem))
    o_ref[...] = x_ref[...] + _from_row_tiles(buf_ref)


def _moe_routed(x2d, t3, route, counts, router, w_gate, w_up, w_down, tm=512):
    t = x2d.shape[0]
    te = MOE_TILE
    n_tiles = t // te + N_GROUPS
    n_rows = n_tiles * te
    grp = route[:, 0].astype(jnp.int32)
    rank = route[:, 1].astype(jnp.int32)
    cnt = counts[0, 0:N_GROUPS].astype(jnp.int32)
    padded = ((cnt + te - 1) // te) * te
    ends = jnp.cumsum(padded)
    pos = (ends - padded)[grp] + rank
    starts = jnp.arange(n_tiles, dtype=jnp.int32) * te
    tile_grp = jnp.minimum(jnp.sum((starts[:, None] >= ends[None, :]).astype(jnp.int32), axis=1), N_GROUPS - 1)
    tile_valid = (starts < ends[-1]).astype(jnp.int32)

    xs = pl.pallas_call(
        _moe_scatter_kernel,
        grid_spec=pltpu.PrefetchScalarGridSpec(
            num_scalar_prefetch=1, grid=(t // tm,),
            in_specs=[pl.BlockSpec((tm,) + ROW_TILE, lambda i, pos: (i, 0, 0)), pl.BlockSpec(memory_space=pl.ANY)],
            out_specs=pl.BlockSpec(memory_space=pl.ANY),
            scratch_shapes=[pltpu.SemaphoreType.DMA(())]),
        out_shape=jax.ShapeDtypeStruct((n_rows,) + ROW_TILE, F32),
        input_output_aliases={2: 0},
        compiler_params=_cparams(("arbitrary",)),
        name="moe_scatter",
    )(pos, t3, jnp.zeros((n_rows,) + ROW_TILE, F32))

    wg = w_gate.reshape(N_EXPERTS, D_MODEL, D_EXPERT).astype(BF16)
    wu = w_up.reshape(N_EXPERTS, D_MODEL, D_EXPERT).astype(BF16)
    wd = w_down.reshape(N_EXPERTS, D_EXPERT, D_MODEL).astype(BF16)
    wsel = lambda i, e, g, v: (g[i] * EXPERTS_PER_GROUP + e, 0, 0)
    ys = pl.pallas_call(
        _moe_expert_kernel,
        grid_spec=pltpu.PrefetchScalarGridSpec(
            num_scalar_prefetch=2, grid=(n_tiles, EXPERTS_PER_GROUP),
            in_specs=[pl.BlockSpec((te,) + ROW_TILE, lambda i, e, g, v: (i, 0, 0)),
                      pl.BlockSpec((D_MODEL, LANES), lambda i, e, g, v: (0, 0)),
                      pl.BlockSpec((D_MODEL, LANES), lambda i, e, g, v: (0, 0)),
                      pl.BlockSpec((1, LANES), lambda i, e, g, v: (0, 0)),
                      pl.BlockSpec((None, D_MODEL, D_EXPERT), wsel),
                      pl.BlockSpec((None, D_MODEL, D_EXPERT), wsel),
                      pl.BlockSpec((None, D_EXPERT, D_MODEL), wsel)],
            out_specs=pl.BlockSpec((te,) + ROW_TILE, lambda i, e, g, v: (i, 0, 0)),
            scratch_shapes=[pltpu.VMEM((te, D_MODEL), BF16), pltpu.VMEM((te, LANES), F32),
                            pltpu.VMEM((te, D_MODEL), F32)]),
        out_shape=jax.ShapeDtypeStruct((n_rows,) + ROW_TILE, F32),
        compiler_params=_cparams(("arbitrary", "arbitrary")),
        name="moe_experts",
    )(tile_grp, tile_valid, xs, *router, wg, wu, wd)

    return pl.pallas_call(
        _moe_combine_kernel,
        grid_spec=pltpu.PrefetchScalarGridSpec(
            num_scalar_prefetch=1, grid=(t // tm,),
            in_specs=[pl.BlockSpec((tm, D_MODEL), lambda i, pos: (i, 0)), pl.BlockSpec(memory_space=pl.ANY)],
            out_specs=pl.BlockSpec((tm, D_MODEL), lambda i, pos: (i, 0)),
            scratch_shapes=[pltpu.VMEM((tm,) + ROW_TILE, F32), pltpu.SemaphoreType.DMA(())]),
        out_shape=jax.ShapeDtypeStruct((t, D_MODEL), F32),
        compiler_params=_cparams(("arbitrary",)),
        name="moe_combine",
    )(pos, x2d, ys)


def _alibi_slopes():
    n = 2 * GROUP_HEADS
    s = 2.0 ** (-8.0 * np.arange(1, n + 1) / n)
    return s[0::2], s[1::2]


def kernel(x, norm_mix, norm_ffn, w_in, w_out, diff_q_gain, diff_k_gain, diff_lambda, diff_sub_gain, mlstm_conv, mlstm_gate_bias, mlstm_out_gain, hgrn_lower_bounds, hgrn_out_gain, nsa_cmp_pos, nsa_cmp_w1, nsa_cmp_w2, nsa_q_gain, nsa_k_gain, nsa_out_gain, moe_w_group, moe_b_group, moe_w_expert, moe_b_expert, moe_w_gate, moe_w_up, moe_w_down):
    b, s, d = x.shape
    slopes_diff, slopes_nsa = _alibi_slopes()
    lb_soft = jax.nn.softmax(hgrn_lower_bounds.astype(F32), axis=0)
    lower_bounds = jnp.cumsum(lb_soft, axis=0) - lb_soft[0]
    x2d = x.reshape(b * s, d)
    for l in range(norm_mix.shape[0]):
        p = _inproj(x2d, norm_mix[l], _pack_w_in(w_in[l])).reshape(b, s, P_COLS)
        y_a = _diff_attention(p, diff_q_gain[l], diff_k_gain[l], diff_lambda[l], diff_sub_gain[l], slopes_diff, l)
        y_b = _mlstm(p, mlstm_conv[l], mlstm_gate_bias[l], mlstm_out_gain[l])
        y_c = _hgrn2(p, lower_bounds[l], hgrn_out_gain[l])
        y_d = _nsa(p, nsa_cmp_pos[l], nsa_cmp_w1[l], nsa_cmp_w2[l], nsa_q_gain[l], nsa_k_gain[l],
                   nsa_out_gain[l], slopes_nsa)
        ys = [y.reshape(b * s, GROUP_WIDTH) for y in (y_a, y_b, y_c, y_d)]
        x2d, t3, route, counts, router = _outproj(x2d, ys, w_out[l], norm_ffn[l], moe_w_group[l], moe_b_group[l],
                                                  moe_w_expert[l], moe_b_expert[l])
        x2d = _moe_routed(x2d, t3, route, counts, router, moe_w_gate[l], moe_w_up[l], moe_w_down[l])
    return x2d.reshape(b, s, d)
```

```python
import functools
import math

import numpy as np
import jax
import jax.numpy as jnp
from jax import lax
from jax.experimental import pallas as pl
from jax.experimental.pallas import tpu as pltpu

F32 = jnp.float32
BF16 = jnp.bfloat16

D_MODEL = 1024
HEAD_DIM = 64
GROUP_HEADS = 4
GROUP_WIDTH = GROUP_HEADS * HEAD_DIM
DIFF_HALF = HEAD_DIM // 2
EPS = 1e-6
NEG_INF = -1e30
LOG2E = math.log2(math.e)
M_INIT = -1e30
MASKED = -2e30

LANES = 128
VMEM_LIMIT = 48 * 1024 * 1024

COL_A = 0
COL_B = 768
COL_C = 1792
COL_D = 2816
COL_G = 3456
P_COLS = 3584


def _cparams(sem, flags=None):
    return pltpu.CompilerParams(dimension_semantics=sem, vmem_limit_bytes=VMEM_LIMIT, flags=flags)


def _block_ones(width, seg):
    i = np.arange(width)
    return jnp.asarray((i[:, None] // seg == i[None, :] // seg).astype(np.float32), BF16)


def _seg_sum(x, bd):
    hi = x.astype(BF16)
    lo = (x - hi.astype(F32)).astype(BF16)
    return (jnp.dot(hi, bd, preferred_element_type=F32) + jnp.dot(lo, bd, preferred_element_type=F32))


def _inproj_kernel(x_ref, g_ref, w_ref, o_ref):
    x = x_ref[...]
    ms = jnp.mean(x * x, axis=-1, keepdims=True)
    h = (x * lax.rsqrt(ms + EPS) * g_ref[...]).astype(BF16)
    o_ref[...] = jnp.dot(h, w_ref[...], preferred_element_type=F32)


def _inproj(x2d, gain, w_cat, tm=256):
    t = x2d.shape[0]
    return pl.pallas_call(
        _inproj_kernel,
        grid=(t // tm,),
        in_specs=[pl.BlockSpec((tm, D_MODEL), lambda i: (i, 0)),
                  pl.BlockSpec((1, D_MODEL), lambda i: (0, 0)),
                  pl.BlockSpec((D_MODEL, P_COLS), lambda i: (0, 0))],
        out_specs=pl.BlockSpec((tm, P_COLS), lambda i: (i, 0)),
        out_shape=jax.ShapeDtypeStruct((t, P_COLS), F32),
        compiler_params=_cparams(("arbitrary",)),
        name="inproj",
    )(x2d, gain.reshape(1, D_MODEL), w_cat)


IN_COLS = 3476


def _pack_w_in_kernel(w_ref, o_ref):
    w = w_ref[...]
    a_b = w[:, 0:1536]
    gates_b = w[:, 1536:1544]
    rest = w[:, 1544:3464]
    gates_d = w[:, 3464:3476]
    pad = jnp.zeros((w.shape[0], P_COLS - COL_G - 20), F32)
    o_ref[...] = jnp.concatenate([a_b, rest, gates_b, gates_d, pad], axis=1).astype(BF16)


def _pack_w_in(w, tr=128):
    d = w.shape[0]
    return pl.pallas_call(
        _pack_w_in_kernel,
        grid=(d // tr,),
        in_specs=[pl.BlockSpec((tr, IN_COLS), lambda i: (i, 0))],
        out_specs=pl.BlockSpec((tr, P_COLS), lambda i: (i, 0)),
        out_shape=jax.ShapeDtypeStruct((d, P_COLS), BF16),
        compiler_params=_cparams(("arbitrary",)),
        name="pack_w_in",
    )(w)


def _diff_prep_kernel(p_ref, qg_ref, kg_ref, bd_ref, w_ref, wt_ref, q_ref, k_ref, vt_ref):
    p = p_ref[...]
    bd = bd_ref[...]

    def norm(x, g):
        ss = _seg_sum(x * x, bd)
        return x * lax.rsqrt(ss * (1.0 / DIFF_HALF) + EPS) * g

    q = norm(p[:, 0:256], qg_ref[...]) * (DIFF_HALF ** -0.5 * LOG2E)
    k = norm(p[:, 256:512], kg_ref[...])
    q_ref[...] = q.T.astype(BF16)
    k_ref[...] = k.astype(BF16)
    v = p[:, 512:768]
    v_t = v.T
    vw_t = (v * w_ref[...]).T
    w_t = wt_ref[...]
    tm = v_t.shape[1]
    first = lax.broadcasted_iota(jnp.int32, (V_AUG - HEAD_DIM, tm), 0) == 0
    blocks = []
    for h in range(GROUP_HEADS):
        hs = slice(HEAD_DIM * h, HEAD_DIM * (h + 1))
        blocks += [vw_t[hs, :], jnp.where(first, jnp.broadcast_to(w_t[h:h + 1, :], first.shape), 0.0),
                   v_t[hs, :], jnp.where(first, 1.0, 0.0)]
    vt_ref[...] = jnp.concatenate(blocks, axis=0).astype(BF16)


V_AUG = 80
V_HEAD = 2 * V_AUG


def _key_weights(tile, slopes):
    kl = np.arange(tile, dtype=np.float64) - (tile - 1)
    w = np.stack([np.exp2(sl * LOG2E * kl) for sl in slopes])
    wide = np.repeat(w.T, HEAD_DIM, axis=1)
    w8 = np.zeros((8, tile)); w8[:len(slopes)] = w
    return jnp.asarray(wide.astype(np.float32)), jnp.asarray(w8.astype(np.float32))


def _diag_bias(tq, slopes):
    k = np.arange(tq)[:, None]
    q = np.arange(tq)[None, :]
    tabs = [np.where(k <= q, sl * LOG2E * k.astype(np.float64), MASKED) for sl in slopes]
    return jnp.asarray(np.stack(tabs).astype(np.float32))


AHEAD = 2


def _flash_update(s_ref, bias_ref, off, vt, m_ref, acc_ref, penalty=None):
    tq = s_ref.shape[1]
    for c in range(tq // LANES):
        cols = slice(LANES * c, LANES * (c + 1))
        s = s_ref[:, cols]
        if bias_ref is not None:
            s = s + bias_ref[:, cols]
        if penalty is not None:
            s = s + penalty
        m_tile = jnp.max(s, axis=0, keepdims=True)
        p = jnp.exp2(s - m_tile).astype(BF16)
        m_old = m_ref[0:1, cols]
        m_new = jnp.maximum(m_old, m_tile + off)
        m_ref[0:1, cols] = m_new
        pv = jnp.dot(vt, p, preferred_element_type=F32)
        acc_ref[:, cols] = jnp.exp2(m_old - m_new) * acc_ref[:, cols] + jnp.exp2(m_tile + off - m_new) * pv


def _diff_attn_kernel(lam_ref, qt_ref, k_ref, vt_ref, db_ref, sg_ref, bd_ref, o_ref,
                      qm_ref, acc_ref, m_ref, s_ref, *, tq, slopes, out_scale):
    qi = pl.program_id(1)
    n_half = 2 * GROUP_HEADS
    per_tile = LANES // DIFF_HALF
    row = lax.broadcasted_iota(jnp.int32, (LANES, tq), 0)
    for j in range(n_half):
        slab = qt_ref[LANES * (j // per_tile):LANES * (j // per_tile + 1), :].astype(F32)
        r0 = DIFF_HALF * (j % per_tile)
        qm_ref[j] = jnp.where((row >= r0) & (row < r0 + DIFF_HALF), slab, 0.0).astype(BF16)
    acc_ref[...] = jnp.zeros(acc_ref.shape, F32)
    m_ref[...] = jnp.full(m_ref.shape, M_INIT, F32)

    n_slots = s_ref.shape[0]

    def scores(kt, j):
        start = pl.multiple_of(kt * tq, tq)
        kj = k_ref[pl.ds(start, tq), LANES * (j // per_tile):LANES * (j // per_tile + 1)]
        s_ref[j % n_slots] = jnp.dot(kj, qm_ref[j], preferred_element_type=F32)

    def process(kt, j, diag):
        h = j // 2
        start = pl.multiple_of(kt * tq, tq)
        off = 0.0 if diag else ((kt - qi) * tq + (tq - 1)).astype(F32) * (slopes[h] * LOG2E)
        base = V_HEAD * h + (V_AUG if diag else 0)
        vt = vt_ref[base:base + V_AUG, pl.ds(start, tq)]
        _flash_update(s_ref.at[j % n_slots], db_ref.at[h] if diag else None, off, vt, m_ref.at[j], acc_ref.at[j])

    def step(kt, diag):
        for j in range(n_half):
            if j + AHEAD < n_half:
                scores(kt, j + AHEAD)
            elif not diag:
                scores(kt + 1, j + AHEAD - n_half)
            process(kt, j, diag)

    for j in range(AHEAD):
        scores(0, j)

    def body(kt, carry):
        step(kt, False)
        return carry

    lax.fori_loop(0, qi, body, 0)
    step(qi, True)

    lam = lam_ref[0, 0]
    heads = []
    for h in range(GROUP_HEADS):
        a0, a1 = acc_ref[2 * h], acc_ref[2 * h + 1]
        o0 = a0[0:HEAD_DIM, :] / a0[HEAD_DIM:HEAD_DIM + 1, :]
        o1 = a1[0:HEAD_DIM, :] / a1[HEAD_DIM:HEAD_DIM + 1, :]
        heads.append(o0 - lam * o1)
    o = jnp.concatenate(heads, axis=0).T
    ss = _seg_sum(o * o, bd_ref[...])
    o_ref[...] = o * lax.rsqrt(ss * (1.0 / HEAD_DIM) + EPS) * (sg_ref[...] * out_scale)


def _diff_attention(p3, q_gain, k_gain, lam_vecs, sub_gain, slopes, layer_idx, tq=256):
    b, s, _ = p3.shape
    tm = tq
    bd32 = _block_ones(GROUP_WIDTH, DIFF_HALF)
    bd64 = _block_ones(GROUP_WIDTH, HEAD_DIM)
    qg = jnp.tile(q_gain.astype(F32), 2 * GROUP_HEADS).reshape(1, GROUP_WIDTH)
    kg = jnp.tile(k_gain.astype(F32), 2 * GROUP_HEADS).reshape(1, GROUP_WIDTH)
    w_wide, w_t = _key_weights(tm, slopes)
    v_rows = GROUP_HEADS * V_HEAD
    qn, kn, vt = pl.pallas_call(
        _diff_prep_kernel,
        grid=(b, s // tm),
        in_specs=[pl.BlockSpec((None, tm, 768), lambda bi, i: (bi, i, COL_A // 768)),
                  pl.BlockSpec((1, GROUP_WIDTH), lambda bi, i: (0, 0)),
                  pl.BlockSpec((1, GROUP_WIDTH), lambda bi, i: (0, 0)),
                  pl.BlockSpec((GROUP_WIDTH, GROUP_WIDTH), lambda bi, i: (0, 0)),
                  pl.BlockSpec((tm, GROUP_WIDTH), lambda bi, i: (0, 0)),
                  pl.BlockSpec((8, tm), lambda bi, i: (0, 0))],
        out_specs=[pl.BlockSpec((None, GROUP_WIDTH, tm), lambda bi, i: (bi, 0, i)),
                   pl.BlockSpec((None, tm, GROUP_WIDTH), lambda bi, i: (bi, i, 0)),
                   pl.BlockSpec((None, v_rows, tm), lambda bi, i: (bi, 0, i))],
        out_shape=[jax.ShapeDtypeStruct((b, GROUP_WIDTH, s), BF16),
                   jax.ShapeDtypeStruct((b, s, GROUP_WIDTH), BF16),
                   jax.ShapeDtypeStruct((b, v_rows, s), BF16)],
        compiler_params=_cparams(("arbitrary", "arbitrary")),
        name="diff_prep",
    )(p3, qg, kg, bd32, w_wide, w_t)

    lam_init = 0.8 - 0.6 * math.exp(-0.3 * layer_idx)
    lv = lam_vecs.astype(F32)
    lam = (jnp.exp(jnp.dot(lv[0], lv[1])) - jnp.exp(jnp.dot(lv[2], lv[3])) + lam_init).reshape(1, 1)
    kern = functools.partial(_diff_attn_kernel, tq=tq, slopes=tuple(float(v) for v in slopes),
                             out_scale=1.0 - lam_init)
    return pl.pallas_call(
        kern,
        grid=(b, s // tq),
        in_specs=[pl.BlockSpec(memory_space=pltpu.SMEM),
                  pl.BlockSpec((None, GROUP_WIDTH, tq), lambda bi, i: (bi, 0, i)),
                  pl.BlockSpec((None, s, GROUP_WIDTH), lambda bi, i: (bi, 0, 0)),
                  pl.BlockSpec((None, v_rows, s), lambda bi, i: (bi, 0, 0)),
                  pl.BlockSpec((GROUP_HEADS, tq, tq), lambda bi, i: (0, 0, 0)),
                  pl.BlockSpec((1, GROUP_WIDTH), lambda bi, i: (0, 0)),
                  pl.BlockSpec((GROUP_WIDTH, GROUP_WIDTH), lambda bi, i: (0, 0))],
        out_specs=pl.BlockSpec((None, tq, GROUP_WIDTH), lambda bi, i: (bi, i, 0)),
        out_shape=jax.ShapeDtypeStruct((b, s, GROUP_WIDTH), F32),
        scratch_shapes=[pltpu.VMEM((2 * GROUP_HEADS, LANES, tq), BF16),
                        pltpu.VMEM((2 * GROUP_HEADS, V_AUG, tq), F32),
                        pltpu.VMEM((2 * GROUP_HEADS, 8, tq), F32),
                        pltpu.VMEM((2 * AHEAD, tq, tq), F32)],
        compiler_params=_cparams(("arbitrary", "arbitrary")),
        name="diff_attn",
    )(lam, qn, kn, vt, _diag_bias(tq, slopes), sub_gain.astype(F32).reshape(1, GROUP_WIDTH), bd64)


MLSTM_CONV = 4
CONV_HALO = 8
AUG = 128


def _tril_sum(x, tril):
    hi = x.astype(BF16)
    lo = (x - hi.astype(F32)).astype(BF16)
    return jnp.dot(tril, hi, preferred_element_type=F32) + jnp.dot(tril, lo, preferred_element_type=F32)


def _log_sigmoid(x):
    return jnp.minimum(x, 0.0) - jnp.log1p(jnp.exp(-jnp.abs(x)))


def _sigmoid(x):
    return 1.0 / (1.0 + jnp.exp(-x))


def _mlstm_kernel(q_ref, k_ref, v_ref, o_ref, g_ref, cw_ref, gb_ref, og_ref, bd_ref, y_ref,
                  halo_ref, c_ref, m_ref, *, L):
    ci = pl.program_id(1)

    @pl.when(ci == 0)
    def _():
        halo_ref[0:CONV_HALO, :] = jnp.zeros((CONV_HALO, 2 * GROUP_WIDTH), F32)
        c_ref[...] = jnp.zeros_like(c_ref)
        m_ref[...] = jnp.zeros_like(m_ref)

    halo_ref[CONV_HALO:CONV_HALO + L, 0:GROUP_WIDTH] = q_ref[...]
    halo_ref[CONV_HALO:CONV_HALO + L, GROUP_WIDTH:2 * GROUP_WIDTH] = k_ref[...]
    conv = jnp.zeros((L, 2 * GROUP_WIDTH), F32)
    for j in range(MLSTM_CONV):
        start = CONV_HALO - (MLSTM_CONV - 1) + j
        conv = conv + halo_ref[start:start + L, :] * cw_ref[j:j + 1, :]
    halo_ref[0:CONV_HALO, :] = halo_ref[L:L + CONV_HALO, :]
    qk = conv * _sigmoid(conv)
    q = qk[:, 0:GROUP_WIDTH] * (HEAD_DIM ** -0.5)
    k = qk[:, GROUP_WIDTH:2 * GROUP_WIDTH]
    v = v_ref[...]

    gates = g_ref[...] + gb_ref[...]
    ri = lax.broadcasted_iota(jnp.int32, (L, L), 0)
    cj = lax.broadcasted_iota(jnp.int32, (L, L), 1)
    causal = ri >= cj
    tril = jnp.where(causal, 1.0, 0.0).astype(BF16)
    gcum = _tril_sum(_log_sigmoid(gates), tril)
    gcum_t = gcum.T
    gates_t = gates.T
    lane_aug = lax.broadcasted_iota(jnp.int32, (L, AUG), 1)
    m_all = m_ref[...]

    outs = []
    for h in range(GROUP_HEADS):
        hs = slice(HEAD_DIM * h, HEAD_DIM * (h + 1))
        q_h, k_h = q[:, hs], k[:, hs]
        v_aug = jnp.where(lane_aug == HEAD_DIM, 1.0,
                          jnp.concatenate([v[:, hs], jnp.zeros((L, AUG - HEAD_DIM), F32)], axis=1))
        g_col = gcum[:, 4 + h:5 + h]
        li_col = gates[:, h:h + 1]
        g_row = gcum_t[4 + h:5 + h, :]
        li_row = gates_t[h:h + 1, :]
        g_last = g_row[:, L - 1:L]
        m_prev = m_all[:, h:h + 1]
        log_d = jnp.where(causal, g_col - g_row + li_row, NEG_INF)
        log_inter = g_col + m_prev
        m_t = jnp.maximum(log_inter, jnp.max(log_d, axis=1, keepdims=True))
        s_qk = lax.dot_general(q_h.astype(BF16), k_h.astype(BF16), (((1,), (1,)), ((), ())),
                               preferred_element_type=F32)
        w_intra = s_qk * jnp.exp(log_d - m_t)
        w_inter = jnp.exp(log_inter - m_t)
        c_aug = c_ref[h]
        num = (w_inter * jnp.dot(q_h.astype(BF16), c_aug.astype(BF16), preferred_element_type=F32)
               + jnp.dot(w_intra.astype(BF16), v_aug.astype(BF16), preferred_element_type=F32))
        den = num[:, HEAD_DIM:HEAD_DIM + 1]
        outs.append(num[:, 0:HEAD_DIM] / jnp.maximum(jnp.abs(den), jnp.exp(-m_t)))

        log_a = g_last - g_col + li_col
        m_new = jnp.maximum(g_last + m_prev, jnp.max(log_a, axis=0, keepdims=True))
        a_col = jnp.exp(log_a - m_new)
        decay = jnp.exp(g_last + m_prev - m_new)
        ak_t = (k_h * a_col).T.astype(BF16)
        c_ref[h] = decay * c_aug + jnp.dot(ak_t, v_aug.astype(BF16), preferred_element_type=F32)
        m_ref[:, h:h + 1] = m_new

    hcat = jnp.concatenate(outs, axis=1)
    ss = _seg_sum(hcat * hcat, bd_ref[...])
    y_ref[...] = hcat * lax.rsqrt(ss * (1.0 / HEAD_DIM) + EPS) * og_ref[...] * _sigmoid(o_ref[...])


def _mlstm(p3, conv_w, gate_bias, out_gain, L=256):
    b, s, _ = p3.shape
    gb = jnp.zeros((1, LANES), F32).at[0, 0:GROUP_HEADS].set(gate_bias[0]).at[0, GROUP_HEADS:2 * GROUP_HEADS].set(gate_bias[1])
    col = lambda off: (lambda bi, i: (bi, i, off // GROUP_WIDTH))
    return pl.pallas_call(
        functools.partial(_mlstm_kernel, L=L),
        grid=(b, s // L),
        in_specs=[pl.BlockSpec((None, L, GROUP_WIDTH), col(COL_B)),
                  pl.BlockSpec((None, L, GROUP_WIDTH), col(COL_B + 256)),
                  pl.BlockSpec((None, L, GROUP_WIDTH), col(COL_B + 512)),
                  pl.BlockSpec((None, L, GROUP_WIDTH), col(COL_B + 768)),
                  pl.BlockSpec((None, L, LANES), lambda bi, i: (bi, i, COL_G // LANES)),
                  pl.BlockSpec((MLSTM_CONV, 2 * GROUP_WIDTH), lambda bi, i: (0, 0)),
                  pl.BlockSpec((1, LANES), lambda bi, i: (0, 0)),
                  pl.BlockSpec((1, GROUP_WIDTH), lambda bi, i: (0, 0)),
                  pl.BlockSpec((GROUP_WIDTH, GROUP_WIDTH), lambda bi, i: (0, 0))],
        out_specs=pl.BlockSpec((None, L, GROUP_WIDTH), lambda bi, i: (bi, i, 0)),
        out_shape=jax.ShapeDtypeStruct((b, s, GROUP_WIDTH), F32),
        scratch_shapes=[pltpu.VMEM((CONV_HALO + L, 2 * GROUP_WIDTH), F32),
                        pltpu.VMEM((GROUP_HEADS, HEAD_DIM, AUG), F32),
                        pltpu.VMEM((1, LANES), F32)],
        compiler_params=_cparams(("arbitrary", "arbitrary")),
        name="mlstm",
    )(p3, p3, p3, p3, p3, conv_w.astype(F32), gb, out_gain.astype(F32).reshape(1, GROUP_WIDTH),
      _block_ones(GROUP_WIDTH, HEAD_DIM))


HGRN_CHUNK = 16
LB_FLOOR = 1e-30


def _hgrn_kernel(q_ref, f_ref, i_ref, g_ref, lb_ref, og_ref, bd_ref, y_ref,
                 lf_s, kk_s, vv_s, st_ref, *, TL):
    ci = pl.program_id(1)
    C = HGRN_CHUNK
    W = GROUP_WIDTH

    @pl.when(ci == 0)
    def _():
        lf_s[0:C, :] = jnp.zeros((C, W), F32)
        kk_s[0:C, :] = jnp.zeros((C, W), F32)
        vv_s[0:C, :] = jnp.zeros((C, W), F32)
        st_ref[...] = jnp.zeros_like(st_ref)

    z = f_ref[...]
    a = lb_ref[0:1, :]
    c = lb_ref[1:2, :] + _log_sigmoid(z)
    mx = jnp.maximum(a, c)
    lf = mx + jnp.log1p(jnp.exp(-jnp.abs(a - c)))
    kk = lb_ref[2:3, :] * _sigmoid(-z) + lb_ref[3:4, :]
    qx = q_ref[...]
    qs = qx * _sigmoid(qx)
    vv = i_ref[...]
    lf_s[C:C + TL, :] = lf
    kk_s[C:C + TL, :] = kk
    vv_s[C:C + TL, :] = vv

    bd = bd_ref[...]
    row = lax.broadcasted_iota(jnp.int32, (TL, W), 0)
    rmod = row & (C - 1)

    acc = jnp.zeros((TL, W), F32)
    dsum = jnp.zeros((TL, W), F32)
    for delta in range(C):
        if delta > 0:
            dsum = dsum + lf_s[C - (delta - 1):C - (delta - 1) + TL, :]
        x = qs * kk_s[C - delta:C - delta + TL, :] * jnp.exp(dsum)
        x = jnp.where(rmod >= delta, x, 0.0)
        att = jnp.dot(x.astype(BF16), bd, preferred_element_type=F32)
        acc = acc + att * vv_s[C - delta:C - delta + TL, :]

    ri = lax.broadcasted_iota(jnp.int32, (TL, TL), 0)
    cj = lax.broadcasted_iota(jnp.int32, (TL, TL), 1)
    same = (ri // C) == (cj // C)
    tril = jnp.where(same & (ri >= cj), 1.0, 0.0).astype(BF16)
    ones = jnp.where(same, 1.0, 0.0).astype(BF16)
    bcum = _tril_sum(lf, tril)
    blast = _tril_sum(lf, ones)
    q_a = (qs * jnp.exp(bcum)).astype(BF16)
    k_b = (kk * jnp.exp(blast - bcum)).astype(BF16)
    dec = jnp.exp(blast)
    vv_b = vv.astype(BF16)
    hmask = bd.astype(F32)
    state = st_ref[...]
    inters = []
    for ch in range(TL // C):
        r = slice(ch * C, (ch + 1) * C)
        inters.append(lax.dot_general(q_a[r], state.astype(BF16), (((1,), (1,)), ((), ())),
                                      preferred_element_type=F32))
        upd = lax.dot_general(vv_b[r], k_b[r], (((0,), (0,)), ((), ())), preferred_element_type=F32)
        state = state * dec[ch * C:ch * C + 1, :] + upd * hmask
    st_ref[...] = state
    o = acc + jnp.concatenate(inters, axis=0)
    ss = _seg_sum(o * o, bd)
    y_ref[...] = o * lax.rsqrt(ss * (1.0 / HEAD_DIM) + EPS) * og_ref[...] * _sigmoid(g_ref[...])


def _hgrn2(p3, lower_bound, out_gain, TL=256):
    b, s, _ = p3.shape
    lb = lower_bound.astype(F32)
    lbf = jnp.maximum(lb, LB_FLOOR)
    lbp = jnp.stack([jnp.log(lbf), jnp.log1p(-lb), 1.0 - lb, lb - lbf])
    col = lambda off: (lambda bi, i: (bi, i, off // GROUP_WIDTH))
    C = HGRN_CHUNK
    return pl.pallas_call(
        functools.partial(_hgrn_kernel, TL=TL),
        grid=(b, s // TL),
        in_specs=[pl.BlockSpec((None, TL, GROUP_WIDTH), col(COL_C)),
                  pl.BlockSpec((None, TL, GROUP_WIDTH), col(COL_C + 256)),
                  pl.BlockSpec((None, TL, GROUP_WIDTH), col(COL_C + 512)),
                  pl.BlockSpec((None, TL, GROUP_WIDTH), col(COL_C + 768)),
                  pl.BlockSpec((4, GROUP_WIDTH), lambda bi, i: (0, 0)),
                  pl.BlockSpec((1, GROUP_WIDTH), lambda bi, i: (0, 0)),
                  pl.BlockSpec((GROUP_WIDTH, GROUP_WIDTH), lambda bi, i: (0, 0))],
        out_specs=pl.BlockSpec((None, TL, GROUP_WIDTH), lambda bi, i: (bi, i, 0)),
        out_shape=jax.ShapeDtypeStruct((b, s, GROUP_WIDTH), F32),
        scratch_shapes=[pltpu.VMEM((C + TL, GROUP_WIDTH), F32),
                        pltpu.VMEM((C + TL, GROUP_WIDTH), F32),
                        pltpu.VMEM((C + TL, GROUP_WIDTH), F32),
                        pltpu.VMEM((GROUP_WIDTH, GROUP_WIDTH), F32)],
        compiler_params=_cparams(("arbitrary", "arbitrary")),
        name="hgrn2",
    )(p3, p3, p3, p3, lbp, out_gain.astype(F32).reshape(1, GROUP_WIDTH), _block_ones(GROUP_WIDTH, HEAD_DIM))


CMP_LEN = 32
CMP_STRIDE = 16
SEL_BLOCK = 64
SEL_TOPK = 16
WINDOW = 512
GATE_ROW = 8


def _rms_rows(x, gain):
    return x * lax.rsqrt(jnp.mean(x * x, axis=-1, keepdims=True) + EPS) * gain


NSA_V_ROWS = (GROUP_HEADS + 1) * V_AUG


def _aug_shared_values(v, w_wide, w_t):
    tm = v.shape[0]
    vw_t = (jnp.concatenate([v] * GROUP_HEADS, axis=1) * w_wide).T
    first = lax.broadcasted_iota(jnp.int32, (V_AUG - HEAD_DIM, tm), 0) == 0
    blocks = []
    for h in range(GROUP_HEADS):
        blocks += [vw_t[HEAD_DIM * h:HEAD_DIM * (h + 1), :],
                   jnp.where(first, jnp.broadcast_to(w_t[h:h + 1, :], first.shape), 0.0)]
    blocks += [v.T, jnp.where(first, 1.0, 0.0)]
    return jnp.concatenate(blocks, axis=0).astype(BF16)


def _nsa_prep_kernel(q_ref, kv_ref, g_ref, qg_ref, kg_ref, bd_ref, w_ref, wt_ref,
                     qt_ref, kc_ref, vc_ref, ks_ref, vst_ref, kw_ref, vwt_ref, gt_ref):
    ti = pl.program_id(1)
    q = q_ref[...]
    ss = _seg_sum(q * q, bd_ref[...])
    qn = q * lax.rsqrt(ss * (1.0 / HEAD_DIM) + EPS) * qg_ref[...] * (HEAD_DIM ** -0.5 * LOG2E)
    qt_ref[...] = qn.T.astype(BF16)
    kv = kv_ref[...]
    tm = kv.shape[0]
    kc_ref[...] = kv[:, 0:64]
    vc_ref[...] = kv[:, 64:128]
    ks = _rms_rows(kv[:, 128:192], kg_ref[1:2, :])
    lane = lax.broadcasted_iota(jnp.int32, (tm, LANES), 1)
    blk = jnp.right_shift(ti * tm + lax.broadcasted_iota(jnp.int32, (tm, LANES), 0), SEL_BLOCK.bit_length() - 1)
    onehot = jnp.where(lane == blk + HEAD_DIM, 1.0, 0.0)
    ks_ref[...] = jnp.where(lane < HEAD_DIM, jnp.concatenate([ks, ks], axis=1), onehot).astype(BF16)
    kw = _rms_rows(kv[:, 256:320], kg_ref[2:3, :])
    kw_ref[...] = jnp.concatenate([kw, jnp.zeros_like(kw)], axis=1).astype(BF16)
    vst_ref[...] = _aug_shared_values(kv[:, 192:256], w_ref[...], wt_ref[...])
    vwt_ref[...] = _aug_shared_values(kv[:, 320:384], w_ref[...], wt_ref[...])
    gt_ref[...] = _sigmoid(g_ref[...]).T


def _gelu_tanh(x):
    return 0.5 * x * (1.0 + jnp.tanh(math.sqrt(2.0 / math.pi) * (x + 0.044715 * x * x * x)))


def _nsa_cmp_kernel(kr_ref, vr_ref, w1_ref, w2_ref, pos_ref, kg_ref, kc_ref, vct_ref, sh_ref, *, n_rows):
    half = CMP_STRIDE * HEAD_DIM
    sh_ref[n_rows:n_rows + 8, :] = jnp.zeros((8, HEAD_DIM), F32)
    outs = []
    for j, x_ref in enumerate((kr_ref, vr_ref)):
        r = x_ref[...].astype(BF16)
        w1 = w1_ref[j]
        first = jnp.dot(r, w1[0:half, :], preferred_element_type=F32)
        sh_ref[0:n_rows, :] = jnp.dot(r, w1[half:2 * half, :], preferred_element_type=F32)
        pos8 = jnp.broadcast_to(pos_ref[j], (8, 2 * half))
        posw = jnp.dot(pos8, w1.astype(F32), preferred_element_type=F32)[0:1, :]
        hdn = _gelu_tanh(first + sh_ref[1:n_rows + 1, :] + posw)
        outs.append(jnp.dot(hdn, w2_ref[j].astype(F32), preferred_element_type=F32))
    kc_ref[...] = _rms_rows(outs[0], kg_ref[0:1, :]).astype(BF16)
    vct_ref[...] = outs[1].T.astype(BF16)


def _nsa_attn_kernel(qt_ref, kc_ref, vct_ref, ks_ref, vst_ref, kw_ref, vwt_ref, gt_ref, ov_ref, db_ref, lo_ref,
                     og_ref, bd_ref, y_ref, qa_ref, acc_s, m_s, acc_w, m_w, ss_ref, sw_ref,
                     *, tq, slopes, n_cmp):
    qi = pl.program_id(1)
    n_rows = kc_ref.shape[0]
    n_sel = ov_ref.shape[0]
    qpos = qi * tq + lax.broadcasted_iota(jnp.int32, (1, tq), 1)

    nio = lax.broadcasted_iota(jnp.int32, (n_rows, tq), 0)
    dist_c = qpos - (nio * CMP_STRIDE + (CMP_LEN - 1))
    valid_c = (dist_c >= 0) & (nio < n_cmp)
    dist_cf = dist_c.astype(F32)
    kc = kc_ref[...]
    vct = vct_ref[...]
    o_cmp = []
    p_sum = jnp.zeros((n_rows, tq), F32)
    for h in range(GROUP_HEADS):
        qh = qt_ref[HEAD_DIM * h:HEAD_DIM * (h + 1), :]
        s = jnp.dot(kc, qh, preferred_element_type=F32) - (slopes[h] * LOG2E) * dist_cf
        s = jnp.where(valid_c, s, MASKED)
        m = jnp.maximum(jnp.max(s, axis=0, keepdims=True), M_INIT)
        e = jnp.exp2(s - m)
        p = e / jnp.maximum(jnp.sum(e, axis=0, keepdims=True), 1e-30)
        p_sum = p_sum + p
        o_cmp.append(jnp.dot(vct, p.astype(BF16), preferred_element_type=F32))

    p_hi = p_sum.astype(BF16)
    p_lo = (p_sum - p_hi.astype(F32)).astype(BF16)
    ov = ov_ref[...]
    imp = jnp.dot(ov, p_hi, preferred_element_type=F32) + jnp.dot(ov, p_lo, preferred_element_type=F32)
    jio = lax.broadcasted_iota(jnp.int32, (n_sel, tq), 0)
    qblk = jnp.right_shift(qpos, SEL_BLOCK.bit_length() - 1)
    imp = jnp.where(jio <= qblk, imp, -1.0)
    imp = jnp.where((jio == 0) | (jio == qblk) | (jio == qblk - 1), 2.0, imp)
    rank = jnp.zeros((n_sel, tq), F32)
    for i in range(n_sel):
        row = imp[i:i + 1, :]
        beats = (row > imp) | ((row == imp) & (jio > i))
        rank = rank + jnp.where(beats, 1.0, 0.0)
    bsel = jnp.where(rank < float(min(SEL_TOPK, n_sel)), 0.0, MASKED)
    if n_sel < LANES - HEAD_DIM:
        bsel = jnp.concatenate([bsel, jnp.zeros((LANES - HEAD_DIM - n_sel, tq), F32)], axis=0)
    for h in range(GROUP_HEADS):
        qa_ref[h, 0:HEAD_DIM, :] = qt_ref[HEAD_DIM * h:HEAD_DIM * (h + 1), :]
        qa_ref[h, HEAD_DIM:LANES, :] = bsel.astype(BF16)
    for ref, val in ((acc_s, 0.0), (acc_w, 0.0), (m_s, M_INIT), (m_w, M_INIT)):
        ref[...] = jnp.full(ref.shape, val, F32)

    def tile_off(kt, h):
        return ((kt - qi) * tq + (tq - 1)).astype(F32) * (slopes[h] * LOG2E)

    def values(vt_ref, kt, h, diag):
        base = GROUP_HEADS * V_AUG if diag else V_AUG * h
        return vt_ref[base:base + V_AUG, pl.ds(pl.multiple_of(kt * tq, tq), tq)]

    def sel_scores(kt, h):
        kb = ks_ref[pl.ds(pl.multiple_of(kt * tq, tq), tq), :]
        ss_ref[h] = jnp.dot(kb, qa_ref[h], preferred_element_type=F32)

    def sel_step(kt, diag):
        for h in range(GROUP_HEADS):
            if h + AHEAD < GROUP_HEADS:
                sel_scores(kt, h + AHEAD)
            elif not diag:
                sel_scores(kt + 1, h + AHEAD - GROUP_HEADS)
            _flash_update(ss_ref.at[h], db_ref.at[h] if diag else None, 0.0 if diag else tile_off(kt, h),
                          values(vst_ref, kt, h, diag), m_s.at[h], acc_s.at[h])

    for h in range(AHEAD):
        sel_scores(0, h)

    def body(kt, carry):
        sel_step(kt, False)
        return carry

    lax.fori_loop(0, qi, body, 0)
    sel_step(qi, True)

    def win_step(back, bias_of, diag):
        kt = jnp.maximum(qi - back, 0)
        penalty = None if diag else jnp.where(qi >= back, 0.0, MASKED)
        kb = kw_ref[pl.ds(pl.multiple_of(kt * tq, tq), tq), :]

        def win_scores(h):
            sw_ref[h] = jnp.dot(kb, qa_ref[h], preferred_element_type=F32)

        for h in range(AHEAD):
            win_scores(h)
        for h in range(GROUP_HEADS):
            if h + AHEAD < GROUP_HEADS:
                win_scores(h + AHEAD)
            _flash_update(sw_ref.at[h], bias_of(h), 0.0 if diag else tile_off(kt, h),
                          values(vwt_ref, kt, h, diag), m_w.at[h], acc_w.at[h], penalty)

    win_step(2, lambda h: lo_ref, False)
    win_step(1, lambda h: None, False)
    win_step(0, lambda h: db_ref.at[h], True)

    gt = gt_ref[...]
    heads = []
    for h in range(GROUP_HEADS):
        a_s, a_w = acc_s[h], acc_w[h]
        o_sel = a_s[0:HEAD_DIM, :] / a_s[HEAD_DIM:HEAD_DIM + 1, :]
        o_win = a_w[0:HEAD_DIM, :] / a_w[HEAD_DIM:HEAD_DIM + 1, :]
        g = GATE_ROW + 3 * h
        heads.append(gt[g:g + 1, :] * o_cmp[h] + gt[g + 1:g + 2, :] * o_sel + gt[g + 2:g + 3, :] * o_win)
    o = jnp.concatenate(heads, axis=0).T
    ss = _seg_sum(o * o, bd_ref[...])
    y_ref[...] = o * lax.rsqrt(ss * (1.0 / HEAD_DIM) + EPS) * og_ref[...]


def _nsa(p3, cmp_pos, cmp_w1, cmp_w2, q_gain, k_gain, out_gain, slopes, tq=256):
    b, s, _ = p3.shape
    tm = tq
    n_rows = s // CMP_STRIDE
    n_cmp = (s - CMP_LEN) // CMP_STRIDE + 1
    n_sel = s // SEL_BLOCK
    bd64 = _block_ones(GROUP_WIDTH, HEAD_DIM)
    qg = jnp.tile(q_gain.astype(F32), GROUP_HEADS).reshape(1, GROUP_WIDTH)
    kg = k_gain.astype(F32)
    tok = lambda w: pl.BlockSpec((None, tm, w), lambda bi, i: (bi, i, 0))
    tok_t = lambda w: pl.BlockSpec((None, w, tm), lambda bi, i: (bi, 0, i))
    full = lambda r, c: pl.BlockSpec((r, c), lambda bi, i: (0, 0))
    assert n_sel <= LANES - HEAD_DIM, "block one-hot lanes hold at most 64 selection blocks"
    assert WINDOW == 2 * tq and tq % SEL_BLOCK == 0, "window branch walks exactly the key tiles qi-2, qi-1, qi"
    w_wide, w_t = _key_weights(tm, slopes)
    qt, kc, vc, ks, vst, kw, vwt, gt = pl.pallas_call(
        _nsa_prep_kernel,
        grid=(b, s // tm),
        in_specs=[pl.BlockSpec((None, tm, GROUP_WIDTH), lambda bi, i: (bi, i, COL_D // GROUP_WIDTH)),
                  pl.BlockSpec((None, tm, 384), lambda bi, i: (bi, i, (COL_D + GROUP_WIDTH) // 384)),
                  pl.BlockSpec((None, tm, LANES), lambda bi, i: (bi, i, COL_G // LANES)),
                  full(1, GROUP_WIDTH), full(3, HEAD_DIM), full(GROUP_WIDTH, GROUP_WIDTH),
                  full(tm, GROUP_WIDTH), full(8, tm)],
        out_specs=[tok_t(GROUP_WIDTH), tok(HEAD_DIM), tok(HEAD_DIM), tok(LANES), tok_t(NSA_V_ROWS),
                   tok(LANES), tok_t(NSA_V_ROWS), tok_t(LANES)],
        out_shape=[jax.ShapeDtypeStruct((b, GROUP_WIDTH, s), BF16),
                   jax.ShapeDtypeStruct((b, s, HEAD_DIM), F32),
                   jax.ShapeDtypeStruct((b, s, HEAD_DIM), F32),
                   jax.ShapeDtypeStruct((b, s, LANES), BF16),
                   jax.ShapeDtypeStruct((b, NSA_V_ROWS, s), BF16),
                   jax.ShapeDtypeStruct((b, s, LANES), BF16),
                   jax.ShapeDtypeStruct((b, NSA_V_ROWS, s), BF16),
                   jax.ShapeDtypeStruct((b, LANES, s), F32)],
        compiler_params=_cparams(("arbitrary", "arbitrary")),
        name="nsa_prep",
    )(p3, p3, p3, qg, kg, bd64, w_wide, w_t)

    row_w = CMP_STRIDE * HEAD_DIM
    kcmp, vcmp_t = pl.pallas_call(
        functools.partial(_nsa_cmp_kernel, n_rows=n_rows),
        grid=(b,),
        in_specs=[pl.BlockSpec((None, n_rows, row_w), lambda bi: (bi, 0, 0)),
                  pl.BlockSpec((None, n_rows, row_w), lambda bi: (bi, 0, 0)),
                  pl.BlockSpec((2, 2 * row_w, HEAD_DIM), lambda bi: (0, 0, 0)),
                  pl.BlockSpec((2, HEAD_DIM, HEAD_DIM), lambda bi: (0, 0, 0)),
                  pl.BlockSpec((2, 1, 2 * row_w), lambda bi: (0, 0, 0)),
                  pl.BlockSpec((3, HEAD_DIM), lambda bi: (0, 0))],
        out_specs=[pl.BlockSpec((None, n_rows, HEAD_DIM), lambda bi: (bi, 0, 0)),
                   pl.BlockSpec((None, HEAD_DIM, n_rows), lambda bi: (bi, 0, 0))],
        out_shape=[jax.ShapeDtypeStruct((b, n_rows, HEAD_DIM), BF16),
                   jax.ShapeDtypeStruct((b, HEAD_DIM, n_rows), BF16)],
        scratch_shapes=[pltpu.VMEM((n_rows + 8, HEAD_DIM), F32)],
        compiler_params=_cparams(("arbitrary",)),
        name="nsa_cmp",
    )(kc.reshape(b, n_rows, row_w), vc.reshape(b, n_rows, row_w), cmp_w1.astype(BF16), cmp_w2.astype(BF16),
      cmp_pos.astype(F32).reshape(2, 1, 2 * row_w), kg)

    cs = np.arange(n_rows)[:, None] * CMP_STRIDE
    ss = np.arange(n_sel)[None, :] * SEL_BLOCK
    overlap = np.clip(np.minimum(cs + CMP_LEN, ss + SEL_BLOCK) - np.maximum(cs, ss), 0, None) / CMP_LEN
    overlap[n_cmp:, :] = 0.0
    ov_t = jnp.asarray(overlap.T.astype(np.float32), BF16)

    seq = lambda r, c: pl.BlockSpec((None, r, c), lambda bi, i: (bi, 0, 0))
    kern = functools.partial(_nsa_attn_kernel, tq=tq, slopes=tuple(float(v) for v in slopes), n_cmp=n_cmp)
    return pl.pallas_call(
        kern,
        grid=(b, s // tq),
        in_specs=[pl.BlockSpec((None, GROUP_WIDTH, tq), lambda bi, i: (bi, 0, i)),
                  seq(n_rows, HEAD_DIM), seq(HEAD_DIM, n_rows),
                  seq(s, LANES), seq(NSA_V_ROWS, s), seq(s, LANES), seq(NSA_V_ROWS, s),
                  pl.BlockSpec((None, LANES, tq), lambda bi, i: (bi, 0, i)),
                  full(n_sel, n_rows),
                  pl.BlockSpec((GROUP_HEADS, tq, tq), lambda bi, i: (0, 0, 0)), full(tq, tq),
                  full(1, GROUP_WIDTH), full(GROUP_WIDTH, GROUP_WIDTH)],
        out_specs=pl.BlockSpec((None, tq, GROUP_WIDTH), lambda bi, i: (bi, i, 0)),
        out_shape=jax.ShapeDtypeStruct((b, s, GROUP_WIDTH), F32),
        scratch_shapes=[pltpu.VMEM((GROUP_HEADS, LANES, tq), BF16),
                        pltpu.VMEM((GROUP_HEADS, V_AUG, tq), F32), pltpu.VMEM((GROUP_HEADS, 8, tq), F32),
                        pltpu.VMEM((GROUP_HEADS, V_AUG, tq), F32), pltpu.VMEM((GROUP_HEADS, 8, tq), F32),
                        pltpu.VMEM((GROUP_HEADS, tq, tq), F32), pltpu.VMEM((GROUP_HEADS, tq, tq), F32)],
        compiler_params=_cparams(("arbitrary", "arbitrary")),
        name="nsa_attn",
    )(qt, kcmp, vcmp_t, ks, vst, kw, vwt, gt, ov_t, _diag_bias(tq, slopes), _window_low_bias(tq),
      out_gain.astype(F32).reshape(1, GROUP_WIDTH), bd64)


def _window_low_bias(tq):
    k = np.arange(tq)[:, None]
    q = np.arange(tq)[None, :]
    return jnp.asarray(np.where(k > q, 0.0, MASKED).astype(np.float32))


N_GROUPS = 4
EXPERTS_PER_GROUP = 8
N_EXPERTS = N_GROUPS * EXPERTS_PER_GROUP
D_EXPERT = 256
ROUTER_LANE0 = N_GROUPS


def _split3_dot(a, b_hi, b_lo):
    a_hi = a.astype(BF16)
    a_lo = (a - a_hi.astype(F32)).astype(BF16)
    return (jnp.dot(a_hi, b_hi, preferred_element_type=F32) + jnp.dot(a_lo, b_hi, preferred_element_type=F32)
            + jnp.dot(a_hi, b_lo, preferred_element_type=F32))


def _route(t, wr_hi, wr_lo, br):
    logits = _split3_dot(t, wr_hi, wr_lo) + br
    lane = lax.broadcasted_iota(jnp.int32, logits.shape, 1)
    lane_f = lane.astype(F32)
    big = float(LANES)
    is_g = lane < N_GROUPS
    gl = jnp.where(is_g, logits, MASKED)
    gmax = jnp.max(gl, axis=-1, keepdims=True)
    g_prob = 1.0 / jnp.sum(jnp.where(is_g, jnp.exp(gl - gmax), 0.0), axis=-1, keepdims=True)
    g_sel = jnp.min(jnp.where(is_g & (gl == gmax), lane_f, big), axis=-1, keepdims=True)
    lo = ROUTER_LANE0 + EXPERTS_PER_GROUP * g_sel
    in_grp = (lane_f >= lo) & (lane_f < lo + EXPERTS_PER_GROUP)
    el = jnp.where(in_grp, logits, MASKED)
    m1 = jnp.max(el, axis=-1, keepdims=True)
    i1 = jnp.min(jnp.where(in_grp & (el == m1), lane_f, big), axis=-1, keepdims=True)
    rest = in_grp & (lane_f != i1)
    el2 = jnp.where(rest, logits, MASKED)
    m2 = jnp.max(el2, axis=-1, keepdims=True)
    i2 = jnp.min(jnp.where(rest & (el2 == m2), lane_f, big), axis=-1, keepdims=True)
    r = jnp.exp(m2 - m1)
    w1 = g_prob / (1.0 + r)
    w2 = g_prob * r / (1.0 + r)
    return jnp.where(lane_f == i1, w1, 0.0) + jnp.where(lane_f == i2, w2, 0.0), g_sel


T_AUG = D_MODEL + LANES


def _outproj_kernel(x_ref, ya_ref, yb_ref, yc_ref, yd_ref, w_ref, g_ref, wr_hi_ref, wr_lo_ref, br_ref,
                    xo_ref, tg_ref, route_ref, cnt_ref, run_ref):
    @pl.when(pl.program_id(0) == 0)
    def _():
        run_ref[...] = jnp.zeros_like(run_ref)

    acc = x_ref[...]
    for gi, y_ref in enumerate((ya_ref, yb_ref, yc_ref, yd_ref)):
        acc = acc + jnp.dot(y_ref[...].astype(BF16), w_ref[GROUP_WIDTH * gi:GROUP_WIDTH * (gi + 1), :],
                            preferred_element_type=F32)
    xo_ref[...] = acc
    ms = jnp.mean(acc * acc, axis=-1, keepdims=True)
    t = acc * lax.rsqrt(ms + EPS) * g_ref[...]
    tg_ref[:, 0:D_MODEL] = t
    gate, g_sel = _route(t, wr_hi_ref[...], wr_lo_ref[...], br_ref[...])
    tg_ref[:, D_MODEL:T_AUG] = gate

    tm = t.shape[0]
    lane = lax.broadcasted_iota(jnp.int32, (tm, LANES), 1)
    onehot = jnp.where(lane.astype(F32) == g_sel, 1.0, 0.0)
    ri = lax.broadcasted_iota(jnp.int32, (tm, tm), 0)
    ci = lax.broadcasted_iota(jnp.int32, (tm, tm), 1)
    before = jnp.where(ri > ci, 1.0, 0.0).astype(BF16)
    prefix = jnp.dot(before, onehot.astype(BF16), preferred_element_type=F32)
    rank = jnp.sum(onehot * (run_ref[...] + prefix), axis=-1, keepdims=True)
    route_ref[...] = jnp.where(lane == 0, g_sel, 0.0) + jnp.where(lane == 1, rank, 0.0)
    run_ref[...] = run_ref[...] + jnp.sum(onehot, axis=0, keepdims=True)
    cnt_ref[...] = run_ref[...]


def _outproj(x2d, ys, w_out, ffn_gain, w_group, b_group, w_expert, b_expert, tm=256):
    t = x2d.shape[0]
    wr = jnp.zeros((D_MODEL, LANES), F32).at[:, 0:N_GROUPS].set(w_group).at[:, ROUTER_LANE0:ROUTER_LANE0 + N_EXPERTS].set(w_expert)
    wr_hi = wr.astype(BF16)
    wr_lo = (wr - wr_hi.astype(F32)).astype(BF16)
    br = jnp.zeros((1, LANES), F32).at[0, 0:N_GROUPS].set(b_group).at[0, ROUTER_LANE0:ROUTER_LANE0 + N_EXPERTS].set(b_expert)
    row = lambda w: pl.BlockSpec((tm, w), lambda i: (i, 0))
    full = lambda r, c: pl.BlockSpec((r, c), lambda i: (0, 0))
    return list(pl.pallas_call(
        _outproj_kernel,
        grid=(t // tm,),
        in_specs=[row(D_MODEL), row(GROUP_WIDTH), row(GROUP_WIDTH), row(GROUP_WIDTH), row(GROUP_WIDTH),
                  full(D_MODEL, D_MODEL), full(1, D_MODEL), full(D_MODEL, LANES), full(D_MODEL, LANES), full(1, LANES)],
        out_specs=[row(D_MODEL), row(T_AUG), row(LANES), full(1, LANES)],
        out_shape=[jax.ShapeDtypeStruct((t, D_MODEL), F32),
                   jax.ShapeDtypeStruct((t, T_AUG), F32),
                   jax.ShapeDtypeStruct((t, LANES), F32),
                   jax.ShapeDtypeStruct((1, LANES), F32)],
        scratch_shapes=[pltpu.VMEM((1, LANES), F32)],
        compiler_params=_cparams(("arbitrary",)),
        name="outproj_router",
    )(x2d, *ys, w_out.astype(BF16), ffn_gain.reshape(1, D_MODEL), wr_hi, wr_lo, br))


MOE_TILE = 1024
DMA_PRIORITIES = 2


def _row_copies(n, make_copy):
    def start(k, carry):
        for p in range(DMA_PRIORITIES):
            make_copy(DMA_PRIORITIES * k + p).start(priority=p)
        return carry

    def wait(r, carry):
        make_copy(r).wait()
        return carry

    lax.fori_loop(0, n // DMA_PRIORITIES, start, 0)
    lax.fori_loop(0, n, wait, 0)


def _moe_scatter_kernel(pos_ref, tg_ref, xs_in_ref, xs_ref, sem):
    del xs_in_ref
    tm = tg_ref.shape[0]
    base = pl.program_id(0) * tm
    _row_copies(tm, lambda r: pltpu.make_async_copy(tg_ref.at[pl.ds(r, 1)],
                                                    xs_ref.at[pl.ds(pos_ref[base + r], 1)], sem))


def _moe_expert_kernel(grp_ref, valid_ref, xs_ref, wg_ref, wu_ref, wd_ref, y_ref, acc_ref):
    i, e = pl.program_id(0), pl.program_id(1)

    @pl.when(e == 0)
    def _():
        acc_ref[...] = jnp.zeros_like(acc_ref)

    @pl.when(valid_ref[i] == 1)
    def _():
        x = xs_ref[:, 0:D_MODEL].astype(BF16)
        gate = xs_ref[:, D_MODEL:T_AUG]
        lane = lax.broadcasted_iota(jnp.int32, gate.shape, 1)
        col = ROUTER_LANE0 + grp_ref[i] * EXPERTS_PER_GROUP + e
        w = jnp.sum(jnp.where(lane == col, gate, 0.0), axis=-1, keepdims=True)
        a = jnp.dot(x, wg_ref[...], preferred_element_type=F32)
        u = jnp.dot(x, wu_ref[...], preferred_element_type=F32)
        act = a * _sigmoid(a) * u * w
        acc_ref[...] += jnp.dot(act.astype(BF16), wd_ref[...], preferred_element_type=F32)

    @pl.when(e == EXPERTS_PER_GROUP - 1)
    def _():
        y_ref[...] = acc_ref[...]


def _moe_combine_kernel(pos_ref, x_ref, ys_ref, o_ref, buf_ref, sem):
    tm = x_ref.shape[0]
    base = pl.program_id(0) * tm
    _row_copies(tm, lambda r: pltpu.make_async_copy(ys_ref.at[pl.ds(pos_ref[base + r], 1)],
                                                    buf_ref.at[pl.ds(r, 1)], sem))
    o_ref[...] = x_ref[...] + buf_ref[...]


def _moe_routed(x2d, tg, route, counts, w_gate, w_up, w_down, tm=512):
    t = x2d.shape[0]
    te = MOE_TILE
    n_tiles = t // te + N_GROUPS
    n_rows = n_tiles * te
    grp = route[:, 0].astype(jnp.int32)
    rank = route[:, 1].astype(jnp.int32)
    cnt = counts[0, 0:N_GROUPS].astype(jnp.int32)
    padded = ((cnt + te - 1) // te) * te
    ends = jnp.cumsum(padded)
    pos = (ends - padded)[grp] + rank
    starts = jnp.arange(n_tiles, dtype=jnp.int32) * te
    tile_grp = jnp.minimum(jnp.sum((starts[:, None] >= ends[None, :]).astype(jnp.int32), axis=1), N_GROUPS - 1)
    tile_valid = (starts < ends[-1]).astype(jnp.int32)

    xs = pl.pallas_call(
        _moe_scatter_kernel,
        grid_spec=pltpu.PrefetchScalarGridSpec(
            num_scalar_prefetch=1, grid=(t // tm,),
            in_specs=[pl.BlockSpec((tm, T_AUG), lambda i, pos: (i, 0)), pl.BlockSpec(memory_space=pl.ANY)],
            out_specs=pl.BlockSpec(memory_space=pl.ANY),
            scratch_shapes=[pltpu.SemaphoreType.DMA(())]),
        out_shape=jax.ShapeDtypeStruct((n_rows, T_AUG), F32),
        input_output_aliases={2: 0},
        compiler_params=_cparams(("arbitrary",)),
        name="moe_scatter",
    )(pos, tg, jnp.zeros((n_rows, T_AUG), F32))

    wg = w_gate.reshape(N_EXPERTS, D_MODEL, D_EXPERT).astype(BF16)
    wu = w_up.reshape(N_EXPERTS, D_MODEL, D_EXPERT).astype(BF16)
    wd = w_down.reshape(N_EXPERTS, D_EXPERT, D_MODEL).astype(BF16)
    wsel = lambda i, e, g, v: (g[i] * EXPERTS_PER_GROUP + e, 0, 0)
    ys = pl.pallas_call(
        _moe_expert_kernel,
        grid_spec=pltpu.PrefetchScalarGridSpec(
            num_scalar_prefetch=2, grid=(n_tiles, EXPERTS_PER_GROUP),
            in_specs=[pl.BlockSpec((te, T_AUG), lambda i, e, g, v: (i, 0)),
                      pl.BlockSpec((None, D_MODEL, D_EXPERT), wsel),
                      pl.BlockSpec((None, D_MODEL, D_EXPERT), wsel),
                      pl.BlockSpec((None, D_EXPERT, D_MODEL), wsel)],
            out_specs=pl.BlockSpec((te, D_MODEL), lambda i, e, g, v: (i, 0)),
            scratch_shapes=[pltpu.VMEM((te, D_MODEL), F32)]),
        out_shape=jax.ShapeDtypeStruct((n_rows, D_MODEL), F32),
        compiler_params=_cparams(("arbitrary", "arbitrary")),
        name="moe_experts",
    )(tile_grp, tile_valid, xs, wg, wu, wd)

    return pl.pallas_call(
        _moe_combine_kernel,
        grid_spec=pltpu.PrefetchScalarGridSpec(
            num_scalar_prefetch=1, grid=(t // tm,),
            in_specs=[pl.BlockSpec((tm, D_MODEL), lambda i, pos: (i, 0)), pl.BlockSpec(memory_space=pl.ANY)],
            out_specs=pl.BlockSpec((tm, D_MODEL), lambda i, pos: (i, 0)),
            scratch_shapes=[pltpu.VMEM((tm, D_MODEL), F32), pltpu.SemaphoreType.DMA(())]),
        out_shape=jax.ShapeDtypeStruct((t, D_MODEL), F32),
        compiler_params=_cparams(("arbitrary",)),
        name="moe_combine",
    )(pos, x2d, ys)


def _alibi_slopes():
    n = 2 * GROUP_HEADS
    s = 2.0 ** (-8.0 * np.arange(1, n + 1) / n)
    return s[0::2], s[1::2]


def kernel(x, norm_mix, norm_ffn, w_in, w_out, diff_q_gain, diff_k_gain, diff_lambda, diff_sub_gain, mlstm_conv, mlstm_gate_bias, mlstm_out_gain, hgrn_lower_bounds, hgrn_out_gain, nsa_cmp_pos, nsa_cmp_w1, nsa_cmp_w2, nsa_q_gain, nsa_k_gain, nsa_out_gain, moe_w_group, moe_b_group, moe_w_expert, moe_b_expert, moe_w_gate, moe_w_up, moe_w_down):
    b, s, d = x.shape
    slopes_diff, slopes_nsa = _alibi_slopes()
    lb_soft = jax.nn.softmax(hgrn_lower_bounds.astype(F32), axis=0)
    lower_bounds = jnp.cumsum(lb_soft, axis=0) - lb_soft[0]
    x2d = x.reshape(b * s, d)
    for l in range(norm_mix.shape[0]):
        p = _inproj(x2d, norm_mix[l], _pack_w_in(w_in[l])).reshape(b, s, P_COLS)
        y_a = _diff_attention(p, diff_q_gain[l], diff_k_gain[l], diff_lambda[l], diff_sub_gain[l], slopes_diff, l)
        y_b = _mlstm(p, mlstm_conv[l], mlstm_gate_bias[l], mlstm_out_gain[l])
        y_c = _hgrn2(p, lower_bounds[l], hgrn_out_gain[l])
        y_d = _nsa(p, nsa_cmp_pos[l], nsa_cmp_w1[l], nsa_cmp_w2[l], nsa_q_gain[l], nsa_k_gain[l],
                   nsa_out_gain[l], slopes_nsa)
        ys = [y.reshape(b * s, GROUP_WIDTH) for y in (y_a, y_b, y_c, y_d)]
        x2d, tg, route, counts = _outproj(x2d, ys, w_out[l], norm_ffn[l], moe_w_group[l], moe_b_group[l],
                                          moe_w_expert[l], moe_b_expert[l])
        x2d = _moe_routed(x2d, tg, route, counts, moe_w_gate[l], moe_w_up[l], moe_w_down[l])
    return x2d.reshape(b, s, d)
```

```python
import functools
import math

import numpy as np
import jax
import jax.numpy as jnp
from jax import lax
from jax.experimental import pallas as pl
from jax.experimental.pallas import tpu as pltpu

F32 = jnp.float32
BF16 = jnp.bfloat16

D_MODEL = 1024
HEAD_DIM = 64
GROUP_HEADS = 4
GROUP_WIDTH = GROUP_HEADS * HEAD_DIM
DIFF_HALF = HEAD_DIM // 2
EPS = 1e-6
NEG_INF = -1e30
LOG2E = math.log2(math.e)
M_INIT = -1e30
MASKED = -2e30

LANES = 128
VMEM_LIMIT = 48 * 1024 * 1024

COL_A = 0
COL_B = 768
COL_C = 1792
COL_D = 2816
COL_G = 3456
P_COLS = 3584


def _cparams(sem, flags=None):
    return pltpu.CompilerParams(dimension_semantics=sem, vmem_limit_bytes=VMEM_LIMIT, flags=flags)


def _block_ones(width, seg):
    i = np.arange(width)
    return jnp.asarray((i[:, None] // seg == i[None, :] // seg).astype(np.float32), BF16)


def _seg_sum(x, bd):
    hi = x.astype(BF16)
    lo = (x - hi.astype(F32)).astype(BF16)
    return (jnp.dot(hi, bd, preferred_element_type=F32) + jnp.dot(lo, bd, preferred_element_type=F32))


def _inproj_kernel(x_ref, g_ref, w_ref, o_ref):
    x = x_ref[...]
    ms = jnp.mean(x * x, axis=-1, keepdims=True)
    h = (x * lax.rsqrt(ms + EPS) * g_ref[...]).astype(BF16)
    o_ref[...] = jnp.dot(h, w_ref[...], preferred_element_type=F32)


def _inproj(x2d, gain, w_cat, tm=256):
    t = x2d.shape[0]
    return pl.pallas_call(
        _inproj_kernel,
        grid=(t // tm,),
        in_specs=[pl.BlockSpec((tm, D_MODEL), lambda i: (i, 0)),
                  pl.BlockSpec((1, D_MODEL), lambda i: (0, 0)),
                  pl.BlockSpec((D_MODEL, P_COLS), lambda i: (0, 0))],
        out_specs=pl.BlockSpec((tm, P_COLS), lambda i: (i, 0)),
        out_shape=jax.ShapeDtypeStruct((t, P_COLS), F32),
        compiler_params=_cparams(("arbitrary",)),
        name="inproj",
    )(x2d, gain.reshape(1, D_MODEL), w_cat)


IN_COLS = 3476


def _pack_w_in_kernel(w_ref, o_ref):
    w = w_ref[...]
    a_b = w[:, 0:1536]
    gates_b = w[:, 1536:1544]
    rest = w[:, 1544:3464]
    gates_d = w[:, 3464:3476]
    pad = jnp.zeros((w.shape[0], P_COLS - COL_G - 20), F32)
    o_ref[...] = jnp.concatenate([a_b, rest, gates_b, gates_d, pad], axis=1).astype(BF16)


def _pack_w_in(w, tr=128):
    d = w.shape[0]
    return pl.pallas_call(
        _pack_w_in_kernel,
        grid=(d // tr,),
        in_specs=[pl.BlockSpec((tr, IN_COLS), lambda i: (i, 0))],
        out_specs=pl.BlockSpec((tr, P_COLS), lambda i: (i, 0)),
        out_shape=jax.ShapeDtypeStruct((d, P_COLS), BF16),
        compiler_params=_cparams(("arbitrary",)),
        name="pack_w_in",
    )(w)


def _diff_prep_kernel(p_ref, qg_ref, kg_ref, bd_ref, w_ref, wt_ref, q_ref, k_ref, vt_ref):
    p = p_ref[...]
    bd = bd_ref[...]

    def norm(x, g):
        ss = _seg_sum(x * x, bd)
        return x * lax.rsqrt(ss * (1.0 / DIFF_HALF) + EPS) * g

    q = norm(p[:, 0:256], qg_ref[...]) * (DIFF_HALF ** -0.5 * LOG2E)
    k = norm(p[:, 256:512], kg_ref[...])
    q_ref[...] = q.T.astype(BF16)
    k_ref[...] = k.astype(BF16)
    v = p[:, 512:768]
    v_t = v.T
    vw_t = (v * w_ref[...]).T
    w_t = wt_ref[...]
    tm = v_t.shape[1]
    first = lax.broadcasted_iota(jnp.int32, (V_AUG - HEAD_DIM, tm), 0) == 0
    blocks = []
    for h in range(GROUP_HEADS):
        hs = slice(HEAD_DIM * h, HEAD_DIM * (h + 1))
        blocks += [vw_t[hs, :], jnp.where(first, jnp.broadcast_to(w_t[h:h + 1, :], first.shape), 0.0),
                   v_t[hs, :], jnp.where(first, 1.0, 0.0)]
    vt_ref[...] = jnp.concatenate(blocks, axis=0).astype(BF16)


V_AUG = 80
V_HEAD = 2 * V_AUG


def _key_weights(tile, slopes):
    kl = np.arange(tile, dtype=np.float64) - (tile - 1)
    w = np.stack([np.exp2(sl * LOG2E * kl) for sl in slopes])
    wide = np.repeat(w.T, HEAD_DIM, axis=1)
    w8 = np.zeros((8, tile)); w8[:len(slopes)] = w
    return jnp.asarray(wide.astype(np.float32)), jnp.asarray(w8.astype(np.float32))


def _diag_bias(tq, slopes):
    k = np.arange(tq)[:, None]
    q = np.arange(tq)[None, :]
    tabs = [np.where(k <= q, sl * LOG2E * k.astype(np.float64), MASKED) for sl in slopes]
    return jnp.asarray(np.stack(tabs).astype(np.float32))


AHEAD = 2


def _flash_update(s_ref, bias_ref, off, vt, m_ref, acc_ref, penalty=None):
    tq = s_ref.shape[1]
    ps, olds, news = [], [], []
    for c in range(tq // LANES):
        cols = slice(LANES * c, LANES * (c + 1))
        s = s_ref[:, cols]
        if bias_ref is not None:
            s = s + bias_ref[:, cols]
        if penalty is not None:
            s = s + penalty
        m_tile = jnp.max(s, axis=0, keepdims=True)
        ps.append(jnp.exp2(s - m_tile).astype(BF16))
        m_old = m_ref[0:1, cols]
        m_new = jnp.maximum(m_old, m_tile + off)
        m_ref[0:1, cols] = m_new
        olds.append(jnp.exp2(m_old - m_new))
        news.append(jnp.exp2(m_tile + off - m_new))
    p, old, new = jnp.concatenate(ps, axis=1), jnp.concatenate(olds, axis=1), jnp.concatenate(news, axis=1)

    def finish():
        pv = jnp.dot(vt, p, preferred_element_type=F32)
        acc_ref[...] = old * acc_ref[...] + new * pv

    return finish


def _run_chains(n, prefetch, softmax):
    finish = None
    for c in range(n):
        prefetch(c)
        nxt = softmax(c)
        if finish is not None:
            finish()
        finish = nxt
    finish()


def _diff_attn_kernel(lam_ref, qt_ref, k_ref, vt_ref, db_ref, sg_ref, bd_ref, o_ref,
                      qm_ref, acc_ref, m_ref, s_ref, *, tq, slopes, out_scale):
    qi = pl.program_id(1)
    n_half = 2 * GROUP_HEADS
    per_tile = LANES // DIFF_HALF
    row = lax.broadcasted_iota(jnp.int32, (LANES, tq), 0)
    for j in range(n_half):
        slab = qt_ref[LANES * (j // per_tile):LANES * (j // per_tile + 1), :].astype(F32)
        r0 = DIFF_HALF * (j % per_tile)
        qm_ref[j] = jnp.where((row >= r0) & (row < r0 + DIFF_HALF), slab, 0.0).astype(BF16)
    acc_ref[...] = jnp.zeros(acc_ref.shape, F32)
    m_ref[...] = jnp.full(m_ref.shape, M_INIT, F32)

    n_slots = s_ref.shape[0]

    def scores(kt, j):
        start = pl.multiple_of(kt * tq, tq)
        kj = k_ref[pl.ds(start, tq), LANES * (j // per_tile):LANES * (j // per_tile + 1)]
        s_ref[j % n_slots] = jnp.dot(kj, qm_ref[j], preferred_element_type=F32)

    def process(kt, j, diag):
        h = j // 2
        start = pl.multiple_of(kt * tq, tq)
        off = 0.0 if diag else ((kt - qi) * tq + (tq - 1)).astype(F32) * (slopes[h] * LOG2E)
        base = V_HEAD * h + (V_AUG if diag else 0)
        vt = vt_ref[base:base + V_AUG, pl.ds(start, tq)]
        return _flash_update(s_ref.at[j % n_slots], db_ref.at[h] if diag else None, off, vt, m_ref.at[j],
                             acc_ref.at[j])

    def step(kt, diag):
        def prefetch(j):
            if j + AHEAD < n_half:
                scores(kt, j + AHEAD)
            elif not diag:
                scores(kt + 1, j + AHEAD - n_half)

        _run_chains(n_half, prefetch, lambda j: process(kt, j, diag))

    for j in range(AHEAD):
        scores(0, j)

    def body(kt, carry):
        step(kt, False)
        return carry

    lax.fori_loop(0, qi, body, 0)
    step(qi, True)

    lam = lam_ref[0, 0]
    heads = []
    for h in range(GROUP_HEADS):
        a0, a1 = acc_ref[2 * h], acc_ref[2 * h + 1]
        o0 = a0[0:HEAD_DIM, :] / a0[HEAD_DIM:HEAD_DIM + 1, :]
        o1 = a1[0:HEAD_DIM, :] / a1[HEAD_DIM:HEAD_DIM + 1, :]
        heads.append(o0 - lam * o1)
    o = jnp.concatenate(heads, axis=0).T
    ss = _seg_sum(o * o, bd_ref[...])
    o_ref[...] = o * lax.rsqrt(ss * (1.0 / HEAD_DIM) + EPS) * (sg_ref[...] * out_scale)


def _diff_attention(p3, q_gain, k_gain, lam_vecs, sub_gain, slopes, layer_idx, tq=256):
    b, s, _ = p3.shape
    tm = tq
    bd32 = _block_ones(GROUP_WIDTH, DIFF_HALF)
    bd64 = _block_ones(GROUP_WIDTH, HEAD_DIM)
    qg = jnp.tile(q_gain.astype(F32), 2 * GROUP_HEADS).reshape(1, GROUP_WIDTH)
    kg = jnp.tile(k_gain.astype(F32), 2 * GROUP_HEADS).reshape(1, GROUP_WIDTH)
    w_wide, w_t = _key_weights(tm, slopes)
    v_rows = GROUP_HEADS * V_HEAD
    qn, kn, vt = pl.pallas_call(
        _diff_prep_kernel,
        grid=(b, s // tm),
        in_specs=[pl.BlockSpec((None, tm, 768), lambda bi, i: (bi, i, COL_A // 768)),
                  pl.BlockSpec((1, GROUP_WIDTH), lambda bi, i: (0, 0)),
                  pl.BlockSpec((1, GROUP_WIDTH), lambda bi, i: (0, 0)),
                  pl.BlockSpec((GROUP_WIDTH, GROUP_WIDTH), lambda bi, i: (0, 0)),
                  pl.BlockSpec((tm, GROUP_WIDTH), lambda bi, i: (0, 0)),
                  pl.BlockSpec((8, tm), lambda bi, i: (0, 0))],
        out_specs=[pl.BlockSpec((None, GROUP_WIDTH, tm), lambda bi, i: (bi, 0, i)),
                   pl.BlockSpec((None, tm, GROUP_WIDTH), lambda bi, i: (bi, i, 0)),
                   pl.BlockSpec((None, v_rows, tm), lambda bi, i: (bi, 0, i))],
        out_shape=[jax.ShapeDtypeStruct((b, GROUP_WIDTH, s), BF16),
                   jax.ShapeDtypeStruct((b, s, GROUP_WIDTH), BF16),
                   jax.ShapeDtypeStruct((b, v_rows, s), BF16)],
        compiler_params=_cparams(("arbitrary", "arbitrary")),
        name="diff_prep",
    )(p3, qg, kg, bd32, w_wide, w_t)

    lam_init = 0.8 - 0.6 * math.exp(-0.3 * layer_idx)
    lv = lam_vecs.astype(F32)
    lam = (jnp.exp(jnp.dot(lv[0], lv[1])) - jnp.exp(jnp.dot(lv[2], lv[3])) + lam_init).reshape(1, 1)
    kern = functools.partial(_diff_attn_kernel, tq=tq, slopes=tuple(float(v) for v in slopes),
                             out_scale=1.0 - lam_init)
    return pl.pallas_call(
        kern,
        grid=(b, s // tq),
        in_specs=[pl.BlockSpec(memory_space=pltpu.SMEM),
                  pl.BlockSpec((None, GROUP_WIDTH, tq), lambda bi, i: (bi, 0, i)),
                  pl.BlockSpec((None, s, GROUP_WIDTH), lambda bi, i: (bi, 0, 0)),
                  pl.BlockSpec((None, v_rows, s), lambda bi, i: (bi, 0, 0)),
                  pl.BlockSpec((GROUP_HEADS, tq, tq), lambda bi, i: (0, 0, 0)),
                  pl.BlockSpec((1, GROUP_WIDTH), lambda bi, i: (0, 0)),
                  pl.BlockSpec((GROUP_WIDTH, GROUP_WIDTH), lambda bi, i: (0, 0))],
        out_specs=pl.BlockSpec((None, tq, GROUP_WIDTH), lambda bi, i: (bi, i, 0)),
        out_shape=jax.ShapeDtypeStruct((b, s, GROUP_WIDTH), F32),
        scratch_shapes=[pltpu.VMEM((2 * GROUP_HEADS, LANES, tq), BF16),
                        pltpu.VMEM((2 * GROUP_HEADS, V_AUG, tq), F32),
                        pltpu.VMEM((2 * GROUP_HEADS, 8, tq), F32),
                        pltpu.VMEM((2 * AHEAD, tq, tq), F32)],
        compiler_params=_cparams(("arbitrary", "arbitrary")),
        name="diff_attn",
    )(lam, qn, kn, vt, _diag_bias(tq, slopes), sub_gain.astype(F32).reshape(1, GROUP_WIDTH), bd64)


MLSTM_CONV = 4
CONV_HALO = 8
AUG = 128


def _tril_sum(x, tril):
    hi = x.astype(BF16)
    lo = (x - hi.astype(F32)).astype(BF16)
    return jnp.dot(tril, hi, preferred_element_type=F32) + jnp.dot(tril, lo, preferred_element_type=F32)


def _log_sigmoid(x):
    return jnp.minimum(x, 0.0) - jnp.log1p(jnp.exp(-jnp.abs(x)))


def _sigmoid(x):
    return 1.0 / (1.0 + jnp.exp(-x))


def _mlstm_kernel(q_ref, k_ref, v_ref, o_ref, g_ref, cw_ref, gb_ref, og_ref, bd_ref, y_ref,
                  halo_ref, c_ref, m_ref, *, L):
    ci = pl.program_id(1)

    @pl.when(ci == 0)
    def _():
        halo_ref[0:CONV_HALO, :] = jnp.zeros((CONV_HALO, 2 * GROUP_WIDTH), F32)
        c_ref[...] = jnp.zeros_like(c_ref)
        m_ref[...] = jnp.zeros_like(m_ref)

    halo_ref[CONV_HALO:CONV_HALO + L, 0:GROUP_WIDTH] = q_ref[...]
    halo_ref[CONV_HALO:CONV_HALO + L, GROUP_WIDTH:2 * GROUP_WIDTH] = k_ref[...]
    conv = jnp.zeros((L, 2 * GROUP_WIDTH), F32)
    for j in range(MLSTM_CONV):
        start = CONV_HALO - (MLSTM_CONV - 1) + j
        conv = conv + halo_ref[start:start + L, :] * cw_ref[j:j + 1, :]
    halo_ref[0:CONV_HALO, :] = halo_ref[L:L + CONV_HALO, :]
    qk = conv * _sigmoid(conv)
    q = qk[:, 0:GROUP_WIDTH] * (HEAD_DIM ** -0.5)
    k = qk[:, GROUP_WIDTH:2 * GROUP_WIDTH]
    v = v_ref[...]

    gates = g_ref[...] + gb_ref[...]
    ri = lax.broadcasted_iota(jnp.int32, (L, L), 0)
    cj = lax.broadcasted_iota(jnp.int32, (L, L), 1)
    causal = ri >= cj
    tril = jnp.where(causal, 1.0, 0.0).astype(BF16)
    gcum = _tril_sum(_log_sigmoid(gates), tril)
    gcum_t = gcum.T
    gates_t = gates.T
    lane_aug = lax.broadcasted_iota(jnp.int32, (L, AUG), 1)
    m_all = m_ref[...]

    outs = []
    for h in range(GROUP_HEADS):
        hs = slice(HEAD_DIM * h, HEAD_DIM * (h + 1))
        q_h, k_h = q[:, hs], k[:, hs]
        v_aug = jnp.where(lane_aug == HEAD_DIM, 1.0,
                          jnp.concatenate([v[:, hs], jnp.zeros((L, AUG - HEAD_DIM), F32)], axis=1))
        g_col = gcum[:, 4 + h:5 + h]
        li_col = gates[:, h:h + 1]
        g_row = gcum_t[4 + h:5 + h, :]
        li_row = gates_t[h:h + 1, :]
        g_last = g_row[:, L - 1:L]
        m_prev = m_all[:, h:h + 1]
        log_d = jnp.where(causal, g_col - g_row + li_row, NEG_INF)
        log_inter = g_col + m_prev
        m_t = jnp.maximum(log_inter, jnp.max(log_d, axis=1, keepdims=True))
        s_qk = lax.dot_general(q_h.astype(BF16), k_h.astype(BF16), (((1,), (1,)), ((), ())),
                               preferred_element_type=F32)
        w_intra = s_qk * jnp.exp(log_d - m_t)
        w_inter = jnp.exp(log_inter - m_t)
        c_aug = c_ref[h]
        num = (w_inter * jnp.dot(q_h.astype(BF16), c_aug.astype(BF16), preferred_element_type=F32)
               + jnp.dot(w_intra.astype(BF16), v_aug.astype(BF16), preferred_element_type=F32))
        den = num[:, HEAD_DIM:HEAD_DIM + 1]
        outs.append(num[:, 0:HEAD_DIM] / jnp.maximum(jnp.abs(den), jnp.exp(-m_t)))

        log_a = g_last - g_col + li_col
        m_new = jnp.maximum(g_last + m_prev, jnp.max(log_a, axis=0, keepdims=True))
        a_col = jnp.exp(log_a - m_new)
        decay = jnp.exp(g_last + m_prev - m_new)
        ak_t = (k_h * a_col).T.astype(BF16)
        c_ref[h] = decay * c_aug + jnp.dot(ak_t, v_aug.astype(BF16), preferred_element_type=F32)
        m_ref[:, h:h + 1] = m_new

    hcat = jnp.concatenate(outs, axis=1)
    ss = _seg_sum(hcat * hcat, bd_ref[...])
    y_ref[...] = hcat * lax.rsqrt(ss * (1.0 / HEAD_DIM) + EPS) * og_ref[...] * _sigmoid(o_ref[...])


def _mlstm(p3, conv_w, gate_bias, out_gain, L=256):
    b, s, _ = p3.shape
    gb = jnp.zeros((1, LANES), F32).at[0, 0:GROUP_HEADS].set(gate_bias[0]).at[0, GROUP_HEADS:2 * GROUP_HEADS].set(gate_bias[1])
    col = lambda off: (lambda bi, i: (bi, i, off // GROUP_WIDTH))
    return pl.pallas_call(
        functools.partial(_mlstm_kernel, L=L),
        grid=(b, s // L),
        in_specs=[pl.BlockSpec((None, L, GROUP_WIDTH), col(COL_B)),
                  pl.BlockSpec((None, L, GROUP_WIDTH), col(COL_B + 256)),
                  pl.BlockSpec((None, L, GROUP_WIDTH), col(COL_B + 512)),
                  pl.BlockSpec((None, L, GROUP_WIDTH), col(COL_B + 768)),
                  pl.BlockSpec((None, L, LANES), lambda bi, i: (bi, i, COL_G // LANES)),
                  pl.BlockSpec((MLSTM_CONV, 2 * GROUP_WIDTH), lambda bi, i: (0, 0)),
                  pl.BlockSpec((1, LANES), lambda bi, i: (0, 0)),
                  pl.BlockSpec((1, GROUP_WIDTH), lambda bi, i: (0, 0)),
                  pl.BlockSpec((GROUP_WIDTH, GROUP_WIDTH), lambda bi, i: (0, 0))],
        out_specs=pl.BlockSpec((None, L, GROUP_WIDTH), lambda bi, i: (bi, i, 0)),
        out_shape=jax.ShapeDtypeStruct((b, s, GROUP_WIDTH), F32),
        scratch_shapes=[pltpu.VMEM((CONV_HALO + L, 2 * GROUP_WIDTH), F32),
                        pltpu.VMEM((GROUP_HEADS, HEAD_DIM, AUG), F32),
                        pltpu.VMEM((1, LANES), F32)],
        compiler_params=_cparams(("arbitrary", "arbitrary")),
        name="mlstm",
    )(p3, p3, p3, p3, p3, conv_w.astype(F32), gb, out_gain.astype(F32).reshape(1, GROUP_WIDTH),
      _block_ones(GROUP_WIDTH, HEAD_DIM))


HGRN_CHUNK = 16
LB_FLOOR = 1e-30


def _hgrn_kernel(q_ref, f_ref, i_ref, g_ref, lb_ref, og_ref, bd_ref, y_ref,
                 lf_s, kk_s, vv_s, st_ref, *, TL):
    ci = pl.program_id(1)
    C = HGRN_CHUNK
    W = GROUP_WIDTH

    @pl.when(ci == 0)
    def _():
        lf_s[0:C, :] = jnp.zeros((C, W), F32)
        kk_s[0:C, :] = jnp.zeros((C, W), F32)
        vv_s[0:C, :] = jnp.zeros((C, W), F32)
        st_ref[...] = jnp.zeros_like(st_ref)

    z = f_ref[...]
    a = lb_ref[0:1, :]
    c = lb_ref[1:2, :] + _log_sigmoid(z)
    mx = jnp.maximum(a, c)
    lf = mx + jnp.log1p(jnp.exp(-jnp.abs(a - c)))
    kk = lb_ref[2:3, :] * _sigmoid(-z) + lb_ref[3:4, :]
    qx = q_ref[...]
    qs = qx * _sigmoid(qx)
    vv = i_ref[...]
    lf_s[C:C + TL, :] = lf
    kk_s[C:C + TL, :] = kk
    vv_s[C:C + TL, :] = vv

    bd = bd_ref[...]
    row = lax.broadcasted_iota(jnp.int32, (TL, W), 0)
    rmod = row & (C - 1)

    acc = jnp.zeros((TL, W), F32)
    dsum = jnp.zeros((TL, W), F32)
    for delta in range(C):
        if delta > 0:
            dsum = dsum + lf_s[C - (delta - 1):C - (delta - 1) + TL, :]
        x = qs * kk_s[C - delta:C - delta + TL, :] * jnp.exp(dsum)
        x = jnp.where(rmod >= delta, x, 0.0)
        att = jnp.dot(x.astype(BF16), bd, preferred_element_type=F32)
        acc = acc + att * vv_s[C - delta:C - delta + TL, :]

    ri = lax.broadcasted_iota(jnp.int32, (TL, TL), 0)
    cj = lax.broadcasted_iota(jnp.int32, (TL, TL), 1)
    same = (ri // C) == (cj // C)
    tril = jnp.where(same & (ri >= cj), 1.0, 0.0).astype(BF16)
    ones = jnp.where(same, 1.0, 0.0).astype(BF16)
    bcum = _tril_sum(lf, tril)
    blast = _tril_sum(lf, ones)
    q_a = (qs * jnp.exp(bcum)).astype(BF16)
    k_b = (kk * jnp.exp(blast - bcum)).astype(BF16)
    dec = jnp.exp(blast)
    vv_b = vv.astype(BF16)
    hmask = bd.astype(F32)
    state = st_ref[...]
    inters = []
    for ch in range(TL // C):
        r = slice(ch * C, (ch + 1) * C)
        inters.append(lax.dot_general(q_a[r], state.astype(BF16), (((1,), (1,)), ((), ())),
                                      preferred_element_type=F32))
        upd = lax.dot_general(vv_b[r], k_b[r], (((0,), (0,)), ((), ())), preferred_element_type=F32)
        state = state * dec[ch * C:ch * C + 1, :] + upd * hmask
    st_ref[...] = state
    o = acc + jnp.concatenate(inters, axis=0)
    ss = _seg_sum(o * o, bd)
    y_ref[...] = o * lax.rsqrt(ss * (1.0 / HEAD_DIM) + EPS) * og_ref[...] * _sigmoid(g_ref[...])


def _hgrn2(p3, lower_bound, out_gain, TL=256):
    b, s, _ = p3.shape
    lb = lower_bound.astype(F32)
    lbf = jnp.maximum(lb, LB_FLOOR)
    lbp = jnp.stack([jnp.log(lbf), jnp.log1p(-lb), 1.0 - lb, lb - lbf])
    col = lambda off: (lambda bi, i: (bi, i, off // GROUP_WIDTH))
    C = HGRN_CHUNK
    return pl.pallas_call(
        functools.partial(_hgrn_kernel, TL=TL),
        grid=(b, s // TL),
        in_specs=[pl.BlockSpec((None, TL, GROUP_WIDTH), col(COL_C)),
                  pl.BlockSpec((None, TL, GROUP_WIDTH), col(COL_C + 256)),
                  pl.BlockSpec((None, TL, GROUP_WIDTH), col(COL_C + 512)),
                  pl.BlockSpec((None, TL, GROUP_WIDTH), col(COL_C + 768)),
                  pl.BlockSpec((4, GROUP_WIDTH), lambda bi, i: (0, 0)),
                  pl.BlockSpec((1, GROUP_WIDTH), lambda bi, i: (0, 0)),
                  pl.BlockSpec((GROUP_WIDTH, GROUP_WIDTH), lambda bi, i: (0, 0))],
        out_specs=pl.BlockSpec((None, TL, GROUP_WIDTH), lambda bi, i: (bi, i, 0)),
        out_shape=jax.ShapeDtypeStruct((b, s, GROUP_WIDTH), F32),
        scratch_shapes=[pltpu.VMEM((C + TL, GROUP_WIDTH), F32),
                        pltpu.VMEM((C + TL, GROUP_WIDTH), F32),
                        pltpu.VMEM((C + TL, GROUP_WIDTH), F32),
                        pltpu.VMEM((GROUP_WIDTH, GROUP_WIDTH), F32)],
        compiler_params=_cparams(("arbitrary", "arbitrary")),
        name="hgrn2",
    )(p3, p3, p3, p3, lbp, out_gain.astype(F32).reshape(1, GROUP_WIDTH), _block_ones(GROUP_WIDTH, HEAD_DIM))


CMP_LEN = 32
CMP_STRIDE = 16
SEL_BLOCK = 64
SEL_TOPK = 16
WINDOW = 512
GATE_ROW = 8


def _rms_rows(x, gain):
    return x * lax.rsqrt(jnp.mean(x * x, axis=-1, keepdims=True) + EPS) * gain


NSA_V_ROWS = (GROUP_HEADS + 1) * V_AUG


def _aug_shared_values(v, w_wide, w_t):
    tm = v.shape[0]
    vw_t = (jnp.concatenate([v] * GROUP_HEADS, axis=1) * w_wide).T
    first = lax.broadcasted_iota(jnp.int32, (V_AUG - HEAD_DIM, tm), 0) == 0
    blocks = []
    for h in range(GROUP_HEADS):
        blocks += [vw_t[HEAD_DIM * h:HEAD_DIM * (h + 1), :],
                   jnp.where(first, jnp.broadcast_to(w_t[h:h + 1, :], first.shape), 0.0)]
    blocks += [v.T, jnp.where(first, 1.0, 0.0)]
    return jnp.concatenate(blocks, axis=0).astype(BF16)


def _nsa_prep_kernel(q_ref, kv_ref, g_ref, qg_ref, kg_ref, bd_ref, w_ref, wt_ref,
                     qt_ref, kc_ref, vc_ref, ks_ref, vst_ref, kw_ref, vwt_ref, gt_ref):
    ti = pl.program_id(1)
    q = q_ref[...]
    ss = _seg_sum(q * q, bd_ref[...])
    qn = q * lax.rsqrt(ss * (1.0 / HEAD_DIM) + EPS) * qg_ref[...] * (HEAD_DIM ** -0.5 * LOG2E)
    qt_ref[...] = qn.T.astype(BF16)
    kv = kv_ref[...]
    tm = kv.shape[0]
    kc_ref[...] = kv[:, 0:64]
    vc_ref[...] = kv[:, 64:128]
    ks = _rms_rows(kv[:, 128:192], kg_ref[1:2, :])
    lane = lax.broadcasted_iota(jnp.int32, (tm, LANES), 1)
    blk = jnp.right_shift(ti * tm + lax.broadcasted_iota(jnp.int32, (tm, LANES), 0), SEL_BLOCK.bit_length() - 1)
    onehot = jnp.where(lane == blk + HEAD_DIM, 1.0, 0.0)
    ks_ref[...] = jnp.where(lane < HEAD_DIM, jnp.concatenate([ks, ks], axis=1), onehot).astype(BF16)
    kw = _rms_rows(kv[:, 256:320], kg_ref[2:3, :])
    kw_ref[...] = jnp.concatenate([kw, jnp.zeros_like(kw)], axis=1).astype(BF16)
    vst_ref[...] = _aug_shared_values(kv[:, 192:256], w_ref[...], wt_ref[...])
    vwt_ref[...] = _aug_shared_values(kv[:, 320:384], w_ref[...], wt_ref[...])
    gt_ref[...] = _sigmoid(g_ref[...]).T


def _gelu_tanh(x):
    return 0.5 * x * (1.0 + jnp.tanh(math.sqrt(2.0 / math.pi) * (x + 0.044715 * x * x * x)))


def _nsa_cmp_kernel(kr_ref, vr_ref, w1_ref, w2_ref, pos_ref, kg_ref, kc_ref, vct_ref, sh_ref, *, n_rows):
    half = CMP_STRIDE * HEAD_DIM
    sh_ref[n_rows:n_rows + 8, :] = jnp.zeros((8, HEAD_DIM), F32)
    outs = []
    for j, x_ref in enumerate((kr_ref, vr_ref)):
        r = x_ref[...].astype(BF16)
        w1 = w1_ref[j]
        first = jnp.dot(r, w1[0:half, :], preferred_element_type=F32)
        sh_ref[0:n_rows, :] = jnp.dot(r, w1[half:2 * half, :], preferred_element_type=F32)
        pos8 = jnp.broadcast_to(pos_ref[j], (8, 2 * half))
        posw = jnp.dot(pos8, w1.astype(F32), preferred_element_type=F32)[0:1, :]
        hdn = _gelu_tanh(first + sh_ref[1:n_rows + 1, :] + posw)
        outs.append(jnp.dot(hdn, w2_ref[j].astype(F32), preferred_element_type=F32))
    kc_ref[...] = _rms_rows(outs[0], kg_ref[0:1, :]).astype(BF16)
    vct_ref[...] = outs[1].T.astype(BF16)


def _nsa_attn_kernel(qt_ref, kc_ref, vct_ref, ks_ref, vst_ref, kw_ref, vwt_ref, gt_ref, ov_ref, db_ref, lo_ref,
                     og_ref, bd_ref, y_ref, qa_ref, acc_s, m_s, acc_w, m_w, ss_ref, sw_ref,
                     *, tq, slopes, n_cmp):
    qi = pl.program_id(1)
    n_rows = kc_ref.shape[0]
    n_sel = ov_ref.shape[0]
    qpos = qi * tq + lax.broadcasted_iota(jnp.int32, (1, tq), 1)

    nio = lax.broadcasted_iota(jnp.int32, (n_rows, tq), 0)
    dist_c = qpos - (nio * CMP_STRIDE + (CMP_LEN - 1))
    valid_c = (dist_c >= 0) & (nio < n_cmp)
    dist_cf = dist_c.astype(F32)
    kc = kc_ref[...]
    vct = vct_ref[...]
    o_cmp = []
    p_sum = jnp.zeros((n_rows, tq), F32)
    for h in range(GROUP_HEADS):
        qh = qt_ref[HEAD_DIM * h:HEAD_DIM * (h + 1), :]
        s = jnp.dot(kc, qh, preferred_element_type=F32) - (slopes[h] * LOG2E) * dist_cf
        s = jnp.where(valid_c, s, MASKED)
        m = jnp.maximum(jnp.max(s, axis=0, keepdims=True), M_INIT)
        e = jnp.exp2(s - m)
        p = e / jnp.maximum(jnp.sum(e, axis=0, keepdims=True), 1e-30)
        p_sum = p_sum + p
        o_cmp.append(jnp.dot(vct, p.astype(BF16), preferred_element_type=F32))

    p_hi = p_sum.astype(BF16)
    p_lo = (p_sum - p_hi.astype(F32)).astype(BF16)
    ov = ov_ref[...]
    imp = jnp.dot(ov, p_hi, preferred_element_type=F32) + jnp.dot(ov, p_lo, preferred_element_type=F32)
    jio = lax.broadcasted_iota(jnp.int32, (n_sel, tq), 0)
    qblk = jnp.right_shift(qpos, SEL_BLOCK.bit_length() - 1)
    imp = jnp.where(jio <= qblk, imp, -1.0)
    imp = jnp.where((jio == 0) | (jio == qblk) | (jio == qblk - 1), 2.0, imp)
    rank = jnp.zeros((n_sel, tq), F32)
    for i in range(n_sel):
        row = imp[i:i + 1, :]
        beats = (row > imp) | ((row == imp) & (jio > i))
        rank = rank + jnp.where(beats, 1.0, 0.0)
    bsel = jnp.where(rank < float(min(SEL_TOPK, n_sel)), 0.0, MASKED)
    if n_sel < LANES - HEAD_DIM:
        bsel = jnp.concatenate([bsel, jnp.zeros((LANES - HEAD_DIM - n_sel, tq), F32)], axis=0)
    for h in range(GROUP_HEADS):
        qa_ref[h, 0:HEAD_DIM, :] = qt_ref[HEAD_DIM * h:HEAD_DIM * (h + 1), :]
        qa_ref[h, HEAD_DIM:LANES, :] = bsel.astype(BF16)
    for ref, val in ((acc_s, 0.0), (acc_w, 0.0), (m_s, M_INIT), (m_w, M_INIT)):
        ref[...] = jnp.full(ref.shape, val, F32)

    def tile_off(kt, h):
        return ((kt - qi) * tq + (tq - 1)).astype(F32) * (slopes[h] * LOG2E)

    def values(vt_ref, kt, h, diag):
        base = GROUP_HEADS * V_AUG if diag else V_AUG * h
        return vt_ref[base:base + V_AUG, pl.ds(pl.multiple_of(kt * tq, tq), tq)]

    def sel_scores(kt, h):
        kb = ks_ref[pl.ds(pl.multiple_of(kt * tq, tq), tq), :]
        ss_ref[h] = jnp.dot(kb, qa_ref[h], preferred_element_type=F32)

    def sel_step(kt, diag):
        def prefetch(h):
            if h + AHEAD < GROUP_HEADS:
                sel_scores(kt, h + AHEAD)
            elif not diag:
                sel_scores(kt + 1, h + AHEAD - GROUP_HEADS)

        _run_chains(GROUP_HEADS, prefetch, lambda h: _flash_update(
            ss_ref.at[h], db_ref.at[h] if diag else None, 0.0 if diag else tile_off(kt, h),
            values(vst_ref, kt, h, diag), m_s.at[h], acc_s.at[h]))

    for h in range(AHEAD):
        sel_scores(0, h)

    def body(kt, carry):
        sel_step(kt, False)
        return carry

    lax.fori_loop(0, qi, body, 0)
    sel_step(qi, True)

    def win_step(back, bias_of, diag):
        kt = jnp.maximum(qi - back, 0)
        penalty = None if diag else jnp.where(qi >= back, 0.0, MASKED)
        kb = kw_ref[pl.ds(pl.multiple_of(kt * tq, tq), tq), :]

        def win_scores(h):
            sw_ref[h] = jnp.dot(kb, qa_ref[h], preferred_element_type=F32)

        for h in range(AHEAD):
            win_scores(h)

        def prefetch(h):
            if h + AHEAD < GROUP_HEADS:
                win_scores(h + AHEAD)

        _run_chains(GROUP_HEADS, prefetch, lambda h: _flash_update(
            sw_ref.at[h], bias_of(h), 0.0 if diag else tile_off(kt, h),
            values(vwt_ref, kt, h, diag), m_w.at[h], acc_w.at[h], penalty))

    win_step(2, lambda h: lo_ref, False)
    win_step(1, lambda h: None, False)
    win_step(0, lambda h: db_ref.at[h], True)

    gt = gt_ref[...]
    heads = []
    for h in range(GROUP_HEADS):
        a_s, a_w = acc_s[h], acc_w[h]
        o_sel = a_s[0:HEAD_DIM, :] / a_s[HEAD_DIM:HEAD_DIM + 1, :]
        o_win = a_w[0:HEAD_DIM, :] / a_w[HEAD_DIM:HEAD_DIM + 1, :]
        g = GATE_ROW + 3 * h
        heads.append(gt[g:g + 1, :] * o_cmp[h] + gt[g + 1:g + 2, :] * o_sel + gt[g + 2:g + 3, :] * o_win)
    o = jnp.concatenate(heads, axis=0).T
    ss = _seg_sum(o * o, bd_ref[...])
    y_ref[...] = o * lax.rsqrt(ss * (1.0 / HEAD_DIM) + EPS) * og_ref[...]


def _nsa(p3, cmp_pos, cmp_w1, cmp_w2, q_gain, k_gain, out_gain, slopes, tq=256):
    b, s, _ = p3.shape
    tm = tq
    n_rows = s // CMP_STRIDE
    n_cmp = (s - CMP_LEN) // CMP_STRIDE + 1
    n_sel = s // SEL_BLOCK
    bd64 = _block_ones(GROUP_WIDTH, HEAD_DIM)
    qg = jnp.tile(q_gain.astype(F32), GROUP_HEADS).reshape(1, GROUP_WIDTH)
    kg = k_gain.astype(F32)
    tok = lambda w: pl.BlockSpec((None, tm, w), lambda bi, i: (bi, i, 0))
    tok_t = lambda w: pl.BlockSpec((None, w, tm), lambda bi, i: (bi, 0, i))
    full = lambda r, c: pl.BlockSpec((r, c), lambda bi, i: (0, 0))
    assert n_sel <= LANES - HEAD_DIM, "block one-hot lanes hold at most 64 selection blocks"
    assert WINDOW == 2 * tq and tq % SEL_BLOCK == 0, "window branch walks exactly the key tiles qi-2, qi-1, qi"
    w_wide, w_t = _key_weights(tm, slopes)
    qt, kc, vc, ks, vst, kw, vwt, gt = pl.pallas_call(
        _nsa_prep_kernel,
        grid=(b, s // tm),
        in_specs=[pl.BlockSpec((None, tm, GROUP_WIDTH), lambda bi, i: (bi, i, COL_D // GROUP_WIDTH)),
                  pl.BlockSpec((None, tm, 384), lambda bi, i: (bi, i, (COL_D + GROUP_WIDTH) // 384)),
                  pl.BlockSpec((None, tm, LANES), lambda bi, i: (bi, i, COL_G // LANES)),
                  full(1, GROUP_WIDTH), full(3, HEAD_DIM), full(GROUP_WIDTH, GROUP_WIDTH),
                  full(tm, GROUP_WIDTH), full(8, tm)],
        out_specs=[tok_t(GROUP_WIDTH), tok(HEAD_DIM), tok(HEAD_DIM), tok(LANES), tok_t(NSA_V_ROWS),
                   tok(LANES), tok_t(NSA_V_ROWS), tok_t(LANES)],
        out_shape=[jax.ShapeDtypeStruct((b, GROUP_WIDTH, s), BF16),
                   jax.ShapeDtypeStruct((b, s, HEAD_DIM), F32),
                   jax.ShapeDtypeStruct((b, s, HEAD_DIM), F32),
                   jax.ShapeDtypeStruct((b, s, LANES), BF16),
                   jax.ShapeDtypeStruct((b, NSA_V_ROWS, s), BF16),
                   jax.ShapeDtypeStruct((b, s, LANES), BF16),
                   jax.ShapeDtypeStruct((b, NSA_V_ROWS, s), BF16),
                   jax.ShapeDtypeStruct((b, LANES, s), F32)],
        compiler_params=_cparams(("arbitrary", "arbitrary")),
        name="nsa_prep",
    )(p3, p3, p3, qg, kg, bd64, w_wide, w_t)

    row_w = CMP_STRIDE * HEAD_DIM
    kcmp, vcmp_t = pl.pallas_call(
        functools.partial(_nsa_cmp_kernel, n_rows=n_rows),
        grid=(b,),
        in_specs=[pl.BlockSpec((None, n_rows, row_w), lambda bi: (bi, 0, 0)),
                  pl.BlockSpec((None, n_rows, row_w), lambda bi: (bi, 0, 0)),
                  pl.BlockSpec((2, 2 * row_w, HEAD_DIM), lambda bi: (0, 0, 0)),
                  pl.BlockSpec((2, HEAD_DIM, HEAD_DIM), lambda bi: (0, 0, 0)),
                  pl.BlockSpec((2, 1, 2 * row_w), lambda bi: (0, 0, 0)),
                  pl.BlockSpec((3, HEAD_DIM), lambda bi: (0, 0))],
        out_specs=[pl.BlockSpec((None, n_rows, HEAD_DIM), lambda bi: (bi, 0, 0)),
                   pl.BlockSpec((None, HEAD_DIM, n_rows), lambda bi: (bi, 0, 0))],
        out_shape=[jax.ShapeDtypeStruct((b, n_rows, HEAD_DIM), BF16),
                   jax.ShapeDtypeStruct((b, HEAD_DIM, n_rows), BF16)],
        scratch_shapes=[pltpu.VMEM((n_rows + 8, HEAD_DIM), F32)],
        compiler_params=_cparams(("arbitrary",)),
        name="nsa_cmp",
    )(kc.reshape(b, n_rows, row_w), vc.reshape(b, n_rows, row_w), cmp_w1.astype(BF16), cmp_w2.astype(BF16),
      cmp_pos.astype(F32).reshape(2, 1, 2 * row_w), kg)

    cs = np.arange(n_rows)[:, None] * CMP_STRIDE
    ss = np.arange(n_sel)[None, :] * SEL_BLOCK
    overlap = np.clip(np.minimum(cs + CMP_LEN, ss + SEL_BLOCK) - np.maximum(cs, ss), 0, None) / CMP_LEN
    overlap[n_cmp:, :] = 0.0
    ov_t = jnp.asarray(overlap.T.astype(np.float32), BF16)

    seq = lambda r, c: pl.BlockSpec((None, r, c), lambda bi, i: (bi, 0, 0))
    kern = functools.partial(_nsa_attn_kernel, tq=tq, slopes=tuple(float(v) for v in slopes), n_cmp=n_cmp)
    return pl.pallas_call(
        kern,
        grid=(b, s // tq),
        in_specs=[pl.BlockSpec((None, GROUP_WIDTH, tq), lambda bi, i: (bi, 0, i)),
                  seq(n_rows, HEAD_DIM), seq(HEAD_DIM, n_rows),
                  seq(s, LANES), seq(NSA_V_ROWS, s), seq(s, LANES), seq(NSA_V_ROWS, s),
                  pl.BlockSpec((None, LANES, tq), lambda bi, i: (bi, 0, i)),
                  full(n_sel, n_rows),
                  pl.BlockSpec((GROUP_HEADS, tq, tq), lambda bi, i: (0, 0, 0)), full(tq, tq),
                  full(1, GROUP_WIDTH), full(GROUP_WIDTH, GROUP_WIDTH)],
        out_specs=pl.BlockSpec((None, tq, GROUP_WIDTH), lambda bi, i: (bi, i, 0)),
        out_shape=jax.ShapeDtypeStruct((b, s, GROUP_WIDTH), F32),
        scratch_shapes=[pltpu.VMEM((GROUP_HEADS, LANES, tq), BF16),
                        pltpu.VMEM((GROUP_HEADS, V_AUG, tq), F32), pltpu.VMEM((GROUP_HEADS, 8, tq), F32),
                        pltpu.VMEM((GROUP_HEADS, V_AUG, tq), F32), pltpu.VMEM((GROUP_HEADS, 8, tq), F32),
                        pltpu.VMEM((GROUP_HEADS, tq, tq), F32), pltpu.VMEM((GROUP_HEADS, tq, tq), F32)],
        compiler_params=_cparams(("arbitrary", "arbitrary")),
        name="nsa_attn",
    )(qt, kcmp, vcmp_t, ks, vst, kw, vwt, gt, ov_t, _diag_bias(tq, slopes), _window_low_bias(tq),
      out_gain.astype(F32).reshape(1, GROUP_WIDTH), bd64)


def _window_low_bias(tq):
    k = np.arange(tq)[:, None]
    q = np.arange(tq)[None, :]
    return jnp.asarray(np.where(k > q, 0.0, MASKED).astype(np.float32))


N_GROUPS = 4
EXPERTS_PER_GROUP = 8
N_EXPERTS = N_GROUPS * EXPERTS_PER_GROUP
D_EXPERT = 256
ROUTER_LANE0 = N_GROUPS


def _split3_dot(a, b_hi, b_lo):
    a_hi = a.astype(BF16)
    a_lo = (a - a_hi.astype(F32)).astype(BF16)
    return (jnp.dot(a_hi, b_hi, preferred_element_type=F32) + jnp.dot(a_lo, b_hi, preferred_element_type=F32)
            + jnp.dot(a_hi, b_lo, preferred_element_type=F32))


def _route(t, wr_hi, wr_lo, br):
    logits = _split3_dot(t, wr_hi, wr_lo) + br
    lane = lax.broadcasted_iota(jnp.int32, logits.shape, 1)
    lane_f = lane.astype(F32)
    big = float(LANES)
    is_g = lane < N_GROUPS
    gl = jnp.where(is_g, logits, MASKED)
    gmax = jnp.max(gl, axis=-1, keepdims=True)
    g_prob = 1.0 / jnp.sum(jnp.where(is_g, jnp.exp(gl - gmax), 0.0), axis=-1, keepdims=True)
    g_sel = jnp.min(jnp.where(is_g & (gl == gmax), lane_f, big), axis=-1, keepdims=True)
    lo = ROUTER_LANE0 + EXPERTS_PER_GROUP * g_sel
    in_grp = (lane_f >= lo) & (lane_f < lo + EXPERTS_PER_GROUP)
    el = jnp.where(in_grp, logits, MASKED)
    m1 = jnp.max(el, axis=-1, keepdims=True)
    i1 = jnp.min(jnp.where(in_grp & (el == m1), lane_f, big), axis=-1, keepdims=True)
    rest = in_grp & (lane_f != i1)
    el2 = jnp.where(rest, logits, MASKED)
    m2 = jnp.max(el2, axis=-1, keepdims=True)
    i2 = jnp.min(jnp.where(rest & (el2 == m2), lane_f, big), axis=-1, keepdims=True)
    r = jnp.exp(m2 - m1)
    w1 = g_prob / (1.0 + r)
    w2 = g_prob * r / (1.0 + r)
    return jnp.where(lane_f == i1, w1, 0.0) + jnp.where(lane_f == i2, w2, 0.0), g_sel


T_AUG = D_MODEL + LANES


def _outproj_kernel(x_ref, ya_ref, yb_ref, yc_ref, yd_ref, w_ref, g_ref, wr_hi_ref, wr_lo_ref, br_ref,
                    xo_ref, tg_ref, route_ref, cnt_ref, run_ref):
    @pl.when(pl.program_id(0) == 0)
    def _():
        run_ref[...] = jnp.zeros_like(run_ref)

    acc = x_ref[...]
    for gi, y_ref in enumerate((ya_ref, yb_ref, yc_ref, yd_ref)):
        acc = acc + jnp.dot(y_ref[...].astype(BF16), w_ref[GROUP_WIDTH * gi:GROUP_WIDTH * (gi + 1), :],
                            preferred_element_type=F32)
    xo_ref[...] = acc
    ms = jnp.mean(acc * acc, axis=-1, keepdims=True)
    t = acc * lax.rsqrt(ms + EPS) * g_ref[...]
    tg_ref[:, 0:D_MODEL] = t
    gate, g_sel = _route(t, wr_hi_ref[...], wr_lo_ref[...], br_ref[...])
    tg_ref[:, D_MODEL:T_AUG] = gate

    tm = t.shape[0]
    lane = lax.broadcasted_iota(jnp.int32, (tm, LANES), 1)
    onehot = jnp.where(lane.astype(F32) == g_sel, 1.0, 0.0)
    ri = lax.broadcasted_iota(jnp.int32, (tm, tm), 0)
    ci = lax.broadcasted_iota(jnp.int32, (tm, tm), 1)
    before = jnp.where(ri > ci, 1.0, 0.0).astype(BF16)
    prefix = jnp.dot(before, onehot.astype(BF16), preferred_element_type=F32)
    rank = jnp.sum(onehot * (run_ref[...] + prefix), axis=-1, keepdims=True)
    route_ref[...] = jnp.where(lane == 0, g_sel, 0.0) + jnp.where(lane == 1, rank, 0.0)
    run_ref[...] = run_ref[...] + jnp.sum(onehot, axis=0, keepdims=True)
    cnt_ref[...] = run_ref[...]


def _outproj(x2d, ys, w_out, ffn_gain, w_group, b_group, w_expert, b_expert, tm=256):
    t = x2d.shape[0]
    wr = jnp.zeros((D_MODEL, LANES), F32).at[:, 0:N_GROUPS].set(w_group).at[:, ROUTER_LANE0:ROUTER_LANE0 + N_EXPERTS].set(w_expert)
    wr_hi = wr.astype(BF16)
    wr_lo = (wr - wr_hi.astype(F32)).astype(BF16)
    br = jnp.zeros((1, LANES), F32).at[0, 0:N_GROUPS].set(b_group).at[0, ROUTER_LANE0:ROUTER_LANE0 + N_EXPERTS].set(b_expert)
    row = lambda w: pl.BlockSpec((tm, w), lambda i: (i, 0))
    full = lambda r, c: pl.BlockSpec((r, c), lambda i: (0, 0))
    return list(pl.pallas_call(
        _outproj_kernel,
        grid=(t // tm,),
        in_specs=[row(D_MODEL), row(GROUP_WIDTH), row(GROUP_WIDTH), row(GROUP_WIDTH), row(GROUP_WIDTH),
                  full(D_MODEL, D_MODEL), full(1, D_MODEL), full(D_MODEL, LANES), full(D_MODEL, LANES), full(1, LANES)],
        out_specs=[row(D_MODEL), row(T_AUG), row(LANES), full(1, LANES)],
        out_shape=[jax.ShapeDtypeStruct((t, D_MODEL), F32),
                   jax.ShapeDtypeStruct((t, T_AUG), F32),
                   jax.ShapeDtypeStruct((t, LANES), F32),
                   jax.ShapeDtypeStruct((1, LANES), F32)],
        scratch_shapes=[pltpu.VMEM((1, LANES), F32)],
        compiler_params=_cparams(("arbitrary",)),
        name="outproj_router",
    )(x2d, *ys, w_out.astype(BF16), ffn_gain.reshape(1, D_MODEL), wr_hi, wr_lo, br))


MOE_TILE = 1024
DMA_PRIORITIES = 2


def _row_copies(n, make_copy):
    def start(k, carry):
        for p in range(DMA_PRIORITIES):
            make_copy(DMA_PRIORITIES * k + p).start(priority=p)
        return carry

    def wait(r, carry):
        make_copy(r).wait()
        return carry

    lax.fori_loop(0, n // DMA_PRIORITIES, start, 0)
    lax.fori_loop(0, n, wait, 0)


def _moe_scatter_kernel(pos_ref, tg_ref, xs_in_ref, xs_ref, sem):
    del xs_in_ref
    tm = tg_ref.shape[0]
    base = pl.program_id(0) * tm
    _row_copies(tm, lambda r: pltpu.make_async_copy(tg_ref.at[pl.ds(r, 1)],
                                                    xs_ref.at[pl.ds(pos_ref[base + r], 1)], sem))


def _moe_expert_kernel(grp_ref, valid_ref, xs_ref, wg_ref, wu_ref, wd_ref, y_ref, acc_ref):
    i, e = pl.program_id(0), pl.program_id(1)

    @pl.when(e == 0)
    def _():
        acc_ref[...] = jnp.zeros_like(acc_ref)

    @pl.when(valid_ref[i] == 1)
    def _():
        x = xs_ref[:, 0:D_MODEL].astype(BF16)
        gate = xs_ref[:, D_MODEL:T_AUG]
        lane = lax.broadcasted_iota(jnp.int32, gate.shape, 1)
        col = ROUTER_LANE0 + grp_ref[i] * EXPERTS_PER_GROUP + e
        w = jnp.sum(jnp.where(lane == col, gate, 0.0), axis=-1, keepdims=True)
        a = jnp.dot(x, wg_ref[...].astype(BF16), preferred_element_type=F32)
        u = jnp.dot(x, wu_ref[...].astype(BF16), preferred_element_type=F32)
        act = a * _sigmoid(a) * u * w
        acc_ref[...] += jnp.dot(act.astype(BF16), wd_ref[...].astype(BF16), preferred_element_type=F32)

    @pl.when(e == EXPERTS_PER_GROUP - 1)
    def _():
        y_ref[...] = acc_ref[...]


def _moe_combine_kernel(pos_ref, x_ref, ys_ref, o_ref, buf_ref, sem):
    tm = x_ref.shape[0]
    base = pl.program_id(0) * tm
    _row_copies(tm, lambda r: pltpu.make_async_copy(ys_ref.at[pl.ds(pos_ref[base + r], 1)],
                                                    buf_ref.at[pl.ds(r, 1)], sem))
    o_ref[...] = x_ref[...] + buf_ref[...]


def _moe_routed(x2d, tg, route, counts, w_gate, w_up, w_down, tm=512):
    t = x2d.shape[0]
    te = MOE_TILE
    n_tiles = t // te + N_GROUPS
    n_rows = n_tiles * te
    grp = route[:, 0].astype(jnp.int32)
    rank = route[:, 1].astype(jnp.int32)
    cnt = counts[0, 0:N_GROUPS].astype(jnp.int32)
    padded = ((cnt + te - 1) // te) * te
    ends = jnp.cumsum(padded)
    pos = (ends - padded)[grp] + rank
    starts = jnp.arange(n_tiles, dtype=jnp.int32) * te
    tile_grp = jnp.minimum(jnp.sum((starts[:, None] >= ends[None, :]).astype(jnp.int32), axis=1), N_GROUPS - 1)
    tile_valid = (starts < ends[-1]).astype(jnp.int32)

    xs = pl.pallas_call(
        _moe_scatter_kernel,
        grid_spec=pltpu.PrefetchScalarGridSpec(
            num_scalar_prefetch=1, grid=(t // tm,),
            in_specs=[pl.BlockSpec((tm, T_AUG), lambda i, pos: (i, 0)), pl.BlockSpec(memory_space=pl.ANY)],
            out_specs=pl.BlockSpec(memory_space=pl.ANY),
            scratch_shapes=[pltpu.SemaphoreType.DMA(())]),
        out_shape=jax.ShapeDtypeStruct((n_rows, T_AUG), F32),
        input_output_aliases={2: 0},
        compiler_params=_cparams(("arbitrary",)),
        name="moe_scatter",
    )(pos, tg, jnp.zeros((n_rows, T_AUG), F32))

    wg = w_gate.reshape(N_EXPERTS, D_MODEL, D_EXPERT)
    wu = w_up.reshape(N_EXPERTS, D_MODEL, D_EXPERT)
    wd = w_down.reshape(N_EXPERTS, D_EXPERT, D_MODEL)
    wsel = lambda i, e, g, v: (g[i] * EXPERTS_PER_GROUP + e, 0, 0)
    ys = pl.pallas_call(
        _moe_expert_kernel,
        grid_spec=pltpu.PrefetchScalarGridSpec(
            num_scalar_prefetch=2, grid=(n_tiles, EXPERTS_PER_GROUP),
            in_specs=[pl.BlockSpec((te, T_AUG), lambda i, e, g, v: (i, 0)),
                      pl.BlockSpec((None, D_MODEL, D_EXPERT), wsel),
                      pl.BlockSpec((None, D_MODEL, D_EXPERT), wsel),
                      pl.BlockSpec((None, D_EXPERT, D_MODEL), wsel)],
            out_specs=pl.BlockSpec((te, D_MODEL), lambda i, e, g, v: (i, 0)),
            scratch_shapes=[pltpu.VMEM((te, D_MODEL), F32)]),
        out_shape=jax.ShapeDtypeStruct((n_rows, D_MODEL), F32),
        compiler_params=_cparams(("arbitrary", "arbitrary")),
        name="moe_experts",
    )(tile_grp, tile_valid, xs, wg, wu, wd)

    return pl.pallas_call(
        _moe_combine_kernel,
        grid_spec=pltpu.PrefetchScalarGridSpec(
            num_scalar_prefetch=1, grid=(t // tm,),
            in_specs=[pl.BlockSpec((tm, D_MODEL), lambda i, pos: (i, 0)), pl.BlockSpec(memory_space=pl.ANY)],
            out_specs=pl.BlockSpec((tm, D_MODEL), lambda i, pos: (i, 0)),
            scratch_shapes=[pltpu.VMEM((tm, D_MODEL), F32), pltpu.SemaphoreType.DMA(())]),
        out_shape=jax.ShapeDtypeStruct((t, D_MODEL), F32),
        compiler_params=_cparams(("arbitrary",)),
        name="moe_combine",
    )(pos, x2d, ys)


def _alibi_slopes():
    n = 2 * GROUP_HEADS
    s = 2.0 ** (-8.0 * np.arange(1, n + 1) / n)
    return s[0::2], s[1::2]


def kernel(x, norm_mix, norm_ffn, w_in, w_out, diff_q_gain, diff_k_gain, diff_lambda, diff_sub_gain, mlstm_conv, mlstm_gate_bias, mlstm_out_gain, hgrn_lower_bounds, hgrn_out_gain, nsa_cmp_pos, nsa_cmp_w1, nsa_cmp_w2, nsa_q_gain, nsa_k_gain, nsa_out_gain, moe_w_group, moe_b_group, moe_w_expert, moe_b_expert, moe_w_gate, moe_w_up, moe_w_down):
    b, s, d = x.shape
    slopes_diff, slopes_nsa = _alibi_slopes()
    lb_soft = jax.nn.softmax(hgrn_lower_bounds.astype(F32), axis=0)
    lower_bounds = jnp.cumsum(lb_soft, axis=0) - lb_soft[0]
    x2d = x.reshape(b * s, d)
    for l in range(norm_mix.shape[0]):
        p = _inproj(x2d, norm_mix[l], _pack_w_in(w_in[l])).reshape(b, s, P_COLS)
        y_a = _diff_attention(p, diff_q_gain[l], diff_k_gain[l], diff_lambda[l], diff_sub_gain[l], slopes_diff, l)
        y_b = _mlstm(p, mlstm_conv[l], mlstm_gate_bias[l], mlstm_out_gain[l])
        y_c = _hgrn2(p, lower_bounds[l], hgrn_out_gain[l])
        y_d = _nsa(p, nsa_cmp_pos[l], nsa_cmp_w1[l], nsa_cmp_w2[l], nsa_q_gain[l], nsa_k_gain[l],
                   nsa_out_gain[l], slopes_nsa)
        ys = [y.reshape(b * s, GROUP_WIDTH) for y in (y_a, y_b, y_c, y_d)]
        x2d, tg, route, counts = _outproj(x2d, ys, w_out[l], norm_ffn[l], moe_w_group[l], moe_b_group[l],
                                          moe_w_expert[l], moe_b_expert[l])
        x2d = _moe_routed(x2d, tg, route, counts, moe_w_gate[l], moe_w_up[l], moe_w_down[l])
    return x2d.reshape(b, s, d)
```

```python
import functools
import math

import numpy as np
import jax
import jax.numpy as jnp
from jax import lax
from jax.experimental import pallas as pl
from jax.experimental.pallas import tpu as pltpu

F32 = jnp.float32
BF16 = jnp.bfloat16

D_MODEL = 1024
HEAD_DIM = 64
GROUP_HEADS = 4
GROUP_WIDTH = GROUP_HEADS * HEAD_DIM
DIFF_HALF = HEAD_DIM // 2
EPS = 1e-6
NEG_INF = -1e30
LOG2E = math.log2(math.e)
M_INIT = -1e30
MASKED = -2e30

LANES = 128
VMEM_LIMIT = 48 * 1024 * 1024

COL_A = 0
COL_B = 768
COL_C = 1792
COL_D = 2816
COL_G = 3456
P_COLS = 3584


def _cparams(sem, flags=None):
    return pltpu.CompilerParams(dimension_semantics=sem, vmem_limit_bytes=VMEM_LIMIT, flags=flags)


def _block_ones(width, seg):
    i = np.arange(width)
    return jnp.asarray((i[:, None] // seg == i[None, :] // seg).astype(np.float32), BF16)


def _seg_sum(x, bd):
    hi = x.astype(BF16)
    lo = (x - hi.astype(F32)).astype(BF16)
    return (jnp.dot(hi, bd, preferred_element_type=F32) + jnp.dot(lo, bd, preferred_element_type=F32))


def _inproj_kernel(x_ref, g_ref, w_ref, o_ref):
    x = x_ref[...]
    ms = jnp.mean(x * x, axis=-1, keepdims=True)
    h = (x * lax.rsqrt(ms + EPS) * g_ref[...]).astype(BF16)
    o_ref[...] = jnp.dot(h, w_ref[...], preferred_element_type=F32)


def _inproj(x2d, gain, w_cat, tm=256):
    t = x2d.shape[0]
    return pl.pallas_call(
        _inproj_kernel,
        grid=(t // tm,),
        in_specs=[pl.BlockSpec((tm, D_MODEL), lambda i: (i, 0)),
                  pl.BlockSpec((1, D_MODEL), lambda i: (0, 0)),
                  pl.BlockSpec((D_MODEL, P_COLS), lambda i: (0, 0))],
        out_specs=pl.BlockSpec((tm, P_COLS), lambda i: (i, 0)),
        out_shape=jax.ShapeDtypeStruct((t, P_COLS), F32),
        compiler_params=_cparams(("arbitrary",)),
        name="inproj",
    )(x2d, gain.reshape(1, D_MODEL), w_cat)


IN_COLS = 3476


def _pack_w_in_kernel(w_ref, o_ref):
    w = w_ref[...]
    a_b = w[:, 0:1536]
    gates_b = w[:, 1536:1544]
    rest = w[:, 1544:3464]
    gates_d = w[:, 3464:3476]
    pad = jnp.zeros((w.shape[0], P_COLS - COL_G - 20), F32)
    o_ref[...] = jnp.concatenate([a_b, rest, gates_b, gates_d, pad], axis=1).astype(BF16)


def _pack_w_in(w, tr=128):
    d = w.shape[0]
    return pl.pallas_call(
        _pack_w_in_kernel,
        grid=(d // tr,),
        in_specs=[pl.BlockSpec((tr, IN_COLS), lambda i: (i, 0))],
        out_specs=pl.BlockSpec((tr, P_COLS), lambda i: (i, 0)),
        out_shape=jax.ShapeDtypeStruct((d, P_COLS), BF16),
        compiler_params=_cparams(("arbitrary",)),
        name="pack_w_in",
    )(w)


def _diff_prep_kernel(p_ref, qg_ref, kg_ref, bd_ref, w_ref, wt_ref, q_ref, k_ref, vt_ref):
    p = p_ref[...]
    bd = bd_ref[...]

    def norm(x, g):
        ss = _seg_sum(x * x, bd)
        return x * lax.rsqrt(ss * (1.0 / DIFF_HALF) + EPS) * g

    q = norm(p[:, 0:256], qg_ref[...]) * (DIFF_HALF ** -0.5 * LOG2E)
    k = norm(p[:, 256:512], kg_ref[...])
    q_ref[...] = q.T.astype(BF16)
    k_ref[...] = k.astype(BF16)
    v = p[:, 512:768]
    v_t = v.T
    vw_t = (v * w_ref[...]).T
    w_t = wt_ref[...]
    tm = v_t.shape[1]
    first = lax.broadcasted_iota(jnp.int32, (V_AUG - HEAD_DIM, tm), 0) == 0
    blocks = []
    for h in range(GROUP_HEADS):
        hs = slice(HEAD_DIM * h, HEAD_DIM * (h + 1))
        blocks += [vw_t[hs, :], jnp.where(first, jnp.broadcast_to(w_t[h:h + 1, :], first.shape), 0.0),
                   v_t[hs, :], jnp.where(first, 1.0, 0.0)]
    vt_ref[...] = jnp.concatenate(blocks, axis=0).astype(BF16)


V_AUG = 80
V_HEAD = 2 * V_AUG


def _key_weights(tile, slopes):
    kl = np.arange(tile, dtype=np.float64) - (tile - 1)
    w = np.stack([np.exp2(sl * LOG2E * kl) for sl in slopes])
    wide = np.repeat(w.T, HEAD_DIM, axis=1)
    w8 = np.zeros((8, tile)); w8[:len(slopes)] = w
    return jnp.asarray(wide.astype(np.float32)), jnp.asarray(w8.astype(np.float32))


def _diag_bias(tq, slopes):
    k = np.arange(tq)[:, None]
    q = np.arange(tq)[None, :]
    tabs = [np.where(k <= q, sl * LOG2E * k.astype(np.float64), MASKED) for sl in slopes]
    return jnp.asarray(np.stack(tabs).astype(np.float32))


AHEAD = 2


def _flash_update(s_ref, bias_ref, off, vt, m_ref, acc_ref, penalty=None):
    tq = s_ref.shape[1]
    ps, olds, news = [], [], []
    for c in range(tq // LANES):
        cols = slice(LANES * c, LANES * (c + 1))
        s = s_ref[:, cols]
        if bias_ref is not None:
            s = s + bias_ref[:, cols]
        if penalty is not None:
            s = s + penalty
        m_tile = jnp.max(s, axis=0, keepdims=True)
        ps.append(jnp.exp2(s - m_tile).astype(BF16))
        m_old = m_ref[0:1, cols]
        m_new = jnp.maximum(m_old, m_tile + off)
        m_ref[0:1, cols] = m_new
        olds.append(jnp.exp2(m_old - m_new))
        news.append(jnp.exp2(m_tile + off - m_new))
    p, old, new = jnp.concatenate(ps, axis=1), jnp.concatenate(olds, axis=1), jnp.concatenate(news, axis=1)

    def finish():
        pv = jnp.dot(vt, p, preferred_element_type=F32)
        acc_ref[...] = old * acc_ref[...] + new * pv

    return finish


def _run_chains(n, prefetch, softmax):
    finish = None
    for c in range(n):
        prefetch(c)
        nxt = softmax(c)
        if finish is not None:
            finish()
        finish = nxt
    finish()


def _diff_attn_kernel(lam_ref, qt_ref, k_ref, vt_ref, db_ref, sg_ref, bd_ref, o_ref,
                      qm_ref, acc_ref, m_ref, s_ref, *, tq, slopes, out_scale):
    qi = pl.program_id(1)
    n_half = 2 * GROUP_HEADS
    per_tile = LANES // DIFF_HALF
    row = lax.broadcasted_iota(jnp.int32, (LANES, tq), 0)
    for j in range(n_half):
        slab = qt_ref[LANES * (j // per_tile):LANES * (j // per_tile + 1), :].astype(F32)
        r0 = DIFF_HALF * (j % per_tile)
        qm_ref[j] = jnp.where((row >= r0) & (row < r0 + DIFF_HALF), slab, 0.0).astype(BF16)
    acc_ref[...] = jnp.zeros(acc_ref.shape, F32)
    m_ref[...] = jnp.full(m_ref.shape, M_INIT, F32)

    n_slots = s_ref.shape[0]

    def scores(kt, j):
        start = pl.multiple_of(kt * tq, tq)
        kj = k_ref[pl.ds(start, tq), LANES * (j // per_tile):LANES * (j // per_tile + 1)]
        s_ref[j % n_slots] = jnp.dot(kj, qm_ref[j], preferred_element_type=F32)

    def process(kt, j, diag):
        h = j // 2
        start = pl.multiple_of(kt * tq, tq)
        off = 0.0 if diag else ((kt - qi) * tq + (tq - 1)).astype(F32) * (slopes[h] * LOG2E)
        base = V_HEAD * h + (V_AUG if diag else 0)
        vt = vt_ref[base:base + V_AUG, pl.ds(start, tq)]
        return _flash_update(s_ref.at[j % n_slots], db_ref.at[h] if diag else None, off, vt, m_ref.at[j],
                             acc_ref.at[j])

    def step(kt, diag):
        def prefetch(j):
            if j + AHEAD < n_half:
                scores(kt, j + AHEAD)
            elif not diag:
                scores(kt + 1, j + AHEAD - n_half)

        _run_chains(n_half, prefetch, lambda j: process(kt, j, diag))

    for j in range(AHEAD):
        scores(0, j)

    def body(kt, carry):
        step(kt, False)
        return carry

    lax.fori_loop(0, qi, body, 0)
    step(qi, True)

    lam = lam_ref[0, 0]
    heads = []
    for h in range(GROUP_HEADS):
        a0, a1 = acc_ref[2 * h], acc_ref[2 * h + 1]
        o0 = a0[0:HEAD_DIM, :] / a0[HEAD_DIM:HEAD_DIM + 1, :]
        o1 = a1[0:HEAD_DIM, :] / a1[HEAD_DIM:HEAD_DIM + 1, :]
        heads.append(o0 - lam * o1)
    o = jnp.concatenate(heads, axis=0).T
    ss = _seg_sum(o * o, bd_ref[...])
    o_ref[...] = o * lax.rsqrt(ss * (1.0 / HEAD_DIM) + EPS) * (sg_ref[...] * out_scale)


def _diff_attention(p3, q_gain, k_gain, lam_vecs, sub_gain, slopes, layer_idx, tq=256):
    b, s, _ = p3.shape
    tm = tq
    bd32 = _block_ones(GROUP_WIDTH, DIFF_HALF)
    bd64 = _block_ones(GROUP_WIDTH, HEAD_DIM)
    qg = jnp.tile(q_gain.astype(F32), 2 * GROUP_HEADS).reshape(1, GROUP_WIDTH)
    kg = jnp.tile(k_gain.astype(F32), 2 * GROUP_HEADS).reshape(1, GROUP_WIDTH)
    w_wide, w_t = _key_weights(tm, slopes)
    v_rows = GROUP_HEADS * V_HEAD
    qn, kn, vt = pl.pallas_call(
        _diff_prep_kernel,
        grid=(b, s // tm),
        in_specs=[pl.BlockSpec((None, tm, 768), lambda bi, i: (bi, i, COL_A // 768)),
                  pl.BlockSpec((1, GROUP_WIDTH), lambda bi, i: (0, 0)),
                  pl.BlockSpec((1, GROUP_WIDTH), lambda bi, i: (0, 0)),
                  pl.BlockSpec((GROUP_WIDTH, GROUP_WIDTH), lambda bi, i: (0, 0)),
                  pl.BlockSpec((tm, GROUP_WIDTH), lambda bi, i: (0, 0)),
                  pl.BlockSpec((8, tm), lambda bi, i: (0, 0))],
        out_specs=[pl.BlockSpec((None, GROUP_WIDTH, tm), lambda bi, i: (bi, 0, i)),
                   pl.BlockSpec((None, tm, GROUP_WIDTH), lambda bi, i: (bi, i, 0)),
                   pl.BlockSpec((None, v_rows, tm), lambda bi, i: (bi, 0, i))],
        out_shape=[jax.ShapeDtypeStruct((b, GROUP_WIDTH, s), BF16),
                   jax.ShapeDtypeStruct((b, s, GROUP_WIDTH), BF16),
                   jax.ShapeDtypeStruct((b, v_rows, s), BF16)],
        compiler_params=_cparams(("arbitrary", "arbitrary")),
        name="diff_prep",
    )(p3, qg, kg, bd32, w_wide, w_t)

    lam_init = 0.8 - 0.6 * math.exp(-0.3 * layer_idx)
    lv = lam_vecs.astype(F32)
    lam = (jnp.exp(jnp.dot(lv[0], lv[1])) - jnp.exp(jnp.dot(lv[2], lv[3])) + lam_init).reshape(1, 1)
    kern = functools.partial(_diff_attn_kernel, tq=tq, slopes=tuple(float(v) for v in slopes),
                             out_scale=1.0 - lam_init)
    return pl.pallas_call(
        kern,
        grid=(b, s // tq),
        in_specs=[pl.BlockSpec(memory_space=pltpu.SMEM),
                  pl.BlockSpec((None, GROUP_WIDTH, tq), lambda bi, i: (bi, 0, i)),
                  pl.BlockSpec((None, s, GROUP_WIDTH), lambda bi, i: (bi, 0, 0)),
                  pl.BlockSpec((None, v_rows, s), lambda bi, i: (bi, 0, 0)),
                  pl.BlockSpec((GROUP_HEADS, tq, tq), lambda bi, i: (0, 0, 0)),
                  pl.BlockSpec((1, GROUP_WIDTH), lambda bi, i: (0, 0)),
                  pl.BlockSpec((GROUP_WIDTH, GROUP_WIDTH), lambda bi, i: (0, 0))],
        out_specs=pl.BlockSpec((None, tq, GROUP_WIDTH), lambda bi, i: (bi, i, 0)),
        out_shape=jax.ShapeDtypeStruct((b, s, GROUP_WIDTH), F32),
        scratch_shapes=[pltpu.VMEM((2 * GROUP_HEADS, LANES, tq), BF16),
                        pltpu.VMEM((2 * GROUP_HEADS, V_AUG, tq), F32),
                        pltpu.VMEM((2 * GROUP_HEADS, 8, tq), F32),
                        pltpu.VMEM((2 * AHEAD, tq, tq), F32)],
        compiler_params=_cparams(("arbitrary", "arbitrary")),
        name="diff_attn",
    )(lam, qn, kn, vt, _diag_bias(tq, slopes), sub_gain.astype(F32).reshape(1, GROUP_WIDTH), bd64)


MLSTM_CONV = 4
CONV_HALO = 8
AUG = 128


def _tril_sum(x, tril):
    hi = x.astype(BF16)
    lo = (x - hi.astype(F32)).astype(BF16)
    return jnp.dot(tril, hi, preferred_element_type=F32) + jnp.dot(tril, lo, preferred_element_type=F32)


def _log_sigmoid(x):
    return jnp.minimum(x, 0.0) - jnp.log1p(jnp.exp(-jnp.abs(x)))


def _sigmoid(x):
    return 1.0 / (1.0 + jnp.exp(-x))


def _mlstm_kernel(q_ref, k_ref, v_ref, o_ref, g_ref, cw_ref, gb_ref, og_ref, bd_ref, y_ref,
                  halo_ref, c_ref, m_ref, *, L):
    ci = pl.program_id(1)

    @pl.when(ci == 0)
    def _():
        halo_ref[0:CONV_HALO, :] = jnp.zeros((CONV_HALO, 2 * GROUP_WIDTH), F32)
        c_ref[...] = jnp.zeros_like(c_ref)
        m_ref[...] = jnp.zeros_like(m_ref)

    halo_ref[CONV_HALO:CONV_HALO + L, 0:GROUP_WIDTH] = q_ref[...]
    halo_ref[CONV_HALO:CONV_HALO + L, GROUP_WIDTH:2 * GROUP_WIDTH] = k_ref[...]
    conv = jnp.zeros((L, 2 * GROUP_WIDTH), F32)
    for j in range(MLSTM_CONV):
        start = CONV_HALO - (MLSTM_CONV - 1) + j
        conv = conv + halo_ref[start:start + L, :] * cw_ref[j:j + 1, :]
    halo_ref[0:CONV_HALO, :] = halo_ref[L:L + CONV_HALO, :]
    qk = conv * _sigmoid(conv)
    q = qk[:, 0:GROUP_WIDTH] * (HEAD_DIM ** -0.5)
    k = qk[:, GROUP_WIDTH:2 * GROUP_WIDTH]
    v = v_ref[...]

    gates = g_ref[...] + gb_ref[...]
    ri = lax.broadcasted_iota(jnp.int32, (L, L), 0)
    cj = lax.broadcasted_iota(jnp.int32, (L, L), 1)
    causal = ri >= cj
    tril = jnp.where(causal, 1.0, 0.0).astype(BF16)
    gcum = _tril_sum(_log_sigmoid(gates), tril)
    gcum_t = gcum.T
    gates_t = gates.T
    lane_aug = lax.broadcasted_iota(jnp.int32, (L, AUG), 1)
    m_all = m_ref[...]

    outs = []
    for h in range(GROUP_HEADS):
        hs = slice(HEAD_DIM * h, HEAD_DIM * (h + 1))
        q_h, k_h = q[:, hs], k[:, hs]
        v_aug = jnp.where(lane_aug == HEAD_DIM, 1.0,
                          jnp.concatenate([v[:, hs], jnp.zeros((L, AUG - HEAD_DIM), F32)], axis=1))
        g_col = gcum[:, 4 + h:5 + h]
        li_col = gates[:, h:h + 1]
        g_row = gcum_t[4 + h:5 + h, :]
        li_row = gates_t[h:h + 1, :]
        g_last = g_row[:, L - 1:L]
        m_prev = m_all[:, h:h + 1]
        log_d = jnp.where(causal, g_col - g_row + li_row, NEG_INF)
        log_inter = g_col + m_prev
        m_t = jnp.maximum(log_inter, jnp.max(log_d, axis=1, keepdims=True))
        s_qk = lax.dot_general(q_h.astype(BF16), k_h.astype(BF16), (((1,), (1,)), ((), ())),
                               preferred_element_type=F32)
        w_intra = s_qk * jnp.exp(log_d - m_t)
        w_inter = jnp.exp(log_inter - m_t)
        c_aug = c_ref[h]
        num = (w_inter * jnp.dot(q_h.astype(BF16), c_aug.astype(BF16), preferred_element_type=F32)
               + jnp.dot(w_intra.astype(BF16), v_aug.astype(BF16), preferred_element_type=F32))
        den = num[:, HEAD_DIM:HEAD_DIM + 1]
        outs.append(num[:, 0:HEAD_DIM] / jnp.maximum(jnp.abs(den), jnp.exp(-m_t)))

        log_a = g_last - g_col + li_col
        m_new = jnp.maximum(g_last + m_prev, jnp.max(log_a, axis=0, keepdims=True))
        a_col = jnp.exp(log_a - m_new)
        decay = jnp.exp(g_last + m_prev - m_new)
        ak_t = (k_h * a_col).T.astype(BF16)
        c_ref[h] = decay * c_aug + jnp.dot(ak_t, v_aug.astype(BF16), preferred_element_type=F32)
        m_ref[:, h:h + 1] = m_new

    hcat = jnp.concatenate(outs, axis=1)
    ss = _seg_sum(hcat * hcat, bd_ref[...])
    y_ref[...] = hcat * lax.rsqrt(ss * (1.0 / HEAD_DIM) + EPS) * og_ref[...] * _sigmoid(o_ref[...])


def _mlstm(p3, conv_w, gate_bias, out_gain, L=256):
    b, s, _ = p3.shape
    gb = jnp.zeros((1, LANES), F32).at[0, 0:GROUP_HEADS].set(gate_bias[0]).at[0, GROUP_HEADS:2 * GROUP_HEADS].set(gate_bias[1])
    col = lambda off: (lambda bi, i: (bi, i, off // GROUP_WIDTH))
    return pl.pallas_call(
        functools.partial(_mlstm_kernel, L=L),
        grid=(b, s // L),
        in_specs=[pl.BlockSpec((None, L, GROUP_WIDTH), col(COL_B)),
                  pl.BlockSpec((None, L, GROUP_WIDTH), col(COL_B + 256)),
                  pl.BlockSpec((None, L, GROUP_WIDTH), col(COL_B + 512)),
                  pl.BlockSpec((None, L, GROUP_WIDTH), col(COL_B + 768)),
                  pl.BlockSpec((None, L, LANES), lambda bi, i: (bi, i, COL_G // LANES)),
                  pl.BlockSpec((MLSTM_CONV, 2 * GROUP_WIDTH), lambda bi, i: (0, 0)),
                  pl.BlockSpec((1, LANES), lambda bi, i: (0, 0)),
                  pl.BlockSpec((1, GROUP_WIDTH), lambda bi, i: (0, 0)),
                  pl.BlockSpec((GROUP_WIDTH, GROUP_WIDTH), lambda bi, i: (0, 0))],
        out_specs=pl.BlockSpec((None, L, GROUP_WIDTH), lambda bi, i: (bi, i, 0)),
        out_shape=jax.ShapeDtypeStruct((b, s, GROUP_WIDTH), F32),
        scratch_shapes=[pltpu.VMEM((CONV_HALO + L, 2 * GROUP_WIDTH), F32),
                        pltpu.VMEM((GROUP_HEADS, HEAD_DIM, AUG), F32),
                        pltpu.VMEM((1, LANES), F32)],
        compiler_params=_cparams(("arbitrary", "arbitrary")),
        name="mlstm",
    )(p3, p3, p3, p3, p3, conv_w.astype(F32), gb, out_gain.astype(F32).reshape(1, GROUP_WIDTH),
      _block_ones(GROUP_WIDTH, HEAD_DIM))


HGRN_CHUNK = 16
LB_FLOOR = 1e-30


def _hgrn_kernel(q_ref, f_ref, i_ref, g_ref, lb_ref, og_ref, bd_ref, y_ref,
                 lf_s, kk_s, vv_s, st_ref, *, TL):
    ci = pl.program_id(1)
    C = HGRN_CHUNK
    W = GROUP_WIDTH

    @pl.when(ci == 0)
    def _():
        lf_s[0:C, :] = jnp.zeros((C, W), F32)
        kk_s[0:C, :] = jnp.zeros((C, W), F32)
        vv_s[0:C, :] = jnp.zeros((C, W), F32)
        st_ref[...] = jnp.zeros_like(st_ref)

    z = f_ref[...]
    a = lb_ref[0:1, :]
    c = lb_ref[1:2, :] + _log_sigmoid(z)
    mx = jnp.maximum(a, c)
    lf = mx + jnp.log1p(jnp.exp(-jnp.abs(a - c)))
    kk = lb_ref[2:3, :] * _sigmoid(-z) + lb_ref[3:4, :]
    qx = q_ref[...]
    qs = qx * _sigmoid(qx)
    vv = i_ref[...]
    lf_s[C:C + TL, :] = lf
    kk_s[C:C + TL, :] = kk
    vv_s[C:C + TL, :] = vv

    bd = bd_ref[...]
    row = lax.broadcasted_iota(jnp.int32, (TL, W), 0)
    rmod = row & (C - 1)

    acc = jnp.zeros((TL, W), F32)
    dsum = jnp.zeros((TL, W), F32)
    for delta in range(C):
        if delta > 0:
            dsum = dsum + lf_s[C - (delta - 1):C - (delta - 1) + TL, :]
        x = qs * kk_s[C - delta:C - delta + TL, :] * jnp.exp(dsum)
        x = jnp.where(rmod >= delta, x, 0.0)
        att = jnp.dot(x.astype(BF16), bd, preferred_element_type=F32)
        acc = acc + att * vv_s[C - delta:C - delta + TL, :]

    ri = lax.broadcasted_iota(jnp.int32, (TL, TL), 0)
    cj = lax.broadcasted_iota(jnp.int32, (TL, TL), 1)
    same = (ri // C) == (cj // C)
    tril = jnp.where(same & (ri >= cj), 1.0, 0.0).astype(BF16)
    ones = jnp.where(same, 1.0, 0.0).astype(BF16)
    bcum = _tril_sum(lf, tril)
    blast = _tril_sum(lf, ones)
    q_a = (qs * jnp.exp(bcum)).astype(BF16)
    k_b = (kk * jnp.exp(blast - bcum)).astype(BF16)
    dec = jnp.exp(blast)
    vv_b = vv.astype(BF16)
    hmask = bd.astype(F32)
    state = st_ref[...]
    inters = []
    for ch in range(TL // C):
        r = slice(ch * C, (ch + 1) * C)
        inters.append(lax.dot_general(q_a[r], state.astype(BF16), (((1,), (1,)), ((), ())),
                                      preferred_element_type=F32))
        upd = lax.dot_general(vv_b[r], k_b[r], (((0,), (0,)), ((), ())), preferred_element_type=F32)
        state = state * dec[ch * C:ch * C + 1, :] + upd * hmask
    st_ref[...] = state
    o = acc + jnp.concatenate(inters, axis=0)
    ss = _seg_sum(o * o, bd)
    y_ref[...] = o * lax.rsqrt(ss * (1.0 / HEAD_DIM) + EPS) * og_ref[...] * _sigmoid(g_ref[...])


def _hgrn2(p3, lower_bound, out_gain, TL=256):
    b, s, _ = p3.shape
    lb = lower_bound.astype(F32)
    lbf = jnp.maximum(lb, LB_FLOOR)
    lbp = jnp.stack([jnp.log(lbf), jnp.log1p(-lb), 1.0 - lb, lb - lbf])
    col = lambda off: (lambda bi, i: (bi, i, off // GROUP_WIDTH))
    C = HGRN_CHUNK
    return pl.pallas_call(
        functools.partial(_hgrn_kernel, TL=TL),
        grid=(b, s // TL),
        in_specs=[pl.BlockSpec((None, TL, GROUP_WIDTH), col(COL_C)),
                  pl.BlockSpec((None, TL, GROUP_WIDTH), col(COL_C + 256)),
                  pl.BlockSpec((None, TL, GROUP_WIDTH), col(COL_C + 512)),
                  pl.BlockSpec((None, TL, GROUP_WIDTH), col(COL_C + 768)),
                  pl.BlockSpec((4, GROUP_WIDTH), lambda bi, i: (0, 0)),
                  pl.BlockSpec((1, GROUP_WIDTH), lambda bi, i: (0, 0)),
                  pl.BlockSpec((GROUP_WIDTH, GROUP_WIDTH), lambda bi, i: (0, 0))],
        out_specs=pl.BlockSpec((None, TL, GROUP_WIDTH), lambda bi, i: (bi, i, 0)),
        out_shape=jax.ShapeDtypeStruct((b, s, GROUP_WIDTH), F32),
        scratch_shapes=[pltpu.VMEM((C + TL, GROUP_WIDTH), F32),
                        pltpu.VMEM((C + TL, GROUP_WIDTH), F32),
                        pltpu.VMEM((C + TL, GROUP_WIDTH), F32),
                        pltpu.VMEM((GROUP_WIDTH, GROUP_WIDTH), F32)],
        compiler_params=_cparams(("arbitrary", "arbitrary")),
        name="hgrn2",
    )(p3, p3, p3, p3, lbp, out_gain.astype(F32).reshape(1, GROUP_WIDTH), _block_ones(GROUP_WIDTH, HEAD_DIM))


CMP_LEN = 32
CMP_STRIDE = 16
SEL_BLOCK = 64
SEL_TOPK = 16
WINDOW = 512
GATE_ROW = 8


def _rms_rows(x, gain):
    return x * lax.rsqrt(jnp.mean(x * x, axis=-1, keepdims=True) + EPS) * gain


NSA_V_ROWS = (GROUP_HEADS + 1) * V_AUG


def _aug_shared_values(v, w_wide, w_t):
    tm = v.shape[0]
    vw_t = (jnp.concatenate([v] * GROUP_HEADS, axis=1) * w_wide).T
    first = lax.broadcasted_iota(jnp.int32, (V_AUG - HEAD_DIM, tm), 0) == 0
    blocks = []
    for h in range(GROUP_HEADS):
        blocks += [vw_t[HEAD_DIM * h:HEAD_DIM * (h + 1), :],
                   jnp.where(first, jnp.broadcast_to(w_t[h:h + 1, :], first.shape), 0.0)]
    blocks += [v.T, jnp.where(first, 1.0, 0.0)]
    return jnp.concatenate(blocks, axis=0).astype(BF16)


def _nsa_prep_kernel(q_ref, kv_ref, g_ref, qg_ref, kg_ref, bd_ref, w_ref, wt_ref,
                     qt_ref, kc_ref, vc_ref, ks_ref, vst_ref, kw_ref, vwt_ref, gt_ref):
    ti = pl.program_id(1)
    q = q_ref[...]
    ss = _seg_sum(q * q, bd_ref[...])
    qn = q * lax.rsqrt(ss * (1.0 / HEAD_DIM) + EPS) * qg_ref[...] * (HEAD_DIM ** -0.5 * LOG2E)
    qt_ref[...] = qn.T.astype(BF16)
    kv = kv_ref[...]
    tm = kv.shape[0]
    kc_ref[...] = kv[:, 0:64]
    vc_ref[...] = kv[:, 64:128]
    ks = _rms_rows(kv[:, 128:192], kg_ref[1:2, :])
    lane = lax.broadcasted_iota(jnp.int32, (tm, LANES), 1)
    blk = jnp.right_shift(ti * tm + lax.broadcasted_iota(jnp.int32, (tm, LANES), 0), SEL_BLOCK.bit_length() - 1)
    onehot = jnp.where(lane == blk + HEAD_DIM, 1.0, 0.0)
    ks_ref[...] = jnp.where(lane < HEAD_DIM, jnp.concatenate([ks, ks], axis=1), onehot).astype(BF16)
    kw = _rms_rows(kv[:, 256:320], kg_ref[2:3, :])
    kw_ref[...] = jnp.concatenate([kw, jnp.zeros_like(kw)], axis=1).astype(BF16)
    vst_ref[...] = _aug_shared_values(kv[:, 192:256], w_ref[...], wt_ref[...])
    vwt_ref[...] = _aug_shared_values(kv[:, 320:384], w_ref[...], wt_ref[...])
    gt_ref[...] = _sigmoid(g_ref[...]).T


def _gelu_tanh(x):
    return 0.5 * x * (1.0 + jnp.tanh(math.sqrt(2.0 / math.pi) * (x + 0.044715 * x * x * x)))


def _nsa_cmp_kernel(kr_ref, vr_ref, w1_ref, w2_ref, pos_ref, kg_ref, kc_ref, vct_ref, sh_ref, *, n_rows):
    half = CMP_STRIDE * HEAD_DIM
    sh_ref[n_rows:n_rows + 8, :] = jnp.zeros((8, HEAD_DIM), F32)
    outs = []
    for j, x_ref in enumerate((kr_ref, vr_ref)):
        r = x_ref[...].astype(BF16)
        w1 = w1_ref[j]
        first = jnp.dot(r, w1[0:half, :], preferred_element_type=F32)
        sh_ref[0:n_rows, :] = jnp.dot(r, w1[half:2 * half, :], preferred_element_type=F32)
        pos8 = jnp.broadcast_to(pos_ref[j], (8, 2 * half))
        posw = jnp.dot(pos8, w1.astype(F32), preferred_element_type=F32)[0:1, :]
        hdn = _gelu_tanh(first + sh_ref[1:n_rows + 1, :] + posw)
        outs.append(jnp.dot(hdn, w2_ref[j].astype(F32), preferred_element_type=F32))
    kc_ref[...] = _rms_rows(outs[0], kg_ref[0:1, :]).astype(BF16)
    vct_ref[...] = outs[1].T.astype(BF16)


def _nsa_attn_kernel(qt_ref, kc_ref, vct_ref, ks_ref, vst_ref, kw_ref, vwt_ref, gt_ref, ov_ref, db_ref, lo_ref,
                     og_ref, bd_ref, y_ref, qa_ref, acc_s, m_s, acc_w, m_w, ss_ref, sw_ref,
                     *, tq, slopes, n_cmp):
    qi = pl.program_id(1)
    n_rows = kc_ref.shape[0]
    n_sel = ov_ref.shape[0]
    qpos = qi * tq + lax.broadcasted_iota(jnp.int32, (1, tq), 1)

    nio = lax.broadcasted_iota(jnp.int32, (n_rows, tq), 0)
    dist_c = qpos - (nio * CMP_STRIDE + (CMP_LEN - 1))
    valid_c = (dist_c >= 0) & (nio < n_cmp)
    dist_cf = dist_c.astype(F32)
    kc = kc_ref[...]
    vct = vct_ref[...]
    o_cmp = []
    p_sum = jnp.zeros((n_rows, tq), F32)
    for h in range(GROUP_HEADS):
        qh = qt_ref[HEAD_DIM * h:HEAD_DIM * (h + 1), :]
        s = jnp.dot(kc, qh, preferred_element_type=F32) - (slopes[h] * LOG2E) * dist_cf
        s = jnp.where(valid_c, s, MASKED)
        m = jnp.maximum(jnp.max(s, axis=0, keepdims=True), M_INIT)
        e = jnp.exp2(s - m)
        p = e / jnp.maximum(jnp.sum(e, axis=0, keepdims=True), 1e-30)
        p_sum = p_sum + p
        o_cmp.append(jnp.dot(vct, p.astype(BF16), preferred_element_type=F32))

    p_hi = p_sum.astype(BF16)
    p_lo = (p_sum - p_hi.astype(F32)).astype(BF16)
    ov = ov_ref[...]
    imp = jnp.dot(ov, p_hi, preferred_element_type=F32) + jnp.dot(ov, p_lo, preferred_element_type=F32)
    jio = lax.broadcasted_iota(jnp.int32, (n_sel, tq), 0)
    qblk = jnp.right_shift(qpos, SEL_BLOCK.bit_length() - 1)
    imp = jnp.where(jio <= qblk, imp, -1.0)
    imp = jnp.where((jio == 0) | (jio == qblk) | (jio == qblk - 1), 2.0, imp)
    rank = jnp.zeros((n_sel, tq), F32)
    for i in range(n_sel):
        row = imp[i:i + 1, :]
        beats = (row > imp) | ((row == imp) & (jio > i))
        rank = rank + jnp.where(beats, 1.0, 0.0)
    bsel = jnp.where(rank < float(min(SEL_TOPK, n_sel)), 0.0, MASKED)
    if n_sel < LANES - HEAD_DIM:
        bsel = jnp.concatenate([bsel, jnp.zeros((LANES - HEAD_DIM - n_sel, tq), F32)], axis=0)
    for h in range(GROUP_HEADS):
        qa_ref[h, 0:HEAD_DIM, :] = qt_ref[HEAD_DIM * h:HEAD_DIM * (h + 1), :]
        qa_ref[h, HEAD_DIM:LANES, :] = bsel.astype(BF16)
    for ref, val in ((acc_s, 0.0), (acc_w, 0.0), (m_s, M_INIT), (m_w, M_INIT)):
        ref[...] = jnp.full(ref.shape, val, F32)

    def tile_off(kt, h):
        return ((kt - qi) * tq + (tq - 1)).astype(F32) * (slopes[h] * LOG2E)

    def values(vt_ref, kt, h, diag):
        base = GROUP_HEADS * V_AUG if diag else V_AUG * h
        return vt_ref[base:base + V_AUG, pl.ds(pl.multiple_of(kt * tq, tq), tq)]

    def sel_scores(kt, h):
        kb = ks_ref[pl.ds(pl.multiple_of(kt * tq, tq), tq), :]
        ss_ref[h] = jnp.dot(kb, qa_ref[h], preferred_element_type=F32)

    def sel_step(kt, diag):
        def prefetch(h):
            if h + AHEAD < GROUP_HEADS:
                sel_scores(kt, h + AHEAD)
            elif not diag:
                sel_scores(kt + 1, h + AHEAD - GROUP_HEADS)

        _run_chains(GROUP_HEADS, prefetch, lambda h: _flash_update(
            ss_ref.at[h], db_ref.at[h] if diag else None, 0.0 if diag else tile_off(kt, h),
            values(vst_ref, kt, h, diag), m_s.at[h], acc_s.at[h]))

    for h in range(AHEAD):
        sel_scores(0, h)

    def body(kt, carry):
        sel_step(kt, False)
        return carry

    lax.fori_loop(0, qi, body, 0)
    sel_step(qi, True)

    def win_step(back, bias_of, diag):
        kt = jnp.maximum(qi - back, 0)
        penalty = None if diag else jnp.where(qi >= back, 0.0, MASKED)
        kb = kw_ref[pl.ds(pl.multiple_of(kt * tq, tq), tq), :]

        def win_scores(h):
            sw_ref[h] = jnp.dot(kb, qa_ref[h], preferred_element_type=F32)

        for h in range(AHEAD):
            win_scores(h)

        def prefetch(h):
            if h + AHEAD < GROUP_HEADS:
                win_scores(h + AHEAD)

        _run_chains(GROUP_HEADS, prefetch, lambda h: _flash_update(
            sw_ref.at[h], bias_of(h), 0.0 if diag else tile_off(kt, h),
            values(vwt_ref, kt, h, diag), m_w.at[h], acc_w.at[h], penalty))

    win_step(2, lambda h: lo_ref, False)
    win_step(1, lambda h: None, False)
    win_step(0, lambda h: db_ref.at[h], True)

    gt = gt_ref[...]
    heads = []
    for h in range(GROUP_HEADS):
        a_s, a_w = acc_s[h], acc_w[h]
        o_sel = a_s[0:HEAD_DIM, :] / a_s[HEAD_DIM:HEAD_DIM + 1, :]
        o_win = a_w[0:HEAD_DIM, :] / a_w[HEAD_DIM:HEAD_DIM + 1, :]
        g = GATE_ROW + 3 * h
        heads.append(gt[g:g + 1, :] * o_cmp[h] + gt[g + 1:g + 2, :] * o_sel + gt[g + 2:g + 3, :] * o_win)
    o = jnp.concatenate(heads, axis=0).T
    ss = _seg_sum(o * o, bd_ref[...])
    y_ref[...] = o * lax.rsqrt(ss * (1.0 / HEAD_DIM) + EPS) * og_ref[...]


def _nsa(p3, cmp_pos, cmp_w1, cmp_w2, q_gain, k_gain, out_gain, slopes, tq=256):
    b, s, _ = p3.shape
    tm = tq
    n_rows = s // CMP_STRIDE
    n_cmp = (s - CMP_LEN) // CMP_STRIDE + 1
    n_sel = s // SEL_BLOCK
    bd64 = _block_ones(GROUP_WIDTH, HEAD_DIM)
    qg = jnp.tile(q_gain.astype(F32), GROUP_HEADS).reshape(1, GROUP_WIDTH)
    kg = k_gain.astype(F32)
    tok = lambda w: pl.BlockSpec((None, tm, w), lambda bi, i: (bi, i, 0))
    tok_t = lambda w: pl.BlockSpec((None, w, tm), lambda bi, i: (bi, 0, i))
    full = lambda r, c: pl.BlockSpec((r, c), lambda bi, i: (0, 0))
    assert n_sel <= LANES - HEAD_DIM, "block one-hot lanes hold at most 64 selection blocks"
    assert WINDOW == 2 * tq and tq % SEL_BLOCK == 0, "window branch walks exactly the key tiles qi-2, qi-1, qi"
    w_wide, w_t = _key_weights(tm, slopes)
    qt, kc, vc, ks, vst, kw, vwt, gt = pl.pallas_call(
        _nsa_prep_kernel,
        grid=(b, s // tm),
        in_specs=[pl.BlockSpec((None, tm, GROUP_WIDTH), lambda bi, i: (bi, i, COL_D // GROUP_WIDTH)),
                  pl.BlockSpec((None, tm, 384), lambda bi, i: (bi, i, (COL_D + GROUP_WIDTH) // 384)),
                  pl.BlockSpec((None, tm, LANES), lambda bi, i: (bi, i, COL_G // LANES)),
                  full(1, GROUP_WIDTH), full(3, HEAD_DIM), full(GROUP_WIDTH, GROUP_WIDTH),
                  full(tm, GROUP_WIDTH), full(8, tm)],
        out_specs=[tok_t(GROUP_WIDTH), tok(HEAD_DIM), tok(HEAD_DIM), tok(LANES), tok_t(NSA_V_ROWS),
                   tok(LANES), tok_t(NSA_V_ROWS), tok_t(LANES)],
        out_shape=[jax.ShapeDtypeStruct((b, GROUP_WIDTH, s), BF16),
                   jax.ShapeDtypeStruct((b, s, HEAD_DIM), F32),
                   jax.ShapeDtypeStruct((b, s, HEAD_DIM), F32),
                   jax.ShapeDtypeStruct((b, s, LANES), BF16),
                   jax.ShapeDtypeStruct((b, NSA_V_ROWS, s), BF16),
                   jax.ShapeDtypeStruct((b, s, LANES), BF16),
                   jax.ShapeDtypeStruct((b, NSA_V_ROWS, s), BF16),
                   jax.ShapeDtypeStruct((b, LANES, s), F32)],
        compiler_params=_cparams(("arbitrary", "arbitrary")),
        name="nsa_prep",
    )(p3, p3, p3, qg, kg, bd64, w_wide, w_t)

    row_w = CMP_STRIDE * HEAD_DIM
    kcmp, vcmp_t = pl.pallas_call(
        functools.partial(_nsa_cmp_kernel, n_rows=n_rows),
        grid=(b,),
        in_specs=[pl.BlockSpec((None, n_rows, row_w), lambda bi: (bi, 0, 0)),
                  pl.BlockSpec((None, n_rows, row_w), lambda bi: (bi, 0, 0)),
                  pl.BlockSpec((2, 2 * row_w, HEAD_DIM), lambda bi: (0, 0, 0)),
                  pl.BlockSpec((2, HEAD_DIM, HEAD_DIM), lambda bi: (0, 0, 0)),
                  pl.BlockSpec((2, 1, 2 * row_w), lambda bi: (0, 0, 0)),
                  pl.BlockSpec((3, HEAD_DIM), lambda bi: (0, 0))],
        out_specs=[pl.BlockSpec((None, n_rows, HEAD_DIM), lambda bi: (bi, 0, 0)),
                   pl.BlockSpec((None, HEAD_DIM, n_rows), lambda bi: (bi, 0, 0))],
        out_shape=[jax.ShapeDtypeStruct((b, n_rows, HEAD_DIM), BF16),
                   jax.ShapeDtypeStruct((b, HEAD_DIM, n_rows), BF16)],
        scratch_shapes=[pltpu.VMEM((n_rows + 8, HEAD_DIM), F32)],
        compiler_params=_cparams(("arbitrary",)),
        name="nsa_cmp",
    )(kc.reshape(b, n_rows, row_w), vc.reshape(b, n_rows, row_w), cmp_w1.astype(BF16), cmp_w2.astype(BF16),
      cmp_pos.astype(F32).reshape(2, 1, 2 * row_w), kg)

    cs = np.arange(n_rows)[:, None] * CMP_STRIDE
    ss = np.arange(n_sel)[None, :] * SEL_BLOCK
    overlap = np.clip(np.minimum(cs + CMP_LEN, ss + SEL_BLOCK) - np.maximum(cs, ss), 0, None) / CMP_LEN
    overlap[n_cmp:, :] = 0.0
    ov_t = jnp.asarray(overlap.T.astype(np.float32), BF16)

    seq = lambda r, c: pl.BlockSpec((None, r, c), lambda bi, i: (bi, 0, 0))
    kern = functools.partial(_nsa_attn_kernel, tq=tq, slopes=tuple(float(v) for v in slopes), n_cmp=n_cmp)
    return pl.pallas_call(
        kern,
        grid=(b, s // tq),
        in_specs=[pl.BlockSpec((None, GROUP_WIDTH, tq), lambda bi, i: (bi, 0, i)),
                  seq(n_rows, HEAD_DIM), seq(HEAD_DIM, n_rows),
                  seq(s, LANES), seq(NSA_V_ROWS, s), seq(s, LANES), seq(NSA_V_ROWS, s),
                  pl.BlockSpec((None, LANES, tq), lambda bi, i: (bi, 0, i)),
                  full(n_sel, n_rows),
                  pl.BlockSpec((GROUP_HEADS, tq, tq), lambda bi, i: (0, 0, 0)), full(tq, tq),
                  full(1, GROUP_WIDTH), full(GROUP_WIDTH, GROUP_WIDTH)],
        out_specs=pl.BlockSpec((None, tq, GROUP_WIDTH), lambda bi, i: (bi, i, 0)),
        out_shape=jax.ShapeDtypeStruct((b, s, GROUP_WIDTH), F32),
        scratch_shapes=[pltpu.VMEM((GROUP_HEADS, LANES, tq), BF16),
                        pltpu.VMEM((GROUP_HEADS, V_AUG, tq), F32), pltpu.VMEM((GROUP_HEADS, 8, tq), F32),
                        pltpu.VMEM((GROUP_HEADS, V_AUG, tq), F32), pltpu.VMEM((GROUP_HEADS, 8, tq), F32),
                        pltpu.VMEM((GROUP_HEADS, tq, tq), F32), pltpu.VMEM((GROUP_HEADS, tq, tq), F32)],
        compiler_params=_cparams(("arbitrary", "arbitrary")),
        name="nsa_attn",
    )(qt, kcmp, vcmp_t, ks, vst, kw, vwt, gt, ov_t, _diag_bias(tq, slopes), _window_low_bias(tq),
      out_gain.astype(F32).reshape(1, GROUP_WIDTH), bd64)


def _window_low_bias(tq):
    k = np.arange(tq)[:, None]
    q = np.arange(tq)[None, :]
    return jnp.asarray(np.where(k > q, 0.0, MASKED).astype(np.float32))


N_GROUPS = 4
EXPERTS_PER_GROUP = 8
N_EXPERTS = N_GROUPS * EXPERTS_PER_GROUP
D_EXPERT = 256
ROUTER_LANE0 = N_GROUPS


def _split3_dot(a, b_hi, b_lo):
    a_hi = a.astype(BF16)
    a_lo = (a - a_hi.astype(F32)).astype(BF16)
    return (jnp.dot(a_hi, b_hi, preferred_element_type=F32) + jnp.dot(a_lo, b_hi, preferred_element_type=F32)
            + jnp.dot(a_hi, b_lo, preferred_element_type=F32))


def _route(t, wr_hi, wr_lo, br):
    logits = _split3_dot(t, wr_hi, wr_lo) + br
    lane = lax.broadcasted_iota(jnp.int32, logits.shape, 1)
    lane_f = lane.astype(F32)
    big = float(LANES)
    is_g = lane < N_GROUPS
    gl = jnp.where(is_g, logits, MASKED)
    gmax = jnp.max(gl, axis=-1, keepdims=True)
    g_prob = 1.0 / jnp.sum(jnp.where(is_g, jnp.exp(gl - gmax), 0.0), axis=-1, keepdims=True)
    g_sel = jnp.min(jnp.where(is_g & (gl == gmax), lane_f, big), axis=-1, keepdims=True)
    lo = ROUTER_LANE0 + EXPERTS_PER_GROUP * g_sel
    in_grp = (lane_f >= lo) & (lane_f < lo + EXPERTS_PER_GROUP)
    el = jnp.where(in_grp, logits, MASKED)
    m1 = jnp.max(el, axis=-1, keepdims=True)
    i1 = jnp.min(jnp.where(in_grp & (el == m1), lane_f, big), axis=-1, keepdims=True)
    rest = in_grp & (lane_f != i1)
    el2 = jnp.where(rest, logits, MASKED)
    m2 = jnp.max(el2, axis=-1, keepdims=True)
    i2 = jnp.min(jnp.where(rest & (el2 == m2), lane_f, big), axis=-1, keepdims=True)
    r = jnp.exp(m2 - m1)
    w1 = g_prob / (1.0 + r)
    w2 = g_prob * r / (1.0 + r)
    return jnp.where(lane_f == i1, w1, 0.0) + jnp.where(lane_f == i2, w2, 0.0), g_sel


T_AUG = D_MODEL + LANES


def _outproj_kernel(x_ref, ya_ref, yb_ref, yc_ref, yd_ref, w_ref, g_ref, wr_hi_ref, wr_lo_ref, br_ref,
                    xo_ref, tg_ref, route_ref, cnt_ref, run_ref):
    @pl.when(pl.program_id(0) == 0)
    def _():
        run_ref[...] = jnp.zeros_like(run_ref)

    acc = x_ref[...]
    for gi, y_ref in enumerate((ya_ref, yb_ref, yc_ref, yd_ref)):
        acc = acc + jnp.dot(y_ref[...].astype(BF16), w_ref[GROUP_WIDTH * gi:GROUP_WIDTH * (gi + 1), :],
                            preferred_element_type=F32)
    xo_ref[...] = acc
    ms = jnp.mean(acc * acc, axis=-1, keepdims=True)
    t = acc * lax.rsqrt(ms + EPS) * g_ref[...]
    tg_ref[:, 0:D_MODEL] = t
    gate, g_sel = _route(t, wr_hi_ref[...], wr_lo_ref[...], br_ref[...])
    tg_ref[:, D_MODEL:T_AUG] = gate

    tm = t.shape[0]
    lane = lax.broadcasted_iota(jnp.int32, (tm, LANES), 1)
    onehot = jnp.where(lane.astype(F32) == g_sel, 1.0, 0.0)
    ri = lax.broadcasted_iota(jnp.int32, (tm, tm), 0)
    ci = lax.broadcasted_iota(jnp.int32, (tm, tm), 1)
    before = jnp.where(ri > ci, 1.0, 0.0).astype(BF16)
    prefix = jnp.dot(before, onehot.astype(BF16), preferred_element_type=F32)
    rank = jnp.sum(onehot * (run_ref[...] + prefix), axis=-1, keepdims=True)
    route_ref[...] = jnp.where(lane == 0, g_sel, 0.0) + jnp.where(lane == 1, rank, 0.0)
    run_ref[...] = run_ref[...] + jnp.sum(onehot, axis=0, keepdims=True)
    cnt_ref[...] = run_ref[...]


def _outproj(x2d, ys, w_out, ffn_gain, w_group, b_group, w_expert, b_expert, tm=256):
    t = x2d.shape[0]
    wr = jnp.zeros((D_MODEL, LANES), F32).at[:, 0:N_GROUPS].set(w_group).at[:, ROUTER_LANE0:ROUTER_LANE0 + N_EXPERTS].set(w_expert)
    wr_hi = wr.astype(BF16)
    wr_lo = (wr - wr_hi.astype(F32)).astype(BF16)
    br = jnp.zeros((1, LANES), F32).at[0, 0:N_GROUPS].set(b_group).at[0, ROUTER_LANE0:ROUTER_LANE0 + N_EXPERTS].set(b_expert)
    row = lambda w: pl.BlockSpec((tm, w), lambda i: (i, 0))
    full = lambda r, c: pl.BlockSpec((r, c), lambda i: (0, 0))
    return list(pl.pallas_call(
        _outproj_kernel,
        grid=(t // tm,),
        in_specs=[row(D_MODEL), row(GROUP_WIDTH), row(GROUP_WIDTH), row(GROUP_WIDTH), row(GROUP_WIDTH),
                  full(D_MODEL, D_MODEL), full(1, D_MODEL), full(D_MODEL, LANES), full(D_MODEL, LANES), full(1, LANES)],
        out_specs=[row(D_MODEL), row(T_AUG), row(LANES), full(1, LANES)],
        out_shape=[jax.ShapeDtypeStruct((t, D_MODEL), F32),
                   jax.ShapeDtypeStruct((t, T_AUG), F32),
                   jax.ShapeDtypeStruct((t, LANES), F32),
                   jax.ShapeDtypeStruct((1, LANES), F32)],
        scratch_shapes=[pltpu.VMEM((1, LANES), F32)],
        compiler_params=_cparams(("arbitrary",)),
        name="outproj_router",
    )(x2d, *ys, w_out.astype(BF16), ffn_gain.reshape(1, D_MODEL), wr_hi, wr_lo, br))


MOE_TILE = 1024
DMA_PRIORITIES = 2


def _row_copies(n, make_copy):
    def start(k, carry):
        for p in range(DMA_PRIORITIES):
            make_copy(DMA_PRIORITIES * k + p).start(priority=p)
        return carry

    def wait(r, carry):
        make_copy(r).wait()
        return carry

    lax.fori_loop(0, n // DMA_PRIORITIES, start, 0)
    lax.fori_loop(0, n, wait, 0)


def _moe_expert_kernel(src_ref, grp_ref, valid_ref, end_ref, tg_ref, wg_ref, wu_ref, wd_ref, y_ref,
                       x_buf, sem, acc_ref):
    i, e = pl.program_id(0), pl.program_id(1)
    te = x_buf.shape[1]
    n_tiles = pl.num_programs(0)
    part = te // EXPERTS_PER_GROUP
    slot = i % 2

    def row_copy(tile, r, dst_slot):
        return pltpu.make_async_copy(tg_ref.at[pl.ds(src_ref[tile * te + r], 1)],
                                     x_buf.at[dst_slot, pl.ds(r, 1)], sem.at[dst_slot])

    @pl.when((i == 0) & (e == 0) & (valid_ref[0] == 1))
    def _():
        def start(r, carry):
            row_copy(0, r, 0).start()
            return carry
        lax.fori_loop(0, te, start, 0)

    @pl.when(e == 0)
    def _():
        acc_ref[...] = jnp.zeros_like(acc_ref)

    def wait_slot(s):
        pltpu.make_async_copy(x_buf.at[s], x_buf.at[s], sem.at[s]).wait()

    @pl.when((e == 0) & (valid_ref[jnp.maximum(i - 1, 0)] == 1))
    def _():
        wait_slot(slot)

    @pl.when(valid_ref[i] == 1)
    def _():
        nxt = jnp.minimum(i + 1, n_tiles - 1)
        for r in range(part):
            row_copy(nxt, e * part + r, 1 - slot).start()
        x = x_buf[slot, :, 0:D_MODEL].astype(BF16)
        gate = x_buf[slot, :, D_MODEL:T_AUG]
        lane = lax.broadcasted_iota(jnp.int32, gate.shape, 1)
        col = ROUTER_LANE0 + grp_ref[i] * EXPERTS_PER_GROUP + e
        w = jnp.sum(jnp.where(lane == col, gate, 0.0), axis=-1, keepdims=True)
        row = i * te + lax.broadcasted_iota(jnp.int32, w.shape, 0)
        w = jnp.where(row < end_ref[i], w, 0.0)
        a = jnp.dot(x, wg_ref[...].astype(BF16), preferred_element_type=F32)
        u = jnp.dot(x, wu_ref[...].astype(BF16), preferred_element_type=F32)
        act = a * _sigmoid(a) * u * w
        acc_ref[...] += jnp.dot(act.astype(BF16), wd_ref[...].astype(BF16), preferred_element_type=F32)

    @pl.when(e == EXPERTS_PER_GROUP - 1)
    def _():
        y_ref[...] = acc_ref[...]

    @pl.when((i == n_tiles - 1) & (e == EXPERTS_PER_GROUP - 1) & (valid_ref[i] == 1))
    def _():
        wait_slot(1 - slot)


def _moe_combine_kernel(pos_ref, x_ref, ys_ref, o_ref, buf_ref, sem):
    tm = x_ref.shape[0]
    base = pl.program_id(0) * tm
    _row_copies(tm, lambda r: pltpu.make_async_copy(ys_ref.at[pl.ds(pos_ref[base + r], 1)],
                                                    buf_ref.at[pl.ds(r, 1)], sem))
    o_ref[...] = x_ref[...] + buf_ref[...]


def _moe_routed(x2d, tg, route, counts, w_gate, w_up, w_down, tm=512):
    t = x2d.shape[0]
    te = MOE_TILE
    n_tiles = t // te + N_GROUPS
    n_rows = n_tiles * te
    grp = route[:, 0].astype(jnp.int32)
    rank = route[:, 1].astype(jnp.int32)
    cnt = counts[0, 0:N_GROUPS].astype(jnp.int32)
    padded = ((cnt + te - 1) // te) * te
    ends = jnp.cumsum(padded)
    pos = (ends - padded)[grp] + rank
    starts = jnp.arange(n_tiles, dtype=jnp.int32) * te
    tile_grp = jnp.minimum(jnp.sum((starts[:, None] >= ends[None, :]).astype(jnp.int32), axis=1), N_GROUPS - 1)
    tile_valid = (starts < ends[-1]).astype(jnp.int32)
    tile_end = ((ends - padded) + cnt)[tile_grp]
    src = jnp.zeros((n_rows,), jnp.int32).at[pos].set(jnp.arange(t, dtype=jnp.int32))

    wg = w_gate.reshape(N_EXPERTS, D_MODEL, D_EXPERT)
    wu = w_up.reshape(N_EXPERTS, D_MODEL, D_EXPERT)
    wd = w_down.reshape(N_EXPERTS, D_EXPERT, D_MODEL)
    wsel = lambda i, e, s, g, v, n: (g[i] * EXPERTS_PER_GROUP + e, 0, 0)
    ys = pl.pallas_call(
        _moe_expert_kernel,
        grid_spec=pltpu.PrefetchScalarGridSpec(
            num_scalar_prefetch=4, grid=(n_tiles, EXPERTS_PER_GROUP),
            in_specs=[pl.BlockSpec(memory_space=pl.ANY),
                      pl.BlockSpec((None, D_MODEL, D_EXPERT), wsel),
                      pl.BlockSpec((None, D_MODEL, D_EXPERT), wsel),
                      pl.BlockSpec((None, D_EXPERT, D_MODEL), wsel)],
            out_specs=pl.BlockSpec((te, D_MODEL), lambda i, e, s, g, v, n: (i, 0)),
            scratch_shapes=[pltpu.VMEM((2, te, T_AUG), F32), pltpu.SemaphoreType.DMA((2,)),
                            pltpu.VMEM((te, D_MODEL), F32)]),
        out_shape=jax.ShapeDtypeStruct((n_rows, D_MODEL), F32),
        compiler_params=_cparams(("arbitrary", "arbitrary")),
        name="moe_experts",
    )(src, tile_grp, tile_valid, tile_end, tg, wg, wu, wd)

    return pl.pallas_call(
        _moe_combine_kernel,
        grid_spec=pltpu.PrefetchScalarGridSpec(
            num_scalar_prefetch=1, grid=(t // tm,),
            in_specs=[pl.BlockSpec((tm, D_MODEL), lambda i, pos: (i, 0)), pl.BlockSpec(memory_space=pl.ANY)],
            out_specs=pl.BlockSpec((tm, D_MODEL), lambda i, pos: (i, 0)),
            scratch_shapes=[pltpu.VMEM((tm, D_MODEL), F32), pltpu.SemaphoreType.DMA(())]),
        out_shape=jax.ShapeDtypeStruct((t, D_MODEL), F32),
        compiler_params=_cparams(("arbitrary",)),
        name="moe_combine",
    )(pos, x2d, ys)


def _alibi_slopes():
    n = 2 * GROUP_HEADS
    s = 2.0 ** (-8.0 * np.arange(1, n + 1) / n)
    return s[0::2], s[1::2]


def kernel(x, norm_mix, norm_ffn, w_in, w_out, diff_q_gain, diff_k_gain, diff_lambda, diff_sub_gain, mlstm_conv, mlstm_gate_bias, mlstm_out_gain, hgrn_lower_bounds, hgrn_out_gain, nsa_cmp_pos, nsa_cmp_w1, nsa_cmp_w2, nsa_q_gain, nsa_k_gain, nsa_out_gain, moe_w_group, moe_b_group, moe_w_expert, moe_b_expert, moe_w_gate, moe_w_up, moe_w_down):
    b, s, d = x.shape
    slopes_diff, slopes_nsa = _alibi_slopes()
    lb_soft = jax.nn.softmax(hgrn_lower_bounds.astype(F32), axis=0)
    lower_bounds = jnp.cumsum(lb_soft, axis=0) - lb_soft[0]
    x2d = x.reshape(b * s, d)
    for l in range(norm_mix.shape[0]):
        p = _inproj(x2d, norm_mix[l], _pack_w_in(w_in[l])).reshape(b, s, P_COLS)
        y_a = _diff_attention(p, diff_q_gain[l], diff_k_gain[l], diff_lambda[l], diff_sub_gain[l], slopes_diff, l)
        y_b = _mlstm(p, mlstm_conv[l], mlstm_gate_bias[l], mlstm_out_gain[l])
        y_c = _hgrn2(p, lower_bounds[l], hgrn_out_gain[l])
        y_d = _nsa(p, nsa_cmp_pos[l], nsa_cmp_w1[l], nsa_cmp_w2[l], nsa_q_gain[l], nsa_k_gain[l],
                   nsa_out_gain[l], slopes_nsa)
        ys = [y.reshape(b * s, GROUP_WIDTH) for y in (y_a, y_b, y_c, y_d)]
        x2d, tg, route, counts = _outproj(x2d, ys, w_out[l], norm_ffn[l], moe_w_group[l], moe_b_group[l],
                                          moe_w_expert[l], moe_b_expert[l])
        x2d = _moe_routed(x2d, tg, route, counts, moe_w_gate[l], moe_w_up[l], moe_w_down[l])
    return x2d.reshape(b, s, d)
```

```python
import functools
import math

import numpy as np
import jax
import jax.numpy as jnp
from jax import lax
from jax.experimental import pallas as pl
from jax.experimental.pallas import tpu as pltpu

F32 = jnp.float32
BF16 = jnp.bfloat16

D_MODEL = 1024
HEAD_DIM = 64
GROUP_HEADS = 4
GROUP_WIDTH = GROUP_HEADS * HEAD_DIM
DIFF_HALF = HEAD_DIM // 2
EPS = 1e-6
NEG_INF = -1e30
LOG2E = math.log2(math.e)
M_INIT = -1e30
MASKED = -2e30

LANES = 128
VMEM_LIMIT = 48 * 1024 * 1024

COL_A = 0
COL_B = 768
COL_C = 1792
COL_D = 2816
COL_G = 3456
P_COLS = 3584


def _cparams(sem, flags=None):
    return pltpu.CompilerParams(dimension_semantics=sem, vmem_limit_bytes=VMEM_LIMIT, flags=flags)


def _block_ones(width, seg):
    i = np.arange(width)
    return jnp.asarray((i[:, None] // seg == i[None, :] // seg).astype(np.float32), BF16)


def _seg_sum(x, bd):
    hi = x.astype(BF16)
    lo = (x - hi.astype(F32)).astype(BF16)
    return (jnp.dot(hi, bd, preferred_element_type=F32) + jnp.dot(lo, bd, preferred_element_type=F32))


def _inproj_kernel(x_ref, g_ref, w_ref, o_ref):
    x = x_ref[...]
    ms = jnp.mean(x * x, axis=-1, keepdims=True)
    h = (x * lax.rsqrt(ms + EPS) * g_ref[...]).astype(BF16)
    o_ref[...] = jnp.dot(h, w_ref[...], preferred_element_type=F32)


def _inproj(x2d, gain, w_cat, tm=256):
    t = x2d.shape[0]
    return pl.pallas_call(
        _inproj_kernel,
        grid=(t // tm,),
        in_specs=[pl.BlockSpec((tm, D_MODEL), lambda i: (i, 0)),
                  pl.BlockSpec((1, D_MODEL), lambda i: (0, 0)),
                  pl.BlockSpec((D_MODEL, P_COLS), lambda i: (0, 0))],
        out_specs=pl.BlockSpec((tm, P_COLS), lambda i: (i, 0)),
        out_shape=jax.ShapeDtypeStruct((t, P_COLS), F32),
        compiler_params=_cparams(("arbitrary",)),
        name="inproj",
    )(x2d, gain.reshape(1, D_MODEL), w_cat)


IN_COLS = 3476


def _pack_w_in_kernel(w_ref, o_ref):
    w = w_ref[...]
    a_b = w[:, 0:1536]
    gates_b = w[:, 1536:1544]
    rest = w[:, 1544:3464]
    gates_d = w[:, 3464:3476]
    pad = jnp.zeros((w.shape[0], P_COLS - COL_G - 20), F32)
    o_ref[...] = jnp.concatenate([a_b, rest, gates_b, gates_d, pad], axis=1).astype(BF16)


def _pack_w_in(w, tr=128):
    d = w.shape[0]
    return pl.pallas_call(
        _pack_w_in_kernel,
        grid=(d // tr,),
        in_specs=[pl.BlockSpec((tr, IN_COLS), lambda i: (i, 0))],
        out_specs=pl.BlockSpec((tr, P_COLS), lambda i: (i, 0)),
        out_shape=jax.ShapeDtypeStruct((d, P_COLS), BF16),
        compiler_params=_cparams(("arbitrary",)),
        name="pack_w_in",
    )(w)


def _diff_prep_kernel(p_ref, qg_ref, kg_ref, bd_ref, w_ref, wt_ref, q_ref, k_ref, vt_ref):
    p = p_ref[...]
    bd = bd_ref[...]

    def norm(x, g):
        ss = _seg_sum(x * x, bd)
        return x * lax.rsqrt(ss * (1.0 / DIFF_HALF) + EPS) * g

    q = norm(p[:, 0:256], qg_ref[...]) * (DIFF_HALF ** -0.5 * LOG2E)
    k = norm(p[:, 256:512], kg_ref[...])
    q_ref[...] = q.T.astype(BF16)
    k_ref[...] = k.astype(BF16)
    v = p[:, 512:768]
    v_t = v.T
    vw_t = (v * w_ref[...]).T
    w_t = wt_ref[...]
    tm = v_t.shape[1]
    first = lax.broadcasted_iota(jnp.int32, (V_AUG - HEAD_DIM, tm), 0) == 0
    blocks = []
    for h in range(GROUP_HEADS):
        hs = slice(HEAD_DIM * h, HEAD_DIM * (h + 1))
        blocks += [vw_t[hs, :], jnp.where(first, jnp.broadcast_to(w_t[h:h + 1, :], first.shape), 0.0),
                   v_t[hs, :], jnp.where(first, 1.0, 0.0)]
    vt_ref[...] = jnp.concatenate(blocks, axis=0).astype(BF16)


V_AUG = 80
V_HEAD = 2 * V_AUG


def _key_weights(tile, slopes):
    kl = np.arange(tile, dtype=np.float64) - (tile - 1)
    w = np.stack([np.exp2(sl * LOG2E * kl) for sl in slopes])
    wide = np.repeat(w.T, HEAD_DIM, axis=1)
    w8 = np.zeros((8, tile)); w8[:len(slopes)] = w
    return jnp.asarray(wide.astype(np.float32)), jnp.asarray(w8.astype(np.float32))


def _diag_bias(tq, slopes):
    k = np.arange(tq)[:, None]
    q = np.arange(tq)[None, :]
    tabs = [np.where(k <= q, sl * LOG2E * k.astype(np.float64), MASKED) for sl in slopes]
    return jnp.asarray(np.stack(tabs).astype(np.float32))


AHEAD = 2


def _flash_update(s_ref, bias_ref, off, vt, m_ref, acc_ref, penalty=None):
    tq = s_ref.shape[1]
    ps, olds, news = [], [], []
    for c in range(tq // LANES):
        cols = slice(LANES * c, LANES * (c + 1))
        s = s_ref[:, cols]
        if bias_ref is not None:
            s = s + bias_ref[:, cols]
        if penalty is not None:
            s = s + penalty
        m_tile = jnp.max(s, axis=0, keepdims=True)
        ps.append(jnp.exp2(s - m_tile).astype(BF16))
        m_old = m_ref[0:1, cols]
        m_new = jnp.maximum(m_old, m_tile + off)
        m_ref[0:1, cols] = m_new
        olds.append(jnp.exp2(m_old - m_new))
        news.append(jnp.exp2(m_tile + off - m_new))
    p, old, new = jnp.concatenate(ps, axis=1), jnp.concatenate(olds, axis=1), jnp.concatenate(news, axis=1)

    def finish():
        pv = jnp.dot(vt, p, preferred_element_type=F32)
        acc_ref[...] = old * acc_ref[...] + new * pv

    return finish


def _run_chains(n, prefetch, softmax):
    finish = None
    for c in range(n):
        prefetch(c)
        nxt = softmax(c)
        if finish is not None:
            finish()
        finish = nxt
    finish()


def _diff_attn_kernel(lam_ref, qt_ref, k_ref, vt_ref, db_ref, sg_ref, bd_ref, o_ref,
                      qm_ref, acc_ref, m_ref, s_ref, *, tq, slopes, out_scale):
    qi = pl.program_id(1)
    n_half = 2 * GROUP_HEADS
    per_tile = LANES // DIFF_HALF
    row = lax.broadcasted_iota(jnp.int32, (LANES, tq), 0)
    for j in range(n_half):
        slab = qt_ref[LANES * (j // per_tile):LANES * (j // per_tile + 1), :].astype(F32)
        r0 = DIFF_HALF * (j % per_tile)
        qm_ref[j] = jnp.where((row >= r0) & (row < r0 + DIFF_HALF), slab, 0.0).astype(BF16)
    acc_ref[...] = jnp.zeros(acc_ref.shape, F32)
    m_ref[...] = jnp.full(m_ref.shape, M_INIT, F32)

    n_slots = s_ref.shape[0]

    def scores(kt, j):
        start = pl.multiple_of(kt * tq, tq)
        kj = k_ref[pl.ds(start, tq), LANES * (j // per_tile):LANES * (j // per_tile + 1)]
        s_ref[j % n_slots] = jnp.dot(kj, qm_ref[j], preferred_element_type=F32)

    def process(kt, j, diag):
        h = j // 2
        start = pl.multiple_of(kt * tq, tq)
        off = 0.0 if diag else ((kt - qi) * tq + (tq - 1)).astype(F32) * (slopes[h] * LOG2E)
        base = V_HEAD * h + (V_AUG if diag else 0)
        vt = vt_ref[base:base + V_AUG, pl.ds(start, tq)]
        return _flash_update(s_ref.at[j % n_slots], db_ref.at[h] if diag else None, off, vt, m_ref.at[j],
                             acc_ref.at[j])

    def step(kt, diag):
        def prefetch(j):
            if j + AHEAD < n_half:
                scores(kt, j + AHEAD)
            elif not diag:
                scores(kt + 1, j + AHEAD - n_half)

        _run_chains(n_half, prefetch, lambda j: process(kt, j, diag))

    for j in range(AHEAD):
        scores(0, j)

    def body(kt, carry):
        step(kt, False)
        return carry

    lax.fori_loop(0, qi, body, 0)
    step(qi, True)

    lam = lam_ref[0, 0]
    heads = []
    for h in range(GROUP_HEADS):
        a0, a1 = acc_ref[2 * h], acc_ref[2 * h + 1]
        o0 = a0[0:HEAD_DIM, :] / a0[HEAD_DIM:HEAD_DIM + 1, :]
        o1 = a1[0:HEAD_DIM, :] / a1[HEAD_DIM:HEAD_DIM + 1, :]
        heads.append(o0 - lam * o1)
    o = jnp.concatenate(heads, axis=0).T
    ss = _seg_sum(o * o, bd_ref[...])
    o_ref[...] = o * lax.rsqrt(ss * (1.0 / HEAD_DIM) + EPS) * (sg_ref[...] * out_scale)


def _diff_attention(p3, q_gain, k_gain, lam_vecs, sub_gain, slopes, layer_idx, tq=256):
    b, s, _ = p3.shape
    tm = tq
    bd32 = _block_ones(GROUP_WIDTH, DIFF_HALF)
    bd64 = _block_ones(GROUP_WIDTH, HEAD_DIM)
    qg = jnp.tile(q_gain.astype(F32), 2 * GROUP_HEADS).reshape(1, GROUP_WIDTH)
    kg = jnp.tile(k_gain.astype(F32), 2 * GROUP_HEADS).reshape(1, GROUP_WIDTH)
    w_wide, w_t = _key_weights(tm, slopes)
    v_rows = GROUP_HEADS * V_HEAD
    qn, kn, vt = pl.pallas_call(
        _diff_prep_kernel,
        grid=(b, s // tm),
        in_specs=[pl.BlockSpec((None, tm, 768), lambda bi, i: (bi, i, COL_A // 768)),
                  pl.BlockSpec((1, GROUP_WIDTH), lambda bi, i: (0, 0)),
                  pl.BlockSpec((1, GROUP_WIDTH), lambda bi, i: (0, 0)),
                  pl.BlockSpec((GROUP_WIDTH, GROUP_WIDTH), lambda bi, i: (0, 0)),
                  pl.BlockSpec((tm, GROUP_WIDTH), lambda bi, i: (0, 0)),
                  pl.BlockSpec((8, tm), lambda bi, i: (0, 0))],
        out_specs=[pl.BlockSpec((None, GROUP_WIDTH, tm), lambda bi, i: (bi, 0, i)),
                   pl.BlockSpec((None, tm, GROUP_WIDTH), lambda bi, i: (bi, i, 0)),
                   pl.BlockSpec((None, v_rows, tm), lambda bi, i: (bi, 0, i))],
        out_shape=[jax.ShapeDtypeStruct((b, GROUP_WIDTH, s), BF16),
                   jax.ShapeDtypeStruct((b, s, GROUP_WIDTH), BF16),
                   jax.ShapeDtypeStruct((b, v_rows, s), BF16)],
        compiler_params=_cparams(("arbitrary", "arbitrary")),
        name="diff_prep",
    )(p3, qg, kg, bd32, w_wide, w_t)

    lam_init = 0.8 - 0.6 * math.exp(-0.3 * layer_idx)
    lv = lam_vecs.astype(F32)
    lam = (jnp.exp(jnp.dot(lv[0], lv[1])) - jnp.exp(jnp.dot(lv[2], lv[3])) + lam_init).reshape(1, 1)
    kern = functools.partial(_diff_attn_kernel, tq=tq, slopes=tuple(float(v) for v in slopes),
                             out_scale=1.0 - lam_init)
    return pl.pallas_call(
        kern,
        grid=(b, s // tq),
        in_specs=[pl.BlockSpec(memory_space=pltpu.SMEM),
                  pl.BlockSpec((None, GROUP_WIDTH, tq), lambda bi, i: (bi, 0, i)),
                  pl.BlockSpec((None, s, GROUP_WIDTH), lambda bi, i: (bi, 0, 0)),
                  pl.BlockSpec((None, v_rows, s), lambda bi, i: (bi, 0, 0)),
                  pl.BlockSpec((GROUP_HEADS, tq, tq), lambda bi, i: (0, 0, 0)),
                  pl.BlockSpec((1, GROUP_WIDTH), lambda bi, i: (0, 0)),
                  pl.BlockSpec((GROUP_WIDTH, GROUP_WIDTH), lambda bi, i: (0, 0))],
        out_specs=pl.BlockSpec((None, tq, GROUP_WIDTH), lambda bi, i: (bi, i, 0)),
        out_shape=jax.ShapeDtypeStruct((b, s, GROUP_WIDTH), F32),
        scratch_shapes=[pltpu.VMEM((2 * GROUP_HEADS, LANES, tq), BF16),
                        pltpu.VMEM((2 * GROUP_HEADS, V_AUG, tq), F32),
                        pltpu.VMEM((2 * GROUP_HEADS, 8, tq), F32),
                        pltpu.VMEM((2 * AHEAD, tq, tq), F32)],
        compiler_params=_cparams(("arbitrary", "arbitrary")),
        name="diff_attn",
    )(lam, qn, kn, vt, _diag_bias(tq, slopes), sub_gain.astype(F32).reshape(1, GROUP_WIDTH), bd64)


MLSTM_CONV = 4
CONV_HALO = 8
AUG = 128


def _tril_sum(x, tril):
    hi = x.astype(BF16)
    lo = (x - hi.astype(F32)).astype(BF16)
    return jnp.dot(tril, hi, preferred_element_type=F32) + jnp.dot(tril, lo, preferred_element_type=F32)


def _log_sigmoid(x):
    return jnp.minimum(x, 0.0) - jnp.log1p(jnp.exp(-jnp.abs(x)))


def _sigmoid(x):
    return 1.0 / (1.0 + jnp.exp(-x))


def _mlstm_kernel(q_ref, k_ref, v_ref, o_ref, g_ref, cw_ref, gb_ref, og_ref, bd_ref, y_ref,
                  halo_ref, c_ref, m_ref, *, L):
    ci = pl.program_id(1)

    @pl.when(ci == 0)
    def _():
        halo_ref[0:CONV_HALO, :] = jnp.zeros((CONV_HALO, 2 * GROUP_WIDTH), F32)
        c_ref[...] = jnp.zeros_like(c_ref)
        m_ref[...] = jnp.zeros_like(m_ref)

    halo_ref[CONV_HALO:CONV_HALO + L, 0:GROUP_WIDTH] = q_ref[...]
    halo_ref[CONV_HALO:CONV_HALO + L, GROUP_WIDTH:2 * GROUP_WIDTH] = k_ref[...]
    conv = jnp.zeros((L, 2 * GROUP_WIDTH), F32)
    for j in range(MLSTM_CONV):
        start = CONV_HALO - (MLSTM_CONV - 1) + j
        conv = conv + halo_ref[start:start + L, :] * cw_ref[j:j + 1, :]
    halo_ref[0:CONV_HALO, :] = halo_ref[L:L + CONV_HALO, :]
    qk = conv * _sigmoid(conv)
    q = qk[:, 0:GROUP_WIDTH] * (HEAD_DIM ** -0.5)
    k = qk[:, GROUP_WIDTH:2 * GROUP_WIDTH]
    v = v_ref[...]

    gates = g_ref[...] + gb_ref[...]
    ri = lax.broadcasted_iota(jnp.int32, (L, L), 0)
    cj = lax.broadcasted_iota(jnp.int32, (L, L), 1)
    causal = ri >= cj
    tril = jnp.where(causal, 1.0, 0.0).astype(BF16)
    gcum = _tril_sum(_log_sigmoid(gates), tril)
    gcum_t = gcum.T
    gates_t = gates.T
    lane_aug = lax.broadcasted_iota(jnp.int32, (L, AUG), 1)
    m_all = m_ref[...]

    outs = []
    for h in range(GROUP_HEADS):
        hs = slice(HEAD_DIM * h, HEAD_DIM * (h + 1))
        q_h, k_h = q[:, hs], k[:, hs]
        v_aug = jnp.where(lane_aug == HEAD_DIM, 1.0,
                          jnp.concatenate([v[:, hs], jnp.zeros((L, AUG - HEAD_DIM), F32)], axis=1))
        g_col = gcum[:, 4 + h:5 + h]
        li_col = gates[:, h:h + 1]
        g_row = gcum_t[4 + h:5 + h, :]
        li_row = gates_t[h:h + 1, :]
        g_last = g_row[:, L - 1:L]
        m_prev = m_all[:, h:h + 1]
        log_d = jnp.where(causal, g_col - g_row + li_row, NEG_INF)
        log_inter = g_col + m_prev
        m_t = jnp.maximum(log_inter, jnp.max(log_d, axis=1, keepdims=True))
        s_qk = lax.dot_general(q_h.astype(BF16), k_h.astype(BF16), (((1,), (1,)), ((), ())),
                               preferred_element_type=F32)
        w_intra = s_qk * jnp.exp(log_d - m_t)
        w_inter = jnp.exp(log_inter - m_t)
        c_aug = c_ref[h]
        num = (w_inter * jnp.dot(q_h.astype(BF16), c_aug.astype(BF16), preferred_element_type=F32)
               + jnp.dot(w_intra.astype(BF16), v_aug.astype(BF16), preferred_element_type=F32))
        den = num[:, HEAD_DIM:HEAD_DIM + 1]
        outs.append(num[:, 0:HEAD_DIM] / jnp.maximum(jnp.abs(den), jnp.exp(-m_t)))

        log_a = g_last - g_col + li_col
        m_new = jnp.maximum(g_last + m_prev, jnp.max(log_a, axis=0, keepdims=True))
        a_col = jnp.exp(log_a - m_new)
        decay = jnp.exp(g_last + m_prev - m_new)
        ak_t = (k_h * a_col).T.astype(BF16)
        c_ref[h] = decay * c_aug + jnp.dot(ak_t, v_aug.astype(BF16), preferred_element_type=F32)
        m_ref[:, h:h + 1] = m_new

    hcat = jnp.concatenate(outs, axis=1)
    ss = _seg_sum(hcat * hcat, bd_ref[...])
    y_ref[...] = hcat * lax.rsqrt(ss * (1.0 / HEAD_DIM) + EPS) * og_ref[...] * _sigmoid(o_ref[...])


def _mlstm(p3, conv_w, gate_bias, out_gain, L=256):
    b, s, _ = p3.shape
    gb = jnp.zeros((1, LANES), F32).at[0, 0:GROUP_HEADS].set(gate_bias[0]).at[0, GROUP_HEADS:2 * GROUP_HEADS].set(gate_bias[1])
    col = lambda off: (lambda bi, i: (bi, i, off // GROUP_WIDTH))
    return pl.pallas_call(
        functools.partial(_mlstm_kernel, L=L),
        grid=(b, s // L),
        in_specs=[pl.BlockSpec((None, L, GROUP_WIDTH), col(COL_B)),
                  pl.BlockSpec((None, L, GROUP_WIDTH), col(COL_B + 256)),
                  pl.BlockSpec((None, L, GROUP_WIDTH), col(COL_B + 512)),
                  pl.BlockSpec((None, L, GROUP_WIDTH), col(COL_B + 768)),
                  pl.BlockSpec((None, L, LANES), lambda bi, i: (bi, i, COL_G // LANES)),
                  pl.BlockSpec((MLSTM_CONV, 2 * GROUP_WIDTH), lambda bi, i: (0, 0)),
                  pl.BlockSpec((1, LANES), lambda bi, i: (0, 0)),
                  pl.BlockSpec((1, GROUP_WIDTH), lambda bi, i: (0, 0)),
                  pl.BlockSpec((GROUP_WIDTH, GROUP_WIDTH), lambda bi, i: (0, 0))],
        out_specs=pl.BlockSpec((None, L, GROUP_WIDTH), lambda bi, i: (bi, i, 0)),
        out_shape=jax.ShapeDtypeStruct((b, s, GROUP_WIDTH), F32),
        scratch_shapes=[pltpu.VMEM((CONV_HALO + L, 2 * GROUP_WIDTH), F32),
                        pltpu.VMEM((GROUP_HEADS, HEAD_DIM, AUG), F32),
                        pltpu.VMEM((1, LANES), F32)],
        compiler_params=_cparams(("arbitrary", "arbitrary")),
        name="mlstm",
    )(p3, p3, p3, p3, p3, conv_w.astype(F32), gb, out_gain.astype(F32).reshape(1, GROUP_WIDTH),
      _block_ones(GROUP_WIDTH, HEAD_DIM))


HGRN_CHUNK = 16
LB_FLOOR = 1e-30


def _hgrn_kernel(q_ref, f_ref, i_ref, g_ref, lb_ref, og_ref, bd_ref, y_ref,
                 lf_s, kk_s, vv_s, st_ref, *, TL):
    ci = pl.program_id(1)
    C = HGRN_CHUNK
    W = GROUP_WIDTH

    @pl.when(ci == 0)
    def _():
        lf_s[0:C, :] = jnp.zeros((C, W), F32)
        kk_s[0:C, :] = jnp.zeros((C, W), F32)
        vv_s[0:C, :] = jnp.zeros((C, W), F32)
        st_ref[...] = jnp.zeros_like(st_ref)

    z = f_ref[...]
    a = lb_ref[0:1, :]
    c = lb_ref[1:2, :] + _log_sigmoid(z)
    mx = jnp.maximum(a, c)
    lf = mx + jnp.log1p(jnp.exp(-jnp.abs(a - c)))
    kk = lb_ref[2:3, :] * _sigmoid(-z) + lb_ref[3:4, :]
    qx = q_ref[...]
    qs = qx * _sigmoid(qx)
    vv = i_ref[...]
    lf_s[C:C + TL, :] = lf
    kk_s[C:C + TL, :] = kk
    vv_s[C:C + TL, :] = vv

    bd = bd_ref[...]
    row = lax.broadcasted_iota(jnp.int32, (TL, W), 0)
    rmod = row & (C - 1)

    acc = jnp.zeros((TL, W), F32)
    dsum = jnp.zeros((TL, W), F32)
    for delta in range(C):
        if delta > 0:
            dsum = dsum + lf_s[C - (delta - 1):C - (delta - 1) + TL, :]
        x = qs * kk_s[C - delta:C - delta + TL, :] * jnp.exp(dsum)
        x = jnp.where(rmod >= delta, x, 0.0)
        att = jnp.dot(x.astype(BF16), bd, preferred_element_type=F32)
        acc = acc + att * vv_s[C - delta:C - delta + TL, :]

    ri = lax.broadcasted_iota(jnp.int32, (TL, TL), 0)
    cj = lax.broadcasted_iota(jnp.int32, (TL, TL), 1)
    same = (ri // C) == (cj // C)
    tril = jnp.where(same & (ri >= cj), 1.0, 0.0).astype(BF16)
    ones = jnp.where(same, 1.0, 0.0).astype(BF16)
    bcum = _tril_sum(lf, tril)
    blast = _tril_sum(lf, ones)
    q_a = (qs * jnp.exp(bcum)).astype(BF16)
    k_b = (kk * jnp.exp(blast - bcum)).astype(BF16)
    dec = jnp.exp(blast)
    vv_b = vv.astype(BF16)
    hmask = bd.astype(F32)
    state = st_ref[...]
    inters = []
    for ch in range(TL // C):
        r = slice(ch * C, (ch + 1) * C)
        inters.append(lax.dot_general(q_a[r], state.astype(BF16), (((1,), (1,)), ((), ())),
                                      preferred_element_type=F32))
        upd = lax.dot_general(vv_b[r], k_b[r], (((0,), (0,)), ((), ())), preferred_element_type=F32)
        state = state * dec[ch * C:ch * C + 1, :] + upd * hmask
    st_ref[...] = state
    o = acc + jnp.concatenate(inters, axis=0)
    ss = _seg_sum(o * o, bd)
    y_ref[...] = o * lax.rsqrt(ss * (1.0 / HEAD_DIM) + EPS) * og_ref[...] * _sigmoid(g_ref[...])


def _hgrn2(p3, lower_bound, out_gain, TL=256):
    b, s, _ = p3.shape
    lb = lower_bound.astype(F32)
    lbf = jnp.maximum(lb, LB_FLOOR)
    lbp = jnp.stack([jnp.log(lbf), jnp.log1p(-lb), 1.0 - lb, lb - lbf])
    col = lambda off: (lambda bi, i: (bi, i, off // GROUP_WIDTH))
    C = HGRN_CHUNK
    return pl.pallas_call(
        functools.partial(_hgrn_kernel, TL=TL),
        grid=(b, s // TL),
        in_specs=[pl.BlockSpec((None, TL, GROUP_WIDTH), col(COL_C)),
                  pl.BlockSpec((None, TL, GROUP_WIDTH), col(COL_C + 256)),
                  pl.BlockSpec((None, TL, GROUP_WIDTH), col(COL_C + 512)),
                  pl.BlockSpec((None, TL, GROUP_WIDTH), col(COL_C + 768)),
                  pl.BlockSpec((4, GROUP_WIDTH), lambda bi, i: (0, 0)),
                  pl.BlockSpec((1, GROUP_WIDTH), lambda bi, i: (0, 0)),
                  pl.BlockSpec((GROUP_WIDTH, GROUP_WIDTH), lambda bi, i: (0, 0))],
        out_specs=pl.BlockSpec((None, TL, GROUP_WIDTH), lambda bi, i: (bi, i, 0)),
        out_shape=jax.ShapeDtypeStruct((b, s, GROUP_WIDTH), F32),
        scratch_shapes=[pltpu.VMEM((C + TL, GROUP_WIDTH), F32),
                        pltpu.VMEM((C + TL, GROUP_WIDTH), F32),
                        pltpu.VMEM((C + TL, GROUP_WIDTH), F32),
                        pltpu.VMEM((GROUP_WIDTH, GROUP_WIDTH), F32)],
        compiler_params=_cparams(("arbitrary", "arbitrary")),
        name="hgrn2",
    )(p3, p3, p3, p3, lbp, out_gain.astype(F32).reshape(1, GROUP_WIDTH), _block_ones(GROUP_WIDTH, HEAD_DIM))


CMP_LEN = 32
CMP_STRIDE = 16
SEL_BLOCK = 64
SEL_TOPK = 16
WINDOW = 512
GATE_ROW = 8


def _rms_rows(x, gain):
    return x * lax.rsqrt(jnp.mean(x * x, axis=-1, keepdims=True) + EPS) * gain


NSA_V_ROWS = (GROUP_HEADS + 1) * V_AUG


def _aug_shared_values(v, w_wide, w_t):
    tm = v.shape[0]
    vw_t = (jnp.concatenate([v] * GROUP_HEADS, axis=1) * w_wide).T
    first = lax.broadcasted_iota(jnp.int32, (V_AUG - HEAD_DIM, tm), 0) == 0
    blocks = []
    for h in range(GROUP_HEADS):
        blocks += [vw_t[HEAD_DIM * h:HEAD_DIM * (h + 1), :],
                   jnp.where(first, jnp.broadcast_to(w_t[h:h + 1, :], first.shape), 0.0)]
    blocks += [v.T, jnp.where(first, 1.0, 0.0)]
    return jnp.concatenate(blocks, axis=0).astype(BF16)


def _nsa_prep_kernel(q_ref, kv_ref, g_ref, qg_ref, kg_ref, bd_ref, w_ref, wt_ref,
                     qt_ref, kc_ref, vc_ref, ks_ref, vst_ref, kw_ref, vwt_ref, gt_ref):
    ti = pl.program_id(1)
    q = q_ref[...]
    ss = _seg_sum(q * q, bd_ref[...])
    qn = q * lax.rsqrt(ss * (1.0 / HEAD_DIM) + EPS) * qg_ref[...] * (HEAD_DIM ** -0.5 * LOG2E)
    qt_ref[...] = qn.T.astype(BF16)
    kv = kv_ref[...]
    tm = kv.shape[0]
    kc_ref[...] = kv[:, 0:64]
    vc_ref[...] = kv[:, 64:128]
    ks = _rms_rows(kv[:, 128:192], kg_ref[1:2, :])
    lane = lax.broadcasted_iota(jnp.int32, (tm, LANES), 1)
    blk = jnp.right_shift(ti * tm + lax.broadcasted_iota(jnp.int32, (tm, LANES), 0), SEL_BLOCK.bit_length() - 1)
    onehot = jnp.where(lane == blk + HEAD_DIM, 1.0, 0.0)
    ks_ref[...] = jnp.where(lane < HEAD_DIM, jnp.concatenate([ks, ks], axis=1), onehot).astype(BF16)
    kw = _rms_rows(kv[:, 256:320], kg_ref[2:3, :])
    kw_ref[...] = jnp.concatenate([kw, jnp.zeros_like(kw)], axis=1).astype(BF16)
    vst_ref[...] = _aug_shared_values(kv[:, 192:256], w_ref[...], wt_ref[...])
    vwt_ref[...] = _aug_shared_values(kv[:, 320:384], w_ref[...], wt_ref[...])
    gt_ref[...] = _sigmoid(g_ref[...]).T


def _gelu_tanh(x):
    return 0.5 * x * (1.0 + jnp.tanh(math.sqrt(2.0 / math.pi) * (x + 0.044715 * x * x * x)))


def _nsa_cmp_kernel(kr_ref, vr_ref, w1_ref, w2_ref, pos_ref, kg_ref, kc_ref, vct_ref, sh_ref, *, n_rows):
    half = CMP_STRIDE * HEAD_DIM
    sh_ref[n_rows:n_rows + 8, :] = jnp.zeros((8, HEAD_DIM), F32)
    outs = []
    for j, x_ref in enumerate((kr_ref, vr_ref)):
        r = x_ref[...].astype(BF16)
        w1 = w1_ref[j]
        first = jnp.dot(r, w1[0:half, :], preferred_element_type=F32)
        sh_ref[0:n_rows, :] = jnp.dot(r, w1[half:2 * half, :], preferred_element_type=F32)
        pos8 = jnp.broadcast_to(pos_ref[j], (8, 2 * half))
        posw = jnp.dot(pos8, w1.astype(F32), preferred_element_type=F32)[0:1, :]
        hdn = _gelu_tanh(first + sh_ref[1:n_rows + 1, :] + posw)
        outs.append(jnp.dot(hdn, w2_ref[j].astype(F32), preferred_element_type=F32))
    kc_ref[...] = _rms_rows(outs[0], kg_ref[0:1, :]).astype(BF16)
    vct_ref[...] = outs[1].T.astype(BF16)


def _nsa_attn_kernel(qt_ref, kc_ref, vct_ref, ks_ref, vst_ref, kw_ref, vwt_ref, gt_ref, ov_ref, db_ref, lo_ref,
                     og_ref, bd_ref, y_ref, qa_ref, acc_s, m_s, acc_w, m_w, ss_ref, sw_ref,
                     *, tq, slopes, n_cmp):
    qi = pl.program_id(1)
    n_rows = kc_ref.shape[0]
    n_sel = ov_ref.shape[0]
    qpos = qi * tq + lax.broadcasted_iota(jnp.int32, (1, tq), 1)

    nio = lax.broadcasted_iota(jnp.int32, (n_rows, tq), 0)
    dist_c = qpos - (nio * CMP_STRIDE + (CMP_LEN - 1))
    valid_c = (dist_c >= 0) & (nio < n_cmp)
    dist_cf = dist_c.astype(F32)
    kc = kc_ref[...]
    vct = vct_ref[...]
    o_cmp = []
    p_sum = jnp.zeros((n_rows, tq), F32)
    for h in range(GROUP_HEADS):
        qh = qt_ref[HEAD_DIM * h:HEAD_DIM * (h + 1), :]
        s = jnp.dot(kc, qh, preferred_element_type=F32) - (slopes[h] * LOG2E) * dist_cf
        s = jnp.where(valid_c, s, MASKED)
        m = jnp.maximum(jnp.max(s, axis=0, keepdims=True), M_INIT)
        e = jnp.exp2(s - m)
        p = e / jnp.maximum(jnp.sum(e, axis=0, keepdims=True), 1e-30)
        p_sum = p_sum + p
        o_cmp.append(jnp.dot(vct, p.astype(BF16), preferred_element_type=F32))

    p_hi = p_sum.astype(BF16)
    p_lo = (p_sum - p_hi.astype(F32)).astype(BF16)
    ov = ov_ref[...]
    imp = jnp.dot(ov, p_hi, preferred_element_type=F32) + jnp.dot(ov, p_lo, preferred_element_type=F32)
    jio = lax.broadcasted_iota(jnp.int32, (n_sel, tq), 0)
    qblk = jnp.right_shift(qpos, SEL_BLOCK.bit_length() - 1)
    imp = jnp.where(jio <= qblk, imp, -1.0)
    imp = jnp.where((jio == 0) | (jio == qblk) | (jio == qblk - 1), 2.0, imp)
    rank = jnp.zeros((n_sel, tq), F32)
    for i in range(n_sel):
        row = imp[i:i + 1, :]
        beats = (row > imp) | ((row == imp) & (jio > i))
        rank = rank + jnp.where(beats, 1.0, 0.0)
    bsel = jnp.where(rank < float(min(SEL_TOPK, n_sel)), 0.0, MASKED)
    if n_sel < LANES - HEAD_DIM:
        bsel = jnp.concatenate([bsel, jnp.zeros((LANES - HEAD_DIM - n_sel, tq), F32)], axis=0)
    for h in range(GROUP_HEADS):
        qa_ref[h, 0:HEAD_DIM, :] = qt_ref[HEAD_DIM * h:HEAD_DIM * (h + 1), :]
        qa_ref[h, HEAD_DIM:LANES, :] = bsel.astype(BF16)
    for ref, val in ((acc_s, 0.0), (acc_w, 0.0), (m_s, M_INIT), (m_w, M_INIT)):
        ref[...] = jnp.full(ref.shape, val, F32)

    def tile_off(kt, h):
        return ((kt - qi) * tq + (tq - 1)).astype(F32) * (slopes[h] * LOG2E)

    def values(vt_ref, kt, h, diag):
        base = GROUP_HEADS * V_AUG if diag else V_AUG * h
        return vt_ref[base:base + V_AUG, pl.ds(pl.multiple_of(kt * tq, tq), tq)]

    def sel_scores(kt, h):
        kb = ks_ref[pl.ds(pl.multiple_of(kt * tq, tq), tq), :]
        ss_ref[h] = jnp.dot(kb, qa_ref[h], preferred_element_type=F32)

    def sel_step(kt, diag):
        def prefetch(h):
            if h + AHEAD < GROUP_HEADS:
                sel_scores(kt, h + AHEAD)
            elif not diag:
                sel_scores(kt + 1, h + AHEAD - GROUP_HEADS)

        _run_chains(GROUP_HEADS, prefetch, lambda h: _flash_update(
            ss_ref.at[h], db_ref.at[h] if diag else None, 0.0 if diag else tile_off(kt, h),
            values(vst_ref, kt, h, diag), m_s.at[h], acc_s.at[h]))

    for h in range(AHEAD):
        sel_scores(0, h)

    def body(kt, carry):
        sel_step(kt, False)
        return carry

    lax.fori_loop(0, qi, body, 0)
    sel_step(qi, True)

    def win_step(back, bias_of, diag):
        kt = jnp.maximum(qi - back, 0)
        penalty = None if diag else jnp.where(qi >= back, 0.0, MASKED)
        kb = kw_ref[pl.ds(pl.multiple_of(kt * tq, tq), tq), :]

        def win_scores(h):
            sw_ref[h] = jnp.dot(kb, qa_ref[h], preferred_element_type=F32)

        for h in range(AHEAD):
            win_scores(h)

        def prefetch(h):
            if h + AHEAD < GROUP_HEADS:
                win_scores(h + AHEAD)

        _run_chains(GROUP_HEADS, prefetch, lambda h: _flash_update(
            sw_ref.at[h], bias_of(h), 0.0 if diag else tile_off(kt, h),
            values(vwt_ref, kt, h, diag), m_w.at[h], acc_w.at[h], penalty))

    win_step(2, lambda h: lo_ref, False)
    win_step(1, lambda h: None, False)
    win_step(0, lambda h: db_ref.at[h], True)

    gt = gt_ref[...]
    heads = []
    for h in range(GROUP_HEADS):
        a_s, a_w = acc_s[h], acc_w[h]
        o_sel = a_s[0:HEAD_DIM, :] / a_s[HEAD_DIM:HEAD_DIM + 1, :]
        o_win = a_w[0:HEAD_DIM, :] / a_w[HEAD_DIM:HEAD_DIM + 1, :]
        g = GATE_ROW + 3 * h
        heads.append(gt[g:g + 1, :] * o_cmp[h] + gt[g + 1:g + 2, :] * o_sel + gt[g + 2:g + 3, :] * o_win)
    o = jnp.concatenate(heads, axis=0).T
    ss = _seg_sum(o * o, bd_ref[...])
    y_ref[...] = o * lax.rsqrt(ss * (1.0 / HEAD_DIM) + EPS) * og_ref[...]


def _nsa(p3, cmp_pos, cmp_w1, cmp_w2, q_gain, k_gain, out_gain, slopes, tq=256):
    b, s, _ = p3.shape
    tm = tq
    n_rows = s // CMP_STRIDE
    n_cmp = (s - CMP_LEN) // CMP_STRIDE + 1
    n_sel = s // SEL_BLOCK
    bd64 = _block_ones(GROUP_WIDTH, HEAD_DIM)
    qg = jnp.tile(q_gain.astype(F32), GROUP_HEADS).reshape(1, GROUP_WIDTH)
    kg = k_gain.astype(F32)
    tok = lambda w: pl.BlockSpec((None, tm, w), lambda bi, i: (bi, i, 0))
    tok_t = lambda w: pl.BlockSpec((None, w, tm), lambda bi, i: (bi, 0, i))
    full = lambda r, c: pl.BlockSpec((r, c), lambda bi, i: (0, 0))
    assert n_sel <= LANES - HEAD_DIM, "block one-hot lanes hold at most 64 selection blocks"
    assert WINDOW == 2 * tq and tq % SEL_BLOCK == 0, "window branch walks exactly the key tiles qi-2, qi-1, qi"
    w_wide, w_t = _key_weights(tm, slopes)
    qt, kc, vc, ks, vst, kw, vwt, gt = pl.pallas_call(
        _nsa_prep_kernel,
        grid=(b, s // tm),
        in_specs=[pl.BlockSpec((None, tm, GROUP_WIDTH), lambda bi, i: (bi, i, COL_D // GROUP_WIDTH)),
                  pl.BlockSpec((None, tm, 384), lambda bi, i: (bi, i, (COL_D + GROUP_WIDTH) // 384)),
                  pl.BlockSpec((None, tm, LANES), lambda bi, i: (bi, i, COL_G // LANES)),
                  full(1, GROUP_WIDTH), full(3, HEAD_DIM), full(GROUP_WIDTH, GROUP_WIDTH),
                  full(tm, GROUP_WIDTH), full(8, tm)],
        out_specs=[tok_t(GROUP_WIDTH), tok(HEAD_DIM), tok(HEAD_DIM), tok(LANES), tok_t(NSA_V_ROWS),
                   tok(LANES), tok_t(NSA_V_ROWS), tok_t(LANES)],
        out_shape=[jax.ShapeDtypeStruct((b, GROUP_WIDTH, s), BF16),
                   jax.ShapeDtypeStruct((b, s, HEAD_DIM), F32),
                   jax.ShapeDtypeStruct((b, s, HEAD_DIM), F32),
                   jax.ShapeDtypeStruct((b, s, LANES), BF16),
                   jax.ShapeDtypeStruct((b, NSA_V_ROWS, s), BF16),
                   jax.ShapeDtypeStruct((b, s, LANES), BF16),
                   jax.ShapeDtypeStruct((b, NSA_V_ROWS, s), BF16),
                   jax.ShapeDtypeStruct((b, LANES, s), F32)],
        compiler_params=_cparams(("arbitrary", "arbitrary")),
        name="nsa_prep",
    )(p3, p3, p3, qg, kg, bd64, w_wide, w_t)

    row_w = CMP_STRIDE * HEAD_DIM
    kcmp, vcmp_t = pl.pallas_call(
        functools.partial(_nsa_cmp_kernel, n_rows=n_rows),
        grid=(b,),
        in_specs=[pl.BlockSpec((None, n_rows, row_w), lambda bi: (bi, 0, 0)),
                  pl.BlockSpec((None, n_rows, row_w), lambda bi: (bi, 0, 0)),
                  pl.BlockSpec((2, 2 * row_w, HEAD_DIM), lambda bi: (0, 0, 0)),
                  pl.BlockSpec((2, HEAD_DIM, HEAD_DIM), lambda bi: (0, 0, 0)),
                  pl.BlockSpec((2, 1, 2 * row_w), lambda bi: (0, 0, 0)),
                  pl.BlockSpec((3, HEAD_DIM), lambda bi: (0, 0))],
        out_specs=[pl.BlockSpec((None, n_rows, HEAD_DIM), lambda bi: (bi, 0, 0)),
                   pl.BlockSpec((None, HEAD_DIM, n_rows), lambda bi: (bi, 0, 0))],
        out_shape=[jax.ShapeDtypeStruct((b, n_rows, HEAD_DIM), BF16),
                   jax.ShapeDtypeStruct((b, HEAD_DIM, n_rows), BF16)],
        scratch_shapes=[pltpu.VMEM((n_rows + 8, HEAD_DIM), F32)],
        compiler_params=_cparams(("arbitrary",)),
        name="nsa_cmp",
    )(kc.reshape(b, n_rows, row_w), vc.reshape(b, n_rows, row_w), cmp_w1.astype(BF16), cmp_w2.astype(BF16),
      cmp_pos.astype(F32).reshape(2, 1, 2 * row_w), kg)

    cs = np.arange(n_rows)[:, None] * CMP_STRIDE
    ss = np.arange(n_sel)[None, :] * SEL_BLOCK
    overlap = np.clip(np.minimum(cs + CMP_LEN, ss + SEL_BLOCK) - np.maximum(cs, ss), 0, None) / CMP_LEN
    overlap[n_cmp:, :] = 0.0
    ov_t = jnp.asarray(overlap.T.astype(np.float32), BF16)

    seq = lambda r, c: pl.BlockSpec((None, r, c), lambda bi, i: (bi, 0, 0))
    kern = functools.partial(_nsa_attn_kernel, tq=tq, slopes=tuple(float(v) for v in slopes), n_cmp=n_cmp)
    return pl.pallas_call(
        kern,
        grid=(b, s // tq),
        in_specs=[pl.BlockSpec((None, GROUP_WIDTH, tq), lambda bi, i: (bi, 0, i)),
                  seq(n_rows, HEAD_DIM), seq(HEAD_DIM, n_rows),
                  seq(s, LANES), seq(NSA_V_ROWS, s), seq(s, LANES), seq(NSA_V_ROWS, s),
                  pl.BlockSpec((None, LANES, tq), lambda bi, i: (bi, 0, i)),
                  full(n_sel, n_rows),
                  pl.BlockSpec((GROUP_HEADS, tq, tq), lambda bi, i: (0, 0, 0)), full(tq, tq),
                  full(1, GROUP_WIDTH), full(GROUP_WIDTH, GROUP_WIDTH)],
        out_specs=pl.BlockSpec((None, tq, GROUP_WIDTH), lambda bi, i: (bi, i, 0)),
        out_shape=jax.ShapeDtypeStruct((b, s, GROUP_WIDTH), F32),
        scratch_shapes=[pltpu.VMEM((GROUP_HEADS, LANES, tq), BF16),
                        pltpu.VMEM((GROUP_HEADS, V_AUG, tq), F32), pltpu.VMEM((GROUP_HEADS, 8, tq), F32),
                        pltpu.VMEM((GROUP_HEADS, V_AUG, tq), F32), pltpu.VMEM((GROUP_HEADS, 8, tq), F32),
                        pltpu.VMEM((GROUP_HEADS, tq, tq), F32), pltpu.VMEM((GROUP_HEADS, tq, tq), F32)],
        compiler_params=_cparams(("arbitrary", "arbitrary")),
        name="nsa_attn",
    )(qt, kcmp, vcmp_t, ks, vst, kw, vwt, gt, ov_t, _diag_bias(tq, slopes), _window_low_bias(tq),
      out_gain.astype(F32).reshape(1, GROUP_WIDTH), bd64)


def _window_low_bias(tq):
    k = np.arange(tq)[:, None]
    q = np.arange(tq)[None, :]
    return jnp.asarray(np.where(k > q, 0.0, MASKED).astype(np.float32))


N_GROUPS = 4
EXPERTS_PER_GROUP = 8
N_EXPERTS = N_GROUPS * EXPERTS_PER_GROUP
D_EXPERT = 256
ROUTER_LANE0 = N_GROUPS


def _split3_dot(a, b_hi, b_lo):
    a_hi = a.astype(BF16)
    a_lo = (a - a_hi.astype(F32)).astype(BF16)
    return (jnp.dot(a_hi, b_hi, preferred_element_type=F32) + jnp.dot(a_lo, b_hi, preferred_element_type=F32)
            + jnp.dot(a_hi, b_lo, preferred_element_type=F32))


def _route(t, wr_hi, wr_lo, br):
    logits = _split3_dot(t, wr_hi, wr_lo) + br
    lane = lax.broadcasted_iota(jnp.int32, logits.shape, 1)
    lane_f = lane.astype(F32)
    big = float(LANES)
    is_g = lane < N_GROUPS
    gl = jnp.where(is_g, logits, MASKED)
    gmax = jnp.max(gl, axis=-1, keepdims=True)
    g_prob = 1.0 / jnp.sum(jnp.where(is_g, jnp.exp(gl - gmax), 0.0), axis=-1, keepdims=True)
    g_sel = jnp.min(jnp.where(is_g & (gl == gmax), lane_f, big), axis=-1, keepdims=True)
    lo = ROUTER_LANE0 + EXPERTS_PER_GROUP * g_sel
    in_grp = (lane_f >= lo) & (lane_f < lo + EXPERTS_PER_GROUP)
    el = jnp.where(in_grp, logits, MASKED)
    m1 = jnp.max(el, axis=-1, keepdims=True)
    i1 = jnp.min(jnp.where(in_grp & (el == m1), lane_f, big), axis=-1, keepdims=True)
    rest = in_grp & (lane_f != i1)
    el2 = jnp.where(rest, logits, MASKED)
    m2 = jnp.max(el2, axis=-1, keepdims=True)
    i2 = jnp.min(jnp.where(rest & (el2 == m2), lane_f, big), axis=-1, keepdims=True)
    r = jnp.exp(m2 - m1)
    w1 = g_prob / (1.0 + r)
    w2 = g_prob * r / (1.0 + r)
    return jnp.where(lane_f == i1, w1, 0.0) + jnp.where(lane_f == i2, w2, 0.0), g_sel


T_AUG = D_MODEL + LANES


def _outproj_kernel(x_ref, ya_ref, yb_ref, yc_ref, yd_ref, w_ref, g_ref, wr_hi_ref, wr_lo_ref, br_ref,
                    xo_ref, tg_ref, route_ref, cnt_ref, run_ref):
    @pl.when(pl.program_id(0) == 0)
    def _():
        run_ref[...] = jnp.zeros_like(run_ref)

    acc = x_ref[...]
    for gi, y_ref in enumerate((ya_ref, yb_ref, yc_ref, yd_ref)):
        acc = acc + jnp.dot(y_ref[...].astype(BF16), w_ref[GROUP_WIDTH * gi:GROUP_WIDTH * (gi + 1), :],
                            preferred_element_type=F32)
    xo_ref[...] = acc
    ms = jnp.mean(acc * acc, axis=-1, keepdims=True)
    t = acc * lax.rsqrt(ms + EPS) * g_ref[...]
    tg_ref[:, 0:D_MODEL] = t
    gate, g_sel = _route(t, wr_hi_ref[...], wr_lo_ref[...], br_ref[...])
    tg_ref[:, D_MODEL:T_AUG] = gate

    tm = t.shape[0]
    lane = lax.broadcasted_iota(jnp.int32, (tm, LANES), 1)
    onehot = jnp.where(lane.astype(F32) == g_sel, 1.0, 0.0)
    ri = lax.broadcasted_iota(jnp.int32, (tm, tm), 0)
    ci = lax.broadcasted_iota(jnp.int32, (tm, tm), 1)
    before = jnp.where(ri > ci, 1.0, 0.0).astype(BF16)
    prefix = jnp.dot(before, onehot.astype(BF16), preferred_element_type=F32)
    rank = jnp.sum(onehot * (run_ref[...] + prefix), axis=-1, keepdims=True)
    route_ref[...] = jnp.where(lane == 0, g_sel, 0.0) + jnp.where(lane == 1, rank, 0.0)
    run_ref[...] = run_ref[...] + jnp.sum(onehot, axis=0, keepdims=True)
    cnt_ref[...] = run_ref[...]


def _outproj(x2d, ys, w_out, ffn_gain, w_group, b_group, w_expert, b_expert, tm=256):
    t = x2d.shape[0]
    wr = jnp.zeros((D_MODEL, LANES), F32).at[:, 0:N_GROUPS].set(w_group).at[:, ROUTER_LANE0:ROUTER_LANE0 + N_EXPERTS].set(w_expert)
    wr_hi = wr.astype(BF16)
    wr_lo = (wr - wr_hi.astype(F32)).astype(BF16)
    br = jnp.zeros((1, LANES), F32).at[0, 0:N_GROUPS].set(b_group).at[0, ROUTER_LANE0:ROUTER_LANE0 + N_EXPERTS].set(b_expert)
    row = lambda w: pl.BlockSpec((tm, w), lambda i: (i, 0))
    full = lambda r, c: pl.BlockSpec((r, c), lambda i: (0, 0))
    return list(pl.pallas_call(
        _outproj_kernel,
        grid=(t // tm,),
        in_specs=[row(D_MODEL), row(GROUP_WIDTH), row(GROUP_WIDTH), row(GROUP_WIDTH), row(GROUP_WIDTH),
                  full(D_MODEL, D_MODEL), full(1, D_MODEL), full(D_MODEL, LANES), full(D_MODEL, LANES), full(1, LANES)],
        out_specs=[row(D_MODEL), row(T_AUG), row(LANES), full(1, LANES)],
        out_shape=[jax.ShapeDtypeStruct((t, D_MODEL), F32),
                   jax.ShapeDtypeStruct((t, T_AUG), F32),
                   jax.ShapeDtypeStruct((t, LANES), F32),
                   jax.ShapeDtypeStruct((1, LANES), F32)],
        scratch_shapes=[pltpu.VMEM((1, LANES), F32)],
        compiler_params=_cparams(("arbitrary",)),
        name="outproj_router",
    )(x2d, *ys, w_out.astype(BF16), ffn_gain.reshape(1, D_MODEL), wr_hi, wr_lo, br))


MOE_TILE = 1024
DMA_PRIORITIES = 2


def _row_copies(n, make_copy):
    def start(k, carry):
        for p in range(DMA_PRIORITIES):
            make_copy(DMA_PRIORITIES * k + p).start(priority=p)
        return carry

    def wait(r, carry):
        make_copy(r).wait()
        return carry

    lax.fori_loop(0, n // DMA_PRIORITIES, start, 0)
    lax.fori_loop(0, n, wait, 0)


def _moe_scatter_kernel(pos_ref, tg_ref, xs_in_ref, xs_ref, sem):
    del xs_in_ref
    tm = tg_ref.shape[0]
    base = pl.program_id(0) * tm
    _row_copies(tm, lambda r: pltpu.make_async_copy(tg_ref.at[pl.ds(r, 1)],
                                                    xs_ref.at[pl.ds(pos_ref[base + r], 1)], sem))


EXPERTS_PER_STEP = 2


def _moe_expert_kernel(grp_ref, valid_ref, xs_ref, wg_ref, wu_ref, wd_ref, y_ref, acc_ref):
    i, step = pl.program_id(0), pl.program_id(1)

    @pl.when(step == 0)
    def _():
        acc_ref[...] = jnp.zeros_like(acc_ref)

    @pl.when(valid_ref[i] == 1)
    def _():
        x = xs_ref[:, 0:D_MODEL].astype(BF16)
        gate = xs_ref[:, D_MODEL:T_AUG]
        lane = lax.broadcasted_iota(jnp.int32, gate.shape, 1)
        total = None
        for k in range(EXPERTS_PER_STEP):
            col = ROUTER_LANE0 + grp_ref[i] * EXPERTS_PER_GROUP + step * EXPERTS_PER_STEP + k
            w = jnp.sum(jnp.where(lane == col, gate, 0.0), axis=-1, keepdims=True)
            a = jnp.dot(x, wg_ref[k].astype(BF16), preferred_element_type=F32)
            u = jnp.dot(x, wu_ref[k].astype(BF16), preferred_element_type=F32)
            act = a * _sigmoid(a) * u * w
            down = jnp.dot(act.astype(BF16), wd_ref[k].astype(BF16), preferred_element_type=F32)
            total = down if total is None else total + down
        acc_ref[...] += total

    @pl.when(step == EXPERTS_PER_GROUP // EXPERTS_PER_STEP - 1)
    def _():
        y_ref[...] = acc_ref[...]


def _moe_combine_kernel(pos_ref, x_ref, ys_ref, o_ref, buf_ref, sem):
    tm = x_ref.shape[0]
    base = pl.program_id(0) * tm
    _row_copies(tm, lambda r: pltpu.make_async_copy(ys_ref.at[pl.ds(pos_ref[base + r], 1)],
                                                    buf_ref.at[pl.ds(r, 1)], sem))
    o_ref[...] = x_ref[...] + buf_ref[...]


def _moe_routed(x2d, tg, route, counts, w_gate, w_up, w_down, tm=512):
    t = x2d.shape[0]
    te = MOE_TILE
    n_tiles = t // te + N_GROUPS
    n_rows = n_tiles * te
    grp = route[:, 0].astype(jnp.int32)
    rank = route[:, 1].astype(jnp.int32)
    cnt = counts[0, 0:N_GROUPS].astype(jnp.int32)
    padded = ((cnt + te - 1) // te) * te
    ends = jnp.cumsum(padded)
    pos = (ends - padded)[grp] + rank
    starts = jnp.arange(n_tiles, dtype=jnp.int32) * te
    tile_grp = jnp.minimum(jnp.sum((starts[:, None] >= ends[None, :]).astype(jnp.int32), axis=1), N_GROUPS - 1)
    tile_valid = (starts < ends[-1]).astype(jnp.int32)

    xs = pl.pallas_call(
        _moe_scatter_kernel,
        grid_spec=pltpu.PrefetchScalarGridSpec(
            num_scalar_prefetch=1, grid=(t // tm,),
            in_specs=[pl.BlockSpec((tm, T_AUG), lambda i, pos: (i, 0)), pl.BlockSpec(memory_space=pl.ANY)],
            out_specs=pl.BlockSpec(memory_space=pl.ANY),
            scratch_shapes=[pltpu.SemaphoreType.DMA(())]),
        out_shape=jax.ShapeDtypeStruct((n_rows, T_AUG), F32),
        input_output_aliases={2: 0},
        compiler_params=_cparams(("arbitrary",)),
        name="moe_scatter",
    )(pos, tg, jnp.zeros((n_rows, T_AUG), F32))

    steps = EXPERTS_PER_GROUP // EXPERTS_PER_STEP
    wg = w_gate.reshape(N_GROUPS * steps, EXPERTS_PER_STEP, D_MODEL, D_EXPERT)
    wu = w_up.reshape(N_GROUPS * steps, EXPERTS_PER_STEP, D_MODEL, D_EXPERT)
    wd = w_down.reshape(N_GROUPS * steps, EXPERTS_PER_STEP, D_EXPERT, D_MODEL)
    wsel = lambda i, e, g, v: (g[i] * steps + e, 0, 0, 0)
    ys = pl.pallas_call(
        _moe_expert_kernel,
        grid_spec=pltpu.PrefetchScalarGridSpec(
            num_scalar_prefetch=2, grid=(n_tiles, steps),
            in_specs=[pl.BlockSpec((te, T_AUG), lambda i, e, g, v: (i, 0)),
                      pl.BlockSpec((None, EXPERTS_PER_STEP, D_MODEL, D_EXPERT), wsel),
                      pl.BlockSpec((None, EXPERTS_PER_STEP, D_MODEL, D_EXPERT), wsel),
                      pl.BlockSpec((None, EXPERTS_PER_STEP, D_EXPERT, D_MODEL), wsel)],
            out_specs=pl.BlockSpec((te, D_MODEL), lambda i, e, g, v: (i, 0)),
            scratch_shapes=[pltpu.VMEM((te, D_MODEL), F32)]),
        out_shape=jax.ShapeDtypeStruct((n_rows, D_MODEL), F32),
        compiler_params=_cparams(("arbitrary", "arbitrary")),
        name="moe_experts",
    )(tile_grp, tile_valid, xs, wg, wu, wd)

    return pl.pallas_call(
        _moe_combine_kernel,
        grid_spec=pltpu.PrefetchScalarGridSpec(
            num_scalar_prefetch=1, grid=(t // tm,),
            in_specs=[pl.BlockSpec((tm, D_MODEL), lambda i, pos: (i, 0)), pl.BlockSpec(memory_space=pl.ANY)],
            out_specs=pl.BlockSpec((tm, D_MODEL), lambda i, pos: (i, 0)),
            scratch_shapes=[pltpu.VMEM((tm, D_MODEL), F32), pltpu.SemaphoreType.DMA(())]),
        out_shape=jax.ShapeDtypeStruct((t, D_MODEL), F32),
        compiler_params=_cparams(("arbitrary",)),
        name="moe_combine",
    )(pos, x2d, ys)


def _alibi_slopes():
    n = 2 * GROUP_HEADS
    s = 2.0 ** (-8.0 * np.arange(1, n + 1) / n)
    return s[0::2], s[1::2]


def kernel(x, norm_mix, norm_ffn, w_in, w_out, diff_q_gain, diff_k_gain, diff_lambda, diff_sub_gain, mlstm_conv, mlstm_gate_bias, mlstm_out_gain, hgrn_lower_bounds, hgrn_out_gain, nsa_cmp_pos, nsa_cmp_w1, nsa_cmp_w2, nsa_q_gain, nsa_k_gain, nsa_out_gain, moe_w_group, moe_b_group, moe_w_expert, moe_b_expert, moe_w_gate, moe_w_up, moe_w_down):
    b, s, d = x.shape
    slopes_diff, slopes_nsa = _alibi_slopes()
    lb_soft = jax.nn.softmax(hgrn_lower_bounds.astype(F32), axis=0)
    lower_bounds = jnp.cumsum(lb_soft, axis=0) - lb_soft[0]
    x2d = x.reshape(b * s, d)
    for l in range(norm_mix.shape[0]):
        p = _inproj(x2d, norm_mix[l], _pack_w_in(w_in[l])).reshape(b, s, P_COLS)
        y_a = _diff_attention(p, diff_q_gain[l], diff_k_gain[l], diff_lambda[l], diff_sub_gain[l], slopes_diff, l)
        y_b = _mlstm(p, mlstm_conv[l], mlstm_gate_bias[l], mlstm_out_gain[l])
        y_c = _hgrn2(p, lower_bounds[l], hgrn_out_gain[l])
        y_d = _nsa(p, nsa_cmp_pos[l], nsa_cmp_w1[l], nsa_cmp_w2[l], nsa_q_gain[l], nsa_k_gain[l],
                   nsa_out_gain[l], slopes_nsa)
        ys = [y.reshape(b * s, GROUP_WIDTH) for y in (y_a, y_b, y_c, y_d)]
        x2d, tg, route, counts = _outproj(x2d, ys, w_out[l], norm_ffn[l], moe_w_group[l], moe_b_group[l],
                                          moe_w_expert[l], moe_b_expert[l])
        x2d = _moe_routed(x2d, tg, route, counts, moe_w_gate[l], moe_w_up[l], moe_w_down[l])
    return x2d.reshape(b, s, d)
```

```python
import functools
import math

import numpy as np
import jax
import jax.numpy as jnp
from jax import lax
from jax.experimental import pallas as pl
from jax.experimental.pallas import tpu as pltpu

F32 = jnp.float32
BF16 = jnp.bfloat16

D_MODEL = 1024
HEAD_DIM = 64
GROUP_HEADS = 4
GROUP_WIDTH = GROUP_HEADS * HEAD_DIM
DIFF_HALF = HEAD_DIM // 2
EPS = 1e-6
NEG_INF = -1e30
LOG2E = math.log2(math.e)
M_INIT = -1e30
MASKED = -2e30

LANES = 128
VMEM_LIMIT = 48 * 1024 * 1024

COL_A = 0
COL_B = 768
COL_C = 1792
COL_D = 2816
COL_G = 3456
P_COLS = 3584


def _cparams(sem, flags=None, vmem_limit=VMEM_LIMIT):
    return pltpu.CompilerParams(dimension_semantics=sem, vmem_limit_bytes=vmem_limit, flags=flags)


def _block_ones(width, seg):
    i = np.arange(width)
    return jnp.asarray((i[:, None] // seg == i[None, :] // seg).astype(np.float32), BF16)


def _seg_sum(x, bd):
    hi = x.astype(BF16)
    lo = (x - hi.astype(F32)).astype(BF16)
    return (jnp.dot(hi, bd, preferred_element_type=F32) + jnp.dot(lo, bd, preferred_element_type=F32))


def _inproj_kernel(x_ref, g_ref, w_ref, o_ref):
    x = x_ref[...]
    ms = jnp.mean(x * x, axis=-1, keepdims=True)
    h = (x * lax.rsqrt(ms + EPS) * g_ref[...]).astype(BF16)
    o_ref[...] = jnp.dot(h, w_ref[...], preferred_element_type=F32)


def _inproj(x2d, gain, w_cat, tm=256):
    t = x2d.shape[0]
    return pl.pallas_call(
        _inproj_kernel,
        grid=(t // tm,),
        in_specs=[pl.BlockSpec((tm, D_MODEL), lambda i: (i, 0)),
                  pl.BlockSpec((1, D_MODEL), lambda i: (0, 0)),
                  pl.BlockSpec((D_MODEL, P_COLS), lambda i: (0, 0))],
        out_specs=pl.BlockSpec((tm, P_COLS), lambda i: (i, 0)),
        out_shape=jax.ShapeDtypeStruct((t, P_COLS), F32),
        compiler_params=_cparams(("arbitrary",)),
        name="inproj",
    )(x2d, gain.reshape(1, D_MODEL), w_cat)


IN_COLS = 3476


def _pack_w_in_kernel(w_ref, o_ref):
    w = w_ref[...]
    a_b = w[:, 0:1536]
    gates_b = w[:, 1536:1544]
    rest = w[:, 1544:3464]
    gates_d = w[:, 3464:3476]
    pad = jnp.zeros((w.shape[0], P_COLS - COL_G - 20), F32)
    o_ref[...] = jnp.concatenate([a_b, rest, gates_b, gates_d, pad], axis=1).astype(BF16)


def _pack_w_in(w, tr=128):
    d = w.shape[0]
    return pl.pallas_call(
        _pack_w_in_kernel,
        grid=(d // tr,),
        in_specs=[pl.BlockSpec((tr, IN_COLS), lambda i: (i, 0))],
        out_specs=pl.BlockSpec((tr, P_COLS), lambda i: (i, 0)),
        out_shape=jax.ShapeDtypeStruct((d, P_COLS), BF16),
        compiler_params=_cparams(("arbitrary",)),
        name="pack_w_in",
    )(w)


def _diff_prep_kernel(p_ref, qg_ref, kg_ref, bd_ref, w_ref, wt_ref, q_ref, k_ref, vt_ref):
    p = p_ref[...]
    bd = bd_ref[...]

    def norm(x, g):
        ss = _seg_sum(x * x, bd)
        return x * lax.rsqrt(ss * (1.0 / DIFF_HALF) + EPS) * g

    q = norm(p[:, 0:256], qg_ref[...]) * (DIFF_HALF ** -0.5 * LOG2E)
    k = norm(p[:, 256:512], kg_ref[...])
    q_ref[...] = q.T.astype(BF16)
    k_ref[...] = k.astype(BF16)
    v = p[:, 512:768]
    v_t = v.T
    vw_t = (v * w_ref[...]).T
    w_t = wt_ref[...]
    tm = v_t.shape[1]
    first = lax.broadcasted_iota(jnp.int32, (V_AUG - HEAD_DIM, tm), 0) == 0
    blocks = []
    for h in range(GROUP_HEADS):
        hs = slice(HEAD_DIM * h, HEAD_DIM * (h + 1))
        blocks += [vw_t[hs, :], jnp.where(first, jnp.broadcast_to(w_t[h:h + 1, :], first.shape), 0.0),
                   v_t[hs, :], jnp.where(first, 1.0, 0.0)]
    vt_ref[...] = jnp.concatenate(blocks, axis=0).astype(BF16)


V_AUG = 80
V_HEAD = 2 * V_AUG


def _key_weights(tile, slopes):
    kl = np.arange(tile, dtype=np.float64) - (tile - 1)
    w = np.stack([np.exp2(sl * LOG2E * kl) for sl in slopes])
    wide = np.repeat(w.T, HEAD_DIM, axis=1)
    w8 = np.zeros((8, tile)); w8[:len(slopes)] = w
    return jnp.asarray(wide.astype(np.float32)), jnp.asarray(w8.astype(np.float32))


def _diag_bias(tq, slopes):
    k = np.arange(tq)[:, None]
    q = np.arange(tq)[None, :]
    tabs = [np.where(k <= q, sl * LOG2E * k.astype(np.float64), MASKED) for sl in slopes]
    return jnp.asarray(np.stack(tabs).astype(np.float32))


AHEAD = 2


def _flash_update(s_ref, bias_ref, off, vt, m_ref, acc_ref, penalty=None):
    tq = s_ref.shape[1]
    ps, olds, news = [], [], []
    for c in range(tq // LANES):
        cols = slice(LANES * c, LANES * (c + 1))
        s = s_ref[:, cols]
        if bias_ref is not None:
            s = s + bias_ref[:, cols]
        if penalty is not None:
            s = s + penalty
        m_tile = jnp.max(s, axis=0, keepdims=True)
        ps.append(jnp.exp2(s - m_tile).astype(BF16))
        m_old = m_ref[0:1, cols]
        m_new = jnp.maximum(m_old, m_tile + off)
        m_ref[0:1, cols] = m_new
        olds.append(jnp.exp2(m_old - m_new))
        news.append(jnp.exp2(m_tile + off - m_new))
    p, old, new = jnp.concatenate(ps, axis=1), jnp.concatenate(olds, axis=1), jnp.concatenate(news, axis=1)

    def finish():
        pv = jnp.dot(vt, p, preferred_element_type=F32)
        acc_ref[...] = old * acc_ref[...] + new * pv

    return finish


def _run_chains(n, prefetch, softmax):
    finish = None
    for c in range(n):
        prefetch(c)
        nxt = softmax(c)
        if finish is not None:
            finish()
        finish = nxt
    finish()


def _diff_attn_kernel(lam_ref, qt_ref, k_ref, vt_ref, db_ref, sg_ref, bd_ref, o_ref,
                      qm_ref, acc_ref, m_ref, s_ref, *, tq, slopes, out_scale):
    qi = pl.program_id(1)
    n_half = 2 * GROUP_HEADS
    per_tile = LANES // DIFF_HALF
    row = lax.broadcasted_iota(jnp.int32, (LANES, tq), 0)
    for j in range(n_half):
        slab = qt_ref[LANES * (j // per_tile):LANES * (j // per_tile + 1), :].astype(F32)
        r0 = DIFF_HALF * (j % per_tile)
        qm_ref[j] = jnp.where((row >= r0) & (row < r0 + DIFF_HALF), slab, 0.0).astype(BF16)
    acc_ref[...] = jnp.zeros(acc_ref.shape, F32)
    m_ref[...] = jnp.full(m_ref.shape, M_INIT, F32)

    n_slots = s_ref.shape[0]

    def scores(kt, j):
        start = pl.multiple_of(kt * tq, tq)
        kj = k_ref[pl.ds(start, tq), LANES * (j // per_tile):LANES * (j // per_tile + 1)]
        s_ref[j % n_slots] = jnp.dot(kj, qm_ref[j], preferred_element_type=F32)

    def process(kt, j, diag):
        h = j // 2
        start = pl.multiple_of(kt * tq, tq)
        off = 0.0 if diag else ((kt - qi) * tq + (tq - 1)).astype(F32) * (slopes[h] * LOG2E)
        base = V_HEAD * h + (V_AUG if diag else 0)
        vt = vt_ref[base:base + V_AUG, pl.ds(start, tq)]
        return _flash_update(s_ref.at[j % n_slots], db_ref.at[h] if diag else None, off, vt, m_ref.at[j],
                             acc_ref.at[j])

    def step(kt, diag):
        def prefetch(j):
            if j + AHEAD < n_half:
                scores(kt, j + AHEAD)
            elif not diag:
                scores(kt + 1, j + AHEAD - n_half)

        _run_chains(n_half, prefetch, lambda j: process(kt, j, diag))

    for j in range(AHEAD):
        scores(0, j)

    def body(kt, carry):
        step(kt, False)
        return carry

    lax.fori_loop(0, qi, body, 0)
    step(qi, True)

    lam = lam_ref[0, 0]
    heads = []
    for h in range(GROUP_HEADS):
        a0, a1 = acc_ref[2 * h], acc_ref[2 * h + 1]
        o0 = a0[0:HEAD_DIM, :] / a0[HEAD_DIM:HEAD_DIM + 1, :]
        o1 = a1[0:HEAD_DIM, :] / a1[HEAD_DIM:HEAD_DIM + 1, :]
        heads.append(o0 - lam * o1)
    o = jnp.concatenate(heads, axis=0).T
    ss = _seg_sum(o * o, bd_ref[...])
    o_ref[...] = o * lax.rsqrt(ss * (1.0 / HEAD_DIM) + EPS) * (sg_ref[...] * out_scale)


def _diff_attention(p3, q_gain, k_gain, lam_vecs, sub_gain, slopes, layer_idx, tq=256):
    b, s, _ = p3.shape
    tm = tq
    bd32 = _block_ones(GROUP_WIDTH, DIFF_HALF)
    bd64 = _block_ones(GROUP_WIDTH, HEAD_DIM)
    qg = jnp.tile(q_gain.astype(F32), 2 * GROUP_HEADS).reshape(1, GROUP_WIDTH)
    kg = jnp.tile(k_gain.astype(F32), 2 * GROUP_HEADS).reshape(1, GROUP_WIDTH)
    w_wide, w_t = _key_weights(tm, slopes)
    v_rows = GROUP_HEADS * V_HEAD
    qn, kn, vt = pl.pallas_call(
        _diff_prep_kernel,
        grid=(b, s // tm),
        in_specs=[pl.BlockSpec((None, tm, 768), lambda bi, i: (bi, i, COL_A // 768)),
                  pl.BlockSpec((1, GROUP_WIDTH), lambda bi, i: (0, 0)),
                  pl.BlockSpec((1, GROUP_WIDTH), lambda bi, i: (0, 0)),
                  pl.BlockSpec((GROUP_WIDTH, GROUP_WIDTH), lambda bi, i: (0, 0)),
                  pl.BlockSpec((tm, GROUP_WIDTH), lambda bi, i: (0, 0)),
                  pl.BlockSpec((8, tm), lambda bi, i: (0, 0))],
        out_specs=[pl.BlockSpec((None, GROUP_WIDTH, tm), lambda bi, i: (bi, 0, i)),
                   pl.BlockSpec((None, tm, GROUP_WIDTH), lambda bi, i: (bi, i, 0)),
                   pl.BlockSpec((None, v_rows, tm), lambda bi, i: (bi, 0, i))],
        out_shape=[jax.ShapeDtypeStruct((b, GROUP_WIDTH, s), BF16),
                   jax.ShapeDtypeStruct((b, s, GROUP_WIDTH), BF16),
                   jax.ShapeDtypeStruct((b, v_rows, s), BF16)],
        compiler_params=_cparams(("arbitrary", "arbitrary")),
        name="diff_prep",
    )(p3, qg, kg, bd32, w_wide, w_t)

    lam_init = 0.8 - 0.6 * math.exp(-0.3 * layer_idx)
    lv = lam_vecs.astype(F32)
    lam = (jnp.exp(jnp.dot(lv[0], lv[1])) - jnp.exp(jnp.dot(lv[2], lv[3])) + lam_init).reshape(1, 1)
    kern = functools.partial(_diff_attn_kernel, tq=tq, slopes=tuple(float(v) for v in slopes),
                             out_scale=1.0 - lam_init)
    return pl.pallas_call(
        kern,
        grid=(b, s // tq),
        in_specs=[pl.BlockSpec(memory_space=pltpu.SMEM),
                  pl.BlockSpec((None, GROUP_WIDTH, tq), lambda bi, i: (bi, 0, i)),
                  pl.BlockSpec((None, s, GROUP_WIDTH), lambda bi, i: (bi, 0, 0)),
                  pl.BlockSpec((None, v_rows, s), lambda bi, i: (bi, 0, 0)),
                  pl.BlockSpec((GROUP_HEADS, tq, tq), lambda bi, i: (0, 0, 0)),
                  pl.BlockSpec((1, GROUP_WIDTH), lambda bi, i: (0, 0)),
                  pl.BlockSpec((GROUP_WIDTH, GROUP_WIDTH), lambda bi, i: (0, 0))],
        out_specs=pl.BlockSpec((None, tq, GROUP_WIDTH), lambda bi, i: (bi, i, 0)),
        out_shape=jax.ShapeDtypeStruct((b, s, GROUP_WIDTH), F32),
        scratch_shapes=[pltpu.VMEM((2 * GROUP_HEADS, LANES, tq), BF16),
                        pltpu.VMEM((2 * GROUP_HEADS, V_AUG, tq), F32),
                        pltpu.VMEM((2 * GROUP_HEADS, 8, tq), F32),
                        pltpu.VMEM((2 * AHEAD, tq, tq), F32)],
        compiler_params=_cparams(("arbitrary", "arbitrary")),
        name="diff_attn",
    )(lam, qn, kn, vt, _diag_bias(tq, slopes), sub_gain.astype(F32).reshape(1, GROUP_WIDTH), bd64)


MLSTM_CONV = 4
CONV_HALO = 8
AUG = 128


def _tril_sum(x, tril):
    hi = x.astype(BF16)
    lo = (x - hi.astype(F32)).astype(BF16)
    return jnp.dot(tril, hi, preferred_element_type=F32) + jnp.dot(tril, lo, preferred_element_type=F32)


def _log_sigmoid(x):
    return jnp.minimum(x, 0.0) - jnp.log1p(jnp.exp(-jnp.abs(x)))


def _sigmoid(x):
    return 1.0 / (1.0 + jnp.exp(-x))


def _mlstm_kernel(q_ref, k_ref, v_ref, o_ref, g_ref, cw_ref, gb_ref, og_ref, bd_ref, y_ref,
                  halo_ref, c_ref, m_ref, *, L):
    ci = pl.program_id(1)

    @pl.when(ci == 0)
    def _():
        halo_ref[0:CONV_HALO, :] = jnp.zeros((CONV_HALO, 2 * GROUP_WIDTH), F32)
        c_ref[...] = jnp.zeros_like(c_ref)
        m_ref[...] = jnp.zeros_like(m_ref)

    halo_ref[CONV_HALO:CONV_HALO + L, 0:GROUP_WIDTH] = q_ref[...]
    halo_ref[CONV_HALO:CONV_HALO + L, GROUP_WIDTH:2 * GROUP_WIDTH] = k_ref[...]
    conv = jnp.zeros((L, 2 * GROUP_WIDTH), F32)
    for j in range(MLSTM_CONV):
        start = CONV_HALO - (MLSTM_CONV - 1) + j
        conv = conv + halo_ref[start:start + L, :] * cw_ref[j:j + 1, :]
    halo_ref[0:CONV_HALO, :] = halo_ref[L:L + CONV_HALO, :]
    qk = conv * _sigmoid(conv)
    q = qk[:, 0:GROUP_WIDTH] * (HEAD_DIM ** -0.5)
    k = qk[:, GROUP_WIDTH:2 * GROUP_WIDTH]
    v = v_ref[...]

    gates = g_ref[...] + gb_ref[...]
    ri = lax.broadcasted_iota(jnp.int32, (L, L), 0)
    cj = lax.broadcasted_iota(jnp.int32, (L, L), 1)
    causal = ri >= cj
    tril = jnp.where(causal, 1.0, 0.0).astype(BF16)
    gcum = _tril_sum(_log_sigmoid(gates), tril)
    gcum_t = gcum.T
    gates_t = gates.T
    lane_aug = lax.broadcasted_iota(jnp.int32, (L, AUG), 1)
    m_all = m_ref[...]

    outs = []
    for h in range(GROUP_HEADS):
        hs = slice(HEAD_DIM * h, HEAD_DIM * (h + 1))
        q_h, k_h = q[:, hs], k[:, hs]
        v_aug = jnp.where(lane_aug == HEAD_DIM, 1.0,
                          jnp.concatenate([v[:, hs], jnp.zeros((L, AUG - HEAD_DIM), F32)], axis=1))
        g_col = gcum[:, 4 + h:5 + h]
        li_col = gates[:, h:h + 1]
        g_row = gcum_t[4 + h:5 + h, :]
        li_row = gates_t[h:h + 1, :]
        g_last = g_row[:, L - 1:L]
        m_prev = m_all[:, h:h + 1]
        log_d = jnp.where(causal, g_col - g_row + li_row, NEG_INF)
        log_inter = g_col + m_prev
        m_t = jnp.maximum(log_inter, jnp.max(log_d, axis=1, keepdims=True))
        s_qk = lax.dot_general(q_h.astype(BF16), k_h.astype(BF16), (((1,), (1,)), ((), ())),
                               preferred_element_type=F32)
        w_intra = s_qk * jnp.exp(log_d - m_t)
        w_inter = jnp.exp(log_inter - m_t)
        c_aug = c_ref[h]
        num = (w_inter * jnp.dot(q_h.astype(BF16), c_aug.astype(BF16), preferred_element_type=F32)
               + jnp.dot(w_intra.astype(BF16), v_aug.astype(BF16), preferred_element_type=F32))
        den = num[:, HEAD_DIM:HEAD_DIM + 1]
        outs.append(num[:, 0:HEAD_DIM] / jnp.maximum(jnp.abs(den), jnp.exp(-m_t)))

        log_a = g_last - g_col + li_col
        m_new = jnp.maximum(g_last + m_prev, jnp.max(log_a, axis=0, keepdims=True))
        a_col = jnp.exp(log_a - m_new)
        decay = jnp.exp(g_last + m_prev - m_new)
        ak_t = (k_h * a_col).T.astype(BF16)
        c_ref[h] = decay * c_aug + jnp.dot(ak_t, v_aug.astype(BF16), preferred_element_type=F32)
        m_ref[:, h:h + 1] = m_new

    hcat = jnp.concatenate(outs, axis=1)
    ss = _seg_sum(hcat * hcat, bd_ref[...])
    y_ref[...] = hcat * lax.rsqrt(ss * (1.0 / HEAD_DIM) + EPS) * og_ref[...] * _sigmoid(o_ref[...])


def _mlstm(p3, conv_w, gate_bias, out_gain, L=256):
    b, s, _ = p3.shape
    gb = jnp.zeros((1, LANES), F32).at[0, 0:GROUP_HEADS].set(gate_bias[0]).at[0, GROUP_HEADS:2 * GROUP_HEADS].set(gate_bias[1])
    col = lambda off: (lambda bi, i: (bi, i, off // GROUP_WIDTH))
    return pl.pallas_call(
        functools.partial(_mlstm_kernel, L=L),
        grid=(b, s // L),
        in_specs=[pl.BlockSpec((None, L, GROUP_WIDTH), col(COL_B)),
                  pl.BlockSpec((None, L, GROUP_WIDTH), col(COL_B + 256)),
                  pl.BlockSpec((None, L, GROUP_WIDTH), col(COL_B + 512)),
                  pl.BlockSpec((None, L, GROUP_WIDTH), col(COL_B + 768)),
                  pl.BlockSpec((None, L, LANES), lambda bi, i: (bi, i, COL_G // LANES)),
                  pl.BlockSpec((MLSTM_CONV, 2 * GROUP_WIDTH), lambda bi, i: (0, 0)),
                  pl.BlockSpec((1, LANES), lambda bi, i: (0, 0)),
                  pl.BlockSpec((1, GROUP_WIDTH), lambda bi, i: (0, 0)),
                  pl.BlockSpec((GROUP_WIDTH, GROUP_WIDTH), lambda bi, i: (0, 0))],
        out_specs=pl.BlockSpec((None, L, GROUP_WIDTH), lambda bi, i: (bi, i, 0)),
        out_shape=jax.ShapeDtypeStruct((b, s, GROUP_WIDTH), F32),
        scratch_shapes=[pltpu.VMEM((CONV_HALO + L, 2 * GROUP_WIDTH), F32),
                        pltpu.VMEM((GROUP_HEADS, HEAD_DIM, AUG), F32),
                        pltpu.VMEM((1, LANES), F32)],
        compiler_params=_cparams(("arbitrary", "arbitrary")),
        name="mlstm",
    )(p3, p3, p3, p3, p3, conv_w.astype(F32), gb, out_gain.astype(F32).reshape(1, GROUP_WIDTH),
      _block_ones(GROUP_WIDTH, HEAD_DIM))


HGRN_CHUNK = 16
LB_FLOOR = 1e-30


def _hgrn_kernel(q_ref, f_ref, i_ref, g_ref, lb_ref, og_ref, bd_ref, y_ref,
                 lf_s, kk_s, vv_s, st_ref, *, TL):
    ci = pl.program_id(1)
    C = HGRN_CHUNK
    W = GROUP_WIDTH

    @pl.when(ci == 0)
    def _():
        lf_s[0:C, :] = jnp.zeros((C, W), F32)
        kk_s[0:C, :] = jnp.zeros((C, W), F32)
        vv_s[0:C, :] = jnp.zeros((C, W), F32)
        st_ref[...] = jnp.zeros_like(st_ref)

    z = f_ref[...]
    a = lb_ref[0:1, :]
    c = lb_ref[1:2, :] + _log_sigmoid(z)
    mx = jnp.maximum(a, c)
    lf = mx + jnp.log1p(jnp.exp(-jnp.abs(a - c)))
    kk = lb_ref[2:3, :] * _sigmoid(-z) + lb_ref[3:4, :]
    qx = q_ref[...]
    qs = qx * _sigmoid(qx)
    vv = i_ref[...]
    lf_s[C:C + TL, :] = lf
    kk_s[C:C + TL, :] = kk
    vv_s[C:C + TL, :] = vv

    bd = bd_ref[...]
    row = lax.broadcasted_iota(jnp.int32, (TL, W), 0)
    rmod = row & (C - 1)

    acc = jnp.zeros((TL, W), F32)
    dsum = jnp.zeros((TL, W), F32)
    for delta in range(C):
        if delta > 0:
            dsum = dsum + lf_s[C - (delta - 1):C - (delta - 1) + TL, :]
        x = qs * kk_s[C - delta:C - delta + TL, :] * jnp.exp(dsum)
        x = jnp.where(rmod >= delta, x, 0.0)
        att = jnp.dot(x.astype(BF16), bd, preferred_element_type=F32)
        acc = acc + att * vv_s[C - delta:C - delta + TL, :]

    ri = lax.broadcasted_iota(jnp.int32, (TL, TL), 0)
    cj = lax.broadcasted_iota(jnp.int32, (TL, TL), 1)
    same = (ri // C) == (cj // C)
    tril = jnp.where(same & (ri >= cj), 1.0, 0.0).astype(BF16)
    ones = jnp.where(same, 1.0, 0.0).astype(BF16)
    bcum = _tril_sum(lf, tril)
    blast = _tril_sum(lf, ones)
    q_a = (qs * jnp.exp(bcum)).astype(BF16)
    k_b = (kk * jnp.exp(blast - bcum)).astype(BF16)
    dec = jnp.exp(blast)
    vv_b = vv.astype(BF16)
    hmask = bd.astype(F32)
    state = st_ref[...]
    inters = []
    for ch in range(TL // C):
        r = slice(ch * C, (ch + 1) * C)
        inters.append(lax.dot_general(q_a[r], state.astype(BF16), (((1,), (1,)), ((), ())),
                                      preferred_element_type=F32))
        upd = lax.dot_general(vv_b[r], k_b[r], (((0,), (0,)), ((), ())), preferred_element_type=F32)
        state = state * dec[ch * C:ch * C + 1, :] + upd * hmask
    st_ref[...] = state
    o = acc + jnp.concatenate(inters, axis=0)
    ss = _seg_sum(o * o, bd)
    y_ref[...] = o * lax.rsqrt(ss * (1.0 / HEAD_DIM) + EPS) * og_ref[...] * _sigmoid(g_ref[...])


def _hgrn2(p3, lower_bound, out_gain, TL=256):
    b, s, _ = p3.shape
    lb = lower_bound.astype(F32)
    lbf = jnp.maximum(lb, LB_FLOOR)
    lbp = jnp.stack([jnp.log(lbf), jnp.log1p(-lb), 1.0 - lb, lb - lbf])
    col = lambda off: (lambda bi, i: (bi, i, off // GROUP_WIDTH))
    C = HGRN_CHUNK
    return pl.pallas_call(
        functools.partial(_hgrn_kernel, TL=TL),
        grid=(b, s // TL),
        in_specs=[pl.BlockSpec((None, TL, GROUP_WIDTH), col(COL_C)),
                  pl.BlockSpec((None, TL, GROUP_WIDTH), col(COL_C + 256)),
                  pl.BlockSpec((None, TL, GROUP_WIDTH), col(COL_C + 512)),
                  pl.BlockSpec((None, TL, GROUP_WIDTH), col(COL_C + 768)),
                  pl.BlockSpec((4, GROUP_WIDTH), lambda bi, i: (0, 0)),
                  pl.BlockSpec((1, GROUP_WIDTH), lambda bi, i: (0, 0)),
                  pl.BlockSpec((GROUP_WIDTH, GROUP_WIDTH), lambda bi, i: (0, 0))],
        out_specs=pl.BlockSpec((None, TL, GROUP_WIDTH), lambda bi, i: (bi, i, 0)),
        out_shape=jax.ShapeDtypeStruct((b, s, GROUP_WIDTH), F32),
        scratch_shapes=[pltpu.VMEM((C + TL, GROUP_WIDTH), F32),
                        pltpu.VMEM((C + TL, GROUP_WIDTH), F32),
                        pltpu.VMEM((C + TL, GROUP_WIDTH), F32),
                        pltpu.VMEM((GROUP_WIDTH, GROUP_WIDTH), F32)],
        compiler_params=_cparams(("arbitrary", "arbitrary")),
        name="hgrn2",
    )(p3, p3, p3, p3, lbp, out_gain.astype(F32).reshape(1, GROUP_WIDTH), _block_ones(GROUP_WIDTH, HEAD_DIM))


CMP_LEN = 32
CMP_STRIDE = 16
SEL_BLOCK = 64
SEL_TOPK = 16
WINDOW = 512
GATE_ROW = 8


def _rms_rows(x, gain):
    return x * lax.rsqrt(jnp.mean(x * x, axis=-1, keepdims=True) + EPS) * gain


NSA_V_ROWS = (GROUP_HEADS + 1) * V_AUG


def _aug_shared_values(v, w_wide, w_t):
    tm = v.shape[0]
    vw_t = (jnp.concatenate([v] * GROUP_HEADS, axis=1) * w_wide).T
    first = lax.broadcasted_iota(jnp.int32, (V_AUG - HEAD_DIM, tm), 0) == 0
    blocks = []
    for h in range(GROUP_HEADS):
        blocks += [vw_t[HEAD_DIM * h:HEAD_DIM * (h + 1), :],
                   jnp.where(first, jnp.broadcast_to(w_t[h:h + 1, :], first.shape), 0.0)]
    blocks += [v.T, jnp.where(first, 1.0, 0.0)]
    return jnp.concatenate(blocks, axis=0).astype(BF16)


def _nsa_prep_kernel(q_ref, kv_ref, g_ref, qg_ref, kg_ref, bd_ref, w_ref, wt_ref,
                     qt_ref, kc_ref, vc_ref, ks_ref, vst_ref, kw_ref, vwt_ref, gt_ref):
    ti = pl.program_id(1)
    q = q_ref[...]
    ss = _seg_sum(q * q, bd_ref[...])
    qn = q * lax.rsqrt(ss * (1.0 / HEAD_DIM) + EPS) * qg_ref[...] * (HEAD_DIM ** -0.5 * LOG2E)
    qt_ref[...] = qn.T.astype(BF16)
    kv = kv_ref[...]
    tm = kv.shape[0]
    kc_ref[...] = kv[:, 0:64]
    vc_ref[...] = kv[:, 64:128]
    ks = _rms_rows(kv[:, 128:192], kg_ref[1:2, :])
    lane = lax.broadcasted_iota(jnp.int32, (tm, LANES), 1)
    blk = jnp.right_shift(ti * tm + lax.broadcasted_iota(jnp.int32, (tm, LANES), 0), SEL_BLOCK.bit_length() - 1)
    onehot = jnp.where(lane == blk + HEAD_DIM, 1.0, 0.0)
    ks_ref[...] = jnp.where(lane < HEAD_DIM, jnp.concatenate([ks, ks], axis=1), onehot).astype(BF16)
    kw = _rms_rows(kv[:, 256:320], kg_ref[2:3, :])
    kw_ref[...] = jnp.concatenate([kw, jnp.zeros_like(kw)], axis=1).astype(BF16)
    vst_ref[...] = _aug_shared_values(kv[:, 192:256], w_ref[...], wt_ref[...])
    vwt_ref[...] = _aug_shared_values(kv[:, 320:384], w_ref[...], wt_ref[...])
    gt_ref[...] = _sigmoid(g_ref[...]).T


def _gelu_tanh(x):
    return 0.5 * x * (1.0 + jnp.tanh(math.sqrt(2.0 / math.pi) * (x + 0.044715 * x * x * x)))


def _nsa_cmp_kernel(kr_ref, vr_ref, w1_ref, w2_ref, pos_ref, kg_ref, kc_ref, vct_ref, sh_ref, *, n_rows):
    half = CMP_STRIDE * HEAD_DIM
    sh_ref[n_rows:n_rows + 8, :] = jnp.zeros((8, HEAD_DIM), F32)
    outs = []
    for j, x_ref in enumerate((kr_ref, vr_ref)):
        r = x_ref[...].astype(BF16)
        w1 = w1_ref[j]
        first = jnp.dot(r, w1[0:half, :], preferred_element_type=F32)
        sh_ref[0:n_rows, :] = jnp.dot(r, w1[half:2 * half, :], preferred_element_type=F32)
        pos8 = jnp.broadcast_to(pos_ref[j], (8, 2 * half))
        posw = jnp.dot(pos8, w1.astype(F32), preferred_element_type=F32)[0:1, :]
        hdn = _gelu_tanh(first + sh_ref[1:n_rows + 1, :] + posw)
        outs.append(jnp.dot(hdn, w2_ref[j].astype(F32), preferred_element_type=F32))
    kc_ref[...] = _rms_rows(outs[0], kg_ref[0:1, :]).astype(BF16)
    vct_ref[...] = outs[1].T.astype(BF16)


def _nsa_attn_kernel(qt_ref, kc_ref, vct_ref, ks_ref, vst_ref, kw_ref, vwt_ref, gt_ref, ov_ref, db_ref, lo_ref,
                     og_ref, bd_ref, y_ref, qa_ref, acc_s, m_s, acc_w, m_w, ss_ref, sw_ref,
                     *, tq, slopes, n_cmp):
    qi = pl.program_id(1)
    n_rows = kc_ref.shape[0]
    n_sel = ov_ref.shape[0]
    qpos = qi * tq + lax.broadcasted_iota(jnp.int32, (1, tq), 1)

    nio = lax.broadcasted_iota(jnp.int32, (n_rows, tq), 0)
    dist_c = qpos - (nio * CMP_STRIDE + (CMP_LEN - 1))
    valid_c = (dist_c >= 0) & (nio < n_cmp)
    dist_cf = dist_c.astype(F32)
    kc = kc_ref[...]
    vct = vct_ref[...]
    o_cmp = []
    p_sum = jnp.zeros((n_rows, tq), F32)
    for h in range(GROUP_HEADS):
        qh = qt_ref[HEAD_DIM * h:HEAD_DIM * (h + 1), :]
        s = jnp.dot(kc, qh, preferred_element_type=F32) - (slopes[h] * LOG2E) * dist_cf
        s = jnp.where(valid_c, s, MASKED)
        m = jnp.maximum(jnp.max(s, axis=0, keepdims=True), M_INIT)
        e = jnp.exp2(s - m)
        p = e / jnp.maximum(jnp.sum(e, axis=0, keepdims=True), 1e-30)
        p_sum = p_sum + p
        o_cmp.append(jnp.dot(vct, p.astype(BF16), preferred_element_type=F32))

    p_hi = p_sum.astype(BF16)
    p_lo = (p_sum - p_hi.astype(F32)).astype(BF16)
    ov = ov_ref[...]
    imp = jnp.dot(ov, p_hi, preferred_element_type=F32) + jnp.dot(ov, p_lo, preferred_element_type=F32)
    jio = lax.broadcasted_iota(jnp.int32, (n_sel, tq), 0)
    qblk = jnp.right_shift(qpos, SEL_BLOCK.bit_length() - 1)
    imp = jnp.where(jio <= qblk, imp, -1.0)
    imp = jnp.where((jio == 0) | (jio == qblk) | (jio == qblk - 1), 2.0, imp)
    rank = jnp.zeros((n_sel, tq), F32)
    for i in range(n_sel):
        row = imp[i:i + 1, :]
        beats = (row > imp) | ((row == imp) & (jio > i))
        rank = rank + jnp.where(beats, 1.0, 0.0)
    bsel = jnp.where(rank < float(min(SEL_TOPK, n_sel)), 0.0, MASKED)
    if n_sel < LANES - HEAD_DIM:
        bsel = jnp.concatenate([bsel, jnp.zeros((LANES - HEAD_DIM - n_sel, tq), F32)], axis=0)
    for h in range(GROUP_HEADS):
        qa_ref[h, 0:HEAD_DIM, :] = qt_ref[HEAD_DIM * h:HEAD_DIM * (h + 1), :]
        qa_ref[h, HEAD_DIM:LANES, :] = bsel.astype(BF16)
    for ref, val in ((acc_s, 0.0), (acc_w, 0.0), (m_s, M_INIT), (m_w, M_INIT)):
        ref[...] = jnp.full(ref.shape, val, F32)

    def tile_off(kt, h):
        return ((kt - qi) * tq + (tq - 1)).astype(F32) * (slopes[h] * LOG2E)

    def values(vt_ref, kt, h, diag):
        base = GROUP_HEADS * V_AUG if diag else V_AUG * h
        return vt_ref[base:base + V_AUG, pl.ds(pl.multiple_of(kt * tq, tq), tq)]

    def sel_scores(kt, h):
        kb = ks_ref[pl.ds(pl.multiple_of(kt * tq, tq), tq), :]
        ss_ref[h] = jnp.dot(kb, qa_ref[h], preferred_element_type=F32)

    def sel_step(kt, diag):
        def prefetch(h):
            if h + AHEAD < GROUP_HEADS:
                sel_scores(kt, h + AHEAD)
            elif not diag:
                sel_scores(kt + 1, h + AHEAD - GROUP_HEADS)

        _run_chains(GROUP_HEADS, prefetch, lambda h: _flash_update(
            ss_ref.at[h], db_ref.at[h] if diag else None, 0.0 if diag else tile_off(kt, h),
            values(vst_ref, kt, h, diag), m_s.at[h], acc_s.at[h]))

    for h in range(AHEAD):
        sel_scores(0, h)

    def body(kt, carry):
        sel_step(kt, False)
        return carry

    lax.fori_loop(0, qi, body, 0)
    sel_step(qi, True)

    def win_step(back, bias_of, diag):
        kt = jnp.maximum(qi - back, 0)
        penalty = None if diag else jnp.where(qi >= back, 0.0, MASKED)
        kb = kw_ref[pl.ds(pl.multiple_of(kt * tq, tq), tq), :]

        def win_scores(h):
            sw_ref[h] = jnp.dot(kb, qa_ref[h], preferred_element_type=F32)

        for h in range(AHEAD):
            win_scores(h)

        def prefetch(h):
            if h + AHEAD < GROUP_HEADS:
                win_scores(h + AHEAD)

        _run_chains(GROUP_HEADS, prefetch, lambda h: _flash_update(
            sw_ref.at[h], bias_of(h), 0.0 if diag else tile_off(kt, h),
            values(vwt_ref, kt, h, diag), m_w.at[h], acc_w.at[h], penalty))

    win_step(2, lambda h: lo_ref, False)
    win_step(1, lambda h: None, False)
    win_step(0, lambda h: db_ref.at[h], True)

    gt = gt_ref[...]
    heads = []
    for h in range(GROUP_HEADS):
        a_s, a_w = acc_s[h], acc_w[h]
        o_sel = a_s[0:HEAD_DIM, :] / a_s[HEAD_DIM:HEAD_DIM + 1, :]
        o_win = a_w[0:HEAD_DIM, :] / a_w[HEAD_DIM:HEAD_DIM + 1, :]
        g = GATE_ROW + 3 * h
        heads.append(gt[g:g + 1, :] * o_cmp[h] + gt[g + 1:g + 2, :] * o_sel + gt[g + 2:g + 3, :] * o_win)
    o = jnp.concatenate(heads, axis=0).T
    ss = _seg_sum(o * o, bd_ref[...])
    y_ref[...] = o * lax.rsqrt(ss * (1.0 / HEAD_DIM) + EPS) * og_ref[...]


def _nsa(p3, cmp_pos, cmp_w1, cmp_w2, q_gain, k_gain, out_gain, slopes, tq=256):
    b, s, _ = p3.shape
    tm = tq
    n_rows = s // CMP_STRIDE
    n_cmp = (s - CMP_LEN) // CMP_STRIDE + 1
    n_sel = s // SEL_BLOCK
    bd64 = _block_ones(GROUP_WIDTH, HEAD_DIM)
    qg = jnp.tile(q_gain.astype(F32), GROUP_HEADS).reshape(1, GROUP_WIDTH)
    kg = k_gain.astype(F32)
    tok = lambda w: pl.BlockSpec((None, tm, w), lambda bi, i: (bi, i, 0))
    tok_t = lambda w: pl.BlockSpec((None, w, tm), lambda bi, i: (bi, 0, i))
    full = lambda r, c: pl.BlockSpec((r, c), lambda bi, i: (0, 0))
    assert n_sel <= LANES - HEAD_DIM, "block one-hot lanes hold at most 64 selection blocks"
    assert WINDOW == 2 * tq and tq % SEL_BLOCK == 0, "window branch walks exactly the key tiles qi-2, qi-1, qi"
    w_wide, w_t = _key_weights(tm, slopes)
    qt, kc, vc, ks, vst, kw, vwt, gt = pl.pallas_call(
        _nsa_prep_kernel,
        grid=(b, s // tm),
        in_specs=[pl.BlockSpec((None, tm, GROUP_WIDTH), lambda bi, i: (bi, i, COL_D // GROUP_WIDTH)),
                  pl.BlockSpec((None, tm, 384), lambda bi, i: (bi, i, (COL_D + GROUP_WIDTH) // 384)),
                  pl.BlockSpec((None, tm, LANES), lambda bi, i: (bi, i, COL_G // LANES)),
                  full(1, GROUP_WIDTH), full(3, HEAD_DIM), full(GROUP_WIDTH, GROUP_WIDTH),
                  full(tm, GROUP_WIDTH), full(8, tm)],
        out_specs=[tok_t(GROUP_WIDTH), tok(HEAD_DIM), tok(HEAD_DIM), tok(LANES), tok_t(NSA_V_ROWS),
                   tok(LANES), tok_t(NSA_V_ROWS), tok_t(LANES)],
        out_shape=[jax.ShapeDtypeStruct((b, GROUP_WIDTH, s), BF16),
                   jax.ShapeDtypeStruct((b, s, HEAD_DIM), F32),
                   jax.ShapeDtypeStruct((b, s, HEAD_DIM), F32),
                   jax.ShapeDtypeStruct((b, s, LANES), BF16),
                   jax.ShapeDtypeStruct((b, NSA_V_ROWS, s), BF16),
                   jax.ShapeDtypeStruct((b, s, LANES), BF16),
                   jax.ShapeDtypeStruct((b, NSA_V_ROWS, s), BF16),
                   jax.ShapeDtypeStruct((b, LANES, s), F32)],
        compiler_params=_cparams(("arbitrary", "arbitrary")),
        name="nsa_prep",
    )(p3, p3, p3, qg, kg, bd64, w_wide, w_t)

    row_w = CMP_STRIDE * HEAD_DIM
    kcmp, vcmp_t = pl.pallas_call(
        functools.partial(_nsa_cmp_kernel, n_rows=n_rows),
        grid=(b,),
        in_specs=[pl.BlockSpec((None, n_rows, row_w), lambda bi: (bi, 0, 0)),
                  pl.BlockSpec((None, n_rows, row_w), lambda bi: (bi, 0, 0)),
                  pl.BlockSpec((2, 2 * row_w, HEAD_DIM), lambda bi: (0, 0, 0)),
                  pl.BlockSpec((2, HEAD_DIM, HEAD_DIM), lambda bi: (0, 0, 0)),
                  pl.BlockSpec((2, 1, 2 * row_w), lambda bi: (0, 0, 0)),
                  pl.BlockSpec((3, HEAD_DIM), lambda bi: (0, 0))],
        out_specs=[pl.BlockSpec((None, n_rows, HEAD_DIM), lambda bi: (bi, 0, 0)),
                   pl.BlockSpec((None, HEAD_DIM, n_rows), lambda bi: (bi, 0, 0))],
        out_shape=[jax.ShapeDtypeStruct((b, n_rows, HEAD_DIM), BF16),
                   jax.ShapeDtypeStruct((b, HEAD_DIM, n_rows), BF16)],
        scratch_shapes=[pltpu.VMEM((n_rows + 8, HEAD_DIM), F32)],
        compiler_params=_cparams(("arbitrary",)),
        name="nsa_cmp",
    )(kc.reshape(b, n_rows, row_w), vc.reshape(b, n_rows, row_w), cmp_w1.astype(BF16), cmp_w2.astype(BF16),
      cmp_pos.astype(F32).reshape(2, 1, 2 * row_w), kg)

    cs = np.arange(n_rows)[:, None] * CMP_STRIDE
    ss = np.arange(n_sel)[None, :] * SEL_BLOCK
    overlap = np.clip(np.minimum(cs + CMP_LEN, ss + SEL_BLOCK) - np.maximum(cs, ss), 0, None) / CMP_LEN
    overlap[n_cmp:, :] = 0.0
    ov_t = jnp.asarray(overlap.T.astype(np.float32), BF16)

    seq = lambda r, c: pl.BlockSpec((None, r, c), lambda bi, i: (bi, 0, 0))
    kern = functools.partial(_nsa_attn_kernel, tq=tq, slopes=tuple(float(v) for v in slopes), n_cmp=n_cmp)
    return pl.pallas_call(
        kern,
        grid=(b, s // tq),
        in_specs=[pl.BlockSpec((None, GROUP_WIDTH, tq), lambda bi, i: (bi, 0, i)),
                  seq(n_rows, HEAD_DIM), seq(HEAD_DIM, n_rows),
                  seq(s, LANES), seq(NSA_V_ROWS, s), seq(s, LANES), seq(NSA_V_ROWS, s),
                  pl.BlockSpec((None, LANES, tq), lambda bi, i: (bi, 0, i)),
                  full(n_sel, n_rows),
                  pl.BlockSpec((GROUP_HEADS, tq, tq), lambda bi, i: (0, 0, 0)), full(tq, tq),
                  full(1, GROUP_WIDTH), full(GROUP_WIDTH, GROUP_WIDTH)],
        out_specs=pl.BlockSpec((None, tq, GROUP_WIDTH), lambda bi, i: (bi, i, 0)),
        out_shape=jax.ShapeDtypeStruct((b, s, GROUP_WIDTH), F32),
        scratch_shapes=[pltpu.VMEM((GROUP_HEADS, LANES, tq), BF16),
                        pltpu.VMEM((GROUP_HEADS, V_AUG, tq), F32), pltpu.VMEM((GROUP_HEADS, 8, tq), F32),
                        pltpu.VMEM((GROUP_HEADS, V_AUG, tq), F32), pltpu.VMEM((GROUP_HEADS, 8, tq), F32),
                        pltpu.VMEM((GROUP_HEADS, tq, tq), F32), pltpu.VMEM((GROUP_HEADS, tq, tq), F32)],
        compiler_params=_cparams(("arbitrary", "arbitrary")),
        name="nsa_attn",
    )(qt, kcmp, vcmp_t, ks, vst, kw, vwt, gt, ov_t, _diag_bias(tq, slopes), _window_low_bias(tq),
      out_gain.astype(F32).reshape(1, GROUP_WIDTH), bd64)


def _window_low_bias(tq):
    k = np.arange(tq)[:, None]
    q = np.arange(tq)[None, :]
    return jnp.asarray(np.where(k > q, 0.0, MASKED).astype(np.float32))


N_GROUPS = 4
EXPERTS_PER_GROUP = 8
N_EXPERTS = N_GROUPS * EXPERTS_PER_GROUP
D_EXPERT = 256
ROUTER_LANE0 = N_GROUPS


def _split3_dot(a, b_hi, b_lo):
    a_hi = a.astype(BF16)
    a_lo = (a - a_hi.astype(F32)).astype(BF16)
    return (jnp.dot(a_hi, b_hi, preferred_element_type=F32) + jnp.dot(a_lo, b_hi, preferred_element_type=F32)
            + jnp.dot(a_hi, b_lo, preferred_element_type=F32))


def _route(t, wr_hi, wr_lo, br):
    logits = _split3_dot(t, wr_hi, wr_lo) + br
    lane = lax.broadcasted_iota(jnp.int32, logits.shape, 1)
    lane_f = lane.astype(F32)
    big = float(LANES)
    is_g = lane < N_GROUPS
    gl = jnp.where(is_g, logits, MASKED)
    gmax = jnp.max(gl, axis=-1, keepdims=True)
    g_prob = 1.0 / jnp.sum(jnp.where(is_g, jnp.exp(gl - gmax), 0.0), axis=-1, keepdims=True)
    g_sel = jnp.min(jnp.where(is_g & (gl == gmax), lane_f, big), axis=-1, keepdims=True)
    lo = ROUTER_LANE0 + EXPERTS_PER_GROUP * g_sel
    in_grp = (lane_f >= lo) & (lane_f < lo + EXPERTS_PER_GROUP)
    el = jnp.where(in_grp, logits, MASKED)
    m1 = jnp.max(el, axis=-1, keepdims=True)
    i1 = jnp.min(jnp.where(in_grp & (el == m1), lane_f, big), axis=-1, keepdims=True)
    rest = in_grp & (lane_f != i1)
    el2 = jnp.where(rest, logits, MASKED)
    m2 = jnp.max(el2, axis=-1, keepdims=True)
    i2 = jnp.min(jnp.where(rest & (el2 == m2), lane_f, big), axis=-1, keepdims=True)
    r = jnp.exp(m2 - m1)
    w1 = g_prob / (1.0 + r)
    w2 = g_prob * r / (1.0 + r)
    return jnp.where(lane_f == i1, w1, 0.0) + jnp.where(lane_f == i2, w2, 0.0), g_sel


T_AUG = D_MODEL + LANES


def _outproj_kernel(x_ref, ya_ref, yb_ref, yc_ref, yd_ref, w_ref, g_ref, wr_hi_ref, wr_lo_ref, br_ref,
                    xo_ref, tg_ref, route_ref, cnt_ref, run_ref):
    @pl.when(pl.program_id(0) == 0)
    def _():
        run_ref[...] = jnp.zeros_like(run_ref)

    acc = x_ref[...]
    for gi, y_ref in enumerate((ya_ref, yb_ref, yc_ref, yd_ref)):
        acc = acc + jnp.dot(y_ref[...].astype(BF16), w_ref[GROUP_WIDTH * gi:GROUP_WIDTH * (gi + 1), :],
                            preferred_element_type=F32)
    xo_ref[...] = acc
    ms = jnp.mean(acc * acc, axis=-1, keepdims=True)
    t = acc * lax.rsqrt(ms + EPS) * g_ref[...]
    tg_ref[:, 0:D_MODEL] = t
    gate, g_sel = _route(t, wr_hi_ref[...], wr_lo_ref[...], br_ref[...])
    tg_ref[:, D_MODEL:T_AUG] = gate

    tm = t.shape[0]
    lane = lax.broadcasted_iota(jnp.int32, (tm, LANES), 1)
    onehot = jnp.where(lane.astype(F32) == g_sel, 1.0, 0.0)
    ri = lax.broadcasted_iota(jnp.int32, (tm, tm), 0)
    ci = lax.broadcasted_iota(jnp.int32, (tm, tm), 1)
    before = jnp.where(ri > ci, 1.0, 0.0).astype(BF16)
    prefix = jnp.dot(before, onehot.astype(BF16), preferred_element_type=F32)
    rank = jnp.sum(onehot * (run_ref[...] + prefix), axis=-1, keepdims=True)
    route_ref[...] = jnp.where(lane == 0, g_sel, 0.0) + jnp.where(lane == 1, rank, 0.0)
    run_ref[...] = run_ref[...] + jnp.sum(onehot, axis=0, keepdims=True)
    cnt_ref[...] = run_ref[...]


def _outproj(x2d, ys, w_out, ffn_gain, w_group, b_group, w_expert, b_expert, tm=256):
    t = x2d.shape[0]
    wr = jnp.zeros((D_MODEL, LANES), F32).at[:, 0:N_GROUPS].set(w_group).at[:, ROUTER_LANE0:ROUTER_LANE0 + N_EXPERTS].set(w_expert)
    wr_hi = wr.astype(BF16)
    wr_lo = (wr - wr_hi.astype(F32)).astype(BF16)
    br = jnp.zeros((1, LANES), F32).at[0, 0:N_GROUPS].set(b_group).at[0, ROUTER_LANE0:ROUTER_LANE0 + N_EXPERTS].set(b_expert)
    row = lambda w: pl.BlockSpec((tm, w), lambda i: (i, 0))
    full = lambda r, c: pl.BlockSpec((r, c), lambda i: (0, 0))
    return list(pl.pallas_call(
        _outproj_kernel,
        grid=(t // tm,),
        in_specs=[row(D_MODEL), row(GROUP_WIDTH), row(GROUP_WIDTH), row(GROUP_WIDTH), row(GROUP_WIDTH),
                  full(D_MODEL, D_MODEL), full(1, D_MODEL), full(D_MODEL, LANES), full(D_MODEL, LANES), full(1, LANES)],
        out_specs=[row(D_MODEL), row(T_AUG), row(LANES), full(1, LANES)],
        out_shape=[jax.ShapeDtypeStruct((t, D_MODEL), F32),
                   jax.ShapeDtypeStruct((t, T_AUG), F32),
                   jax.ShapeDtypeStruct((t, LANES), F32),
                   jax.ShapeDtypeStruct((1, LANES), F32)],
        scratch_shapes=[pltpu.VMEM((1, LANES), F32)],
        compiler_params=_cparams(("arbitrary",)),
        name="outproj_router",
    )(x2d, *ys, w_out.astype(BF16), ffn_gain.reshape(1, D_MODEL), wr_hi, wr_lo, br))


MOE_TILE = 1024
DMA_PRIORITIES = 2


def _row_copies(n, make_copy):
    def start(k, carry):
        for p in range(DMA_PRIORITIES):
            make_copy(DMA_PRIORITIES * k + p).start(priority=p)
        return carry

    def wait(r, carry):
        make_copy(r).wait()
        return carry

    lax.fori_loop(0, n // DMA_PRIORITIES, start, 0)
    lax.fori_loop(0, n, wait, 0)


def _moe_scatter_kernel(pos_ref, tg_ref, xs_in_ref, xs_ref, sem):
    del xs_in_ref
    tm = tg_ref.shape[0]
    base = pl.program_id(0) * tm
    _row_copies(tm, lambda r: pltpu.make_async_copy(tg_ref.at[pl.ds(r, 1)],
                                                    xs_ref.at[pl.ds(pos_ref[base + r], 1)], sem))


EXPERTS_PER_STEP = 4
MOE_VMEM_LIMIT = 56 * 1024 * 1024


def _moe_expert_kernel(grp_ref, valid_ref, xs_ref, wg_ref, wu_ref, wd_ref, y_ref, acc_ref):
    i, step = pl.program_id(0), pl.program_id(1)

    @pl.when(step == 0)
    def _():
        acc_ref[...] = jnp.zeros_like(acc_ref)

    @pl.when(valid_ref[i] == 1)
    def _():
        x = xs_ref[:, 0:D_MODEL].astype(BF16)
        gate = xs_ref[:, D_MODEL:T_AUG]
        lane = lax.broadcasted_iota(jnp.int32, gate.shape, 1)
        total = None
        for k in range(EXPERTS_PER_STEP):
            col = ROUTER_LANE0 + grp_ref[i] * EXPERTS_PER_GROUP + step * EXPERTS_PER_STEP + k
            w = jnp.sum(jnp.where(lane == col, gate, 0.0), axis=-1, keepdims=True)
            a = jnp.dot(x, wg_ref[k].astype(BF16), preferred_element_type=F32)
            u = jnp.dot(x, wu_ref[k].astype(BF16), preferred_element_type=F32)
            act = a * _sigmoid(a) * u * w
            down = jnp.dot(act.astype(BF16), wd_ref[k].astype(BF16), preferred_element_type=F32)
            total = down if total is None else total + down
        acc_ref[...] += total

    @pl.when(step == EXPERTS_PER_GROUP // EXPERTS_PER_STEP - 1)
    def _():
        y_ref[...] = acc_ref[...]


def _moe_combine_kernel(pos_ref, x_ref, ys_ref, o_ref, buf_ref, sem):
    tm = x_ref.shape[0]
    base = pl.program_id(0) * tm
    _row_copies(tm, lambda r: pltpu.make_async_copy(ys_ref.at[pl.ds(pos_ref[base + r], 1)],
                                                    buf_ref.at[pl.ds(r, 1)], sem))
    o_ref[...] = x_ref[...] + buf_ref[...]


def _moe_routed(x2d, tg, route, counts, w_gate, w_up, w_down, tm=512):
    t = x2d.shape[0]
    te = MOE_TILE
    n_tiles = t // te + N_GROUPS
    n_rows = n_tiles * te
    grp = route[:, 0].astype(jnp.int32)
    rank = route[:, 1].astype(jnp.int32)
    cnt = counts[0, 0:N_GROUPS].astype(jnp.int32)
    padded = ((cnt + te - 1) // te) * te
    ends = jnp.cumsum(padded)
    pos = (ends - padded)[grp] + rank
    starts = jnp.arange(n_tiles, dtype=jnp.int32) * te
    tile_grp = jnp.minimum(jnp.sum((starts[:, None] >= ends[None, :]).astype(jnp.int32), axis=1), N_GROUPS - 1)
    tile_valid = (starts < ends[-1]).astype(jnp.int32)

    xs = pl.pallas_call(
        _moe_scatter_kernel,
        grid_spec=pltpu.PrefetchScalarGridSpec(
            num_scalar_prefetch=1, grid=(t // tm,),
            in_specs=[pl.BlockSpec((tm, T_AUG), lambda i, pos: (i, 0)), pl.BlockSpec(memory_space=pl.ANY)],
            out_specs=pl.BlockSpec(memory_space=pl.ANY),
            scratch_shapes=[pltpu.SemaphoreType.DMA(())]),
        out_shape=jax.ShapeDtypeStruct((n_rows, T_AUG), F32),
        input_output_aliases={2: 0},
        compiler_params=_cparams(("arbitrary",)),
        name="moe_scatter",
    )(pos, tg, jnp.zeros((n_rows, T_AUG), F32))

    steps = EXPERTS_PER_GROUP // EXPERTS_PER_STEP
    wg = w_gate.reshape(N_GROUPS * steps, EXPERTS_PER_STEP, D_MODEL, D_EXPERT)
    wu = w_up.reshape(N_GROUPS * steps, EXPERTS_PER_STEP, D_MODEL, D_EXPERT)
    wd = w_down.reshape(N_GROUPS * steps, EXPERTS_PER_STEP, D_EXPERT, D_MODEL)
    wsel = lambda i, e, g, v: (g[i] * steps + e, 0, 0, 0)
    ys = pl.pallas_call(
        _moe_expert_kernel,
        grid_spec=pltpu.PrefetchScalarGridSpec(
            num_scalar_prefetch=2, grid=(n_tiles, steps),
            in_specs=[pl.BlockSpec((te, T_AUG), lambda i, e, g, v: (i, 0)),
                      pl.BlockSpec((None, EXPERTS_PER_STEP, D_MODEL, D_EXPERT), wsel),
                      pl.BlockSpec((None, EXPERTS_PER_STEP, D_MODEL, D_EXPERT), wsel),
                      pl.BlockSpec((None, EXPERTS_PER_STEP, D_EXPERT, D_MODEL), wsel)],
            out_specs=pl.BlockSpec((te, D_MODEL), lambda i, e, g, v: (i, 0)),
            scratch_shapes=[pltpu.VMEM((te, D_MODEL), F32)]),
        out_shape=jax.ShapeDtypeStruct((n_rows, D_MODEL), F32),
        compiler_params=_cparams(("arbitrary", "arbitrary"), vmem_limit=MOE_VMEM_LIMIT),
        name="moe_experts",
    )(tile_grp, tile_valid, xs, wg, wu, wd)

    return pl.pallas_call(
        _moe_combine_kernel,
        grid_spec=pltpu.PrefetchScalarGridSpec(
            num_scalar_prefetch=1, grid=(t // tm,),
            in_specs=[pl.BlockSpec((tm, D_MODEL), lambda i, pos: (i, 0)), pl.BlockSpec(memory_space=pl.ANY)],
            out_specs=pl.BlockSpec((tm, D_MODEL), lambda i, pos: (i, 0)),
            scratch_shapes=[pltpu.VMEM((tm, D_MODEL), F32), pltpu.SemaphoreType.DMA(())]),
        out_shape=jax.ShapeDtypeStruct((t, D_MODEL), F32),
        compiler_params=_cparams(("arbitrary",)),
        name="moe_combine",
    )(pos, x2d, ys)


def _alibi_slopes():
    n = 2 * GROUP_HEADS
    s = 2.0 ** (-8.0 * np.arange(1, n + 1) / n)
    return s[0::2], s[1::2]


def kernel(x, norm_mix, norm_ffn, w_in, w_out, diff_q_gain, diff_k_gain, diff_lambda, diff_sub_gain, mlstm_conv, mlstm_gate_bias, mlstm_out_gain, hgrn_lower_bounds, hgrn_out_gain, nsa_cmp_pos, nsa_cmp_w1, nsa_cmp_w2, nsa_q_gain, nsa_k_gain, nsa_out_gain, moe_w_group, moe_b_group, moe_w_expert, moe_b_expert, moe_w_gate, moe_w_up, moe_w_down):
    b, s, d = x.shape
    slopes_diff, slopes_nsa = _alibi_slopes()
    lb_soft = jax.nn.softmax(hgrn_lower_bounds.astype(F32), axis=0)
    lower_bounds = jnp.cumsum(lb_soft, axis=0) - lb_soft[0]
    x2d = x.reshape(b * s, d)
    for l in range(norm_mix.shape[0]):
        p = _inproj(x2d, norm_mix[l], _pack_w_in(w_in[l])).reshape(b, s, P_COLS)
        y_a = _diff_attention(p, diff_q_gain[l], diff_k_gain[l], diff_lambda[l], diff_sub_gain[l], slopes_diff, l)
        y_b = _mlstm(p, mlstm_conv[l], mlstm_gate_bias[l], mlstm_out_gain[l])
        y_c = _hgrn2(p, lower_bounds[l], hgrn_out_gain[l])
        y_d = _nsa(p, nsa_cmp_pos[l], nsa_cmp_w1[l], nsa_cmp_w2[l], nsa_q_gain[l], nsa_k_gain[l],
                   nsa_out_gain[l], slopes_nsa)
        ys = [y.reshape(b * s, GROUP_WIDTH) for y in (y_a, y_b, y_c, y_d)]
        x2d, tg, route, counts = _outproj(x2d, ys, w_out[l], norm_ffn[l], moe_w_group[l], moe_b_group[l],
                                          moe_w_expert[l], moe_b_expert[l])
        x2d = _moe_routed(x2d, tg, route, counts, moe_w_gate[l], moe_w_up[l], moe_w_down[l])
    return x2d.reshape(b, s, d)
```

```python
import functools
import math

import numpy as np
import jax
import jax.numpy as jnp
from jax import lax
from jax.experimental import pallas as pl
from jax.experimental.pallas import tpu as pltpu

F32 = jnp.float32
BF16 = jnp.bfloat16

D_MODEL = 1024
HEAD_DIM = 64
GROUP_HEADS = 4
GROUP_WIDTH = GROUP_HEADS * HEAD_DIM
DIFF_HALF = HEAD_DIM // 2
EPS = 1e-6
NEG_INF = -1e30
LOG2E = math.log2(math.e)
M_INIT = -1e30
MASKED = -2e30

LANES = 128
VMEM_LIMIT = 48 * 1024 * 1024

COL_A = 0
COL_B = 768
COL_C = 1792
COL_D = 2816
COL_G = 3456
P_COLS = 3584


def _cparams(sem, flags=None, vmem_limit=VMEM_LIMIT):
    return pltpu.CompilerParams(dimension_semantics=sem, vmem_limit_bytes=vmem_limit, flags=flags)


def _block_ones(width, seg):
    i = np.arange(width)
    return jnp.asarray((i[:, None] // seg == i[None, :] // seg).astype(np.float32), BF16)


def _seg_sum(x, bd):
    hi = x.astype(BF16)
    lo = (x - hi.astype(F32)).astype(BF16)
    return (jnp.dot(hi, bd, preferred_element_type=F32) + jnp.dot(lo, bd, preferred_element_type=F32))


def _inproj_kernel(x_ref, g_ref, w_ref, o_ref):
    x = x_ref[...]
    ms = jnp.mean(x * x, axis=-1, keepdims=True)
    h = (x * lax.rsqrt(ms + EPS) * g_ref[...]).astype(BF16)
    o_ref[...] = jnp.dot(h, w_ref[...], preferred_element_type=F32)


def _inproj(x2d, gain, w_cat, tm=256):
    t = x2d.shape[0]
    return pl.pallas_call(
        _inproj_kernel,
        grid=(t // tm,),
        in_specs=[pl.BlockSpec((tm, D_MODEL), lambda i: (i, 0)),
                  pl.BlockSpec((1, D_MODEL), lambda i: (0, 0)),
                  pl.BlockSpec((D_MODEL, P_COLS), lambda i: (0, 0))],
        out_specs=pl.BlockSpec((tm, P_COLS), lambda i: (i, 0)),
        out_shape=jax.ShapeDtypeStruct((t, P_COLS), F32),
        compiler_params=_cparams(("arbitrary",)),
        name="inproj",
    )(x2d, gain.reshape(1, D_MODEL), w_cat)


IN_COLS = 3476


def _pack_w_in_kernel(w_ref, o_ref):
    w = w_ref[...]
    a_b = w[:, 0:1536]
    gates_b = w[:, 1536:1544]
    rest = w[:, 1544:3464]
    gates_d = w[:, 3464:3476]
    pad = jnp.zeros((w.shape[0], P_COLS - COL_G - 20), F32)
    o_ref[...] = jnp.concatenate([a_b, rest, gates_b, gates_d, pad], axis=1).astype(BF16)


def _pack_w_in(w, tr=128):
    d = w.shape[0]
    return pl.pallas_call(
        _pack_w_in_kernel,
        grid=(d // tr,),
        in_specs=[pl.BlockSpec((tr, IN_COLS), lambda i: (i, 0))],
        out_specs=pl.BlockSpec((tr, P_COLS), lambda i: (i, 0)),
        out_shape=jax.ShapeDtypeStruct((d, P_COLS), BF16),
        compiler_params=_cparams(("arbitrary",)),
        name="pack_w_in",
    )(w)


def _diff_prep_kernel(p_ref, qg_ref, kg_ref, bd_ref, w_ref, wt_ref, q_ref, k_ref, vt_ref):
    p = p_ref[...]
    bd = bd_ref[...]

    def norm(x, g):
        ss = _seg_sum(x * x, bd)
        return x * lax.rsqrt(ss * (1.0 / DIFF_HALF) + EPS) * g

    q = norm(p[:, 0:256], qg_ref[...]) * (DIFF_HALF ** -0.5 * LOG2E)
    k = norm(p[:, 256:512], kg_ref[...])
    q_ref[...] = q.T.astype(BF16)
    k_ref[...] = k.astype(BF16)
    v = p[:, 512:768]
    v_t = v.T
    vw_t = (v * w_ref[...]).T
    w_t = wt_ref[...]
    tm = v_t.shape[1]
    first = lax.broadcasted_iota(jnp.int32, (V_AUG - HEAD_DIM, tm), 0) == 0
    blocks = []
    for h in range(GROUP_HEADS):
        hs = slice(HEAD_DIM * h, HEAD_DIM * (h + 1))
        blocks += [vw_t[hs, :], jnp.where(first, jnp.broadcast_to(w_t[h:h + 1, :], first.shape), 0.0),
                   v_t[hs, :], jnp.where(first, 1.0, 0.0)]
    vt_ref[...] = jnp.concatenate(blocks, axis=0).astype(BF16)


V_AUG = 80
V_HEAD = 2 * V_AUG


def _key_weights(tile, slopes):
    kl = np.arange(tile, dtype=np.float64) - (tile - 1)
    w = np.stack([np.exp2(sl * LOG2E * kl) for sl in slopes])
    wide = np.repeat(w.T, HEAD_DIM, axis=1)
    w8 = np.zeros((8, tile)); w8[:len(slopes)] = w
    return jnp.asarray(wide.astype(np.float32)), jnp.asarray(w8.astype(np.float32))


def _diag_bias(tq, slopes):
    k = np.arange(tq)[:, None]
    q = np.arange(tq)[None, :]
    tabs = [np.where(k <= q, sl * LOG2E * k.astype(np.float64), MASKED) for sl in slopes]
    return jnp.asarray(np.stack(tabs).astype(np.float32))


AHEAD = 2


def _flash_update(s_ref, bias_ref, off, vt, m_ref, acc_ref, penalty=None):
    tq = s_ref.shape[1]
    ps, olds, news = [], [], []
    for c in range(tq // LANES):
        cols = slice(LANES * c, LANES * (c + 1))
        s = s_ref[:, cols]
        if bias_ref is not None:
            s = s + bias_ref[:, cols]
        if penalty is not None:
            s = s + penalty
        m_tile = jnp.max(s, axis=0, keepdims=True)
        ps.append(jnp.exp2(s - m_tile).astype(BF16))
        m_old = m_ref[0:1, cols]
        m_new = jnp.maximum(m_old, m_tile + off)
        m_ref[0:1, cols] = m_new
        olds.append(jnp.exp2(m_old - m_new))
        news.append(jnp.exp2(m_tile + off - m_new))
    p, old, new = jnp.concatenate(ps, axis=1), jnp.concatenate(olds, axis=1), jnp.concatenate(news, axis=1)

    def finish():
        pv = jnp.dot(vt, p, preferred_element_type=F32)
        acc_ref[...] = old * acc_ref[...] + new * pv

    return finish


def _run_chains(n, prefetch, softmax):
    finish = None
    for c in range(n):
        prefetch(c)
        nxt = softmax(c)
        if finish is not None:
            finish()
        finish = nxt
    finish()


def _diff_attn_kernel(lam_ref, qt_ref, k_ref, vt_ref, db_ref, sg_ref, bd_ref, o_ref,
                      qm_ref, acc_ref, m_ref, s_ref, *, tq, slopes, out_scale):
    qi = pl.program_id(1)
    n_half = 2 * GROUP_HEADS
    per_tile = LANES // DIFF_HALF
    row = lax.broadcasted_iota(jnp.int32, (LANES, tq), 0)
    for j in range(n_half):
        slab = qt_ref[LANES * (j // per_tile):LANES * (j // per_tile + 1), :].astype(F32)
        r0 = DIFF_HALF * (j % per_tile)
        qm_ref[j] = jnp.where((row >= r0) & (row < r0 + DIFF_HALF), slab, 0.0).astype(BF16)
    acc_ref[...] = jnp.zeros(acc_ref.shape, F32)
    m_ref[...] = jnp.full(m_ref.shape, M_INIT, F32)

    n_slots = s_ref.shape[0]

    def scores(kt, j):
        start = pl.multiple_of(kt * tq, tq)
        kj = k_ref[pl.ds(start, tq), LANES * (j // per_tile):LANES * (j // per_tile + 1)]
        s_ref[j % n_slots] = jnp.dot(kj, qm_ref[j], preferred_element_type=F32)

    def process(kt, j, diag):
        h = j // 2
        start = pl.multiple_of(kt * tq, tq)
        off = 0.0 if diag else ((kt - qi) * tq + (tq - 1)).astype(F32) * (slopes[h] * LOG2E)
        base = V_HEAD * h + (V_AUG if diag else 0)
        vt = vt_ref[base:base + V_AUG, pl.ds(start, tq)]
        return _flash_update(s_ref.at[j % n_slots], db_ref.at[h] if diag else None, off, vt, m_ref.at[j],
                             acc_ref.at[j])

    def step(kt, diag):
        def prefetch(j):
            if j + AHEAD < n_half:
                scores(kt, j + AHEAD)
            elif not diag:
                scores(kt + 1, j + AHEAD - n_half)

        _run_chains(n_half, prefetch, lambda j: process(kt, j, diag))

    for j in range(AHEAD):
        scores(0, j)

    def body(kt, carry):
        step(kt, False)
        return carry

    lax.fori_loop(0, qi, body, 0)
    step(qi, True)

    lam = lam_ref[0, 0]
    heads = []
    for h in range(GROUP_HEADS):
        a0, a1 = acc_ref[2 * h], acc_ref[2 * h + 1]
        o0 = a0[0:HEAD_DIM, :] / a0[HEAD_DIM:HEAD_DIM + 1, :]
        o1 = a1[0:HEAD_DIM, :] / a1[HEAD_DIM:HEAD_DIM + 1, :]
        heads.append(o0 - lam * o1)
    o = jnp.concatenate(heads, axis=0).T
    ss = _seg_sum(o * o, bd_ref[...])
    o_ref[...] = o * lax.rsqrt(ss * (1.0 / HEAD_DIM) + EPS) * (sg_ref[...] * out_scale)


def _diff_attention(p3, q_gain, k_gain, lam_vecs, sub_gain, slopes, layer_idx, tq=256):
    b, s, _ = p3.shape
    tm = tq
    bd32 = _block_ones(GROUP_WIDTH, DIFF_HALF)
    bd64 = _block_ones(GROUP_WIDTH, HEAD_DIM)
    qg = jnp.tile(q_gain.astype(F32), 2 * GROUP_HEADS).reshape(1, GROUP_WIDTH)
    kg = jnp.tile(k_gain.astype(F32), 2 * GROUP_HEADS).reshape(1, GROUP_WIDTH)
    w_wide, w_t = _key_weights(tm, slopes)
    v_rows = GROUP_HEADS * V_HEAD
    qn, kn, vt = pl.pallas_call(
        _diff_prep_kernel,
        grid=(b, s // tm),
        in_specs=[pl.BlockSpec((None, tm, 768), lambda bi, i: (bi, i, COL_A // 768)),
                  pl.BlockSpec((1, GROUP_WIDTH), lambda bi, i: (0, 0)),
                  pl.BlockSpec((1, GROUP_WIDTH), lambda bi, i: (0, 0)),
                  pl.BlockSpec((GROUP_WIDTH, GROUP_WIDTH), lambda bi, i: (0, 0)),
                  pl.BlockSpec((tm, GROUP_WIDTH), lambda bi, i: (0, 0)),
                  pl.BlockSpec((8, tm), lambda bi, i: (0, 0))],
        out_specs=[pl.BlockSpec((None, GROUP_WIDTH, tm), lambda bi, i: (bi, 0, i)),
                   pl.BlockSpec((None, tm, GROUP_WIDTH), lambda bi, i: (bi, i, 0)),
                   pl.BlockSpec((None, v_rows, tm), lambda bi, i: (bi, 0, i))],
        out_shape=[jax.ShapeDtypeStruct((b, GROUP_WIDTH, s), BF16),
                   jax.ShapeDtypeStruct((b, s, GROUP_WIDTH), BF16),
                   jax.ShapeDtypeStruct((b, v_rows, s), BF16)],
        compiler_params=_cparams(("arbitrary", "arbitrary")),
        name="diff_prep",
    )(p3, qg, kg, bd32, w_wide, w_t)

    lam_init = 0.8 - 0.6 * math.exp(-0.3 * layer_idx)
    lv = lam_vecs.astype(F32)
    lam = (jnp.exp(jnp.dot(lv[0], lv[1])) - jnp.exp(jnp.dot(lv[2], lv[3])) + lam_init).reshape(1, 1)
    kern = functools.partial(_diff_attn_kernel, tq=tq, slopes=tuple(float(v) for v in slopes),
                             out_scale=1.0 - lam_init)
    return pl.pallas_call(
        kern,
        grid=(b, s // tq),
        in_specs=[pl.BlockSpec(memory_space=pltpu.SMEM),
                  pl.BlockSpec((None, GROUP_WIDTH, tq), lambda bi, i: (bi, 0, i)),
                  pl.BlockSpec((None, s, GROUP_WIDTH), lambda bi, i: (bi, 0, 0)),
                  pl.BlockSpec((None, v_rows, s), lambda bi, i: (bi, 0, 0)),
                  pl.BlockSpec((GROUP_HEADS, tq, tq), lambda bi, i: (0, 0, 0)),
                  pl.BlockSpec((1, GROUP_WIDTH), lambda bi, i: (0, 0)),
                  pl.BlockSpec((GROUP_WIDTH, GROUP_WIDTH), lambda bi, i: (0, 0))],
        out_specs=pl.BlockSpec((None, tq, GROUP_WIDTH), lambda bi, i: (bi, i, 0)),
        out_shape=jax.ShapeDtypeStruct((b, s, GROUP_WIDTH), F32),
        scratch_shapes=[pltpu.VMEM((2 * GROUP_HEADS, LANES, tq), BF16),
                        pltpu.VMEM((2 * GROUP_HEADS, V_AUG, tq), F32),
                        pltpu.VMEM((2 * GROUP_HEADS, 8, tq), F32),
                        pltpu.VMEM((2 * AHEAD, tq, tq), F32)],
        compiler_params=_cparams(("arbitrary", "arbitrary")),
        name="diff_attn",
    )(lam, qn, kn, vt, _diag_bias(tq, slopes), sub_gain.astype(F32).reshape(1, GROUP_WIDTH), bd64)


MLSTM_CONV = 4
CONV_HALO = 8
AUG = 128


def _tril_sum(x, tril):
    hi = x.astype(BF16)
    lo = (x - hi.astype(F32)).astype(BF16)
    return jnp.dot(tril, hi, preferred_element_type=F32) + jnp.dot(tril, lo, preferred_element_type=F32)


def _log_sigmoid(x):
    return jnp.minimum(x, 0.0) - jnp.log1p(jnp.exp(-jnp.abs(x)))


def _sigmoid(x):
    return 1.0 / (1.0 + jnp.exp(-x))


def _mlstm_kernel(q_ref, k_ref, v_ref, o_ref, g_ref, cw_ref, gb_ref, og_ref, bd_ref, y_ref,
                  halo_ref, c_ref, m_ref, *, L):
    ci = pl.program_id(1)

    @pl.when(ci == 0)
    def _():
        halo_ref[0:CONV_HALO, :] = jnp.zeros((CONV_HALO, 2 * GROUP_WIDTH), F32)
        c_ref[...] = jnp.zeros_like(c_ref)
        m_ref[...] = jnp.zeros_like(m_ref)

    halo_ref[CONV_HALO:CONV_HALO + L, 0:GROUP_WIDTH] = q_ref[...]
    halo_ref[CONV_HALO:CONV_HALO + L, GROUP_WIDTH:2 * GROUP_WIDTH] = k_ref[...]
    conv = jnp.zeros((L, 2 * GROUP_WIDTH), F32)
    for j in range(MLSTM_CONV):
        start = CONV_HALO - (MLSTM_CONV - 1) + j
        conv = conv + halo_ref[start:start + L, :] * cw_ref[j:j + 1, :]
    halo_ref[0:CONV_HALO, :] = halo_ref[L:L + CONV_HALO, :]
    qk = conv * _sigmoid(conv)
    q = qk[:, 0:GROUP_WIDTH] * (HEAD_DIM ** -0.5)
    k = qk[:, GROUP_WIDTH:2 * GROUP_WIDTH]
    v = v_ref[...]

    gates = g_ref[...] + gb_ref[...]
    ri = lax.broadcasted_iota(jnp.int32, (L, L), 0)
    cj = lax.broadcasted_iota(jnp.int32, (L, L), 1)
    causal = ri >= cj
    tril = jnp.where(causal, 1.0, 0.0).astype(BF16)
    gcum = _tril_sum(_log_sigmoid(gates), tril)
    gcum_t = gcum.T
    gates_t = gates.T
    lane_aug = lax.broadcasted_iota(jnp.int32, (L, AUG), 1)
    m_all = m_ref[...]

    outs = []
    for h in range(GROUP_HEADS):
        hs = slice(HEAD_DIM * h, HEAD_DIM * (h + 1))
        q_h, k_h = q[:, hs], k[:, hs]
        v_aug = jnp.where(lane_aug == HEAD_DIM, 1.0,
                          jnp.concatenate([v[:, hs], jnp.zeros((L, AUG - HEAD_DIM), F32)], axis=1))
        g_col = gcum[:, 4 + h:5 + h]
        li_col = gates[:, h:h + 1]
        g_row = gcum_t[4 + h:5 + h, :]
        li_row = gates_t[h:h + 1, :]
        g_last = g_row[:, L - 1:L]
        m_prev = m_all[:, h:h + 1]
        log_d = jnp.where(causal, g_col - g_row + li_row, NEG_INF)
        log_inter = g_col + m_prev
        m_t = jnp.maximum(log_inter, jnp.max(log_d, axis=1, keepdims=True))
        s_qk = lax.dot_general(q_h.astype(BF16), k_h.astype(BF16), (((1,), (1,)), ((), ())),
                               preferred_element_type=F32)
        w_intra = s_qk * jnp.exp(log_d - m_t)
        w_inter = jnp.exp(log_inter - m_t)
        c_aug = c_ref[h]
        num = (w_inter * jnp.dot(q_h.astype(BF16), c_aug.astype(BF16), preferred_element_type=F32)
               + jnp.dot(w_intra.astype(BF16), v_aug.astype(BF16), preferred_element_type=F32))
        den = num[:, HEAD_DIM:HEAD_DIM + 1]
        outs.append(num[:, 0:HEAD_DIM] / jnp.maximum(jnp.abs(den), jnp.exp(-m_t)))

        log_a = g_last - g_col + li_col
        m_new = jnp.maximum(g_last + m_prev, jnp.max(log_a, axis=0, keepdims=True))
        a_col = jnp.exp(log_a - m_new)
        decay = jnp.exp(g_last + m_prev - m_new)
        ak_t = (k_h * a_col).T.astype(BF16)
        c_ref[h] = decay * c_aug + jnp.dot(ak_t, v_aug.astype(BF16), preferred_element_type=F32)
        m_ref[:, h:h + 1] = m_new

    hcat = jnp.concatenate(outs, axis=1)
    ss = _seg_sum(hcat * hcat, bd_ref[...])
    y_ref[...] = hcat * lax.rsqrt(ss * (1.0 / HEAD_DIM) + EPS) * og_ref[...] * _sigmoid(o_ref[...])


def _mlstm(p3, conv_w, gate_bias, out_gain, L=512):
    b, s, _ = p3.shape
    gb = jnp.zeros((1, LANES), F32).at[0, 0:GROUP_HEADS].set(gate_bias[0]).at[0, GROUP_HEADS:2 * GROUP_HEADS].set(gate_bias[1])
    col = lambda off: (lambda bi, i: (bi, i, off // GROUP_WIDTH))
    return pl.pallas_call(
        functools.partial(_mlstm_kernel, L=L),
        grid=(b, s // L),
        in_specs=[pl.BlockSpec((None, L, GROUP_WIDTH), col(COL_B)),
                  pl.BlockSpec((None, L, GROUP_WIDTH), col(COL_B + 256)),
                  pl.BlockSpec((None, L, GROUP_WIDTH), col(COL_B + 512)),
                  pl.BlockSpec((None, L, GROUP_WIDTH), col(COL_B + 768)),
                  pl.BlockSpec((None, L, LANES), lambda bi, i: (bi, i, COL_G // LANES)),
                  pl.BlockSpec((MLSTM_CONV, 2 * GROUP_WIDTH), lambda bi, i: (0, 0)),
                  pl.BlockSpec((1, LANES), lambda bi, i: (0, 0)),
                  pl.BlockSpec((1, GROUP_WIDTH), lambda bi, i: (0, 0)),
                  pl.BlockSpec((GROUP_WIDTH, GROUP_WIDTH), lambda bi, i: (0, 0))],
        out_specs=pl.BlockSpec((None, L, GROUP_WIDTH), lambda bi, i: (bi, i, 0)),
        out_shape=jax.ShapeDtypeStruct((b, s, GROUP_WIDTH), F32),
        scratch_shapes=[pltpu.VMEM((CONV_HALO + L, 2 * GROUP_WIDTH), F32),
                        pltpu.VMEM((GROUP_HEADS, HEAD_DIM, AUG), F32),
                        pltpu.VMEM((1, LANES), F32)],
        compiler_params=_cparams(("arbitrary", "arbitrary")),
        name="mlstm",
    )(p3, p3, p3, p3, p3, conv_w.astype(F32), gb, out_gain.astype(F32).reshape(1, GROUP_WIDTH),
      _block_ones(GROUP_WIDTH, HEAD_DIM))


HGRN_CHUNK = 16
LB_FLOOR = 1e-30


def _hgrn_kernel(q_ref, f_ref, i_ref, g_ref, lb_ref, og_ref, bd_ref, y_ref,
                 lf_s, kk_s, vv_s, st_ref, *, TL):
    ci = pl.program_id(1)
    C = HGRN_CHUNK
    W = GROUP_WIDTH

    @pl.when(ci == 0)
    def _():
        lf_s[0:C, :] = jnp.zeros((C, W), F32)
        kk_s[0:C, :] = jnp.zeros((C, W), F32)
        vv_s[0:C, :] = jnp.zeros((C, W), F32)
        st_ref[...] = jnp.zeros_like(st_ref)

    z = f_ref[...]
    a = lb_ref[0:1, :]
    c = lb_ref[1:2, :] + _log_sigmoid(z)
    mx = jnp.maximum(a, c)
    lf = mx + jnp.log1p(jnp.exp(-jnp.abs(a - c)))
    kk = lb_ref[2:3, :] * _sigmoid(-z) + lb_ref[3:4, :]
    qx = q_ref[...]
    qs = qx * _sigmoid(qx)
    vv = i_ref[...]
    lf_s[C:C + TL, :] = lf
    kk_s[C:C + TL, :] = kk
    vv_s[C:C + TL, :] = vv

    bd = bd_ref[...]
    row = lax.broadcasted_iota(jnp.int32, (TL, W), 0)
    rmod = row & (C - 1)

    acc = jnp.zeros((TL, W), F32)
    dsum = jnp.zeros((TL, W), F32)
    for delta in range(C):
        if delta > 0:
            dsum = dsum + lf_s[C - (delta - 1):C - (delta - 1) + TL, :]
        x = qs * kk_s[C - delta:C - delta + TL, :] * jnp.exp(dsum)
        x = jnp.where(rmod >= delta, x, 0.0)
        att = jnp.dot(x.astype(BF16), bd, preferred_element_type=F32)
        acc = acc + att * vv_s[C - delta:C - delta + TL, :]

    ri = lax.broadcasted_iota(jnp.int32, (TL, TL), 0)
    cj = lax.broadcasted_iota(jnp.int32, (TL, TL), 1)
    same = (ri // C) == (cj // C)
    tril = jnp.where(same & (ri >= cj), 1.0, 0.0).astype(BF16)
    ones = jnp.where(same, 1.0, 0.0).astype(BF16)
    bcum = _tril_sum(lf, tril)
    blast = _tril_sum(lf, ones)
    q_a = (qs * jnp.exp(bcum)).astype(BF16)
    k_b = (kk * jnp.exp(blast - bcum)).astype(BF16)
    dec = jnp.exp(blast)
    vv_b = vv.astype(BF16)
    hmask = bd.astype(F32)
    state = st_ref[...]
    inters = []
    for ch in range(TL // C):
        r = slice(ch * C, (ch + 1) * C)
        inters.append(lax.dot_general(q_a[r], state.astype(BF16), (((1,), (1,)), ((), ())),
                                      preferred_element_type=F32))
        upd = lax.dot_general(vv_b[r], k_b[r], (((0,), (0,)), ((), ())), preferred_element_type=F32)
        state = state * dec[ch * C:ch * C + 1, :] + upd * hmask
    st_ref[...] = state
    o = acc + jnp.concatenate(inters, axis=0)
    ss = _seg_sum(o * o, bd)
    y_ref[...] = o * lax.rsqrt(ss * (1.0 / HEAD_DIM) + EPS) * og_ref[...] * _sigmoid(g_ref[...])


def _hgrn2(p3, lower_bound, out_gain, TL=256):
    b, s, _ = p3.shape
    lb = lower_bound.astype(F32)
    lbf = jnp.maximum(lb, LB_FLOOR)
    lbp = jnp.stack([jnp.log(lbf), jnp.log1p(-lb), 1.0 - lb, lb - lbf])
    col = lambda off: (lambda bi, i: (bi, i, off // GROUP_WIDTH))
    C = HGRN_CHUNK
    return pl.pallas_call(
        functools.partial(_hgrn_kernel, TL=TL),
        grid=(b, s // TL),
        in_specs=[pl.BlockSpec((None, TL, GROUP_WIDTH), col(COL_C)),
                  pl.BlockSpec((None, TL, GROUP_WIDTH), col(COL_C + 256)),
                  pl.BlockSpec((None, TL, GROUP_WIDTH), col(COL_C + 512)),
                  pl.BlockSpec((None, TL, GROUP_WIDTH), col(COL_C + 768)),
                  pl.BlockSpec((4, GROUP_WIDTH), lambda bi, i: (0, 0)),
                  pl.BlockSpec((1, GROUP_WIDTH), lambda bi, i: (0, 0)),
                  pl.BlockSpec((GROUP_WIDTH, GROUP_WIDTH), lambda bi, i: (0, 0))],
        out_specs=pl.BlockSpec((None, TL, GROUP_WIDTH), lambda bi, i: (bi, i, 0)),
        out_shape=jax.ShapeDtypeStruct((b, s, GROUP_WIDTH), F32),
        scratch_shapes=[pltpu.VMEM((C + TL, GROUP_WIDTH), F32),
                        pltpu.VMEM((C + TL, GROUP_WIDTH), F32),
                        pltpu.VMEM((C + TL, GROUP_WIDTH), F32),
                        pltpu.VMEM((GROUP_WIDTH, GROUP_WIDTH), F32)],
        compiler_params=_cparams(("arbitrary", "arbitrary")),
        name="hgrn2",
    )(p3, p3, p3, p3, lbp, out_gain.astype(F32).reshape(1, GROUP_WIDTH), _block_ones(GROUP_WIDTH, HEAD_DIM))


CMP_LEN = 32
CMP_STRIDE = 16
SEL_BLOCK = 64
SEL_TOPK = 16
WINDOW = 512
GATE_ROW = 8


def _rms_rows(x, gain):
    return x * lax.rsqrt(jnp.mean(x * x, axis=-1, keepdims=True) + EPS) * gain


NSA_V_ROWS = (GROUP_HEADS + 1) * V_AUG


def _aug_shared_values(v, w_wide, w_t):
    tm = v.shape[0]
    vw_t = (jnp.concatenate([v] * GROUP_HEADS, axis=1) * w_wide).T
    first = lax.broadcasted_iota(jnp.int32, (V_AUG - HEAD_DIM, tm), 0) == 0
    blocks = []
    for h in range(GROUP_HEADS):
        blocks += [vw_t[HEAD_DIM * h:HEAD_DIM * (h + 1), :],
                   jnp.where(first, jnp.broadcast_to(w_t[h:h + 1, :], first.shape), 0.0)]
    blocks += [v.T, jnp.where(first, 1.0, 0.0)]
    return jnp.concatenate(blocks, axis=0).astype(BF16)


def _nsa_prep_kernel(q_ref, kv_ref, g_ref, qg_ref, kg_ref, bd_ref, w_ref, wt_ref,
                     qt_ref, kc_ref, vc_ref, ks_ref, vst_ref, kw_ref, vwt_ref, gt_ref):
    ti = pl.program_id(1)
    q = q_ref[...]
    ss = _seg_sum(q * q, bd_ref[...])
    qn = q * lax.rsqrt(ss * (1.0 / HEAD_DIM) + EPS) * qg_ref[...] * (HEAD_DIM ** -0.5 * LOG2E)
    qt_ref[...] = qn.T.astype(BF16)
    kv = kv_ref[...]
    tm = kv.shape[0]
    kc_ref[...] = kv[:, 0:64]
    vc_ref[...] = kv[:, 64:128]
    ks = _rms_rows(kv[:, 128:192], kg_ref[1:2, :])
    lane = lax.broadcasted_iota(jnp.int32, (tm, LANES), 1)
    blk = jnp.right_shift(ti * tm + lax.broadcasted_iota(jnp.int32, (tm, LANES), 0), SEL_BLOCK.bit_length() - 1)
    onehot = jnp.where(lane == blk + HEAD_DIM, 1.0, 0.0)
    ks_ref[...] = jnp.where(lane < HEAD_DIM, jnp.concatenate([ks, ks], axis=1), onehot).astype(BF16)
    kw = _rms_rows(kv[:, 256:320], kg_ref[2:3, :])
    kw_ref[...] = jnp.concatenate([kw, jnp.zeros_like(kw)], axis=1).astype(BF16)
    vst_ref[...] = _aug_shared_values(kv[:, 192:256], w_ref[...], wt_ref[...])
    vwt_ref[...] = _aug_shared_values(kv[:, 320:384], w_ref[...], wt_ref[...])
    gt_ref[...] = _sigmoid(g_ref[...]).T


def _gelu_tanh(x):
    return 0.5 * x * (1.0 + jnp.tanh(math.sqrt(2.0 / math.pi) * (x + 0.044715 * x * x * x)))


def _nsa_cmp_kernel(kr_ref, vr_ref, w1_ref, w2_ref, pos_ref, kg_ref, kc_ref, vct_ref, sh_ref, *, n_rows):
    half = CMP_STRIDE * HEAD_DIM
    sh_ref[n_rows:n_rows + 8, :] = jnp.zeros((8, HEAD_DIM), F32)
    outs = []
    for j, x_ref in enumerate((kr_ref, vr_ref)):
        r = x_ref[...].astype(BF16)
        w1 = w1_ref[j]
        first = jnp.dot(r, w1[0:half, :], preferred_element_type=F32)
        sh_ref[0:n_rows, :] = jnp.dot(r, w1[half:2 * half, :], preferred_element_type=F32)
        pos8 = jnp.broadcast_to(pos_ref[j], (8, 2 * half))
        posw = jnp.dot(pos8, w1.astype(F32), preferred_element_type=F32)[0:1, :]
        hdn = _gelu_tanh(first + sh_ref[1:n_rows + 1, :] + posw)
        outs.append(jnp.dot(hdn, w2_ref[j].astype(F32), preferred_element_type=F32))
    kc_ref[...] = _rms_rows(outs[0], kg_ref[0:1, :]).astype(BF16)
    vct_ref[...] = outs[1].T.astype(BF16)


def _nsa_attn_kernel(qt_ref, kc_ref, vct_ref, ks_ref, vst_ref, kw_ref, vwt_ref, gt_ref, ov_ref, db_ref, lo_ref,
                     og_ref, bd_ref, y_ref, qa_ref, acc_s, m_s, acc_w, m_w, ss_ref, sw_ref,
                     *, tq, slopes, n_cmp):
    qi = pl.program_id(1)
    n_rows = kc_ref.shape[0]
    n_sel = ov_ref.shape[0]
    qpos = qi * tq + lax.broadcasted_iota(jnp.int32, (1, tq), 1)

    nio = lax.broadcasted_iota(jnp.int32, (n_rows, tq), 0)
    dist_c = qpos - (nio * CMP_STRIDE + (CMP_LEN - 1))
    valid_c = (dist_c >= 0) & (nio < n_cmp)
    dist_cf = dist_c.astype(F32)
    kc = kc_ref[...]
    vct = vct_ref[...]
    o_cmp = []
    p_sum = jnp.zeros((n_rows, tq), F32)
    for h in range(GROUP_HEADS):
        qh = qt_ref[HEAD_DIM * h:HEAD_DIM * (h + 1), :]
        s = jnp.dot(kc, qh, preferred_element_type=F32) - (slopes[h] * LOG2E) * dist_cf
        s = jnp.where(valid_c, s, MASKED)
        m = jnp.maximum(jnp.max(s, axis=0, keepdims=True), M_INIT)
        e = jnp.exp2(s - m)
        p = e / jnp.maximum(jnp.sum(e, axis=0, keepdims=True), 1e-30)
        p_sum = p_sum + p
        o_cmp.append(jnp.dot(vct, p.astype(BF16), preferred_element_type=F32))

    p_hi = p_sum.astype(BF16)
    p_lo = (p_sum - p_hi.astype(F32)).astype(BF16)
    ov = ov_ref[...]
    imp = jnp.dot(ov, p_hi, preferred_element_type=F32) + jnp.dot(ov, p_lo, preferred_element_type=F32)
    jio = lax.broadcasted_iota(jnp.int32, (n_sel, tq), 0)
    qblk = jnp.right_shift(qpos, SEL_BLOCK.bit_length() - 1)
    imp = jnp.where(jio <= qblk, imp, -1.0)
    imp = jnp.where((jio == 0) | (jio == qblk) | (jio == qblk - 1), 2.0, imp)
    rank = jnp.zeros((n_sel, tq), F32)
    for i in range(n_sel):
        row = imp[i:i + 1, :]
        beats = (row > imp) | ((row == imp) & (jio > i))
        rank = rank + jnp.where(beats, 1.0, 0.0)
    bsel = jnp.where(rank < float(min(SEL_TOPK, n_sel)), 0.0, MASKED)
    if n_sel < LANES - HEAD_DIM:
        bsel = jnp.concatenate([bsel, jnp.zeros((LANES - HEAD_DIM - n_sel, tq), F32)], axis=0)
    for h in range(GROUP_HEADS):
        qa_ref[h, 0:HEAD_DIM, :] = qt_ref[HEAD_DIM * h:HEAD_DIM * (h + 1), :]
        qa_ref[h, HEAD_DIM:LANES, :] = bsel.astype(BF16)
    for ref, val in ((acc_s, 0.0), (acc_w, 0.0), (m_s, M_INIT), (m_w, M_INIT)):
        ref[...] = jnp.full(ref.shape, val, F32)

    def tile_off(kt, h):
        return ((kt - qi) * tq + (tq - 1)).astype(F32) * (slopes[h] * LOG2E)

    def values(vt_ref, kt, h, diag):
        base = GROUP_HEADS * V_AUG if diag else V_AUG * h
        return vt_ref[base:base + V_AUG, pl.ds(pl.multiple_of(kt * tq, tq), tq)]

    def sel_scores(kt, h):
        kb = ks_ref[pl.ds(pl.multiple_of(kt * tq, tq), tq), :]
        ss_ref[h] = jnp.dot(kb, qa_ref[h], preferred_element_type=F32)

    def sel_step(kt, diag):
        def prefetch(h):
            if h + AHEAD < GROUP_HEADS:
                sel_scores(kt, h + AHEAD)
            elif not diag:
                sel_scores(kt + 1, h + AHEAD - GROUP_HEADS)

        _run_chains(GROUP_HEADS, prefetch, lambda h: _flash_update(
            ss_ref.at[h], db_ref.at[h] if diag else None, 0.0 if diag else tile_off(kt, h),
            values(vst_ref, kt, h, diag), m_s.at[h], acc_s.at[h]))

    for h in range(AHEAD):
        sel_scores(0, h)

    def body(kt, carry):
        sel_step(kt, False)
        return carry

    lax.fori_loop(0, qi, body, 0)
    sel_step(qi, True)

    def win_step(back, bias_of, diag):
        kt = jnp.maximum(qi - back, 0)
        penalty = None if diag else jnp.where(qi >= back, 0.0, MASKED)
        kb = kw_ref[pl.ds(pl.multiple_of(kt * tq, tq), tq), :]

        def win_scores(h):
            sw_ref[h] = jnp.dot(kb, qa_ref[h], preferred_element_type=F32)

        for h in range(AHEAD):
            win_scores(h)

        def prefetch(h):
            if h + AHEAD < GROUP_HEADS:
                win_scores(h + AHEAD)

        _run_chains(GROUP_HEADS, prefetch, lambda h: _flash_update(
            sw_ref.at[h], bias_of(h), 0.0 if diag else tile_off(kt, h),
            values(vwt_ref, kt, h, diag), m_w.at[h], acc_w.at[h], penalty))

    win_step(2, lambda h: lo_ref, False)
    win_step(1, lambda h: None, False)
    win_step(0, lambda h: db_ref.at[h], True)

    gt = gt_ref[...]
    heads = []
    for h in range(GROUP_HEADS):
        a_s, a_w = acc_s[h], acc_w[h]
        o_sel = a_s[0:HEAD_DIM, :] / a_s[HEAD_DIM:HEAD_DIM + 1, :]
        o_win = a_w[0:HEAD_DIM, :] / a_w[HEAD_DIM:HEAD_DIM + 1, :]
        g = GATE_ROW + 3 * h
        heads.append(gt[g:g + 1, :] * o_cmp[h] + gt[g + 1:g + 2, :] * o_sel + gt[g + 2:g + 3, :] * o_win)
    o = jnp.concatenate(heads, axis=0).T
    ss = _seg_sum(o * o, bd_ref[...])
    y_ref[...] = o * lax.rsqrt(ss * (1.0 / HEAD_DIM) + EPS) * og_ref[...]


def _nsa(p3, cmp_pos, cmp_w1, cmp_w2, q_gain, k_gain, out_gain, slopes, tq=256):
    b, s, _ = p3.shape
    tm = tq
    n_rows = s // CMP_STRIDE
    n_cmp = (s - CMP_LEN) // CMP_STRIDE + 1
    n_sel = s // SEL_BLOCK
    bd64 = _block_ones(GROUP_WIDTH, HEAD_DIM)
    qg = jnp.tile(q_gain.astype(F32), GROUP_HEADS).reshape(1, GROUP_WIDTH)
    kg = k_gain.astype(F32)
    tok = lambda w: pl.BlockSpec((None, tm, w), lambda bi, i: (bi, i, 0))
    tok_t = lambda w: pl.BlockSpec((None, w, tm), lambda bi, i: (bi, 0, i))
    full = lambda r, c: pl.BlockSpec((r, c), lambda bi, i: (0, 0))
    assert n_sel <= LANES - HEAD_DIM, "block one-hot lanes hold at most 64 selection blocks"
    assert WINDOW == 2 * tq and tq % SEL_BLOCK == 0, "window branch walks exactly the key tiles qi-2, qi-1, qi"
    w_wide, w_t = _key_weights(tm, slopes)
    qt, kc, vc, ks, vst, kw, vwt, gt = pl.pallas_call(
        _nsa_prep_kernel,
        grid=(b, s // tm),
        in_specs=[pl.BlockSpec((None, tm, GROUP_WIDTH), lambda bi, i: (bi, i, COL_D // GROUP_WIDTH)),
                  pl.BlockSpec((None, tm, 384), lambda bi, i: (bi, i, (COL_D + GROUP_WIDTH) // 384)),
                  pl.BlockSpec((None, tm, LANES), lambda bi, i: (bi, i, COL_G // LANES)),
                  full(1, GROUP_WIDTH), full(3, HEAD_DIM), full(GROUP_WIDTH, GROUP_WIDTH),
                  full(tm, GROUP_WIDTH), full(8, tm)],
        out_specs=[tok_t(GROUP_WIDTH), tok(HEAD_DIM), tok(HEAD_DIM), tok(LANES), tok_t(NSA_V_ROWS),
                   tok(LANES), tok_t(NSA_V_ROWS), tok_t(LANES)],
        out_shape=[jax.ShapeDtypeStruct((b, GROUP_WIDTH, s), BF16),
                   jax.ShapeDtypeStruct((b, s, HEAD_DIM), F32),
                   jax.ShapeDtypeStruct((b, s, HEAD_DIM), F32),
                   jax.ShapeDtypeStruct((b, s, LANES), BF16),
                   jax.ShapeDtypeStruct((b, NSA_V_ROWS, s), BF16),
                   jax.ShapeDtypeStruct((b, s, LANES), BF16),
                   jax.ShapeDtypeStruct((b, NSA_V_ROWS, s), BF16),
                   jax.ShapeDtypeStruct((b, LANES, s), F32)],
        compiler_params=_cparams(("arbitrary", "arbitrary")),
        name="nsa_prep",
    )(p3, p3, p3, qg, kg, bd64, w_wide, w_t)

    row_w = CMP_STRIDE * HEAD_DIM
    kcmp, vcmp_t = pl.pallas_call(
        functools.partial(_nsa_cmp_kernel, n_rows=n_rows),
        grid=(b,),
        in_specs=[pl.BlockSpec((None, n_rows, row_w), lambda bi: (bi, 0, 0)),
                  pl.BlockSpec((None, n_rows, row_w), lambda bi: (bi, 0, 0)),
                  pl.BlockSpec((2, 2 * row_w, HEAD_DIM), lambda bi: (0, 0, 0)),
                  pl.BlockSpec((2, HEAD_DIM, HEAD_DIM), lambda bi: (0, 0, 0)),
                  pl.BlockSpec((2, 1, 2 * row_w), lambda bi: (0, 0, 0)),
                  pl.BlockSpec((3, HEAD_DIM), lambda bi: (0, 0))],
        out_specs=[pl.BlockSpec((None, n_rows, HEAD_DIM), lambda bi: (bi, 0, 0)),
                   pl.BlockSpec((None, HEAD_DIM, n_rows), lambda bi: (bi, 0, 0))],
        out_shape=[jax.ShapeDtypeStruct((b, n_rows, HEAD_DIM), BF16),
                   jax.ShapeDtypeStruct((b, HEAD_DIM, n_rows), BF16)],
        scratch_shapes=[pltpu.VMEM((n_rows + 8, HEAD_DIM), F32)],
        compiler_params=_cparams(("arbitrary",)),
        name="nsa_cmp",
    )(kc.reshape(b, n_rows, row_w), vc.reshape(b, n_rows, row_w), cmp_w1.astype(BF16), cmp_w2.astype(BF16),
      cmp_pos.astype(F32).reshape(2, 1, 2 * row_w), kg)

    cs = np.arange(n_rows)[:, None] * CMP_STRIDE
    ss = np.arange(n_sel)[None, :] * SEL_BLOCK
    overlap = np.clip(np.minimum(cs + CMP_LEN, ss + SEL_BLOCK) - np.maximum(cs, ss), 0, None) / CMP_LEN
    overlap[n_cmp:, :] = 0.0
    ov_t = jnp.asarray(overlap.T.astype(np.float32), BF16)

    seq = lambda r, c: pl.BlockSpec((None, r, c), lambda bi, i: (bi, 0, 0))
    kern = functools.partial(_nsa_attn_kernel, tq=tq, slopes=tuple(float(v) for v in slopes), n_cmp=n_cmp)
    return pl.pallas_call(
        kern,
        grid=(b, s // tq),
        in_specs=[pl.BlockSpec((None, GROUP_WIDTH, tq), lambda bi, i: (bi, 0, i)),
                  seq(n_rows, HEAD_DIM), seq(HEAD_DIM, n_rows),
                  seq(s, LANES), seq(NSA_V_ROWS, s), seq(s, LANES), seq(NSA_V_ROWS, s),
                  pl.BlockSpec((None, LANES, tq), lambda bi, i: (bi, 0, i)),
                  full(n_sel, n_rows),
                  pl.BlockSpec((GROUP_HEADS, tq, tq), lambda bi, i: (0, 0, 0)), full(tq, tq),
                  full(1, GROUP_WIDTH), full(GROUP_WIDTH, GROUP_WIDTH)],
        out_specs=pl.BlockSpec((None, tq, GROUP_WIDTH), lambda bi, i: (bi, i, 0)),
        out_shape=jax.ShapeDtypeStruct((b, s, GROUP_WIDTH), F32),
        scratch_shapes=[pltpu.VMEM((GROUP_HEADS, LANES, tq), BF16),
                        pltpu.VMEM((GROUP_HEADS, V_AUG, tq), F32), pltpu.VMEM((GROUP_HEADS, 8, tq), F32),
                        pltpu.VMEM((GROUP_HEADS, V_AUG, tq), F32), pltpu.VMEM((GROUP_HEADS, 8, tq), F32),
                        pltpu.VMEM((GROUP_HEADS, tq, tq), F32), pltpu.VMEM((GROUP_HEADS, tq, tq), F32)],
        compiler_params=_cparams(("arbitrary", "arbitrary")),
        name="nsa_attn",
    )(qt, kcmp, vcmp_t, ks, vst, kw, vwt, gt, ov_t, _diag_bias(tq, slopes), _window_low_bias(tq),
      out_gain.astype(F32).reshape(1, GROUP_WIDTH), bd64)


def _window_low_bias(tq):
    k = np.arange(tq)[:, None]
    q = np.arange(tq)[None, :]
    return jnp.asarray(np.where(k > q, 0.0, MASKED).astype(np.float32))


N_GROUPS = 4
EXPERTS_PER_GROUP = 8
N_EXPERTS = N_GROUPS * EXPERTS_PER_GROUP
D_EXPERT = 256
ROUTER_LANE0 = N_GROUPS


def _split3_dot(a, b_hi, b_lo):
    a_hi = a.astype(BF16)
    a_lo = (a - a_hi.astype(F32)).astype(BF16)
    return (jnp.dot(a_hi, b_hi, preferred_element_type=F32) + jnp.dot(a_lo, b_hi, preferred_element_type=F32)
            + jnp.dot(a_hi, b_lo, preferred_element_type=F32))


def _route(t, wr_hi, wr_lo, br):
    logits = _split3_dot(t, wr_hi, wr_lo) + br
    lane = lax.broadcasted_iota(jnp.int32, logits.shape, 1)
    lane_f = lane.astype(F32)
    big = float(LANES)
    is_g = lane < N_GROUPS
    gl = jnp.where(is_g, logits, MASKED)
    gmax = jnp.max(gl, axis=-1, keepdims=True)
    g_prob = 1.0 / jnp.sum(jnp.where(is_g, jnp.exp(gl - gmax), 0.0), axis=-1, keepdims=True)
    g_sel = jnp.min(jnp.where(is_g & (gl == gmax), lane_f, big), axis=-1, keepdims=True)
    lo = ROUTER_LANE0 + EXPERTS_PER_GROUP * g_sel
    in_grp = (lane_f >= lo) & (lane_f < lo + EXPERTS_PER_GROUP)
    el = jnp.where(in_grp, logits, MASKED)
    m1 = jnp.max(el, axis=-1, keepdims=True)
    i1 = jnp.min(jnp.where(in_grp & (el == m1), lane_f, big), axis=-1, keepdims=True)
    rest = in_grp & (lane_f != i1)
    el2 = jnp.where(rest, logits, MASKED)
    m2 = jnp.max(el2, axis=-1, keepdims=True)
    i2 = jnp.min(jnp.where(rest & (el2 == m2), lane_f, big), axis=-1, keepdims=True)
    r = jnp.exp(m2 - m1)
    w1 = g_prob / (1.0 + r)
    w2 = g_prob * r / (1.0 + r)
    return jnp.where(lane_f == i1, w1, 0.0) + jnp.where(lane_f == i2, w2, 0.0), g_sel


T_AUG = D_MODEL + LANES


def _outproj_kernel(x_ref, ya_ref, yb_ref, yc_ref, yd_ref, w_ref, g_ref, wr_hi_ref, wr_lo_ref, br_ref,
                    xo_ref, tg_ref, route_ref, cnt_ref, run_ref):
    @pl.when(pl.program_id(0) == 0)
    def _():
        run_ref[...] = jnp.zeros_like(run_ref)

    acc = x_ref[...]
    for gi, y_ref in enumerate((ya_ref, yb_ref, yc_ref, yd_ref)):
        acc = acc + jnp.dot(y_ref[...].astype(BF16), w_ref[GROUP_WIDTH * gi:GROUP_WIDTH * (gi + 1), :],
                            preferred_element_type=F32)
    xo_ref[...] = acc
    ms = jnp.mean(acc * acc, axis=-1, keepdims=True)
    t = acc * lax.rsqrt(ms + EPS) * g_ref[...]
    tg_ref[:, 0:D_MODEL] = t
    gate, g_sel = _route(t, wr_hi_ref[...], wr_lo_ref[...], br_ref[...])
    tg_ref[:, D_MODEL:T_AUG] = gate

    tm = t.shape[0]
    lane = lax.broadcasted_iota(jnp.int32, (tm, LANES), 1)
    onehot = jnp.where(lane.astype(F32) == g_sel, 1.0, 0.0)
    ri = lax.broadcasted_iota(jnp.int32, (tm, tm), 0)
    ci = lax.broadcasted_iota(jnp.int32, (tm, tm), 1)
    before = jnp.where(ri > ci, 1.0, 0.0).astype(BF16)
    prefix = jnp.dot(before, onehot.astype(BF16), preferred_element_type=F32)
    rank = jnp.sum(onehot * (run_ref[...] + prefix), axis=-1, keepdims=True)
    route_ref[...] = jnp.where(lane == 0, g_sel, 0.0) + jnp.where(lane == 1, rank, 0.0)
    run_ref[...] = run_ref[...] + jnp.sum(onehot, axis=0, keepdims=True)
    cnt_ref[...] = run_ref[...]


def _outproj(x2d, ys, w_out, ffn_gain, w_group, b_group, w_expert, b_expert, tm=256):
    t = x2d.shape[0]
    wr = jnp.zeros((D_MODEL, LANES), F32).at[:, 0:N_GROUPS].set(w_group).at[:, ROUTER_LANE0:ROUTER_LANE0 + N_EXPERTS].set(w_expert)
    wr_hi = wr.astype(BF16)
    wr_lo = (wr - wr_hi.astype(F32)).astype(BF16)
    br = jnp.zeros((1, LANES), F32).at[0, 0:N_GROUPS].set(b_group).at[0, ROUTER_LANE0:ROUTER_LANE0 + N_EXPERTS].set(b_expert)
    row = lambda w: pl.BlockSpec((tm, w), lambda i: (i, 0))
    full = lambda r, c: pl.BlockSpec((r, c), lambda i: (0, 0))
    return list(pl.pallas_call(
        _outproj_kernel,
        grid=(t // tm,),
        in_specs=[row(D_MODEL), row(GROUP_WIDTH), row(GROUP_WIDTH), row(GROUP_WIDTH), row(GROUP_WIDTH),
                  full(D_MODEL, D_MODEL), full(1, D_MODEL), full(D_MODEL, LANES), full(D_MODEL, LANES), full(1, LANES)],
        out_specs=[row(D_MODEL), row(T_AUG), row(LANES), full(1, LANES)],
        out_shape=[jax.ShapeDtypeStruct((t, D_MODEL), F32),
                   jax.ShapeDtypeStruct((t, T_AUG), F32),
                   jax.ShapeDtypeStruct((t, LANES), F32),
                   jax.ShapeDtypeStruct((1, LANES), F32)],
        scratch_shapes=[pltpu.VMEM((1, LANES), F32)],
        compiler_params=_cparams(("arbitrary",)),
        name="outproj_router",
    )(x2d, *ys, w_out.astype(BF16), ffn_gain.reshape(1, D_MODEL), wr_hi, wr_lo, br))


MOE_TILE = 1024
DMA_PRIORITIES = 2


def _row_copies(n, make_copy):
    def start(k, carry):
        for p in range(DMA_PRIORITIES):
            make_copy(DMA_PRIORITIES * k + p).start(priority=p)
        return carry

    def wait(r, carry):
        make_copy(r).wait()
        return carry

    lax.fori_loop(0, n // DMA_PRIORITIES, start, 0)
    lax.fori_loop(0, n, wait, 0)


def _moe_scatter_kernel(pos_ref, tg_ref, xs_in_ref, xs_ref, sem):
    del xs_in_ref
    tm = tg_ref.shape[0]
    base = pl.program_id(0) * tm
    _row_copies(tm, lambda r: pltpu.make_async_copy(tg_ref.at[pl.ds(r, 1)],
                                                    xs_ref.at[pl.ds(pos_ref[base + r], 1)], sem))


EXPERTS_PER_STEP = 4
MOE_VMEM_LIMIT = 56 * 1024 * 1024


def _moe_expert_kernel(grp_ref, valid_ref, xs_ref, wg_ref, wu_ref, wd_ref, y_ref, acc_ref):
    i, step = pl.program_id(0), pl.program_id(1)

    @pl.when(step == 0)
    def _():
        acc_ref[...] = jnp.zeros_like(acc_ref)

    @pl.when(valid_ref[i] == 1)
    def _():
        x = xs_ref[:, 0:D_MODEL].astype(BF16)
        gate = xs_ref[:, D_MODEL:T_AUG]
        lane = lax.broadcasted_iota(jnp.int32, gate.shape, 1)
        total = None
        for k in range(EXPERTS_PER_STEP):
            col = ROUTER_LANE0 + grp_ref[i] * EXPERTS_PER_GROUP + step * EXPERTS_PER_STEP + k
            w = jnp.sum(jnp.where(lane == col, gate, 0.0), axis=-1, keepdims=True)
            a = jnp.dot(x, wg_ref[k].astype(BF16), preferred_element_type=F32)
            u = jnp.dot(x, wu_ref[k].astype(BF16), preferred_element_type=F32)
            act = a * _sigmoid(a) * u * w
            down = jnp.dot(act.astype(BF16), wd_ref[k].astype(BF16), preferred_element_type=F32)
            total = down if total is None else total + down
        acc_ref[...] += total

    @pl.when(step == EXPERTS_PER_GROUP // EXPERTS_PER_STEP - 1)
    def _():
        y_ref[...] = acc_ref[...]


def _moe_combine_kernel(pos_ref, x_ref, ys_ref, o_ref, buf_ref, sem):
    tm = x_ref.shape[0]
    base = pl.program_id(0) * tm
    _row_copies(tm, lambda r: pltpu.make_async_copy(ys_ref.at[pl.ds(pos_ref[base + r], 1)],
                                                    buf_ref.at[pl.ds(r, 1)], sem))
    o_ref[...] = x_ref[...] + buf_ref[...]


def _moe_routed(x2d, tg, route, counts, w_gate, w_up, w_down, tm=512):
    t = x2d.shape[0]
    te = MOE_TILE
    n_tiles = t // te + N_GROUPS
    n_rows = n_tiles * te
    grp = route[:, 0].astype(jnp.int32)
    rank = route[:, 1].astype(jnp.int32)
    cnt = counts[0, 0:N_GROUPS].astype(jnp.int32)
    padded = ((cnt + te - 1) // te) * te
    ends = jnp.cumsum(padded)
    pos = (ends - padded)[grp] + rank
    starts = jnp.arange(n_tiles, dtype=jnp.int32) * te
    tile_grp = jnp.minimum(jnp.sum((starts[:, None] >= ends[None, :]).astype(jnp.int32), axis=1), N_GROUPS - 1)
    tile_valid = (starts < ends[-1]).astype(jnp.int32)

    xs = pl.pallas_call(
        _moe_scatter_kernel,
        grid_spec=pltpu.PrefetchScalarGridSpec(
            num_scalar_prefetch=1, grid=(t // tm,),
            in_specs=[pl.BlockSpec((tm, T_AUG), lambda i, pos: (i, 0)), pl.BlockSpec(memory_space=pl.ANY)],
            out_specs=pl.BlockSpec(memory_space=pl.ANY),
            scratch_shapes=[pltpu.SemaphoreType.DMA(())]),
        out_shape=jax.ShapeDtypeStruct((n_rows, T_AUG), F32),
        input_output_aliases={2: 0},
        compiler_params=_cparams(("arbitrary",)),
        name="moe_scatter",
    )(pos, tg, jnp.zeros((n_rows, T_AUG), F32))

    steps = EXPERTS_PER_GROUP // EXPERTS_PER_STEP
    wg = w_gate.reshape(N_GROUPS * steps, EXPERTS_PER_STEP, D_MODEL, D_EXPERT)
    wu = w_up.reshape(N_GROUPS * steps, EXPERTS_PER_STEP, D_MODEL, D_EXPERT)
    wd = w_down.reshape(N_GROUPS * steps, EXPERTS_PER_STEP, D_EXPERT, D_MODEL)
    wsel = lambda i, e, g, v: (g[i] * steps + e, 0, 0, 0)
    ys = pl.pallas_call(
        _moe_expert_kernel,
        grid_spec=pltpu.PrefetchScalarGridSpec(
            num_scalar_prefetch=2, grid=(n_tiles, steps),
            in_specs=[pl.BlockSpec((te, T_AUG), lambda i, e, g, v: (i, 0)),
                      pl.BlockSpec((None, EXPERTS_PER_STEP, D_MODEL, D_EXPERT), wsel),
                      pl.BlockSpec((None, EXPERTS_PER_STEP, D_MODEL, D_EXPERT), wsel),
                      pl.BlockSpec((None, EXPERTS_PER_STEP, D_EXPERT, D_MODEL), wsel)],
            out_specs=pl.BlockSpec((te, D_MODEL), lambda i, e, g, v: (i, 0)),
            scratch_shapes=[pltpu.VMEM((te, D_MODEL), F32)]),
        out_shape=jax.ShapeDtypeStruct((n_rows, D_MODEL), F32),
        compiler_params=_cparams(("arbitrary", "arbitrary"), vmem_limit=MOE_VMEM_LIMIT),
        name="moe_experts",
    )(tile_grp, tile_valid, xs, wg, wu, wd)

    return pl.pallas_call(
        _moe_combine_kernel,
        grid_spec=pltpu.PrefetchScalarGridSpec(
            num_scalar_prefetch=1, grid=(t // tm,),
            in_specs=[pl.BlockSpec((tm, D_MODEL), lambda i, pos: (i, 0)), pl.BlockSpec(memory_space=pl.ANY)],
            out_specs=pl.BlockSpec((tm, D_MODEL), lambda i, pos: (i, 0)),
            scratch_shapes=[pltpu.VMEM((tm, D_MODEL), F32), pltpu.SemaphoreType.DMA(())]),
        out_shape=jax.ShapeDtypeStruct((t, D_MODEL), F32),
        compiler_params=_cparams(("arbitrary",)),
        name="moe_combine",
    )(pos, x2d, ys)


def _alibi_slopes():
    n = 2 * GROUP_HEADS
    s = 2.0 ** (-8.0 * np.arange(1, n + 1) / n)
    return s[0::2], s[1::2]


def kernel(x, norm_mix, norm_ffn, w_in, w_out, diff_q_gain, diff_k_gain, diff_lambda, diff_sub_gain, mlstm_conv, mlstm_gate_bias, mlstm_out_gain, hgrn_lower_bounds, hgrn_out_gain, nsa_cmp_pos, nsa_cmp_w1, nsa_cmp_w2, nsa_q_gain, nsa_k_gain, nsa_out_gain, moe_w_group, moe_b_group, moe_w_expert, moe_b_expert, moe_w_gate, moe_w_up, moe_w_down):
    b, s, d = x.shape
    slopes_diff, slopes_nsa = _alibi_slopes()
    lb_soft = jax.nn.softmax(hgrn_lower_bounds.astype(F32), axis=0)
    lower_bounds = jnp.cumsum(lb_soft, axis=0) - lb_soft[0]
    x2d = x.reshape(b * s, d)
    for l in range(norm_mix.shape[0]):
        p = _inproj(x2d, norm_mix[l], _pack_w_in(w_in[l])).reshape(b, s, P_COLS)
        y_a = _diff_attention(p, diff_q_gain[l], diff_k_gain[l], diff_lambda[l], diff_sub_gain[l], slopes_diff, l)
        y_b = _mlstm(p, mlstm_conv[l], mlstm_gate_bias[l], mlstm_out_gain[l])
        y_c = _hgrn2(p, lower_bounds[l], hgrn_out_gain[l])
        y_d = _nsa(p, nsa_cmp_pos[l], nsa_cmp_w1[l], nsa_cmp_w2[l], nsa_q_gain[l], nsa_k_gain[l],
                   nsa_out_gain[l], slopes_nsa)
        ys = [y.reshape(b * s, GROUP_WIDTH) for y in (y_a, y_b, y_c, y_d)]
        x2d, tg, route, counts = _outproj(x2d, ys, w_out[l], norm_ffn[l], moe_w_group[l], moe_b_group[l],
                                          moe_w_expert[l], moe_b_expert[l])
        x2d = _moe_routed(x2d, tg, route, counts, moe_w_gate[l], moe_w_up[l], moe_w_down[l])
    return x2d.reshape(b, s, d)
```
